```python
import jax, jax.numpy as jnp
from jax import lax
import numpy as np

D_MODEL = 1024
BATCH = 8
SEQ = 2048
DEPTH = 2

LRU_WIDTH = 512
LRU_BLOCKS = 8
LRU_BLOCK = LRU_WIDTH // LRU_BLOCKS
CONV_W = 4
LRU_C = 8.0
HG_HEADS = 4
HG_DK = 128
HG_DV = 128
HG_CHUNK = 64
HG_FDIM = HG_HEADS * HG_DK
HG_VDIM = HG_HEADS * HG_DV
EVEN_IN = 2 * LRU_WIDTH + 2 * HG_FDIM + 2 * HG_VDIM
EVEN_MIX = LRU_WIDTH + HG_VDIM

HEAD_DIM = 64
NSA_HEADS = 8
NSA_GROUPS = 2
NSA_HPG = NSA_HEADS // NSA_GROUPS
NSA_KV = NSA_GROUPS * HEAD_DIM
CMP_LEN = 32
CMP_STRIDE = 16
CMP_HIDDEN = 128
SLC_LEN = 64
SLC_TOPN = 16
SLC_QBLK = 64
WIN = 512
WIN_QBLK = 128
FORCE_BONUS = 1e4
MOBA_HEADS = 8
MOBA_BLOCK = 256
MOBA_TOPK = 3
MOBA_QBLK = 32
ODD_IN = NSA_HEADS * HEAD_DIM + 6 * NSA_KV + 3 * NSA_HEADS + 3 * MOBA_HEADS * HEAD_DIM
ODD_MIX = NSA_HEADS * HEAD_DIM + MOBA_HEADS * HEAD_DIM

N_EXPERTS = 256
TOP_K = 8
N_GROUPS = 8
TOPK_GROUPS = 4
EXPERT_HIDDEN = 256
SHARED_HIDDEN = 256
ROUTED_SCALE = 2.5
DISPATCH_BLK = 128

DN_ALPHA = (2 * DEPTH) ** 0.25
DN_BETA = (8 * DEPTH) ** -0.25
N_EVEN = (DEPTH + 1) // 2
N_ODD = DEPTH // 2
LN_EPS = 1e-5
NEG = -1e30

kernel_name = "hybrid_lru_hgrn2_nsa_moba_moe"


def layer_norm(x, g, b):
    xf = x.astype(jnp.float32)
    mu = xf.mean(-1, keepdims=True)
    var = jnp.square(xf - mu).mean(-1, keepdims=True)
    return ((xf - mu) * lax.rsqrt(var + LN_EPS) * g + b).astype(x.dtype)


def masked_softmax(s, mask):
    s = jnp.where(mask, s.astype(jnp.float32), NEG)
    m = s.max(-1, keepdims=True)
    p = jnp.where(mask, jnp.exp(s - m), 0.0)
    return p / jnp.maximum(p.sum(-1, keepdims=True), 1e-30)


def alibi_slopes():
    n = NSA_HEADS + MOBA_HEADS
    s = 2.0 ** (-8.0 * np.arange(1, n + 1) / n)
    return jnp.asarray(s[0::2], jnp.float32), jnp.asarray(s[1::2], jnp.float32)


def slc_overlap(nc, ns):
    i = np.arange(nc)[:, None]
    j = np.arange(ns)[None, :]
    return ((CMP_STRIDE * i < SLC_LEN * (j + 1)) & (CMP_STRIDE * i + CMP_LEN > SLC_LEN * j)).astype(np.float32)


def rg_lru_block(xb, gb, conv_w, conv_b, wa, ba, wx, bx, lam):
    B, T, C = xb.shape
    xc = lax.conv_general_dilated(xb, conv_w[:, None, :], window_strides=(1,), padding=[(CONV_W - 1, 0)],
                                  dimension_numbers=('NWC', 'WIO', 'NWC'), feature_group_count=C) + conv_b
    xh = xc.reshape(B, T, LRU_BLOCKS, LRU_BLOCK)
    r = jax.nn.sigmoid(jnp.einsum('bthi,hij->bthj', xh, wa).reshape(B, T, C) + ba)
    i = jax.nn.sigmoid(jnp.einsum('bthi,hij->bthj', xh, wx).reshape(B, T, C) + bx)
    log_a = (-LRU_C * r * jax.nn.softplus(-lam)).astype(jnp.float32)
    a = jnp.exp(log_a)
    mult = jnp.sqrt(-jnp.expm1(2.0 * log_a)).at[:, 0].set(1.0)
    b = mult * (i * xc).astype(jnp.float32)

    def combine(l, rr):
        a1, b1 = l
        a2, b2 = rr
        return a1 * a2, a2 * b1 + b2

    _, h = lax.associative_scan(combine, (a, b), axis=1)
    return h.astype(xb.dtype) * jax.nn.gelu(gb)


def hgrn2_block(q, f_pre, v, g, lb, norm_g):
    B, T, _ = q.shape
    f32 = jnp.float32
    n_chunk = T // HG_CHUNK
    q = jax.nn.silu(q.astype(f32))
    f = lb + (1.0 - lb) * jax.nn.sigmoid(f_pre.astype(f32))
    k = 1.0 - f
    logf = jnp.log(f)

    def heads(z, d):
        return z.reshape(B, n_chunk, HG_CHUNK, HG_HEADS, d).transpose(1, 0, 3, 2, 4)

    qc, kc, gc, vc = heads(q, HG_DK), heads(k, HG_DK), heads(logf, HG_DK), heads(v.astype(f32), HG_DV)
    causal = jnp.tril(jnp.ones((HG_CHUNK, HG_CHUNK), bool))[:, :, None]

    def step(S, inp):
        qb, kb, gb, vb = inp
        cum = jnp.cumsum(gb, axis=2)
        o_inter = jnp.einsum('bhtk,bhkv->bhtv', qb * jnp.exp(cum), S)
        diff = cum[:, :, :, None, :] - cum[:, :, None, :, :]
        decay = jnp.exp(jnp.where(causal, diff, NEG))
        A = jnp.einsum('bhtk,bhsk,bhtsk->bhts', qb, kb, decay)
        o = o_inter + jnp.einsum('bhts,bhsv->bhtv', A, vb)
        last = cum[:, :, -1:, :]
        S = jnp.exp(last[:, :, 0, :])[..., None] * S + jnp.einsum('bhsk,bhsv->bhkv', kb * jnp.exp(last - cum), vb)
        return S, o

    S0 = jnp.zeros((B, HG_HEADS, HG_DK, HG_DV), f32)
    _, o = lax.scan(step, S0, (qc, kc, gc, vc))
    o = o.transpose(1, 0, 3, 2, 4).reshape(B, T, HG_HEADS, HG_DV)
    o = o * lax.rsqrt(jnp.mean(o * o, -1, keepdims=True) + 1e-6) * norm_g.reshape(HG_HEADS, HG_DV)
    return (o.reshape(B, T, HG_VDIM) * jax.nn.sigmoid(g.astype(f32))).astype(g.dtype)


def even_mixer(x, w_in, conv_w, conv_b, wa, ba, wx, bx, lam, lb, norm_g, w_out):
    u = x @ w_in
    sizes = [LRU_WIDTH, LRU_WIDTH, HG_FDIM, HG_FDIM, HG_VDIM, HG_VDIM]
    xb, gb, q, f_pre, v, g = jnp.split(u, np.cumsum(sizes)[:-1], axis=-1)
    ya = rg_lru_block(xb, gb, conv_w, conv_b, wa, ba, wx, bx, lam)
    yb = hgrn2_block(q, f_pre, v, g, lb, norm_g)
    return jnp.concatenate([ya, yb], -1).astype(x.dtype) @ w_out


def nsa_block(q, kc, vc, ks, vs, kw, vw, gates, pe_k, pe_v, wk1, wk2, wv1, wv2, slopes):
    B, T, _ = q.shape
    G, I, Dh = NSA_GROUPS, NSA_HPG, HEAD_DIM
    f32 = jnp.float32
    scale = Dh ** -0.5
    qh = q.reshape(B, T, G, I, Dh).transpose(0, 2, 3, 1, 4)

    def grp(z):
        return z.reshape(B, T, G, Dh).transpose(0, 2, 1, 3)

    kc, vc, ks, vs, kw, vw = grp(kc), grp(vc), grp(ks), grp(vs), grp(kw), grp(vw)
    sl = slopes.reshape(1, G, I, 1, 1)
    pos = jnp.arange(T)

    nc = (T - CMP_LEN) // CMP_STRIDE + 1
    cidx = jnp.arange(nc)[:, None] * CMP_STRIDE + jnp.arange(CMP_LEN)[None, :]

    def compress(z, pe, w1, w2):
        blk = (z[:, :, cidx] + pe).reshape(B, G, nc, CMP_LEN * Dh)
        return jax.nn.gelu(blk @ w1) @ w2

    k_cmp = compress(kc, pe_k, wk1, wk2)
    v_cmp = compress(vc, pe_v, wv1, wv2)
    d_c = pos[:, None] - (jnp.arange(nc) * CMP_STRIDE + CMP_LEN - 1)[None, :]
    s_c = jnp.einsum('bgitd,bgcd->bgitc', qh, k_cmp).astype(f32) * scale - sl * d_c.astype(f32)
    p_c = masked_softmax(s_c, d_c >= 0)
    o_c = jnp.einsum('bgitc,bgcd->bgitd', p_c, v_cmp)

    ns = T // SLC_LEN
    n_sel = min(SLC_TOPN, ns)
    imp = jnp.einsum('bgtc,cj->bgtj', p_c.sum(2), jnp.asarray(slc_overlap(nc, ns)))
    own = pos // SLC_LEN
    j = jnp.arange(ns)
    forced = (j[None] == 0) | (j[None] == own[:, None]) | (j[None] == own[:, None] - 1)
    imp = jnp.where(j[None] <= own[:, None], imp + FORCE_BONUS * forced, NEG)
    _, sel = lax.top_k(imp, n_sel)

    ksb = ks.reshape(B, G, ns, SLC_LEN, Dh)
    vsb = vs.reshape(B, G, ns, SLC_LEN, Dh)
    nq = T // SLC_QBLK
    q_blocks = qh.reshape(B, G, I, nq, SLC_QBLK, Dh).transpose(3, 0, 1, 2, 4, 5)
    sel_blocks = sel.reshape(B, G, nq, SLC_QBLK, n_sel).transpose(2, 0, 1, 3, 4)
    bi = jnp.arange(B)[:, None, None, None]
    gi = jnp.arange(G)[None, :, None, None]

    def slc_step(args):
        qb, sb, q0 = args
        kg = ksb[bi, gi, sb].reshape(B, G, SLC_QBLK, n_sel * SLC_LEN, Dh)
        vg = vsb[bi, gi, sb].reshape(B, G, SLC_QBLK, n_sel * SLC_LEN, Dh)
        tpos = q0 + jnp.arange(SLC_QBLK)
        spos = (sb[..., None] * SLC_LEN + jnp.arange(SLC_LEN)).reshape(B, G, SLC_QBLK, n_sel * SLC_LEN)
        dist = (tpos[:, None] - spos)[:, :, None]
        s = jnp.einsum('bgiqd,bgqkd->bgiqk', qb, kg).astype(f32) * scale - sl * dist.astype(f32)
        p = masked_softmax(s, dist >= 0)
        return jnp.einsum('bgiqk,bgqkd->bgiqd', p, vg)

    o_s = lax.map(slc_step, (q_blocks, sel_blocks, jnp.arange(nq) * SLC_QBLK))
    o_s = o_s.transpose(1, 2, 3, 0, 4, 5).reshape(B, G, I, T, Dh)

    nw = T // WIN_QBLK
    kwid = WIN + WIN_QBLK
    kidx = jnp.arange(nw)[:, None] * WIN_QBLK + jnp.arange(kwid)[None, :]
    padw = ((0, 0), (0, 0), (WIN, 0), (0, 0))
    kwb = jnp.pad(kw, padw)[:, :, kidx]
    vwb = jnp.pad(vw, padw)[:, :, kidx]
    qw = qh.reshape(B, G, I, nw, WIN_QBLK, Dh)
    tpos = jnp.arange(nw)[:, None] * WIN_QBLK + jnp.arange(WIN_QBLK)[None, :]
    spos = kidx - WIN
    dist = tpos[:, :, None] - spos[:, None, :]
    m_w = (dist >= 0) & (dist < WIN) & (spos[:, None, :] >= 0)
    s_w = jnp.einsum('bgiwqd,bgwkd->bgiwqk', qw, kwb).astype(f32) * scale - sl[..., None] * dist.astype(f32)
    p_w = masked_softmax(s_w, m_w)
    o_w = jnp.einsum('bgiwqk,bgwkd->bgiwqd', p_w, vwb).reshape(B, G, I, T, Dh)

    g = jax.nn.sigmoid(gates.astype(f32)).reshape(B, T, G, I, 3).transpose(0, 2, 3, 1, 4)
    o = g[..., 0:1] * o_c + g[..., 1:2] * o_s + g[..., 2:3] * o_w
    return o.transpose(0, 3, 1, 2, 4).reshape(B, T, G * I * Dh)


def moba_block(q, k, v, slopes):
    B, T, _ = q.shape
    H, Dh, BS = MOBA_HEADS, HEAD_DIM, MOBA_BLOCK
    f32 = jnp.float32
    scale = Dh ** -0.5
    tp = -(-T // BS) * BS

    def heads(z):
        return jnp.pad(z.reshape(B, T, H, Dh).transpose(0, 2, 1, 3), ((0, 0), (0, 0), (0, tp - T), (0, 0)))

    qh, kh, vh = heads(q), heads(k), heads(v)
    nb = tp // BS
    kb = kh.reshape(B, H, nb, BS, Dh)
    vb = vh.reshape(B, H, nb, BS, Dh)
    kmean = kb.mean(3)
    n_sel = min(MOBA_TOPK, nb - 1)
    sl = slopes.reshape(1, H, 1, 1)
    bi = jnp.arange(B)[:, None, None, None]
    hi = jnp.arange(H)[None, :, None, None]
    nq = tp // MOBA_QBLK
    q_blocks = qh.reshape(B, H, nq, MOBA_QBLK, Dh).transpose(2, 0, 1, 3, 4)

    def step(args):
        qb, q0 = args
        tpos = q0 + jnp.arange(MOBA_QBLK)
        ob = q0 // BS
        k_own = lax.dynamic_slice_in_dim(kh, ob * BS, BS, axis=2)
        v_own = lax.dynamic_slice_in_dim(vh, ob * BS, BS, axis=2)
        d_own = tpos[:, None] - (ob * BS + jnp.arange(BS))[None, :]
        s_own = jnp.einsum('bhqd,bhkd->bhqk', qb, k_own).astype(f32) * scale - sl * d_own.astype(f32)
        m_own = jnp.broadcast_to(d_own >= 0, s_own.shape)
        if n_sel == 0:
            return jnp.einsum('bhqk,bhkd->bhqd', masked_softmax(s_own, m_own), v_own)
        gate = jnp.einsum('bhqd,bhjd->bhqj', qb, kmean).astype(f32)
        gate = jnp.where(jnp.arange(nb) < ob, gate, NEG)
        _, sel = lax.top_k(gate, n_sel)
        kg = kb[bi, hi, sel].reshape(B, H, MOBA_QBLK, n_sel * BS, Dh)
        vg = vb[bi, hi, sel].reshape(B, H, MOBA_QBLK, n_sel * BS, Dh)
        spos = (sel[..., None] * BS + jnp.arange(BS)).reshape(B, H, MOBA_QBLK, n_sel * BS)
        s_sel = jnp.einsum('bhqd,bhqkd->bhqk', qb, kg).astype(f32) * scale - sl * (tpos[:, None] - spos).astype(f32)
        m_sel = jnp.repeat(sel < ob, BS, axis=-1)
        p = masked_softmax(jnp.concatenate([s_sel, s_own], -1), jnp.concatenate([m_sel, m_own], -1))
        nk = n_sel * BS
        return (jnp.einsum('bhqk,bhqkd->bhqd', p[..., :nk], vg)
                + jnp.einsum('bhqk,bhkd->bhqd', p[..., nk:], v_own))

    o = lax.map(step, (q_blocks, jnp.arange(nq) * MOBA_QBLK))
    return o.transpose(1, 0, 3, 2, 4).reshape(B, tp, H * Dh)[:, :T]


def odd_mixer(x, w_in, pe_k, pe_v, wk1, wk2, wv1, wv2, w_out):
    u = x @ w_in
    sizes = [NSA_HEADS * HEAD_DIM] + [NSA_KV] * 6 + [3 * NSA_HEADS] + [MOBA_HEADS * HEAD_DIM] * 3
    q, kc, vc, ks, vs, kw, vw, gates, mq, mk, mv = jnp.split(u, np.cumsum(sizes)[:-1], axis=-1)
    s_nsa, s_moba = alibi_slopes()
    yc = nsa_block(q, kc, vc, ks, vs, kw, vw, gates, pe_k, pe_v, wk1, wk2, wv1, wv2, s_nsa)
    yd = moba_block(mq, mk, mv, s_moba)
    return jnp.concatenate([yc, yd], -1).astype(x.dtype) @ w_out


def routed_experts(xf, eidx, w, w_gate, w_up, w_down):
    N, D = xf.shape
    A = N * TOP_K
    flat_e = eidx.reshape(-1)
    order = jnp.argsort(flat_e)
    se = flat_e[order]
    stok = (order // TOP_K).astype(jnp.int32)
    sw = w.reshape(-1)[order]
    counts = jnp.bincount(flat_e, length=N_EXPERTS)
    padded = (counts + DISPATCH_BLK - 1) // DISPATCH_BLK * DISPATCH_BLK
    pend = jnp.cumsum(padded)
    pstart = pend - padded
    gstart = jnp.cumsum(counts) - counts
    dest = pstart[se] + jnp.arange(A) - gstart[se]
    n_blk = (A + DISPATCH_BLK - 1) // DISPATCH_BLK + N_EXPERTS
    n_rows = n_blk * DISPATCH_BLK
    row_tok = jnp.zeros((n_rows,), jnp.int32).at[dest].set(stok)
    row_w = jnp.zeros((n_rows,), sw.dtype).at[dest].set(sw)
    blk_e = jnp.minimum(jnp.searchsorted(pend, jnp.arange(n_blk) * DISPATCH_BLK, side='right'), N_EXPERTS - 1)
    xs = xf[row_tok].reshape(n_blk, DISPATCH_BLK, D)

    def run(args):
        xb, e = args
        return (jax.nn.silu(xb @ w_gate[e]) * (xb @ w_up[e])) @ w_down[e]

    y = lax.map(run, (xs, blk_e)).reshape(n_rows, D)
    return jax.ops.segment_sum(y * row_w[:, None].astype(y.dtype), row_tok, num_segments=N)


def moe(x, w_router, b_router, w_gate, w_up, w_down, ws_gate, ws_up, ws_down):
    B, T, D = x.shape
    xf = x.reshape(B * T, D)
    N = B * T
    scores = jax.nn.sigmoid((xf @ w_router).astype(jnp.float32))
    biased = scores + b_router
    gscore = lax.top_k(biased.reshape(N, N_GROUPS, N_EXPERTS // N_GROUPS), 2)[0].sum(-1)
    _, gidx = lax.top_k(gscore, TOPK_GROUPS)
    gmask = jax.nn.one_hot(gidx, N_GROUPS).sum(-2) > 0
    emask = jnp.repeat(gmask, N_EXPERTS // N_GROUPS, axis=-1)
    _, eidx = lax.top_k(jnp.where(emask, biased, NEG), TOP_K)
    w = jnp.take_along_axis(scores, eidx, axis=1)
    w = w / w.sum(-1, keepdims=True) * ROUTED_SCALE
    routed = routed_experts(xf, eidx, w, w_gate, w_up, w_down)
    shared = (jax.nn.silu(xf @ ws_gate) * (xf @ ws_up)) @ ws_down
    return (routed.astype(x.dtype) + shared).reshape(B, T, D)


def setup_inputs(seed: int = 0) -> dict:
    key = jax.random.key(seed)
    keys = iter(jax.random.split(key, 32))

    def nrm(shape, scale):
        return jax.random.normal(next(keys), shape, jnp.float32) * scale

    D = D_MODEL
    x = nrm((BATCH, SEQ, D), 1.0)
    even_w_in = nrm((N_EVEN, D, EVEN_IN), D ** -0.5)
    lru_conv_w = nrm((N_EVEN, CONV_W, LRU_WIDTH), CONV_W ** -0.5)
    lru_conv_b = nrm((N_EVEN, LRU_WIDTH), 0.01)
    lru_wa = nrm((N_EVEN, LRU_BLOCKS, LRU_BLOCK, LRU_BLOCK), LRU_BLOCK ** -0.5)
    lru_ba = nrm((N_EVEN, LRU_WIDTH), 0.01)
    lru_wx = nrm((N_EVEN, LRU_BLOCKS, LRU_BLOCK, LRU_BLOCK), LRU_BLOCK ** -0.5)
    lru_bx = nrm((N_EVEN, LRU_WIDTH), 0.01)
    a_c = jax.random.uniform(next(keys), (N_EVEN, LRU_WIDTH), jnp.float32, 0.9, 0.999)
    a = a_c ** (1.0 / LRU_C)
    lru_lambda = jnp.log(a) - jnp.log1p(-a)
    hg_lower_bound = nrm((DEPTH + 1, HG_FDIM), 0.5)
    hg_norm_g = 1.0 + nrm((N_EVEN, HG_VDIM), 0.02)
    even_w_out = nrm((N_EVEN, EVEN_MIX, D), EVEN_MIX ** -0.5 * DN_BETA)
    odd_w_in = nrm((N_ODD, D, ODD_IN), D ** -0.5)
    nsa_pe_k = nrm((N_ODD, CMP_LEN, HEAD_DIM), 0.1)
    nsa_pe_v = nrm((N_ODD, CMP_LEN, HEAD_DIM), 0.1)
    nsa_wk1 = nrm((N_ODD, CMP_LEN * HEAD_DIM, CMP_HIDDEN), (CMP_LEN * HEAD_DIM) ** -0.5)
    nsa_wk2 = nrm((N_ODD, CMP_HIDDEN, HEAD_DIM), CMP_HIDDEN ** -0.5)
    nsa_wv1 = nrm((N_ODD, CMP_LEN * HEAD_DIM, CMP_HIDDEN), (CMP_LEN * HEAD_DIM) ** -0.5)
    nsa_wv2 = nrm((N_ODD, CMP_HIDDEN, HEAD_DIM), CMP_HIDDEN ** -0.5)
    odd_w_out = nrm((N_ODD, ODD_MIX, D), ODD_MIX ** -0.5 * DN_BETA)
    ln_g = 1.0 + nrm((DEPTH, 2, D), 0.02)
    ln_b = nrm((DEPTH, 2, D), 0.01)
    w_router = nrm((DEPTH, D, N_EXPERTS), D ** -0.5)
    b_router = nrm((DEPTH, N_EXPERTS), 0.01)
    w_gate = nrm((DEPTH, N_EXPERTS, D, EXPERT_HIDDEN), D ** -0.5)
    w_up = nrm((DEPTH, N_EXPERTS, D, EXPERT_HIDDEN), D ** -0.5)
    w_down = nrm((DEPTH, N_EXPERTS, EXPERT_HIDDEN, D), EXPERT_HIDDEN ** -0.5 * DN_BETA)
    ws_gate = nrm((DEPTH, D, SHARED_HIDDEN), D ** -0.5)
    ws_up = nrm((DEPTH, D, SHARED_HIDDEN), D ** -0.5)
    ws_down = nrm((DEPTH, SHARED_HIDDEN, D), SHARED_HIDDEN ** -0.5 * DN_BETA)
    return {"x": x, "even_w_in": even_w_in, "lru_conv_w": lru_conv_w, "lru_conv_b": lru_conv_b,
            "lru_wa": lru_wa, "lru_ba": lru_ba, "lru_wx": lru_wx, "lru_bx": lru_bx,
            "lru_lambda": lru_lambda, "hg_lower_bound": hg_lower_bound, "hg_norm_g": hg_norm_g,
            "even_w_out": even_w_out, "odd_w_in": odd_w_in, "nsa_pe_k": nsa_pe_k, "nsa_pe_v": nsa_pe_v,
            "nsa_wk1": nsa_wk1, "nsa_wk2": nsa_wk2, "nsa_wv1": nsa_wv1, "nsa_wv2": nsa_wv2,
            "odd_w_out": odd_w_out, "ln_g": ln_g, "ln_b": ln_b, "w_router": w_router,
            "b_router": b_router, "w_gate": w_gate, "w_up": w_up, "w_down": w_down,
            "ws_gate": ws_gate, "ws_up": ws_up, "ws_down": ws_down}


def reference(x, even_w_in, lru_conv_w, lru_conv_b, lru_wa, lru_ba, lru_wx, lru_bx, lru_lambda,
              hg_lower_bound, hg_norm_g, even_w_out, odd_w_in, nsa_pe_k, nsa_pe_v, nsa_wk1, nsa_wk2,
              nsa_wv1, nsa_wv2, odd_w_out, ln_g, ln_b, w_router, b_router, w_gate, w_up, w_down,
              ws_gate, ws_up, ws_down):
    lb_all = jnp.cumsum(jax.nn.softmax(hg_lower_bound.astype(jnp.float32), axis=0), axis=0)
    for layer in range(DEPTH):
        li = layer // 2
        if layer % 2 == 0:
            mix = even_mixer(x, even_w_in[li], lru_conv_w[li], lru_conv_b[li], lru_wa[li], lru_ba[li],
                             lru_wx[li], lru_bx[li], lru_lambda[li], lb_all[layer], hg_norm_g[li], even_w_out[li])
        else:
            mix = odd_mixer(x, odd_w_in[li], nsa_pe_k[li], nsa_pe_v[li], nsa_wk1[li], nsa_wk2[li],
                            nsa_wv1[li], nsa_wv2[li], odd_w_out[li])
        x = layer_norm(DN_ALPHA * x + mix, ln_g[layer, 0], ln_b[layer, 0])
        ff = moe(x, w_router[layer], b_router[layer], w_gate[layer], w_up[layer], w_down[layer],
                 ws_gate[layer], ws_up[layer], ws_down[layer])
        x = layer_norm(DN_ALPHA * x + ff, ln_g[layer, 1], ln_b[layer, 1])
    return x
```

```python
import functools

import numpy as np
import jax
import jax.numpy as jnp
from jax import lax
from jax.experimental import pallas as pl
from jax.experimental.pallas import tpu as pltpu

F32 = jnp.float32
BF16 = jnp.bfloat16

D_MODEL = 1024
DEPTH = 2
LRU_WIDTH = 512
LRU_BLOCKS = 8
LRU_BLOCK = LRU_WIDTH // LRU_BLOCKS
CONV_W = 4
LRU_C = 8.0
HG_HEADS = 4
HG_DK = 128
HG_FDIM = HG_HEADS * HG_DK
HG_VDIM = HG_HEADS * HG_DK
HEAD_DIM = 64
NSA_HEADS = 8
NSA_GROUPS = 2
NSA_HPG = NSA_HEADS // NSA_GROUPS
NSA_KV = NSA_GROUPS * HEAD_DIM
CMP_LEN = 32
CMP_STRIDE = 16
CMP_HIDDEN = 128
SLC_LEN = 64
SLC_TOPN = 16
WIN = 512
FORCE_BONUS = 1e4
MOBA_HEADS = 8
MOBA_BLOCK = 256
MOBA_TOPK = 3
N_EXPERTS = 256
TOP_K = 8
N_GROUPS = 8
TOPK_GROUPS = 4
EXPERT_HIDDEN = 256
ROUTED_SCALE = 2.5
DN_ALPHA = (2 * DEPTH) ** 0.25
LN_EPS = 1e-5
NEG = -1e30
NEG_INF = float("-inf")

LANES = 128
SUBLANES = 8
VMEM_LIMIT = 48 * 1024 * 1024

HG_SUB = 16
EXPERT_ROWS = 256


def _cparams(sem):
    return pltpu.CompilerParams(dimension_semantics=sem, vmem_limit_bytes=VMEM_LIMIT)


def _dot(a, b):
    return jnp.dot(a.astype(BF16), b.astype(BF16), preferred_element_type=F32)


def _dot_nt(a, b):
    return lax.dot_general(a.astype(BF16), b.astype(BF16), (((1,), (1,)), ((), ())),
                           preferred_element_type=F32)


def _dot_tn(a, b):
    return lax.dot_general(a.astype(BF16), b.astype(BF16), (((0,), (0,)), ((), ())),
                           preferred_element_type=F32)


def _split2(x):
    hi = x.astype(BF16)
    lo = (x - hi.astype(F32)).astype(BF16)
    return hi, lo


def _split3(x):
    hi = x.astype(BF16)
    r = x - hi.astype(F32)
    mid = r.astype(BF16)
    lo = (r - mid.astype(F32)).astype(BF16)
    return hi, mid, lo


def _layer_norm(v, g, b):
    mu = jnp.mean(v, axis=-1, keepdims=True)
    d = v - mu
    var = jnp.mean(d * d, axis=-1, keepdims=True)
    return d * lax.rsqrt(var + LN_EPS) * g + b


def _sigmoid(x):
    return 1.0 / (1.0 + jnp.exp(-x))


def _silu(x):
    return x * _sigmoid(x)


def _proj_kernel(x_ref, w_ref, o_ref):
    o_ref[...] = jnp.dot(x_ref[...].astype(BF16), w_ref[...], preferred_element_type=F32)


def _proj(x, w, tm=256):
    n, k = x.shape
    m = w.shape[1]
    return pl.pallas_call(
        _proj_kernel,
        grid=(n // tm,),
        in_specs=[pl.BlockSpec((tm, k), lambda i: (i, 0)),
                  pl.BlockSpec((k, m), lambda i: (0, 0))],
        out_specs=pl.BlockSpec((tm, m), lambda i: (i, 0)),
        out_shape=jax.ShapeDtypeStruct((n, m), F32),
        compiler_params=_cparams(("parallel",)),
        name="proj",
    )(x, w)


def _proj_ln_kernel(y_ref, w_ref, x_ref, g_ref, b_ref, o_ref):
    mix = jnp.dot(y_ref[...].astype(BF16), w_ref[...], preferred_element_type=F32)
    o_ref[...] = _layer_norm(DN_ALPHA * x_ref[...] + mix, g_ref[...], b_ref[...])


def _proj_ln(y, w, xres, g, b, tm=256):
    n, k = y.shape
    d = w.shape[1]
    return pl.pallas_call(
        _proj_ln_kernel,
        grid=(n // tm,),
        in_specs=[pl.BlockSpec((tm, k), lambda i: (i, 0)),
                  pl.BlockSpec((k, d), lambda i: (0, 0)),
                  pl.BlockSpec((tm, d), lambda i: (i, 0)),
                  pl.BlockSpec((1, d), lambda i: (0, 0)),
                  pl.BlockSpec((1, d), lambda i: (0, 0))],
        out_specs=pl.BlockSpec((tm, d), lambda i: (i, 0)),
        out_shape=jax.ShapeDtypeStruct((n, d), F32),
        compiler_params=_cparams(("parallel",)),
        name="proj_ln",
    )(y, w, xres, g.reshape(1, d), b.reshape(1, d))


def _lru_kernel(xb_ref, gb_ref, cw_ref, cb_ref, wa_ref, ba_ref, wx_ref, bx_ref, lam_ref,
                o_ref, xs_ref, a_ref, b_ref, h_ref):
    ti = pl.program_id(1)
    tc = xb_ref.shape[1]
    c = xb_ref.shape[2]

    @pl.when(ti == 0)
    def _():
        xs_ref[0:SUBLANES, :] = jnp.zeros((SUBLANES, c), F32)
        h_ref[...] = jnp.zeros_like(h_ref)

    xs_ref[SUBLANES:SUBLANES + tc, :] = xb_ref[0]
    xc = cb_ref[...]
    for w in range(CONV_W):
        xc = xc + cw_ref[w:w + 1, :] * xs_ref[pl.ds(SUBLANES - (CONV_W - 1) + w, tc), :]
    xs_ref[0:SUBLANES, :] = xs_ref[tc:tc + SUBLANES, :]

    r = _sigmoid(_dot(xc, wa_ref[...]) + ba_ref[...])
    i = _sigmoid(_dot(xc, wx_ref[...]) + bx_ref[...])
    nl = -lam_ref[...]
    softplus = jnp.maximum(nl, 0.0) + jnp.log(1.0 + jnp.exp(-jnp.abs(nl)))
    log_a = (-LRU_C * r) * softplus
    a = jnp.exp(log_a)
    mult = jnp.sqrt(1.0 - jnp.exp(2.0 * log_a))
    trow = ti * tc + lax.broadcasted_iota(jnp.int32, (tc, c), 0)
    mult = jnp.where(trow == 0, 1.0, mult)
    a_ref[...] = a
    b_ref[...] = mult * (i * xc)

    row = lax.broadcasted_iota(jnp.int32, (SUBLANES, c), 0)

    def body(g, carry):
        r0 = pl.multiple_of(g * SUBLANES, SUBLANES)
        av = a_ref[pl.ds(r0, SUBLANES), :]
        bv = b_ref[pl.ds(r0, SUBLANES), :]
        for d in (1, 2, 4):
            a_sh = pltpu.roll(av, d, 0)
            b_sh = pltpu.roll(bv, d, 0)
            m = row >= d
            bv = jnp.where(m, av * b_sh + bv, bv)
            av = jnp.where(m, av * a_sh, av)
        h = av * carry + bv
        b_ref[pl.ds(r0, SUBLANES), :] = h
        return h[SUBLANES - 1:SUBLANES, :]

    h_ref[...] = lax.fori_loop(0, tc // SUBLANES, body, h_ref[...])
    o_ref[0] = b_ref[...] * jax.nn.gelu(gb_ref[0])


def _lru(u3, conv_w, conv_b, wa_bd, ba, wx_bd, bx, lam, tc=256):
    b, t, _ = u3.shape
    c = LRU_WIDTH
    row = lambda v: v.reshape(1, c)
    full = lambda shp: pl.BlockSpec(shp, lambda bi, ti: (0, 0))
    return pl.pallas_call(
        _lru_kernel,
        grid=(b, t // tc),
        in_specs=[pl.BlockSpec((1, tc, c), lambda bi, ti: (bi, ti, 0)),
                  pl.BlockSpec((1, tc, c), lambda bi, ti: (bi, ti, 1)),
                  full((CONV_W, c)), full((1, c)), full((c, c)), full((1, c)),
                  full((c, c)), full((1, c)), full((1, c))],
        out_specs=pl.BlockSpec((1, tc, c), lambda bi, ti: (bi, ti, 0)),
        out_shape=jax.ShapeDtypeStruct((b, t, c), F32),
        scratch_shapes=[pltpu.VMEM((tc + SUBLANES, c), F32), pltpu.VMEM((tc, c), F32),
                        pltpu.VMEM((tc, c), F32), pltpu.VMEM((1, c), F32)],
        compiler_params=_cparams(("parallel", "arbitrary")),
        name="rg_lru",
    )(u3, u3, conv_w, row(conv_b), wa_bd, row(ba), wx_bd, row(bx), row(lam))


def _hgrn_kernel(q_ref, f_ref, v_ref, g_ref, lb_ref, ng_ref, tri_ref, ones_ref, o_ref,
                 st_ref, q_s, k_s, c_s, qe_s, kd_s, dl_s, o_s):
    ti = pl.program_id(2)
    tc = q_ref.shape[1]
    dk = q_ref.shape[2]

    @pl.when(ti == 0)
    def _():
        st_ref[...] = jnp.zeros_like(st_ref)

    lbv = lb_ref[...]
    q = _silu(q_ref[0])
    f = lbv + (1.0 - lbv) * _sigmoid(f_ref[0])
    kk = 1.0 - f
    logf = jnp.log(f)
    parts = _split3(logf)
    tri = tri_ref[...]
    ones = ones_ref[...]
    cum = sum(jnp.dot(tri, p, preferred_element_type=F32) for p in parts)
    last = sum(jnp.dot(ones, p, preferred_element_type=F32) for p in parts)
    q_s[...] = q
    k_s[...] = kk
    c_s[...] = cum
    qe_s[...] = q * jnp.exp(cum)
    kd_s[...] = kk * jnp.exp(last - cum)
    dl_s[...] = jnp.exp(last)

    s_idx = lax.broadcasted_iota(jnp.int32, (HG_SUB, dk), 0)

    def chunk(ci, _):
        r0 = pl.multiple_of(ci * HG_SUB, HG_SUB)
        rows = pl.ds(r0, HG_SUB)
        st = st_ref[...]
        vc = v_ref[0, rows, :]
        qc = q_s[rows, :]
        kc = k_s[rows, :]
        cc = c_s[rows, :]
        o = _dot_nt(qe_s[rows, :], st)
        for t in range(HG_SUB):
            d = cc[t:t + 1, :] - cc
            dec = jnp.exp(jnp.where(s_idx <= t, d, NEG))
            w = (qc[t:t + 1, :] * kc) * dec
            a_t = jnp.sum(w, axis=1, keepdims=True)
            o_t = jnp.sum(a_t * vc, axis=0, keepdims=True)
            o = o + jnp.where(s_idx == t, o_t, 0.0)
        o_s[rows, :] = o
        st_ref[...] = st * dl_s[pl.ds(r0, 1), :] + _dot_tn(vc, kd_s[rows, :])
        return 0

    lax.fori_loop(0, tc // HG_SUB, chunk, 0)
    o = o_s[...]
    o = o * lax.rsqrt(jnp.mean(o * o, axis=-1, keepdims=True) + 1e-6) * ng_ref[...]
    o_ref[0] = o * _sigmoid(g_ref[0])


def _hgrn2(u3, lb, norm_g, tc=256):
    b, t, _ = u3.shape
    dk = HG_DK
    base = (2 * LRU_WIDTH) // dk
    idx = np.arange(tc)
    same = (idx[:, None] // HG_SUB) == (idx[None, :] // HG_SUB)
    tri = jnp.asarray(same & (idx[None, :] <= idx[:, None]), BF16)
    ones = jnp.asarray(same, BF16)

    def col(off):
        return pl.BlockSpec((1, tc, dk), lambda bi, hi, ti: (bi, ti, base + off * HG_HEADS + hi))

    vec = pl.BlockSpec((1, dk), lambda bi, hi, ti: (0, hi))
    cst = pl.BlockSpec((tc, tc), lambda bi, hi, ti: (0, 0))
    return pl.pallas_call(
        _hgrn_kernel,
        grid=(b, HG_HEADS, t // tc),
        in_specs=[col(0), col(1), col(2), col(3), vec, vec, cst, cst],
        out_specs=pl.BlockSpec((1, tc, dk), lambda bi, hi, ti: (bi, ti, hi)),
        out_shape=jax.ShapeDtypeStruct((b, t, HG_VDIM), F32),
        scratch_shapes=[pltpu.VMEM((dk, dk), F32)] + [pltpu.VMEM((tc, dk), F32)] * 7,
        compiler_params=_cparams(("parallel", "parallel", "arbitrary")),
        name="hgrn2",
    )(u3, u3, u3, u3, lb.reshape(1, HG_FDIM), norm_g.reshape(1, HG_VDIM), tri, ones)


def _nsa_cmp_kv_kernel(zk_ref, zv_ref, pek_ref, pev_ref, wk1_ref, wk2_ref, wv1_ref, wv2_ref,
                       ok_ref, ov_ref):
    half = zk_ref.shape[3]
    nrow = zk_ref.shape[2]

    def compress(z, pe_ref, w1_ref, w2_ref):
        lo = _dot(z + pe_ref[0:1, :], w1_ref[0:half, :])
        hi = _dot(z + pe_ref[1:2, :], w1_ref[half:2 * half, :])
        pre = lo + pltpu.roll(hi, nrow - 1, 0)
        return _dot(jax.nn.gelu(pre), w2_ref[...])

    ok_ref[0, 0] = compress(zk_ref[0, 0], pek_ref, wk1_ref, wk2_ref)
    ov_ref[0, 0] = compress(zv_ref[0, 0], pev_ref, wv1_ref, wv2_ref)


def _nsa_compress(kc, vc, pe_k, pe_v, wk1, wk2, wv1, wv2):
    b, g, t, dh = kc.shape
    nrow = t // CMP_STRIDE
    half = CMP_STRIDE * dh
    zk = kc.reshape(b, g, nrow, half)
    zv = vc.reshape(b, g, nrow, half)
    zspec = pl.BlockSpec((1, 1, nrow, half), lambda bi, gi: (bi, gi, 0, 0))
    full = lambda shp: pl.BlockSpec(shp, lambda bi, gi: (0, 0))
    ospec = pl.BlockSpec((1, 1, nrow, dh), lambda bi, gi: (bi, gi, 0, 0))
    oshape = jax.ShapeDtypeStruct((b, g, nrow, dh), F32)
    return pl.pallas_call(
        _nsa_cmp_kv_kernel,
        grid=(b, g),
        in_specs=[zspec, zspec, full((2, half)), full((2, half)),
                  full((2 * half, CMP_HIDDEN)), full((CMP_HIDDEN, dh)),
                  full((2 * half, CMP_HIDDEN)), full((CMP_HIDDEN, dh))],
        out_specs=[ospec, ospec],
        out_shape=[oshape, oshape],
        compiler_params=_cparams(("parallel", "parallel")),
        name="nsa_compress",
    )(zk, zv, pe_k.reshape(2, half), pe_v.reshape(2, half),
      wk1.astype(BF16), wk2.astype(BF16), wv1.astype(BF16), wv2.astype(BF16))


def _nsa_sel_kernel(slope_ref, q_ref, kc_ref, vc_ref, ov_ref, o_ref, sel_ref):
    gi = pl.program_id(0) % NSA_GROUPS
    qi = pl.program_id(1)
    hpg = q_ref.shape[2]
    tq = q_ref.shape[3]
    dh = q_ref.shape[4]
    nc = kc_ref.shape[2]
    ns = sel_ref.shape[3]
    scale = dh ** -0.5

    q = q_ref[0, 0].reshape(hpg * tq, dh)
    khi, klo = _split2(kc_ref[0, 0])
    s = (_dot_nt(q, khi) + _dot_nt(q, klo)) * scale
    tpos = qi * tq + lax.broadcasted_iota(jnp.int32, (tq, nc), 0)
    cend = lax.broadcasted_iota(jnp.int32, (tq, nc), 1) * CMP_STRIDE + (CMP_LEN - 1)
    d_c = (tpos - cend).astype(F32)
    valid = d_c >= 0.0
    ps = []
    for i in range(hpg):
        si = s[i * tq:(i + 1) * tq, :] - slope_ref[gi * hpg + i] * d_c
        si = jnp.where(valid, si, NEG)
        m = jnp.max(si, axis=-1, keepdims=True)
        p = jnp.where(valid, jnp.exp(si - m), 0.0)
        ps.append(p / jnp.maximum(jnp.sum(p, axis=-1, keepdims=True), 1e-30))
    pall = jnp.concatenate(ps, axis=0)
    o_ref[0, 0] = _dot(pall, vc_ref[0, 0]).reshape(hpg, tq, dh)

    psum = ps[0]
    for i in range(1, hpg):
        psum = psum + ps[i]
    phi, plo = _split2(psum)
    ovl = ov_ref[...]
    imp = jnp.dot(phi, ovl, preferred_element_type=F32) + jnp.dot(plo, ovl, preferred_element_type=F32)

    j = lax.broadcasted_iota(jnp.int32, (tq, ns), 1)
    own = (qi * tq + lax.broadcasted_iota(jnp.int32, (tq, ns), 0)) // SLC_LEN
    forced = (j == 0) | (j == own) | (j == own - 1)
    imp = jnp.where(j <= own, imp + FORCE_BONUS * forced.astype(F32), NEG)
    rank = jnp.zeros((tq, ns), F32)
    for jp in range(ns):
        col = imp[:, jp:jp + 1]
        beats = (col > imp) | ((col == imp) & (j > jp))
        rank = rank + beats.astype(F32)
    sel_ref[0, 0] = (rank < float(min(SLC_TOPN, ns))).astype(F32)


def _nsa_select(slopes, qh, k_cmp, v_cmp, tq=256):
    b, g, hpg, t, dh = qh.shape
    nc = k_cmp.shape[2]
    ns = t // SLC_LEN
    ci = np.arange(nc)[:, None]
    cj = np.arange(ns)[None, :]
    ovl = ((CMP_STRIDE * ci < SLC_LEN * (cj + 1)) & (CMP_STRIDE * ci + CMP_LEN > SLC_LEN * cj))
    ovl = ovl & (ci < (t - CMP_LEN) // CMP_STRIDE + 1)
    grid_spec = pltpu.PrefetchScalarGridSpec(
        num_scalar_prefetch=1,
        grid=(b * g, t // tq),
        in_specs=[pl.BlockSpec((1, 1, hpg, tq, dh), lambda bg, qi, s: (bg // g, bg % g, 0, qi, 0)),
                  pl.BlockSpec((1, 1, nc, dh), lambda bg, qi, s: (bg // g, bg % g, 0, 0)),
                  pl.BlockSpec((1, 1, nc, dh), lambda bg, qi, s: (bg // g, bg % g, 0, 0)),
                  pl.BlockSpec((nc, ns), lambda bg, qi, s: (0, 0))],
        out_specs=[pl.BlockSpec((1, 1, hpg, tq, dh), lambda bg, qi, s: (bg // g, bg % g, 0, qi, 0)),
                   pl.BlockSpec((1, 1, tq, ns), lambda bg, qi, s: (bg // g, bg % g, qi, 0))],
    )
    return pl.pallas_call(
        _nsa_sel_kernel,
        grid_spec=grid_spec,
        out_shape=[jax.ShapeDtypeStruct((b, g, hpg, t, dh), F32),
                   jax.ShapeDtypeStruct((b, g, t, ns), F32)],
        compiler_params=_cparams(("parallel", "parallel")),
        name="nsa_select",
    )(slopes, qh, k_cmp, v_cmp, jnp.asarray(ovl, BF16))


def _flash_kernel(slope_ref, q_ref, k_ref, v_ref, *rest, mode, n_kvh, tk, blk):
    if mode == "slc":
        bs_ref, e_ref, o_ref, m_ref, l_ref, acc_ref, mask_ref = rest
    elif mode == "moba":
        e_ref, o_ref, m_ref, l_ref, acc_ref, mask_ref = rest
    else:
        o_ref, m_ref, l_ref, acc_ref = rest
    hk = pl.program_id(0) % n_kvh
    qi = pl.program_id(1)
    hpg = q_ref.shape[2]
    tq = q_ref.shape[3]
    dh = q_ref.shape[4]
    t_all = k_ref.shape[2]
    scale = dh ** -0.5
    q0 = qi * tq

    q = q_ref[0, 0].reshape(hpg * tq, dh)
    m_ref[...] = jnp.full(m_ref.shape, NEG, F32)
    l_ref[...] = jnp.zeros_like(l_ref)
    acc_ref[...] = jnp.zeros_like(acc_ref)

    if mode == "slc":
        mask_ref[...] = jnp.dot(bs_ref[0, 0].astype(BF16), e_ref[...], preferred_element_type=F32)
        lo = 0
        hi = (q0 + tq + tk - 1) // tk
    elif mode == "moba":
        nb = t_all // blk
        kf = k_ref[0, 0].astype(F32).reshape(nb, blk, dh)
        kmean = jnp.sum(kf, axis=1) / float(blk)
        kmean = jnp.concatenate([kmean, jnp.zeros((LANES - nb, dh), F32)], axis=0)
        mhi, mlo = _split2(kmean)
        gate = _dot_nt(q, mhi) + _dot_nt(q, mlo)
        j = lax.broadcasted_iota(jnp.int32, (tq, LANES), 1)
        ob = q0 // blk
        gate = jnp.where(j < ob, gate, NEG)
        rank = jnp.zeros((tq, LANES), F32)
        for jp in range(nb):
            col = gate[:, jp:jp + 1]
            beats = (col > gate) | ((col == gate) & (j > jp))
            rank = rank + beats.astype(F32)
        n_sel = float(min(MOBA_TOPK, nb - 1))
        bsel = ((rank < n_sel) & (j < ob)) | (j == ob)
        mask_ref[...] = jnp.dot(bsel.astype(BF16), e_ref[...], preferred_element_type=F32)
        lo = 0
        hi = ob + 1
    else:
        lo = jnp.maximum(q0 - (WIN - 1), 0) // tk
        hi = (q0 + tq + tk - 1) // tk

    def body(kb, _):
        ks = pl.multiple_of(kb * tk, tk)
        kblk = k_ref[0, 0, pl.ds(ks, tk), :]
        vblk = v_ref[0, 0, pl.ds(ks, tk), :]
        s = _dot_nt(q, kblk) * scale
        tpos = q0 + lax.broadcasted_iota(jnp.int32, (tq, tk), 0)
        spos = ks + lax.broadcasted_iota(jnp.int32, (tq, tk), 1)
        dist = (tpos - spos).astype(F32)
        if mode == "win":
            valid = (dist >= 0.0) & (dist < float(WIN))
        else:
            valid = (dist >= 0.0) & (mask_ref[:, pl.ds(ks, tk)] > 0.5)
        ps = []
        for i in range(hpg):
            rows = slice(i * tq, (i + 1) * tq)
            si = s[rows, :] - slope_ref[hk * hpg + i] * dist
            si = jnp.where(valid, si, NEG)
            m_old = m_ref[rows, :]
            m_new = jnp.maximum(m_old, jnp.max(si, axis=-1, keepdims=True))
            p = jnp.where(valid, jnp.exp(si - m_new), 0.0)
            alpha = jnp.exp(m_old - m_new)
            l_ref[rows, :] = alpha * l_ref[rows, :] + jnp.sum(p, axis=-1, keepdims=True)
            acc_ref[rows, :] = alpha * acc_ref[rows, :]
            m_ref[rows, :] = m_new
            ps.append(p)
        pall = ps[0] if hpg == 1 else jnp.concatenate(ps, axis=0)
        acc_ref[...] = acc_ref[...] + _dot(pall, vblk)
        return 0

    lax.fori_loop(lo, hi, body, 0)
    o = acc_ref[...] / jnp.maximum(l_ref[...], 1e-30)
    o_ref[0, 0] = o.reshape(hpg, tq, dh)


def _flash(mode, slopes, qh, kh, vh, bsel=None, tq=64, tk=256):
    b, n_kvh, hpg, t, dh = qh.shape
    blk = {"slc": SLC_LEN, "moba": MOBA_BLOCK, "win": 0}[mode]
    qspec = pl.BlockSpec((1, 1, hpg, tq, dh), lambda bh, qi, s: (bh // n_kvh, bh % n_kvh, 0, qi, 0))
    kvspec = pl.BlockSpec((1, 1, t, dh), lambda bh, qi, s: (bh // n_kvh, bh % n_kvh, 0, 0))
    in_specs = [qspec, kvspec, kvspec]
    args = [qh, kh, vh]
    scratch = [pltpu.VMEM((hpg * tq, 1), F32), pltpu.VMEM((hpg * tq, 1), F32),
               pltpu.VMEM((hpg * tq, dh), F32)]
    if mode == "slc":
        ns = t // blk
        in_specs.append(pl.BlockSpec((1, 1, tq, ns), lambda bh, qi, s: (bh // n_kvh, bh % n_kvh, qi, 0)))
        args.append(bsel)
        expand = (np.arange(t)[None, :] // blk) == np.arange(ns)[:, None]
    elif mode == "moba":
        assert tq == blk
        expand = (np.arange(t)[None, :] // blk) == np.arange(LANES)[:, None]
    if mode != "win":
        in_specs.append(pl.BlockSpec(expand.shape, lambda bh, qi, s: (0, 0)))
        args.append(jnp.asarray(expand, BF16))
        scratch.append(pltpu.VMEM((tq, t), F32))
    grid_spec = pltpu.PrefetchScalarGridSpec(
        num_scalar_prefetch=1,
        grid=(b * n_kvh, t // tq),
        in_specs=in_specs,
        out_specs=qspec,
        scratch_shapes=scratch,
    )
    return pl.pallas_call(
        functools.partial(_flash_kernel, mode=mode, n_kvh=n_kvh, tk=tk, blk=blk),
        grid_spec=grid_spec,
        out_shape=jax.ShapeDtypeStruct((b, n_kvh, hpg, t, dh), F32),
        compiler_params=_cparams(("parallel", "parallel")),
        name="flash_" + mode,
    )(slopes, *args)


def _router_kernel(x_ref, whi_ref, wlo_ref, b_ref, triu_ref, meta_ref, cnt_ref, carry_ref):
    step = pl.program_id(0)
    tm = x_ref.shape[0]
    ne = whi_ref.shape[0]
    gsz = ne // N_GROUPS

    @pl.when(step == 0)
    def _():
        carry_ref[...] = jnp.zeros_like(carry_ref)

    xhi, xlo = _split2(x_ref[...])
    whi = whi_ref[...]
    logits = _dot_nt(whi, xhi) + _dot_nt(wlo_ref[...], xhi) + _dot_nt(whi, xlo)
    scores = _sigmoid(logits)
    biased = scores + b_ref[:, 0:1]

    v3 = biased.reshape(N_GROUPS, gsz, tm)
    i3 = lax.broadcasted_iota(jnp.int32, (N_GROUPS, gsz, tm), 1).astype(F32)
    m1 = jnp.max(v3, axis=1, keepdims=True)
    idx1 = jnp.min(jnp.where(v3 == m1, i3, float(gsz)), axis=1, keepdims=True)
    m2 = jnp.max(jnp.where(i3 == idx1, NEG_INF, v3), axis=1, keepdims=True)
    gs = (m1 + m2).reshape(N_GROUPS, tm)
    gi = lax.broadcasted_iota(jnp.int32, (N_GROUPS, tm), 0).astype(F32)
    gmask = jnp.zeros((N_GROUPS, tm), F32)
    for _ in range(TOPK_GROUPS):
        m = jnp.max(gs, axis=0, keepdims=True)
        pick = jnp.min(jnp.where(gs == m, gi, float(N_GROUPS)), axis=0, keepdims=True)
        hit = gi == pick
        gmask = jnp.where(hit, 1.0, gmask)
        gs = jnp.where(hit, NEG_INF, gs)
    emask = jnp.broadcast_to(gmask.reshape(N_GROUPS, 1, tm), (N_GROUPS, gsz, tm)).reshape(ne, tm)
    cur = jnp.where(emask > 0.5, biased, NEG)

    ei = lax.broadcasted_iota(jnp.int32, (ne, tm), 0).astype(F32)
    selm = jnp.zeros((ne, tm), F32)
    idxs, svals = [], []
    for _ in range(TOP_K):
        m = jnp.max(cur, axis=0, keepdims=True)
        idx = jnp.min(jnp.where(cur == m, ei, float(ne)), axis=0, keepdims=True)
        hit = ei == idx
        svals.append(jnp.sum(jnp.where(hit, scores, 0.0), axis=0, keepdims=True))
        idxs.append(idx)
        cur = jnp.where(hit, NEG_INF, cur)
        selm = jnp.where(hit, 1.0, selm)

    carry = carry_ref[:, 0:1]
    rank_full = jnp.dot(selm.astype(BF16), triu_ref[...], preferred_element_type=F32) + carry
    ssum = svals[0]
    for k in range(1, TOP_K):
        ssum = ssum + svals[k]
    for k in range(TOP_K):
        meta_ref[k:k + 1, :] = idxs[k]
        meta_ref[TOP_K + k:TOP_K + k + 1, :] = svals[k] / ssum * ROUTED_SCALE
        meta_ref[2 * TOP_K + k:2 * TOP_K + k + 1, :] = jnp.sum(
            jnp.where(ei == idxs[k], rank_full, 0.0), axis=0, keepdims=True)
    new_carry = carry + jnp.sum(selm, axis=1, keepdims=True)
    carry_ref[...] = jnp.broadcast_to(new_carry, carry_ref.shape)
    cnt_ref[...] = jnp.broadcast_to(new_carry, cnt_ref.shape)


def _router(x, w_router, b_router, tm=256):
    n, d = x.shape
    ne = w_router.shape[1]
    wt = w_router.T
    whi = wt.astype(BF16)
    wlo = (wt - whi.astype(F32)).astype(BF16)
    triu = jnp.asarray(np.arange(tm)[:, None] < np.arange(tm)[None, :], BF16)
    meta, cnt = pl.pallas_call(
        _router_kernel,
        grid=(n // tm,),
        in_specs=[pl.BlockSpec((tm, d), lambda i: (i, 0)),
                  pl.BlockSpec((ne, d), lambda i: (0, 0)),
                  pl.BlockSpec((ne, d), lambda i: (0, 0)),
                  pl.BlockSpec((ne, LANES), lambda i: (0, 0)),
                  pl.BlockSpec((tm, tm), lambda i: (0, 0))],
        out_specs=[pl.BlockSpec((3 * TOP_K, tm), lambda i: (0, i)),
                   pl.BlockSpec((ne, LANES), lambda i: (0, 0))],
        out_shape=[jax.ShapeDtypeStruct((3 * TOP_K, n), F32),
                   jax.ShapeDtypeStruct((ne, LANES), F32)],
        scratch_shapes=[pltpu.VMEM((ne, LANES), F32)],
        compiler_params=_cparams(("arbitrary",)),
        name="moe_router",
    )(x, whi, wlo, jnp.broadcast_to(b_router.reshape(ne, 1), (ne, LANES)), triu)
    return meta, cnt[:, 0]


def _expert_kernel(be_ref, nu_ref, xs_ref, wg_ref, wu_ref, wd_ref, o_ref):
    bi = pl.program_id(0)

    @pl.when(bi < nu_ref[0])
    def _():
        x = xs_ref[...]
        h = _silu(_dot(x, wg_ref[0])) * _dot(x, wu_ref[0])
        o_ref[...] = _dot(h, wd_ref[0])

    @pl.when(bi >= nu_ref[0])
    def _():
        o_ref[...] = jnp.zeros_like(o_ref)


def _expert_ffn(blk_e, n_used, xs, w_gate, w_up, w_down):
    n_rows, d = xs.shape
    hdim = w_gate.shape[2]
    n_blk = n_rows // EXPERT_ROWS
    grid_spec = pltpu.PrefetchScalarGridSpec(
        num_scalar_prefetch=2,
        grid=(n_blk,),
        in_specs=[pl.BlockSpec((EXPERT_ROWS, d), lambda i, be, nu: (i, 0)),
                  pl.BlockSpec((1, d, hdim), lambda i, be, nu: (be[i], 0, 0)),
                  pl.BlockSpec((1, d, hdim), lambda i, be, nu: (be[i], 0, 0)),
                  pl.BlockSpec((1, hdim, d), lambda i, be, nu: (be[i], 0, 0))],
        out_specs=pl.BlockSpec((EXPERT_ROWS, d), lambda i, be, nu: (i, 0)),
    )
    return pl.pallas_call(
        _expert_kernel,
        grid_spec=grid_spec,
        out_shape=jax.ShapeDtypeStruct((n_rows, d), F32),
        compiler_params=_cparams(("arbitrary",)),
        name="moe_experts",
    )(blk_e, n_used, xs, w_gate, w_up, w_down)


def _shared_ln_kernel(x_ref, r_ref, wg_ref, wu_ref, wd_ref, g_ref, b_ref, o_ref):
    x = x_ref[...]
    xb = x.astype(BF16)
    h = _silu(jnp.dot(xb, wg_ref[...], preferred_element_type=F32)) * jnp.dot(
        xb, wu_ref[...], preferred_element_type=F32)
    shared = jnp.dot(h.astype(BF16), wd_ref[...], preferred_element_type=F32)
    o_ref[...] = _layer_norm(DN_ALPHA * x + (r_ref[...] + shared), g_ref[...], b_ref[...])


def _shared_ln(x, routed, ws_gate, ws_up, ws_down, g, b, tm=256):
    n, d = x.shape
    hdim = ws_gate.shape[1]
    row = pl.BlockSpec((tm, d), lambda i: (i, 0))
    return pl.pallas_call(
        _shared_ln_kernel,
        grid=(n // tm,),
        in_specs=[row, row,
                  pl.BlockSpec((d, hdim), lambda i: (0, 0)),
                  pl.BlockSpec((d, hdim), lambda i: (0, 0)),
                  pl.BlockSpec((hdim, d), lambda i: (0, 0)),
                  pl.BlockSpec((1, d), lambda i: (0, 0)),
                  pl.BlockSpec((1, d), lambda i: (0, 0))],
        out_specs=row,
        out_shape=jax.ShapeDtypeStruct((n, d), F32),
        compiler_params=_cparams(("parallel",)),
        name="shared_ln",
    )(x, routed, ws_gate.astype(BF16), ws_up.astype(BF16), ws_down.astype(BF16),
      g.reshape(1, d), b.reshape(1, d))


def _moe_ln(x, w_router, b_router, w_gate, w_up, w_down, ws_gate, ws_up, ws_down, g, b):
    n, d = x.shape
    ne = w_router.shape[1]
    meta, counts = _router(x, w_router, b_router)
    eidx = meta[0:TOP_K].astype(jnp.int32)
    wts = meta[TOP_K:2 * TOP_K]
    rank = meta[2 * TOP_K:3 * TOP_K].astype(jnp.int32)
    counts = counts.astype(jnp.int32)
    padded = (counts + EXPERT_ROWS - 1) // EXPERT_ROWS * EXPERT_ROWS
    pend = jnp.cumsum(padded)
    pstart = pend - padded
    dest = pstart[eidx] + rank
    n_blk = (n * TOP_K) // EXPERT_ROWS + ne
    n_rows = n_blk * EXPERT_ROWS
    tok = jnp.broadcast_to(jnp.arange(n, dtype=jnp.int32)[None, :], (TOP_K, n))
    row_tok = jnp.zeros((n_rows,), jnp.int32).at[dest.reshape(-1)].set(tok.reshape(-1))
    blk_e = jnp.minimum(jnp.searchsorted(pend, jnp.arange(n_blk, dtype=jnp.int32) * EXPERT_ROWS, side="right"),
                        ne - 1).astype(jnp.int32)
    n_used = (pend[-1] // EXPERT_ROWS).astype(jnp.int32).reshape(1)
    xs = x.astype(BF16)[row_tok]
    ys = _expert_ffn(blk_e, n_used, xs, w_gate, w_up, w_down)
    routed = jnp.einsum("knd,kn->nd", ys[dest], wts)
    return _shared_ln(x, routed, ws_gate, ws_up, ws_down, g, b)


def _block_diag(w):
    nb, bs, _ = w.shape
    eye = jnp.eye(nb, dtype=w.dtype)
    return (eye[:, None, :, None] * w[:, :, None, :]).reshape(nb * bs, nb * bs)


def _even_mixer_ln(x, w_in, conv_w, conv_b, wa, ba, wx, bx, lam, lb, norm_g, w_out, g, b):
    bsz, t, d = x.shape
    xf = x.reshape(bsz * t, d)
    u = _proj(xf, w_in.astype(BF16)).reshape(bsz, t, -1)
    ya = _lru(u, conv_w, conv_b, _block_diag(wa).astype(BF16), ba, _block_diag(wx).astype(BF16), bx, lam)
    yb = _hgrn2(u, lb, norm_g)
    y = jnp.concatenate([ya, yb], axis=-1).reshape(bsz * t, -1)
    return _proj_ln(y, w_out.astype(BF16), xf, g, b).reshape(bsz, t, d)


def _odd_mixer_ln(x, w_in, pe_k, pe_v, wk1, wk2, wv1, wv2, w_out, g, b):
    bsz, t, d = x.shape
    xf = x.reshape(bsz * t, d)
    nq = NSA_HEADS * HEAD_DIM
    nm = MOBA_HEADS * HEAD_DIM
    g0 = nq + 6 * NSA_KV
    ng = 3 * NSA_HEADS
    perm = np.concatenate([np.arange(0, g0), np.arange(g0 + ng, g0 + ng + 3 * nm), np.arange(g0, g0 + ng)])
    w_perm = jnp.pad(w_in[:, perm], ((0, 0), (0, LANES - ng))).astype(BF16)
    u = _proj(xf, w_perm).reshape(bsz, t, -1)

    def grp(off):
        z = u[:, :, off:off + NSA_KV].reshape(bsz, t, NSA_GROUPS, HEAD_DIM)
        return z.transpose(0, 2, 1, 3)

    qh = u[:, :, 0:nq].reshape(bsz, t, NSA_GROUPS, NSA_HPG, HEAD_DIM).transpose(0, 2, 3, 1, 4).astype(BF16)
    kc, vc, ks, vs, kw, vw = (grp(nq + i * NSA_KV) for i in range(6))
    m0 = g0

    def mh(off):
        z = u[:, :, off:off + nm].reshape(bsz, t, MOBA_HEADS, HEAD_DIM)
        return z.transpose(0, 2, 1, 3).astype(BF16)

    mq, mk, mv = mh(m0), mh(m0 + nm), mh(m0 + 2 * nm)
    gates = u[:, :, m0 + 3 * nm:m0 + 3 * nm + ng]

    n_all = NSA_HEADS + MOBA_HEADS
    s_all = 2.0 ** (-8.0 * np.arange(1, n_all + 1) / n_all)
    s_nsa = jnp.asarray(s_all[0::2], F32)
    s_moba = jnp.asarray(s_all[1::2], F32)

    k_cmp, v_cmp = _nsa_compress(kc, vc, pe_k, pe_v, wk1, wk2, wv1, wv2)
    o_c, bsel = _nsa_select(s_nsa, qh, k_cmp, v_cmp)
    o_s = _flash("slc", s_nsa, qh, ks.astype(BF16), vs.astype(BF16), bsel, tq=64)
    o_w = _flash("win", s_nsa, qh, kw.astype(BF16), vw.astype(BF16), tq=128)
    o_m = _flash("moba", s_moba, mq[:, :, None], mk, mv, tq=MOBA_BLOCK)

    gt = jax.nn.sigmoid(gates).reshape(bsz, t, NSA_GROUPS, NSA_HPG, 3).transpose(0, 2, 3, 1, 4)
    o = gt[..., 0:1] * o_c + gt[..., 1:2] * o_s + gt[..., 2:3] * o_w
    yc = o.transpose(0, 3, 1, 2, 4).reshape(bsz, t, nq)
    yd = o_m[:, :, 0].transpose(0, 2, 1, 3).reshape(bsz, t, nm)
    y = jnp.concatenate([yc, yd], axis=-1).reshape(bsz * t, -1)
    return _proj_ln(y, w_out.astype(BF16), xf, g, b).reshape(bsz, t, d)


def kernel(x, even_w_in, lru_conv_w, lru_conv_b, lru_wa, lru_ba, lru_wx, lru_bx, lru_lambda,
           hg_lower_bound, hg_norm_g, even_w_out, odd_w_in, nsa_pe_k, nsa_pe_v, nsa_wk1, nsa_wk2,
           nsa_wv1, nsa_wv2, odd_w_out, ln_g, ln_b, w_router, b_router, w_gate, w_up, w_down,
           ws_gate, ws_up, ws_down):
    bsz, t, d = x.shape
    lb_all = jnp.cumsum(jax.nn.softmax(hg_lower_bound.astype(F32), axis=0), axis=0)
    for layer in range(DEPTH):
        li = layer // 2
        if layer % 2 == 0:
            x = _even_mixer_ln(x, even_w_in[li], lru_conv_w[li], lru_conv_b[li], lru_wa[li], lru_ba[li],
                               lru_wx[li], lru_bx[li], lru_lambda[li], lb_all[layer], hg_norm_g[li],
                               even_w_out[li], ln_g[layer, 0], ln_b[layer, 0])
        else:
            x = _odd_mixer_ln(x, odd_w_in[li], nsa_pe_k[li], nsa_pe_v[li], nsa_wk1[li], nsa_wk2[li],
                              nsa_wv1[li], nsa_wv2[li], odd_w_out[li], ln_g[layer, 0], ln_b[layer, 0])
        x = _moe_ln(x.reshape(bsz * t, d), w_router[layer], b_router[layer], w_gate[layer], w_up[layer],
                    w_down[layer], ws_gate[layer], ws_up[layer], ws_down[layer],
                    ln_g[layer, 1], ln_b[layer, 1]).reshape(bsz, t, d)
    return x
```

```python
import functools

import numpy as np
import jax
import jax.numpy as jnp
from jax import lax
from jax.experimental import pallas as pl
from jax.experimental.pallas import tpu as pltpu

F32 = jnp.float32
BF16 = jnp.bfloat16

D_MODEL = 1024
DEPTH = 2
LRU_WIDTH = 512
LRU_BLOCKS = 8
LRU_BLOCK = LRU_WIDTH // LRU_BLOCKS
CONV_W = 4
LRU_C = 8.0
HG_HEADS = 4
HG_DK = 128
HG_FDIM = HG_HEADS * HG_DK
HG_VDIM = HG_HEADS * HG_DK
HEAD_DIM = 64
NSA_HEADS = 8
NSA_GROUPS = 2
NSA_HPG = NSA_HEADS // NSA_GROUPS
NSA_KV = NSA_GROUPS * HEAD_DIM
CMP_LEN = 32
CMP_STRIDE = 16
CMP_HIDDEN = 128
SLC_LEN = 64
SLC_TOPN = 16
WIN = 512
FORCE_BONUS = 1e4
MOBA_HEADS = 8
MOBA_BLOCK = 256
MOBA_TOPK = 3
N_EXPERTS = 256
TOP_K = 8
N_GROUPS = 8
TOPK_GROUPS = 4
EXPERT_HIDDEN = 256
ROUTED_SCALE = 2.5
DN_ALPHA = (2 * DEPTH) ** 0.25
LN_EPS = 1e-5
NEG = -1e30
NEG_INF = float("-inf")

LANES = 128
SUBLANES = 8
VMEM_LIMIT = 48 * 1024 * 1024

HG_SUB = 16
EXPERT_ROWS = 256


def _cparams(sem):
    return pltpu.CompilerParams(dimension_semantics=sem, vmem_limit_bytes=VMEM_LIMIT)


def _dot(a, b):
    return jnp.dot(a.astype(BF16), b.astype(BF16), preferred_element_type=F32)


def _dot_nt(a, b):
    return lax.dot_general(a.astype(BF16), b.astype(BF16), (((1,), (1,)), ((), ())),
                           preferred_element_type=F32)


def _dot_tn(a, b):
    return lax.dot_general(a.astype(BF16), b.astype(BF16), (((0,), (0,)), ((), ())),
                           preferred_element_type=F32)


def _split2(x):
    hi = x.astype(BF16)
    lo = (x - hi.astype(F32)).astype(BF16)
    return hi, lo


def _split3(x):
    hi = x.astype(BF16)
    r = x - hi.astype(F32)
    mid = r.astype(BF16)
    lo = (r - mid.astype(F32)).astype(BF16)
    return hi, mid, lo


def _layer_norm(v, g, b):
    mu = jnp.mean(v, axis=-1, keepdims=True)
    d = v - mu
    var = jnp.mean(d * d, axis=-1, keepdims=True)
    return d * lax.rsqrt(var + LN_EPS) * g + b


def _sigmoid(x):
    return 1.0 / (1.0 + jnp.exp(-x))


def _silu(x):
    return x * _sigmoid(x)


def _proj_kernel(x_ref, w_ref, o_ref):
    o_ref[...] = jnp.dot(x_ref[...].astype(BF16), w_ref[...], preferred_element_type=F32)


def _proj(x, w, tm=256):
    n, k = x.shape
    m = w.shape[1]
    return pl.pallas_call(
        _proj_kernel,
        grid=(n // tm,),
        in_specs=[pl.BlockSpec((tm, k), lambda i: (i, 0)),
                  pl.BlockSpec((k, m), lambda i: (0, 0))],
        out_specs=pl.BlockSpec((tm, m), lambda i: (i, 0)),
        out_shape=jax.ShapeDtypeStruct((n, m), F32),
        compiler_params=_cparams(("parallel",)),
        name="proj",
    )(x, w)


def _proj_ln_kernel(y_ref, w_ref, x_ref, g_ref, b_ref, o_ref):
    mix = jnp.dot(y_ref[...].astype(BF16), w_ref[...], preferred_element_type=F32)
    o_ref[...] = _layer_norm(DN_ALPHA * x_ref[...] + mix, g_ref[...], b_ref[...])


def _proj_ln(y, w, xres, g, b, tm=256):
    n, k = y.shape
    d = w.shape[1]
    return pl.pallas_call(
        _proj_ln_kernel,
        grid=(n // tm,),
        in_specs=[pl.BlockSpec((tm, k), lambda i: (i, 0)),
                  pl.BlockSpec((k, d), lambda i: (0, 0)),
                  pl.BlockSpec((tm, d), lambda i: (i, 0)),
                  pl.BlockSpec((1, d), lambda i: (0, 0)),
                  pl.BlockSpec((1, d), lambda i: (0, 0))],
        out_specs=pl.BlockSpec((tm, d), lambda i: (i, 0)),
        out_shape=jax.ShapeDtypeStruct((n, d), F32),
        compiler_params=_cparams(("parallel",)),
        name="proj_ln",
    )(y, w, xres, g.reshape(1, d), b.reshape(1, d))


def _lru_kernel(xb_ref, gb_ref, cw_ref, cb_ref, wa_ref, ba_ref, wx_ref, bx_ref, lam_ref,
                o_ref, xs_ref, a_ref, b_ref, h_ref):
    ti = pl.program_id(1)
    tc = xb_ref.shape[1]
    c = xb_ref.shape[2]

    @pl.when(ti == 0)
    def _():
        xs_ref[0:SUBLANES, :] = jnp.zeros((SUBLANES, c), F32)
        h_ref[...] = jnp.zeros_like(h_ref)

    xs_ref[SUBLANES:SUBLANES + tc, :] = xb_ref[0]
    xc = cb_ref[...]
    for w in range(CONV_W):
        xc = xc + cw_ref[w:w + 1, :] * xs_ref[pl.ds(SUBLANES - (CONV_W - 1) + w, tc), :]
    xs_ref[0:SUBLANES, :] = xs_ref[tc:tc + SUBLANES, :]

    r = _sigmoid(_dot(xc, wa_ref[...]) + ba_ref[...])
    i = _sigmoid(_dot(xc, wx_ref[...]) + bx_ref[...])
    nl = -lam_ref[...]
    softplus = jnp.maximum(nl, 0.0) + jnp.log(1.0 + jnp.exp(-jnp.abs(nl)))
    log_a = (-LRU_C * r) * softplus
    a = jnp.exp(log_a)
    mult = jnp.sqrt(1.0 - jnp.exp(2.0 * log_a))
    trow = ti * tc + lax.broadcasted_iota(jnp.int32, (tc, c), 0)
    mult = jnp.where(trow == 0, 1.0, mult)
    a_ref[...] = a
    b_ref[...] = mult * (i * xc)

    row = lax.broadcasted_iota(jnp.int32, (SUBLANES, c), 0)

    def body(g, carry):
        r0 = pl.multiple_of(g * SUBLANES, SUBLANES)
        av = a_ref[pl.ds(r0, SUBLANES), :]
        bv = b_ref[pl.ds(r0, SUBLANES), :]
        for d in (1, 2, 4):
            a_sh = pltpu.roll(av, d, 0)
            b_sh = pltpu.roll(bv, d, 0)
            m = row >= d
            bv = jnp.where(m, av * b_sh + bv, bv)
            av = jnp.where(m, av * a_sh, av)
        h = av * carry + bv
        b_ref[pl.ds(r0, SUBLANES), :] = h
        return h[SUBLANES - 1:SUBLANES, :]

    h_ref[...] = lax.fori_loop(0, tc // SUBLANES, body, h_ref[...])
    o_ref[0] = b_ref[...] * jax.nn.gelu(gb_ref[0])


def _lru(u3, conv_w, conv_b, wa_bd, ba, wx_bd, bx, lam, tc=256):
    b, t, _ = u3.shape
    c = LRU_WIDTH
    row = lambda v: v.reshape(1, c)
    full = lambda shp: pl.BlockSpec(shp, lambda bi, ti: (0, 0))
    return pl.pallas_call(
        _lru_kernel,
        grid=(b, t // tc),
        in_specs=[pl.BlockSpec((1, tc, c), lambda bi, ti: (bi, ti, 0)),
                  pl.BlockSpec((1, tc, c), lambda bi, ti: (bi, ti, 1)),
                  full((CONV_W, c)), full((1, c)), full((c, c)), full((1, c)),
                  full((c, c)), full((1, c)), full((1, c))],
        out_specs=pl.BlockSpec((1, tc, c), lambda bi, ti: (bi, ti, 0)),
        out_shape=jax.ShapeDtypeStruct((b, t, c), F32),
        scratch_shapes=[pltpu.VMEM((tc + SUBLANES, c), F32), pltpu.VMEM((tc, c), F32),
                        pltpu.VMEM((tc, c), F32), pltpu.VMEM((1, c), F32)],
        compiler_params=_cparams(("parallel", "arbitrary")),
        name="rg_lru",
    )(u3, u3, conv_w, row(conv_b), wa_bd, row(ba), wx_bd, row(bx), row(lam))


def _hgrn_kernel(q_ref, f_ref, v_ref, g_ref, lb_ref, ng_ref, tri_ref, ones_ref, o_ref,
                 st_ref, q_s, k_s, c_s, qe_s, kd_s, dl_s, o_s):
    ti = pl.program_id(2)
    tc = q_ref.shape[1]
    dk = q_ref.shape[2]

    @pl.when(ti == 0)
    def _():
        st_ref[...] = jnp.zeros_like(st_ref)

    lbv = lb_ref[...]
    q = _silu(q_ref[0])
    f = lbv + (1.0 - lbv) * _sigmoid(f_ref[0])
    kk = 1.0 - f
    logf = jnp.log(f)
    parts = _split3(logf)
    tri = tri_ref[...]
    ones = ones_ref[...]
    cum = sum(jnp.dot(tri, p, preferred_element_type=F32) for p in parts)
    last = sum(jnp.dot(ones, p, preferred_element_type=F32) for p in parts)
    q_s[...] = q
    k_s[...] = kk
    c_s[...] = cum
    qe_s[...] = q * jnp.exp(cum)
    kd_s[...] = kk * jnp.exp(last - cum)
    dl_s[...] = jnp.exp(last)

    s_idx = lax.broadcasted_iota(jnp.int32, (HG_SUB, dk), 0)

    def chunk(ci, _):
        r0 = pl.multiple_of(ci * HG_SUB, HG_SUB)
        rows = pl.ds(r0, HG_SUB)
        st = st_ref[...]
        vc = v_ref[0, rows, :]
        qc = q_s[rows, :]
        kc = k_s[rows, :]
        cc = c_s[rows, :]
        o = _dot_nt(qe_s[rows, :], st)
        for t in range(HG_SUB):
            d = cc[t:t + 1, :] - cc
            dec = jnp.exp(jnp.where(s_idx <= t, d, NEG))
            w = (qc[t:t + 1, :] * kc) * dec
            a_t = jnp.sum(w, axis=1, keepdims=True)
            o_t = jnp.sum(a_t * vc, axis=0, keepdims=True)
            o = o + jnp.where(s_idx == t, o_t, 0.0)
        o_s[rows, :] = o
        st_ref[...] = st * dl_s[pl.ds(r0, 1), :] + _dot_tn(vc, kd_s[rows, :])
        return 0

    lax.fori_loop(0, tc // HG_SUB, chunk, 0)
    o = o_s[...]
    o = o * lax.rsqrt(jnp.mean(o * o, axis=-1, keepdims=True) + 1e-6) * ng_ref[...]
    o_ref[0] = o * _sigmoid(g_ref[0])


def _hgrn2(u3, lb, norm_g, tc=256):
    b, t, _ = u3.shape
    dk = HG_DK
    base = (2 * LRU_WIDTH) // dk
    idx = np.arange(tc)
    same = (idx[:, None] // HG_SUB) == (idx[None, :] // HG_SUB)
    tri = jnp.asarray(same & (idx[None, :] <= idx[:, None]), BF16)
    ones = jnp.asarray(same, BF16)

    def col(off):
        return pl.BlockSpec((1, tc, dk), lambda bi, hi, ti: (bi, ti, base + off * HG_HEADS + hi))

    vec = pl.BlockSpec((1, dk), lambda bi, hi, ti: (0, hi))
    cst = pl.BlockSpec((tc, tc), lambda bi, hi, ti: (0, 0))
    return pl.pallas_call(
        _hgrn_kernel,
        grid=(b, HG_HEADS, t // tc),
        in_specs=[col(0), col(1), col(2), col(3), vec, vec, cst, cst],
        out_specs=pl.BlockSpec((1, tc, dk), lambda bi, hi, ti: (bi, ti, hi)),
        out_shape=jax.ShapeDtypeStruct((b, t, HG_VDIM), F32),
        scratch_shapes=[pltpu.VMEM((dk, dk), F32)] + [pltpu.VMEM((tc, dk), F32)] * 7,
        compiler_params=_cparams(("parallel", "parallel", "arbitrary")),
        name="hgrn2",
    )(u3, u3, u3, u3, lb.reshape(1, HG_FDIM), norm_g.reshape(1, HG_VDIM), tri, ones)


def _nsa_cmp_kv_kernel(zk_ref, zv_ref, pek_ref, pev_ref, wk1_ref, wk2_ref, wv1_ref, wv2_ref,
                       ok_ref, ov_ref):
    half = zk_ref.shape[3]
    nrow = zk_ref.shape[2]

    def compress(z, pe_ref, w1_ref, w2_ref):
        lo = _dot(z + pe_ref[0:1, :], w1_ref[0:half, :])
        hi = _dot(z + pe_ref[1:2, :], w1_ref[half:2 * half, :])
        pre = lo + pltpu.roll(hi, nrow - 1, 0)
        return _dot(jax.nn.gelu(pre), w2_ref[...])

    ok_ref[0, 0] = compress(zk_ref[0, 0], pek_ref, wk1_ref, wk2_ref)
    ov_ref[0, 0] = compress(zv_ref[0, 0], pev_ref, wv1_ref, wv2_ref)


def _nsa_compress(kc, vc, pe_k, pe_v, wk1, wk2, wv1, wv2):
    b, g, t, dh = kc.shape
    nrow = t // CMP_STRIDE
    half = CMP_STRIDE * dh
    zk = kc.reshape(b, g, nrow, half)
    zv = vc.reshape(b, g, nrow, half)
    zspec = pl.BlockSpec((1, 1, nrow, half), lambda bi, gi: (bi, gi, 0, 0))
    full = lambda shp: pl.BlockSpec(shp, lambda bi, gi: (0, 0))
    ospec = pl.BlockSpec((1, 1, nrow, dh), lambda bi, gi: (bi, gi, 0, 0))
    oshape = jax.ShapeDtypeStruct((b, g, nrow, dh), F32)
    return pl.pallas_call(
        _nsa_cmp_kv_kernel,
        grid=(b, g),
        in_specs=[zspec, zspec, full((2, half)), full((2, half)),
                  full((2 * half, CMP_HIDDEN)), full((CMP_HIDDEN, dh)),
                  full((2 * half, CMP_HIDDEN)), full((CMP_HIDDEN, dh))],
        out_specs=[ospec, ospec],
        out_shape=[oshape, oshape],
        compiler_params=_cparams(("parallel", "parallel")),
        name="nsa_compress",
    )(zk, zv, pe_k.reshape(2, half), pe_v.reshape(2, half),
      wk1.astype(BF16), wk2.astype(BF16), wv1.astype(BF16), wv2.astype(BF16))


def _nsa_sel_kernel(slope_ref, q_ref, kc_ref, vc_ref, ov_ref, o_ref, sel_ref):
    gi = pl.program_id(0) % NSA_GROUPS
    qi = pl.program_id(1)
    hpg = q_ref.shape[2]
    tq = q_ref.shape[3]
    dh = q_ref.shape[4]
    nc = kc_ref.shape[2]
    ns = sel_ref.shape[3]
    scale = dh ** -0.5

    q = q_ref[0, 0].reshape(hpg * tq, dh)
    khi, klo = _split2(kc_ref[0, 0])
    s = (_dot_nt(q, khi) + _dot_nt(q, klo)) * scale
    tpos = qi * tq + lax.broadcasted_iota(jnp.int32, (tq, nc), 0)
    cend = lax.broadcasted_iota(jnp.int32, (tq, nc), 1) * CMP_STRIDE + (CMP_LEN - 1)
    d_c = (tpos - cend).astype(F32)
    valid = d_c >= 0.0
    ps = []
    for i in range(hpg):
        si = s[i * tq:(i + 1) * tq, :] - slope_ref[gi * hpg + i] * d_c
        si = jnp.where(valid, si, NEG)
        m = jnp.max(si, axis=-1, keepdims=True)
        p = jnp.where(valid, jnp.exp(si - m), 0.0)
        ps.append(p / jnp.maximum(jnp.sum(p, axis=-1, keepdims=True), 1e-30))
    pall = jnp.concatenate(ps, axis=0)
    o_ref[0, 0] = _dot(pall, vc_ref[0, 0]).reshape(hpg, tq, dh)

    psum = ps[0]
    for i in range(1, hpg):
        psum = psum + ps[i]
    phi, plo = _split2(psum)
    ovl = ov_ref[...]
    imp = jnp.dot(phi, ovl, preferred_element_type=F32) + jnp.dot(plo, ovl, preferred_element_type=F32)

    j = lax.broadcasted_iota(jnp.int32, (tq, ns), 1)
    own = (qi * tq + lax.broadcasted_iota(jnp.int32, (tq, ns), 0)) // SLC_LEN
    forced = (j == 0) | (j == own) | (j == own - 1)
    imp = jnp.where(j <= own, imp + FORCE_BONUS * forced.astype(F32), NEG)
    rank = jnp.zeros((tq, ns), F32)
    for jp in range(ns):
        col = imp[:, jp:jp + 1]
        beats = (col > imp) | ((col == imp) & (j > jp))
        rank = rank + beats.astype(F32)
    sel_ref[0, 0] = (rank < float(min(SLC_TOPN, ns))).astype(F32)


def _nsa_select(slopes, qh, k_cmp, v_cmp, tq=256):
    b, g, hpg, t, dh = qh.shape
    nc = k_cmp.shape[2]
    ns = t // SLC_LEN
    ci = np.arange(nc)[:, None]
    cj = np.arange(ns)[None, :]
    ovl = ((CMP_STRIDE * ci < SLC_LEN * (cj + 1)) & (CMP_STRIDE * ci + CMP_LEN > SLC_LEN * cj))
    ovl = ovl & (ci < (t - CMP_LEN) // CMP_STRIDE + 1)
    grid_spec = pltpu.PrefetchScalarGridSpec(
        num_scalar_prefetch=1,
        grid=(b * g, t // tq),
        in_specs=[pl.BlockSpec((1, 1, hpg, tq, dh), lambda bg, qi, s: (bg // g, bg % g, 0, qi, 0)),
                  pl.BlockSpec((1, 1, nc, dh), lambda bg, qi, s: (bg // g, bg % g, 0, 0)),
                  pl.BlockSpec((1, 1, nc, dh), lambda bg, qi, s: (bg // g, bg % g, 0, 0)),
                  pl.BlockSpec((nc, ns), lambda bg, qi, s: (0, 0))],
        out_specs=[pl.BlockSpec((1, 1, hpg, tq, dh), lambda bg, qi, s: (bg // g, bg % g, 0, qi, 0)),
                   pl.BlockSpec((1, 1, tq, ns), lambda bg, qi, s: (bg // g, bg % g, qi, 0))],
    )
    return pl.pallas_call(
        _nsa_sel_kernel,
        grid_spec=grid_spec,
        out_shape=[jax.ShapeDtypeStruct((b, g, hpg, t, dh), F32),
                   jax.ShapeDtypeStruct((b, g, t, ns), F32)],
        compiler_params=_cparams(("parallel", "parallel")),
        name="nsa_select",
    )(slopes, qh, k_cmp, v_cmp, jnp.asarray(ovl, BF16))


def _flash_kernel(slope_ref, q_ref, k_ref, v_ref, *rest, mode, n_kvh, tk, blk):
    if mode == "slc":
        bs_ref, e_ref, o_ref, m_ref, l_ref, acc_ref, mask_ref = rest
    elif mode == "moba":
        e_ref, o_ref, m_ref, l_ref, acc_ref, mask_ref = rest
    else:
        o_ref, m_ref, l_ref, acc_ref = rest
    hk = pl.program_id(0) % n_kvh
    qi = pl.program_id(1)
    hpg = q_ref.shape[2]
    tq = q_ref.shape[3]
    dh = q_ref.shape[4]
    t_all = k_ref.shape[2]
    scale = dh ** -0.5
    q0 = qi * tq

    q = q_ref[0, 0].reshape(hpg * tq, dh)
    m_ref[...] = jnp.full(m_ref.shape, NEG, F32)
    l_ref[...] = jnp.zeros_like(l_ref)
    acc_ref[...] = jnp.zeros_like(acc_ref)

    if mode == "slc":
        mask_ref[...] = jnp.dot(bs_ref[0, 0].astype(BF16), e_ref[...], preferred_element_type=F32)
        lo = 0
        hi = (q0 + tq + tk - 1) // tk
    elif mode == "moba":
        nb = t_all // blk
        kf = k_ref[0, 0].astype(F32).reshape(nb, blk, dh)
        kmean = jnp.sum(kf, axis=1) / float(blk)
        kmean = jnp.concatenate([kmean, jnp.zeros((LANES - nb, dh), F32)], axis=0)
        mhi, mlo = _split2(kmean)
        gate = _dot_nt(q, mhi) + _dot_nt(q, mlo)
        j = lax.broadcasted_iota(jnp.int32, (tq, LANES), 1)
        ob = q0 // blk
        gate = jnp.where(j < ob, gate, NEG)
        rank = jnp.zeros((tq, LANES), F32)
        for jp in range(nb):
            col = gate[:, jp:jp + 1]
            beats = (col > gate) | ((col == gate) & (j > jp))
            rank = rank + beats.astype(F32)
        n_sel = float(min(MOBA_TOPK, nb - 1))
        bsel = ((rank < n_sel) & (j < ob)) | (j == ob)
        mask_ref[...] = jnp.dot(bsel.astype(BF16), e_ref[...], preferred_element_type=F32)
        lo = 0
        hi = ob + 1
    else:
        lo = jnp.maximum(q0 - (WIN - 1), 0) // tk
        hi = (q0 + tq + tk - 1) // tk

    def body(kb, _):
        ks = pl.multiple_of(kb * tk, tk)
        kblk = k_ref[0, 0, pl.ds(ks, tk), :]
        vblk = v_ref[0, 0, pl.ds(ks, tk), :]
        s = _dot_nt(q, kblk) * scale
        tpos = q0 + lax.broadcasted_iota(jnp.int32, (tq, tk), 0)
        spos = ks + lax.broadcasted_iota(jnp.int32, (tq, tk), 1)
        dist = (tpos - spos).astype(F32)
        if mode == "win":
            valid = (dist >= 0.0) & (dist < float(WIN))
        else:
            valid = (dist >= 0.0) & (mask_ref[:, pl.ds(ks, tk)] > 0.5)
        ps = []
        for i in range(hpg):
            rows = slice(i * tq, (i + 1) * tq)
            si = s[rows, :] - slope_ref[hk * hpg + i] * dist
            si = jnp.where(valid, si, NEG)
            m_old = m_ref[rows, :]
            m_new = jnp.maximum(m_old, jnp.max(si, axis=-1, keepdims=True))
            p = jnp.where(valid, jnp.exp(si - m_new), 0.0)
            alpha = jnp.exp(m_old - m_new)
            l_ref[rows, :] = alpha * l_ref[rows, :] + jnp.sum(p, axis=-1, keepdims=True)
            acc_ref[rows, :] = alpha * acc_ref[rows, :]
            m_ref[rows, :] = m_new
            ps.append(p)
        pall = ps[0] if hpg == 1 else jnp.concatenate(ps, axis=0)
        acc_ref[...] = acc_ref[...] + _dot(pall, vblk)
        return 0

    lax.fori_loop(lo, hi, body, 0)
    o = acc_ref[...] / jnp.maximum(l_ref[...], 1e-30)
    o_ref[0, 0] = o.reshape(hpg, tq, dh)


def _flash(mode, slopes, qh, kh, vh, bsel=None, tq=64, tk=256):
    b, n_kvh, hpg, t, dh = qh.shape
    blk = {"slc": SLC_LEN, "moba": MOBA_BLOCK, "win": 0}[mode]
    qspec = pl.BlockSpec((1, 1, hpg, tq, dh), lambda bh, qi, s: (bh // n_kvh, bh % n_kvh, 0, qi, 0))
    kvspec = pl.BlockSpec((1, 1, t, dh), lambda bh, qi, s: (bh // n_kvh, bh % n_kvh, 0, 0))
    in_specs = [qspec, kvspec, kvspec]
    args = [qh, kh, vh]
    scratch = [pltpu.VMEM((hpg * tq, 1), F32), pltpu.VMEM((hpg * tq, 1), F32),
               pltpu.VMEM((hpg * tq, dh), F32)]
    if mode == "slc":
        ns = t // blk
        in_specs.append(pl.BlockSpec((1, 1, tq, ns), lambda bh, qi, s: (bh // n_kvh, bh % n_kvh, qi, 0)))
        args.append(bsel)
        expand = (np.arange(t)[None, :] // blk) == np.arange(ns)[:, None]
    elif mode == "moba":
        assert tq == blk
        expand = (np.arange(t)[None, :] // blk) == np.arange(LANES)[:, None]
    if mode != "win":
        in_specs.append(pl.BlockSpec(expand.shape, lambda bh, qi, s: (0, 0)))
        args.append(jnp.asarray(expand, BF16))
        scratch.append(pltpu.VMEM((tq, t), F32))
    grid_spec = pltpu.PrefetchScalarGridSpec(
        num_scalar_prefetch=1,
        grid=(b * n_kvh, t // tq),
        in_specs=in_specs,
        out_specs=qspec,
        scratch_shapes=scratch,
    )
    return pl.pallas_call(
        functools.partial(_flash_kernel, mode=mode, n_kvh=n_kvh, tk=tk, blk=blk),
        grid_spec=grid_spec,
        out_shape=jax.ShapeDtypeStruct((b, n_kvh, hpg, t, dh), F32),
        compiler_params=_cparams(("parallel", "parallel")),
        name="flash_" + mode,
    )(slopes, *args)


def _router_kernel(x_ref, whi_ref, wlo_ref, b_ref, triu_ref, meta_ref, cnt_ref, carry_ref):
    step = pl.program_id(0)
    tm = x_ref.shape[0]
    ne = whi_ref.shape[0]
    gsz = ne // N_GROUPS

    @pl.when(step == 0)
    def _():
        carry_ref[...] = jnp.zeros_like(carry_ref)

    xhi, xlo = _split2(x_ref[...])
    whi = whi_ref[...]
    logits = _dot_nt(whi, xhi) + _dot_nt(wlo_ref[...], xhi) + _dot_nt(whi, xlo)
    scores = _sigmoid(logits)
    biased = scores + b_ref[:, 0:1]

    v3 = biased.reshape(N_GROUPS, gsz, tm)
    i3 = lax.broadcasted_iota(jnp.int32, (N_GROUPS, gsz, tm), 1).astype(F32)
    m1 = jnp.max(v3, axis=1, keepdims=True)
    idx1 = jnp.min(jnp.where(v3 == m1, i3, float(gsz)), axis=1, keepdims=True)
    m2 = jnp.max(jnp.where(i3 == idx1, NEG_INF, v3), axis=1, keepdims=True)
    gs = (m1 + m2).reshape(N_GROUPS, tm)
    gi = lax.broadcasted_iota(jnp.int32, (N_GROUPS, tm), 0).astype(F32)
    gmask = jnp.zeros((N_GROUPS, tm), F32)
    for _ in range(TOPK_GROUPS):
        m = jnp.max(gs, axis=0, keepdims=True)
        pick = jnp.min(jnp.where(gs == m, gi, float(N_GROUPS)), axis=0, keepdims=True)
        hit = gi == pick
        gmask = jnp.where(hit, 1.0, gmask)
        gs = jnp.where(hit, NEG_INF, gs)
    emask = jnp.broadcast_to(gmask.reshape(N_GROUPS, 1, tm), (N_GROUPS, gsz, tm)).reshape(ne, tm)
    cur = jnp.where(emask > 0.5, biased, NEG)

    ei = lax.broadcasted_iota(jnp.int32, (ne, tm), 0).astype(F32)
    selm = jnp.zeros((ne, tm), F32)
    idxs, svals = [], []
    for _ in range(TOP_K):
        m = jnp.max(cur, axis=0, keepdims=True)
        idx = jnp.min(jnp.where(cur == m, ei, float(ne)), axis=0, keepdims=True)
        hit = ei == idx
        svals.append(jnp.sum(jnp.where(hit, scores, 0.0), axis=0, keepdims=True))
        idxs.append(idx)
        cur = jnp.where(hit, NEG_INF, cur)
        selm = jnp.where(hit, 1.0, selm)

    carry = carry_ref[:, 0:1]
    rank_full = jnp.dot(selm.astype(BF16), triu_ref[...], preferred_element_type=F32) + carry
    ssum = svals[0]
    for k in range(1, TOP_K):
        ssum = ssum + svals[k]
    for k in range(TOP_K):
        meta_ref[k:k + 1, :] = idxs[k]
        meta_ref[TOP_K + k:TOP_K + k + 1, :] = svals[k] / ssum * ROUTED_SCALE
        meta_ref[2 * TOP_K + k:2 * TOP_K + k + 1, :] = jnp.sum(
            jnp.where(ei == idxs[k], rank_full, 0.0), axis=0, keepdims=True)
    new_carry = carry + jnp.sum(selm, axis=1, keepdims=True)
    carry_ref[...] = jnp.broadcast_to(new_carry, carry_ref.shape)
    cnt_ref[...] = jnp.broadcast_to(new_carry, cnt_ref.shape)


def _router(x, w_router, b_router, tm=256):
    n, d = x.shape
    ne = w_router.shape[1]
    wt = w_router.T
    whi = wt.astype(BF16)
    wlo = (wt - whi.astype(F32)).astype(BF16)
    triu = jnp.asarray(np.arange(tm)[:, None] < np.arange(tm)[None, :], BF16)
    meta, cnt = pl.pallas_call(
        _router_kernel,
        grid=(n // tm,),
        in_specs=[pl.BlockSpec((tm, d), lambda i: (i, 0)),
                  pl.BlockSpec((ne, d), lambda i: (0, 0)),
                  pl.BlockSpec((ne, d), lambda i: (0, 0)),
                  pl.BlockSpec((ne, LANES), lambda i: (0, 0)),
                  pl.BlockSpec((tm, tm), lambda i: (0, 0))],
        out_specs=[pl.BlockSpec((3 * TOP_K, tm), lambda i: (0, i)),
                   pl.BlockSpec((ne, LANES), lambda i: (0, 0))],
        out_shape=[jax.ShapeDtypeStruct((3 * TOP_K, n), F32),
                   jax.ShapeDtypeStruct((ne, LANES), F32)],
        scratch_shapes=[pltpu.VMEM((ne, LANES), F32)],
        compiler_params=_cparams(("arbitrary",)),
        name="moe_router",
    )(x, whi, wlo, jnp.broadcast_to(b_router.reshape(ne, 1), (ne, LANES)), triu)
    return meta, cnt[:, 0]


def _expert_kernel(be_ref, nu_ref, xs_ref, wg_ref, wu_ref, wd_ref, o_ref):
    bi = pl.program_id(0)

    @pl.when(bi < nu_ref[0])
    def _():
        x = xs_ref[...]
        h = _silu(_dot(x, wg_ref[0])) * _dot(x, wu_ref[0])
        o_ref[...] = _dot(h, wd_ref[0])

    @pl.when(bi >= nu_ref[0])
    def _():
        o_ref[...] = jnp.zeros_like(o_ref)


def _expert_ffn(blk_e, n_used, xs, w_gate, w_up, w_down):
    n_rows, d = xs.shape
    hdim = w_gate.shape[2]
    n_blk = n_rows // EXPERT_ROWS
    grid_spec = pltpu.PrefetchScalarGridSpec(
        num_scalar_prefetch=2,
        grid=(n_blk,),
        in_specs=[pl.BlockSpec((EXPERT_ROWS, d), lambda i, be, nu: (i, 0)),
                  pl.BlockSpec((1, d, hdim), lambda i, be, nu: (be[i], 0, 0)),
                  pl.BlockSpec((1, d, hdim), lambda i, be, nu: (be[i], 0, 0)),
                  pl.BlockSpec((1, hdim, d), lambda i, be, nu: (be[i], 0, 0))],
        out_specs=pl.BlockSpec((EXPERT_ROWS, d), lambda i, be, nu: (i, 0)),
    )
    return pl.pallas_call(
        _expert_kernel,
        grid_spec=grid_spec,
        out_shape=jax.ShapeDtypeStruct((n_rows, d), F32),
        compiler_params=_cparams(("arbitrary",)),
        name="moe_experts",
    )(blk_e, n_used, xs, w_gate, w_up, w_down)


def _dest_kernel(meta_ref, pstart_ref, dest_ref):
    ne = pstart_ref.shape[0]
    tm = meta_ref.shape[1]
    ei = lax.broadcasted_iota(jnp.int32, (ne, tm), 0).astype(F32)
    ps = pstart_ref[:, 0:1]
    for k in range(TOP_K):
        base = jnp.sum(jnp.where(ei == meta_ref[k:k + 1, :], ps, 0.0), axis=0, keepdims=True)
        dest_ref[k:k + 1, :] = (base + meta_ref[2 * TOP_K + k:2 * TOP_K + k + 1, :]).astype(jnp.int32)


def _dest_rows(meta, pstart, tm=512):
    n = meta.shape[1]
    ne = pstart.shape[0]
    return pl.pallas_call(
        _dest_kernel,
        grid=(n // tm,),
        in_specs=[pl.BlockSpec((3 * TOP_K, tm), lambda i: (0, i)),
                  pl.BlockSpec((ne, LANES), lambda i: (0, 0))],
        out_specs=pl.BlockSpec((TOP_K, tm), lambda i: (0, i)),
        out_shape=jax.ShapeDtypeStruct((TOP_K, n), jnp.int32),
        compiler_params=_cparams(("parallel",)),
        name="moe_dest",
    )(meta, jnp.broadcast_to(pstart.astype(F32).reshape(ne, 1), (ne, LANES)))


def _dispatch_kernel(dest_ref, x_ref, xs_in_ref, xs_ref, sem):
    del xs_in_ref
    tm = x_ref.shape[0]

    def row_copy(t, k):
        return pltpu.make_async_copy(x_ref.at[pl.ds(t, 1), :], xs_ref.at[pl.ds(dest_ref[k, t], 1), :], sem)

    def body(t, _):
        for k in range(TOP_K):
            row_copy(t, k).start()
        return 0

    lax.fori_loop(0, tm, body, 0)
    done = xs_ref.at[pl.ds(0, TOP_K * tm), :]
    pltpu.make_async_copy(done, done, sem).wait()


def _dispatch(dest, x, n_rows, tm=256):
    n, d = x.shape
    return pl.pallas_call(
        _dispatch_kernel,
        grid=(n // tm,),
        in_specs=[pl.BlockSpec((TOP_K, tm), lambda i: (0, i), memory_space=pltpu.SMEM),
                  pl.BlockSpec((tm, d), lambda i: (i, 0)),
                  pl.BlockSpec(memory_space=pl.ANY)],
        out_specs=pl.BlockSpec(memory_space=pl.ANY),
        out_shape=jax.ShapeDtypeStruct((n_rows, d), F32),
        scratch_shapes=[pltpu.SemaphoreType.DMA(())],
        input_output_aliases={2: 0},
        compiler_params=_cparams(("arbitrary",)),
        name="moe_dispatch",
    )(dest, x, jnp.zeros((n_rows, d), F32))


def _combine_ln_kernel(dest_ref, w_ref, x_ref, ys_ref, wg_ref, wu_ref, wd_ref, g_ref, b_ref, o_ref,
                       buf, sem):
    tm = x_ref.shape[0]

    def row_copy(t, k):
        return pltpu.make_async_copy(ys_ref.at[pl.ds(dest_ref[k, t], 1), :], buf.at[k, pl.ds(t, 1), :], sem)

    def body(t, _):
        for k in range(TOP_K):
            row_copy(t, k).start()
        return 0

    lax.fori_loop(0, tm, body, 0)
    x = x_ref[...]
    xb = x.astype(BF16)
    h = _silu(jnp.dot(xb, wg_ref[...], preferred_element_type=F32)) * jnp.dot(
        xb, wu_ref[...], preferred_element_type=F32)
    ff = jnp.dot(h.astype(BF16), wd_ref[...], preferred_element_type=F32)
    pltpu.make_async_copy(buf, buf, sem).wait()
    for k in range(TOP_K):
        ff = ff + buf[k] * w_ref[:, k:k + 1]
    o_ref[...] = _layer_norm(DN_ALPHA * x + ff, g_ref[...], b_ref[...])


def _combine_ln(dest, wts, x, ys, ws_gate, ws_up, ws_down, g, b, tm=256):
    n, d = x.shape
    hdim = ws_gate.shape[1]
    row = pl.BlockSpec((tm, d), lambda i: (i, 0))
    return pl.pallas_call(
        _combine_ln_kernel,
        grid=(n // tm,),
        in_specs=[pl.BlockSpec((TOP_K, tm), lambda i: (0, i), memory_space=pltpu.SMEM),
                  pl.BlockSpec((tm, TOP_K), lambda i: (i, 0)),
                  row,
                  pl.BlockSpec(memory_space=pl.ANY),
                  pl.BlockSpec((d, hdim), lambda i: (0, 0)),
                  pl.BlockSpec((d, hdim), lambda i: (0, 0)),
                  pl.BlockSpec((hdim, d), lambda i: (0, 0)),
                  pl.BlockSpec((1, d), lambda i: (0, 0)),
                  pl.BlockSpec((1, d), lambda i: (0, 0))],
        out_specs=row,
        out_shape=jax.ShapeDtypeStruct((n, d), F32),
        scratch_shapes=[pltpu.VMEM((TOP_K, tm, d), F32), pltpu.SemaphoreType.DMA(())],
        compiler_params=_cparams(("arbitrary",)),
        name="moe_combine_ln",
    )(dest, wts, x, ys, ws_gate.astype(BF16), ws_up.astype(BF16), ws_down.astype(BF16),
      g.reshape(1, d), b.reshape(1, d))


def _moe_ln(x, w_router, b_router, w_gate, w_up, w_down, ws_gate, ws_up, ws_down, g, b):
    n, d = x.shape
    ne = w_router.shape[1]
    meta, counts = _router(x, w_router, b_router)
    counts = counts.astype(jnp.int32)
    padded = (counts + EXPERT_ROWS - 1) // EXPERT_ROWS * EXPERT_ROWS
    pend = jnp.cumsum(padded)
    dest = _dest_rows(meta, pend - padded)
    n_blk = (n * TOP_K) // EXPERT_ROWS + ne
    blk_e = jnp.minimum(jnp.searchsorted(pend, jnp.arange(n_blk, dtype=jnp.int32) * EXPERT_ROWS, side="right"),
                        ne - 1).astype(jnp.int32)
    n_used = (pend[-1] // EXPERT_ROWS).astype(jnp.int32).reshape(1)
    xs = _dispatch(dest, x, n_blk * EXPERT_ROWS)
    ys = _expert_ffn(blk_e, n_used, xs, w_gate, w_up, w_down)
    return _combine_ln(dest, meta[TOP_K:2 * TOP_K].T, x, ys, ws_gate, ws_up, ws_down, g, b)


def _block_diag(w):
    nb, bs, _ = w.shape
    eye = jnp.eye(nb, dtype=w.dtype)
    return (eye[:, None, :, None] * w[:, :, None, :]).reshape(nb * bs, nb * bs)


def _even_mixer_ln(x, w_in, conv_w, conv_b, wa, ba, wx, bx, lam, lb, norm_g, w_out, g, b):
    bsz, t, d = x.shape
    xf = x.reshape(bsz * t, d)
    u = _proj(xf, w_in.astype(BF16)).reshape(bsz, t, -1)
    ya = _lru(u, conv_w, conv_b, _block_diag(wa).astype(BF16), ba, _block_diag(wx).astype(BF16), bx, lam)
    yb = _hgrn2(u, lb, norm_g)
    y = jnp.concatenate([ya, yb], axis=-1).reshape(bsz * t, -1)
    return _proj_ln(y, w_out.astype(BF16), xf, g, b).reshape(bsz, t, d)


def _odd_mixer_ln(x, w_in, pe_k, pe_v, wk1, wk2, wv1, wv2, w_out, g, b):
    bsz, t, d = x.shape
    xf = x.reshape(bsz * t, d)
    nq = NSA_HEADS * HEAD_DIM
    nm = MOBA_HEADS * HEAD_DIM
    g0 = nq + 6 * NSA_KV
    ng = 3 * NSA_HEADS
    perm = np.concatenate([np.arange(0, g0), np.arange(g0 + ng, g0 + ng + 3 * nm), np.arange(g0, g0 + ng)])
    w_perm = jnp.pad(w_in[:, perm], ((0, 0), (0, LANES - ng))).astype(BF16)
    u = _proj(xf, w_perm).reshape(bsz, t, -1)

    def grp(off):
        z = u[:, :, off:off + NSA_KV].reshape(bsz, t, NSA_GROUPS, HEAD_DIM)
        return z.transpose(0, 2, 1, 3)

    qh = u[:, :, 0:nq].reshape(bsz, t, NSA_GROUPS, NSA_HPG, HEAD_DIM).transpose(0, 2, 3, 1, 4).astype(BF16)
    kc, vc, ks, vs, kw, vw = (grp(nq + i * NSA_KV) for i in range(6))
    m0 = g0

    def mh(off):
        z = u[:, :, off:off + nm].reshape(bsz, t, MOBA_HEADS, HEAD_DIM)
        return z.transpose(0, 2, 1, 3).astype(BF16)

    mq, mk, mv = mh(m0), mh(m0 + nm), mh(m0 + 2 * nm)
    gates = u[:, :, m0 + 3 * nm:m0 + 3 * nm + ng]

    n_all = NSA_HEADS + MOBA_HEADS
    s_all = 2.0 ** (-8.0 * np.arange(1, n_all + 1) / n_all)
    s_nsa = jnp.asarray(s_all[0::2], F32)
    s_moba = jnp.asarray(s_all[1::2], F32)

    k_cmp, v_cmp = _nsa_compress(kc, vc, pe_k, pe_v, wk1, wk2, wv1, wv2)
    o_c, bsel = _nsa_select(s_nsa, qh, k_cmp, v_cmp)
    o_s = _flash("slc", s_nsa, qh, ks.astype(BF16), vs.astype(BF16), bsel, tq=64)
    o_w = _flash("win", s_nsa, qh, kw.astype(BF16), vw.astype(BF16), tq=128)
    o_m = _flash("moba", s_moba, mq[:, :, None], mk, mv, tq=MOBA_BLOCK)

    gt = jax.nn.sigmoid(gates).reshape(bsz, t, NSA_GROUPS, NSA_HPG, 3).transpose(0, 2, 3, 1, 4)
    o = gt[..., 0:1] * o_c + gt[..., 1:2] * o_s + gt[..., 2:3] * o_w
    yc = o.transpose(0, 3, 1, 2, 4).reshape(bsz, t, nq)
    yd = o_m[:, :, 0].transpose(0, 2, 1, 3).reshape(bsz, t, nm)
    y = jnp.concatenate([yc, yd], axis=-1).reshape(bsz * t, -1)
    return _proj_ln(y, w_out.astype(BF16), xf, g, b).reshape(bsz, t, d)


def kernel(x, even_w_in, lru_conv_w, lru_conv_b, lru_wa, lru_ba, lru_wx, lru_bx, lru_lambda,
           hg_lower_bound, hg_norm_g, even_w_out, odd_w_in, nsa_pe_k, nsa_pe_v, nsa_wk1, nsa_wk2,
           nsa_wv1, nsa_wv2, odd_w_out, ln_g, ln_b, w_router, b_router, w_gate, w_up, w_down,
           ws_gate, ws_up, ws_down):
    bsz, t, d = x.shape
    lb_all = jnp.cumsum(jax.nn.softmax(hg_lower_bound.astype(F32), axis=0), axis=0)
    for layer in range(DEPTH):
        li = layer // 2
        if layer % 2 == 0:
            x = _even_mixer_ln(x, even_w_in[li], lru_conv_w[li], lru_conv_b[li], lru_wa[li], lru_ba[li],
                               lru_wx[li], lru_bx[li], lru_lambda[li], lb_all[layer], hg_norm_g[li],
                               even_w_out[li], ln_g[layer, 0], ln_b[layer, 0])
        else:
            x = _odd_mixer_ln(x, odd_w_in[li], nsa_pe_k[li], nsa_pe_v[li], nsa_wk1[li], nsa_wk2[li],
                              nsa_wv1[li], nsa_wv2[li], odd_w_out[li], ln_g[layer, 0], ln_b[layer, 0])
        x = _moe_ln(x.reshape(bsz * t, d), w_router[layer], b_router[layer], w_gate[layer], w_up[layer],
                    w_down[layer], ws_gate[layer], ws_up[layer], ws_down[layer],
                    ln_g[layer, 1], ln_b[layer, 1]).reshape(bsz, t, d)
    return x
```

```python
import functools

import numpy as np
import jax
import jax.numpy as jnp
from jax import lax
from jax.experimental import pallas as pl
from jax.experimental.pallas import tpu as pltpu

F32 = jnp.float32
BF16 = jnp.bfloat16

D_MODEL = 1024
DEPTH = 2
LRU_WIDTH = 512
LRU_BLOCKS = 8
LRU_BLOCK = LRU_WIDTH // LRU_BLOCKS
CONV_W = 4
LRU_C = 8.0
HG_HEADS = 4
HG_DK = 128
HG_FDIM = HG_HEADS * HG_DK
HG_VDIM = HG_HEADS * HG_DK
HEAD_DIM = 64
NSA_HEADS = 8
NSA_GROUPS = 2
NSA_HPG = NSA_HEADS // NSA_GROUPS
NSA_KV = NSA_GROUPS * HEAD_DIM
CMP_LEN = 32
CMP_STRIDE = 16
CMP_HIDDEN = 128
SLC_LEN = 64
SLC_TOPN = 16
WIN = 512
FORCE_BONUS = 1e4
MOBA_HEADS = 8
MOBA_BLOCK = 256
MOBA_TOPK = 3
N_EXPERTS = 256
TOP_K = 8
N_GROUPS = 8
TOPK_GROUPS = 4
EXPERT_HIDDEN = 256
ROUTED_SCALE = 2.5
DN_ALPHA = (2 * DEPTH) ** 0.25
LN_EPS = 1e-5
NEG = -1e30
NEG_INF = float("-inf")

LANES = 128
SUBLANES = 8
VMEM_LIMIT = 48 * 1024 * 1024

HG_SUB = 16
EXPERT_ROWS = 256


def _cparams(sem):
    return pltpu.CompilerParams(dimension_semantics=sem, vmem_limit_bytes=VMEM_LIMIT)


def _dot(a, b):
    return jnp.dot(a.astype(BF16), b.astype(BF16), preferred_element_type=F32)


def _dot_nt(a, b):
    return lax.dot_general(a.astype(BF16), b.astype(BF16), (((1,), (1,)), ((), ())),
                           preferred_element_type=F32)


def _dot_tn(a, b):
    return lax.dot_general(a.astype(BF16), b.astype(BF16), (((0,), (0,)), ((), ())),
                           preferred_element_type=F32)


def _split2(x):
    hi = x.astype(BF16)
    lo = (x - hi.astype(F32)).astype(BF16)
    return hi, lo


def _split3(x):
    hi = x.astype(BF16)
    r = x - hi.astype(F32)
    mid = r.astype(BF16)
    lo = (r - mid.astype(F32)).astype(BF16)
    return hi, mid, lo


def _layer_norm(v, g, b):
    mu = jnp.mean(v, axis=-1, keepdims=True)
    d = v - mu
    var = jnp.mean(d * d, axis=-1, keepdims=True)
    return d * lax.rsqrt(var + LN_EPS) * g + b


def _sigmoid(x):
    return 1.0 / (1.0 + jnp.exp(-x))


def _silu(x):
    return x * _sigmoid(x)


def _proj_kernel(x_ref, w_ref, o_ref):
    o_ref[...] = jnp.dot(x_ref[...].astype(BF16), w_ref[...], preferred_element_type=F32)


def _proj(x, w, tm=256):
    n, k = x.shape
    m = w.shape[1]
    return pl.pallas_call(
        _proj_kernel,
        grid=(n // tm,),
        in_specs=[pl.BlockSpec((tm, k), lambda i: (i, 0)),
                  pl.BlockSpec((k, m), lambda i: (0, 0))],
        out_specs=pl.BlockSpec((tm, m), lambda i: (i, 0)),
        out_shape=jax.ShapeDtypeStruct((n, m), F32),
        compiler_params=_cparams(("parallel",)),
        name="proj",
    )(x, w)


def _proj_ln_kernel(y_ref, w_ref, x_ref, g_ref, b_ref, o_ref):
    mix = jnp.dot(y_ref[...].astype(BF16), w_ref[...], preferred_element_type=F32)
    o_ref[...] = _layer_norm(DN_ALPHA * x_ref[...] + mix, g_ref[...], b_ref[...])


def _proj_ln(y, w, xres, g, b, tm=256):
    n, k = y.shape
    d = w.shape[1]
    return pl.pallas_call(
        _proj_ln_kernel,
        grid=(n // tm,),
        in_specs=[pl.BlockSpec((tm, k), lambda i: (i, 0)),
                  pl.BlockSpec((k, d), lambda i: (0, 0)),
                  pl.BlockSpec((tm, d), lambda i: (i, 0)),
                  pl.BlockSpec((1, d), lambda i: (0, 0)),
                  pl.BlockSpec((1, d), lambda i: (0, 0))],
        out_specs=pl.BlockSpec((tm, d), lambda i: (i, 0)),
        out_shape=jax.ShapeDtypeStruct((n, d), F32),
        compiler_params=_cparams(("parallel",)),
        name="proj_ln",
    )(y, w, xres, g.reshape(1, d), b.reshape(1, d))


def _lru_kernel(xb_ref, gb_ref, cw_ref, cb_ref, wa_ref, ba_ref, wx_ref, bx_ref, lam_ref,
                o_ref, xs_ref, a_ref, b_ref, h_ref):
    ti = pl.program_id(1)
    tc = xb_ref.shape[1]
    c = xb_ref.shape[2]

    @pl.when(ti == 0)
    def _():
        xs_ref[0:SUBLANES, :] = jnp.zeros((SUBLANES, c), F32)
        h_ref[...] = jnp.zeros_like(h_ref)

    xs_ref[SUBLANES:SUBLANES + tc, :] = xb_ref[0]
    xc = cb_ref[...]
    for w in range(CONV_W):
        xc = xc + cw_ref[w:w + 1, :] * xs_ref[pl.ds(SUBLANES - (CONV_W - 1) + w, tc), :]
    xs_ref[0:SUBLANES, :] = xs_ref[tc:tc + SUBLANES, :]

    r = _sigmoid(_dot(xc, wa_ref[...]) + ba_ref[...])
    i = _sigmoid(_dot(xc, wx_ref[...]) + bx_ref[...])
    nl = -lam_ref[...]
    softplus = jnp.maximum(nl, 0.0) + jnp.log(1.0 + jnp.exp(-jnp.abs(nl)))
    log_a = (-LRU_C * r) * softplus
    a = jnp.exp(log_a)
    mult = jnp.sqrt(1.0 - jnp.exp(2.0 * log_a))
    trow = ti * tc + lax.broadcasted_iota(jnp.int32, (tc, c), 0)
    mult = jnp.where(trow == 0, 1.0, mult)
    a_ref[...] = a
    b_ref[...] = mult * (i * xc)

    row = lax.broadcasted_iota(jnp.int32, (SUBLANES, c), 0)

    def body(g, carry):
        r0 = pl.multiple_of(g * SUBLANES, SUBLANES)
        av = a_ref[pl.ds(r0, SUBLANES), :]
        bv = b_ref[pl.ds(r0, SUBLANES), :]
        for d in (1, 2, 4):
            a_sh = pltpu.roll(av, d, 0)
            b_sh = pltpu.roll(bv, d, 0)
            m = row >= d
            bv = jnp.where(m, av * b_sh + bv, bv)
            av = jnp.where(m, av * a_sh, av)
        h = av * carry + bv
        b_ref[pl.ds(r0, SUBLANES), :] = h
        return h[SUBLANES - 1:SUBLANES, :]

    h_ref[...] = lax.fori_loop(0, tc // SUBLANES, body, h_ref[...])
    o_ref[0] = b_ref[...] * jax.nn.gelu(gb_ref[0])


def _lru(u3, conv_w, conv_b, wa_bd, ba, wx_bd, bx, lam, tc=256):
    b, t, _ = u3.shape
    c = LRU_WIDTH
    row = lambda v: v.reshape(1, c)
    full = lambda shp: pl.BlockSpec(shp, lambda bi, ti: (0, 0))
    return pl.pallas_call(
        _lru_kernel,
        grid=(b, t // tc),
        in_specs=[pl.BlockSpec((1, tc, c), lambda bi, ti: (bi, ti, 0)),
                  pl.BlockSpec((1, tc, c), lambda bi, ti: (bi, ti, 1)),
                  full((CONV_W, c)), full((1, c)), full((c, c)), full((1, c)),
                  full((c, c)), full((1, c)), full((1, c))],
        out_specs=pl.BlockSpec((1, tc, c), lambda bi, ti: (bi, ti, 0)),
        out_shape=jax.ShapeDtypeStruct((b, t, c), F32),
        scratch_shapes=[pltpu.VMEM((tc + SUBLANES, c), F32), pltpu.VMEM((tc, c), F32),
                        pltpu.VMEM((tc, c), F32), pltpu.VMEM((1, c), F32)],
        compiler_params=_cparams(("parallel", "arbitrary")),
        name="rg_lru",
    )(u3, u3, conv_w, row(conv_b), wa_bd, row(ba), wx_bd, row(bx), row(lam))


def _hgrn_kernel(q_ref, f_ref, v_ref, g_ref, lb_ref, ng_ref, tri_ref, ones_ref, o_ref,
                 st_ref, q_s, k_s, c_s, qe_s, kd_s, dl_s, o_s):
    ti = pl.program_id(2)
    tc = q_ref.shape[1]
    dk = q_ref.shape[2]

    @pl.when(ti == 0)
    def _():
        st_ref[...] = jnp.zeros_like(st_ref)

    lbv = lb_ref[...]
    q = _silu(q_ref[0])
    f = lbv + (1.0 - lbv) * _sigmoid(f_ref[0])
    kk = 1.0 - f
    logf = jnp.log(f)
    parts = _split3(logf)
    tri = tri_ref[...]
    ones = ones_ref[...]
    cum = sum(jnp.dot(tri, p, preferred_element_type=F32) for p in parts)
    last = sum(jnp.dot(ones, p, preferred_element_type=F32) for p in parts)
    q_s[...] = q
    k_s[...] = kk
    c_s[...] = cum
    qe_s[...] = q * jnp.exp(cum)
    kd_s[...] = kk * jnp.exp(last - cum)
    dl_s[...] = jnp.exp(last)

    s_idx = lax.broadcasted_iota(jnp.int32, (HG_SUB, dk), 0)

    def chunk(ci, _):
        r0 = pl.multiple_of(ci * HG_SUB, HG_SUB)
        rows = pl.ds(r0, HG_SUB)
        st = st_ref[...]
        vc = v_ref[0, rows, :]
        qc = q_s[rows, :]
        kc = k_s[rows, :]
        cc = c_s[rows, :]
        o = _dot_nt(qe_s[rows, :], st)
        for t in range(HG_SUB):
            d = cc[t:t + 1, :] - cc
            dec = jnp.exp(jnp.where(s_idx <= t, d, NEG))
            w = (qc[t:t + 1, :] * kc) * dec
            a_t = jnp.sum(w, axis=1, keepdims=True)
            o_t = jnp.sum(a_t * vc, axis=0, keepdims=True)
            o = o + jnp.where(s_idx == t, o_t, 0.0)
        o_s[rows, :] = o
        st_ref[...] = st * dl_s[pl.ds(r0, 1), :] + _dot_tn(vc, kd_s[rows, :])
        return 0

    lax.fori_loop(0, tc // HG_SUB, chunk, 0)
    o = o_s[...]
    o = o * lax.rsqrt(jnp.mean(o * o, axis=-1, keepdims=True) + 1e-6) * ng_ref[...]
    o_ref[0] = o * _sigmoid(g_ref[0])


def _hgrn2(u3, lb, norm_g, tc=256):
    b, t, _ = u3.shape
    dk = HG_DK
    base = (2 * LRU_WIDTH) // dk
    idx = np.arange(tc)
    same = (idx[:, None] // HG_SUB) == (idx[None, :] // HG_SUB)
    tri = jnp.asarray(same & (idx[None, :] <= idx[:, None]), BF16)
    ones = jnp.asarray(same, BF16)

    def col(off):
        return pl.BlockSpec((1, tc, dk), lambda bi, hi, ti: (bi, ti, base + off * HG_HEADS + hi))

    vec = pl.BlockSpec((1, dk), lambda bi, hi, ti: (0, hi))
    cst = pl.BlockSpec((tc, tc), lambda bi, hi, ti: (0, 0))
    return pl.pallas_call(
        _hgrn_kernel,
        grid=(b, HG_HEADS, t // tc),
        in_specs=[col(0), col(1), col(2), col(3), vec, vec, cst, cst],
        out_specs=pl.BlockSpec((1, tc, dk), lambda bi, hi, ti: (bi, ti, hi)),
        out_shape=jax.ShapeDtypeStruct((b, t, HG_VDIM), F32),
        scratch_shapes=[pltpu.VMEM((dk, dk), F32)] + [pltpu.VMEM((tc, dk), F32)] * 7,
        compiler_params=_cparams(("parallel", "parallel", "arbitrary")),
        name="hgrn2",
    )(u3, u3, u3, u3, lb.reshape(1, HG_FDIM), norm_g.reshape(1, HG_VDIM), tri, ones)


def _nsa_cmp_kv_kernel(zk_ref, zv_ref, pek_ref, pev_ref, wk1_ref, wk2_ref, wv1_ref, wv2_ref,
                       ok_ref, ov_ref):
    half = zk_ref.shape[3]
    nrow = zk_ref.shape[2]

    def compress(z, pe_ref, w1_ref, w2_ref):
        lo = _dot(z + pe_ref[0:1, :], w1_ref[0:half, :])
        hi = _dot(z + pe_ref[1:2, :], w1_ref[half:2 * half, :])
        pre = lo + pltpu.roll(hi, nrow - 1, 0)
        return _dot(jax.nn.gelu(pre), w2_ref[...])

    ok_ref[0, 0] = compress(zk_ref[0, 0], pek_ref, wk1_ref, wk2_ref)
    ov_ref[0, 0] = compress(zv_ref[0, 0], pev_ref, wv1_ref, wv2_ref)


def _nsa_compress(kc, vc, pe_k, pe_v, wk1, wk2, wv1, wv2):
    b, g, t, dh = kc.shape
    nrow = t // CMP_STRIDE
    half = CMP_STRIDE * dh
    zk = kc.reshape(b, g, nrow, half)
    zv = vc.reshape(b, g, nrow, half)
    zspec = pl.BlockSpec((1, 1, nrow, half), lambda bi, gi: (bi, gi, 0, 0))
    full = lambda shp: pl.BlockSpec(shp, lambda bi, gi: (0, 0))
    ospec = pl.BlockSpec((1, 1, nrow, dh), lambda bi, gi: (bi, gi, 0, 0))
    oshape = jax.ShapeDtypeStruct((b, g, nrow, dh), F32)
    return pl.pallas_call(
        _nsa_cmp_kv_kernel,
        grid=(b, g),
        in_specs=[zspec, zspec, full((2, half)), full((2, half)),
                  full((2 * half, CMP_HIDDEN)), full((CMP_HIDDEN, dh)),
                  full((2 * half, CMP_HIDDEN)), full((CMP_HIDDEN, dh))],
        out_specs=[ospec, ospec],
        out_shape=[oshape, oshape],
        compiler_params=_cparams(("parallel", "parallel")),
        name="nsa_compress",
    )(zk, zv, pe_k.reshape(2, half), pe_v.reshape(2, half),
      wk1.astype(BF16), wk2.astype(BF16), wv1.astype(BF16), wv2.astype(BF16))


def _nsa_sel_kernel(slope_ref, q_ref, kc_ref, vc_ref, ov_ref, o_ref, sel_ref):
    gi = pl.program_id(0) % NSA_GROUPS
    qi = pl.program_id(1)
    hpg = q_ref.shape[2]
    tq = q_ref.shape[3]
    dh = q_ref.shape[4]
    nc = kc_ref.shape[2]
    ns = sel_ref.shape[3]
    scale = dh ** -0.5

    q = q_ref[0, 0].reshape(hpg * tq, dh)
    khi, klo = _split2(kc_ref[0, 0])
    s = (_dot_nt(q, khi) + _dot_nt(q, klo)) * scale
    tpos = qi * tq + lax.broadcasted_iota(jnp.int32, (tq, nc), 0)
    cend = lax.broadcasted_iota(jnp.int32, (tq, nc), 1) * CMP_STRIDE + (CMP_LEN - 1)
    d_c = (tpos - cend).astype(F32)
    valid = d_c >= 0.0
    ps = []
    for i in range(hpg):
        si = s[i * tq:(i + 1) * tq, :] - slope_ref[gi * hpg + i] * d_c
        si = jnp.where(valid, si, NEG)
        m = jnp.max(si, axis=-1, keepdims=True)
        p = jnp.where(valid, jnp.exp(si - m), 0.0)
        ps.append(p / jnp.maximum(jnp.sum(p, axis=-1, keepdims=True), 1e-30))
    pall = jnp.concatenate(ps, axis=0)
    o_ref[0, 0] = _dot(pall, vc_ref[0, 0]).reshape(hpg, tq, dh)

    psum = ps[0]
    for i in range(1, hpg):
        psum = psum + ps[i]
    phi, plo = _split2(psum)
    ovl = ov_ref[...]
    imp = jnp.dot(phi, ovl, preferred_element_type=F32) + jnp.dot(plo, ovl, preferred_element_type=F32)

    j = lax.broadcasted_iota(jnp.int32, (tq, ns), 1)
    own = (qi * tq + lax.broadcasted_iota(jnp.int32, (tq, ns), 0)) // SLC_LEN
    forced = (j == 0) | (j == own) | (j == own - 1)
    imp = jnp.where(j <= own, imp + FORCE_BONUS * forced.astype(F32), NEG)
    rank = jnp.zeros((tq, ns), F32)
    for jp in range(ns):
        col = imp[:, jp:jp + 1]
        beats = (col > imp) | ((col == imp) & (j > jp))
        rank = rank + beats.astype(F32)
    sel_ref[0, 0] = (rank < float(min(SLC_TOPN, ns))).astype(F32)


def _nsa_select(slopes, qh, k_cmp, v_cmp, tq=256):
    b, g, hpg, t, dh = qh.shape
    nc = k_cmp.shape[2]
    ns = t // SLC_LEN
    ci = np.arange(nc)[:, None]
    cj = np.arange(ns)[None, :]
    ovl = ((CMP_STRIDE * ci < SLC_LEN * (cj + 1)) & (CMP_STRIDE * ci + CMP_LEN > SLC_LEN * cj))
    ovl = ovl & (ci < (t - CMP_LEN) // CMP_STRIDE + 1)
    grid_spec = pltpu.PrefetchScalarGridSpec(
        num_scalar_prefetch=1,
        grid=(b * g, t // tq),
        in_specs=[pl.BlockSpec((1, 1, hpg, tq, dh), lambda bg, qi, s: (bg // g, bg % g, 0, qi, 0)),
                  pl.BlockSpec((1, 1, nc, dh), lambda bg, qi, s: (bg // g, bg % g, 0, 0)),
                  pl.BlockSpec((1, 1, nc, dh), lambda bg, qi, s: (bg // g, bg % g, 0, 0)),
                  pl.BlockSpec((nc, ns), lambda bg, qi, s: (0, 0))],
        out_specs=[pl.BlockSpec((1, 1, hpg, tq, dh), lambda bg, qi, s: (bg // g, bg % g, 0, qi, 0)),
                   pl.BlockSpec((1, 1, tq, ns), lambda bg, qi, s: (bg // g, bg % g, qi, 0))],
    )
    return pl.pallas_call(
        _nsa_sel_kernel,
        grid_spec=grid_spec,
        out_shape=[jax.ShapeDtypeStruct((b, g, hpg, t, dh), F32),
                   jax.ShapeDtypeStruct((b, g, t, ns), F32)],
        compiler_params=_cparams(("parallel", "parallel")),
        name="nsa_select",
    )(slopes, qh, k_cmp, v_cmp, jnp.asarray(ovl, BF16))


POS_SPLIT = 64
MASKED = -2e30


def _flash_kernel(q_ref, k_ref, v_ref, o_ref, m_ref, l_ref, acc_ref, *mb, mode, tk):
    qi = pl.program_id(1)
    hpg = q_ref.shape[2]
    tq = q_ref.shape[3]
    da = q_ref.shape[4]
    dh = v_ref.shape[3]
    t_all = k_ref.shape[2]
    rows = hpg * tq
    q0 = qi * tq

    q = q_ref[0, 0].reshape(rows, da)
    m_ref[...] = jnp.full(m_ref.shape, NEG, F32)
    l_ref[...] = jnp.zeros_like(l_ref)
    acc_ref[...] = jnp.zeros_like(acc_ref)

    if mode == "moba":
        mb_ref, = mb
        nb = t_all // MOBA_BLOCK
        kf = k_ref[0, 0].astype(F32).reshape(nb, MOBA_BLOCK, da)
        kmean = jnp.sum(kf, axis=1) / float(MOBA_BLOCK)
        kmean = jnp.where(lax.broadcasted_iota(jnp.int32, (nb, da), 1) < dh, kmean, 0.0)
        kmean = jnp.concatenate([kmean, jnp.zeros((LANES - nb, da), F32)], axis=0)
        mhi, mlo = _split2(kmean)
        gate = _dot_nt(q, mhi) + _dot_nt(q, mlo)
        j = lax.broadcasted_iota(jnp.int32, (tq, LANES), 1)
        ob = q0 // MOBA_BLOCK
        gate = jnp.where(j < ob, gate, NEG)
        rank = jnp.zeros((tq, LANES), F32)
        for jp in range(nb):
            col = gate[:, jp:jp + 1]
            beats = (col > gate) | ((col == gate) & (j > jp))
            rank = rank + beats.astype(F32)
        n_sel = float(min(MOBA_TOPK, nb - 1))
        bsel = ((rank < n_sel) & (j < ob)) | (j == ob)
        bias = jnp.where(bsel, 0.0, MASKED)
        for jb in range(nb):
            mb_ref[jb] = jnp.broadcast_to(bias[:, jb:jb + 1], (tq, LANES))
        lo = 0
        hi = ob + 1
    elif mode == "slc":
        lo = 0
        hi = (q0 + tq + tk - 1) // tk
    else:
        lo = jnp.maximum(q0 - (WIN - 1), 0) // tk
        hi = (q0 + tq + tk - 1) // tk

    rel = lax.broadcasted_iota(jnp.int32, (tq, tk), 0) - lax.broadcasted_iota(jnp.int32, (tq, tk), 1)
    if hpg > 1:
        rel = jnp.concatenate([rel] * hpg, axis=0)
    reps = tk // LANES

    def body(kb, _):
        ks = pl.multiple_of(kb * tk, tk)
        s = _dot_nt(q, k_ref[0, 0, pl.ds(ks, tk), :])
        off = ks - q0
        valid = rel >= off
        if mode == "win":
            valid = valid & (rel < off + WIN)
        s = jnp.where(valid, s, MASKED)
        if mode == "moba":
            s = s + jnp.concatenate([mb_ref[kb]] * reps, axis=1)
        m_old = m_ref[...]
        m_new = jnp.maximum(m_old, jnp.max(s, axis=-1, keepdims=True))
        alpha = jnp.exp(m_old - m_new)
        p = jnp.exp(s - jnp.concatenate([m_new] * reps, axis=1))
        l_ref[...] = alpha * l_ref[...] + jnp.sum(p, axis=-1, keepdims=True)
        acc_ref[...] = alpha[:, 0:dh] * acc_ref[...] + _dot(p, v_ref[0, 0, pl.ds(ks, tk), :])
        m_ref[...] = m_new
        return 0

    lax.fori_loop(lo, hi, body, 0)
    o = acc_ref[...] / jnp.maximum(l_ref[:, 0:dh], 1e-30)
    o_ref[0, 0] = o.reshape(hpg, tq, dh)


def _flash(mode, qa, ka, vh, tq, tk=256):
    b, n_kvh, hpg, t, da = qa.shape
    dh = vh.shape[3]
    rows = hpg * tq
    scratch = [pltpu.VMEM((rows, LANES), F32), pltpu.VMEM((rows, LANES), F32), pltpu.VMEM((rows, dh), F32)]
    if mode == "moba":
        assert tq == MOBA_BLOCK and tk == MOBA_BLOCK
        scratch.append(pltpu.VMEM((t // MOBA_BLOCK, tq, LANES), F32))
    return pl.pallas_call(
        functools.partial(_flash_kernel, mode=mode, tk=tk),
        grid=(b * n_kvh, t // tq),
        in_specs=[pl.BlockSpec((1, 1, hpg, tq, da), lambda bh, qi: (bh // n_kvh, bh % n_kvh, 0, qi, 0)),
                  pl.BlockSpec((1, 1, t, da), lambda bh, qi: (bh // n_kvh, bh % n_kvh, 0, 0)),
                  pl.BlockSpec((1, 1, t, dh), lambda bh, qi: (bh // n_kvh, bh % n_kvh, 0, 0))],
        out_specs=pl.BlockSpec((1, 1, hpg, tq, dh), lambda bh, qi: (bh // n_kvh, bh % n_kvh, 0, qi, 0)),
        out_shape=jax.ShapeDtypeStruct((b, n_kvh, hpg, t, dh), F32),
        scratch_shapes=scratch,
        compiler_params=_cparams(("parallel", "parallel")),
        name="flash_" + mode,
    )(qa, ka, vh)


def _router_kernel(x_ref, whi_ref, wlo_ref, b_ref, triu_ref, meta_ref, cnt_ref, carry_ref):
    step = pl.program_id(0)
    tm = x_ref.shape[0]
    ne = whi_ref.shape[0]
    gsz = ne // N_GROUPS

    @pl.when(step == 0)
    def _():
        carry_ref[...] = jnp.zeros_like(carry_ref)

    xhi, xlo = _split2(x_ref[...])
    whi = whi_ref[...]
    logits = _dot_nt(whi, xhi) + _dot_nt(wlo_ref[...], xhi) + _dot_nt(whi, xlo)
    scores = _sigmoid(logits)
    biased = scores + b_ref[:, 0:1]

    v3 = biased.reshape(N_GROUPS, gsz, tm)
    i3 = lax.broadcasted_iota(jnp.int32, (N_GROUPS, gsz, tm), 1).astype(F32)
    m1 = jnp.max(v3, axis=1, keepdims=True)
    idx1 = jnp.min(jnp.where(v3 == m1, i3, float(gsz)), axis=1, keepdims=True)
    m2 = jnp.max(jnp.where(i3 == idx1, NEG_INF, v3), axis=1, keepdims=True)
    gs = (m1 + m2).reshape(N_GROUPS, tm)
    gi = lax.broadcasted_iota(jnp.int32, (N_GROUPS, tm), 0).astype(F32)
    gmask = jnp.zeros((N_GROUPS, tm), F32)
    for _ in range(TOPK_GROUPS):
        m = jnp.max(gs, axis=0, keepdims=True)
        pick = jnp.min(jnp.where(gs == m, gi, float(N_GROUPS)), axis=0, keepdims=True)
        hit = gi == pick
        gmask = jnp.where(hit, 1.0, gmask)
        gs = jnp.where(hit, NEG_INF, gs)
    emask = jnp.broadcast_to(gmask.reshape(N_GROUPS, 1, tm), (N_GROUPS, gsz, tm)).reshape(ne, tm)
    cur = jnp.where(emask > 0.5, biased, NEG)

    ei = lax.broadcasted_iota(jnp.int32, (ne, tm), 0).astype(F32)
    selm = jnp.zeros((ne, tm), F32)
    idxs, svals = [], []
    for _ in range(TOP_K):
        m = jnp.max(cur, axis=0, keepdims=True)
        idx = jnp.min(jnp.where(cur == m, ei, float(ne)), axis=0, keepdims=True)
        hit = ei == idx
        svals.append(jnp.sum(jnp.where(hit, scores, 0.0), axis=0, keepdims=True))
        idxs.append(idx)
        cur = jnp.where(hit, NEG_INF, cur)
        selm = jnp.where(hit, 1.0, selm)

    carry = carry_ref[:, 0:1]
    rank_full = jnp.dot(selm.astype(BF16), triu_ref[...], preferred_element_type=F32) + carry
    ssum = svals[0]
    for k in range(1, TOP_K):
        ssum = ssum + svals[k]
    for k in range(TOP_K):
        meta_ref[k:k + 1, :] = idxs[k]
        meta_ref[TOP_K + k:TOP_K + k + 1, :] = svals[k] / ssum * ROUTED_SCALE
        meta_ref[2 * TOP_K + k:2 * TOP_K + k + 1, :] = jnp.sum(
            jnp.where(ei == idxs[k], rank_full, 0.0), axis=0, keepdims=True)
    new_carry = carry + jnp.sum(selm, axis=1, keepdims=True)
    carry_ref[...] = jnp.broadcast_to(new_carry, carry_ref.shape)
    cnt_ref[...] = jnp.broadcast_to(new_carry, cnt_ref.shape)


def _router(x, w_router, b_router, tm=256):
    n, d = x.shape
    ne = w_router.shape[1]
    wt = w_router.T
    whi = wt.astype(BF16)
    wlo = (wt - whi.astype(F32)).astype(BF16)
    triu = jnp.asarray(np.arange(tm)[:, None] < np.arange(tm)[None, :], BF16)
    meta, cnt = pl.pallas_call(
        _router_kernel,
        grid=(n // tm,),
        in_specs=[pl.BlockSpec((tm, d), lambda i: (i, 0)),
                  pl.BlockSpec((ne, d), lambda i: (0, 0)),
                  pl.BlockSpec((ne, d), lambda i: (0, 0)),
                  pl.BlockSpec((ne, LANES), lambda i: (0, 0)),
                  pl.BlockSpec((tm, tm), lambda i: (0, 0))],
        out_specs=[pl.BlockSpec((3 * TOP_K, tm), lambda i: (0, i)),
                   pl.BlockSpec((ne, LANES), lambda i: (0, 0))],
        out_shape=[jax.ShapeDtypeStruct((3 * TOP_K, n), F32),
                   jax.ShapeDtypeStruct((ne, LANES), F32)],
        scratch_shapes=[pltpu.VMEM((ne, LANES), F32)],
        compiler_params=_cparams(("arbitrary",)),
        name="moe_router",
    )(x, whi, wlo, jnp.broadcast_to(b_router.reshape(ne, 1), (ne, LANES)), triu)
    return meta, cnt[:, 0]


def _expert_kernel(be_ref, nu_ref, xs_ref, wg_ref, wu_ref, wd_ref, o_ref):
    bi = pl.program_id(0)

    @pl.when(bi < nu_ref[0])
    def _():
        x = xs_ref[...]
        h = _silu(_dot(x, wg_ref[0])) * _dot(x, wu_ref[0])
        o_ref[...] = _dot(h, wd_ref[0])

    @pl.when(bi >= nu_ref[0])
    def _():
        o_ref[...] = jnp.zeros_like(o_ref)


def _expert_ffn(blk_e, n_used, xs, w_gate, w_up, w_down):
    n_rows, d = xs.shape
    hdim = w_gate.shape[2]
    n_blk = n_rows // EXPERT_ROWS
    grid_spec = pltpu.PrefetchScalarGridSpec(
        num_scalar_prefetch=2,
        grid=(n_blk,),
        in_specs=[pl.BlockSpec((EXPERT_ROWS, d), lambda i, be, nu: (i, 0)),
                  pl.BlockSpec((1, d, hdim), lambda i, be, nu: (be[i], 0, 0)),
                  pl.BlockSpec((1, d, hdim), lambda i, be, nu: (be[i], 0, 0)),
                  pl.BlockSpec((1, hdim, d), lambda i, be, nu: (be[i], 0, 0))],
        out_specs=pl.BlockSpec((EXPERT_ROWS, d), lambda i, be, nu: (i, 0)),
    )
    return pl.pallas_call(
        _expert_kernel,
        grid_spec=grid_spec,
        out_shape=jax.ShapeDtypeStruct((n_rows, d), F32),
        compiler_params=_cparams(("arbitrary",)),
        name="moe_experts",
    )(blk_e, n_used, xs, w_gate, w_up, w_down)


def _dest_kernel(meta_ref, pstart_ref, dest_ref):
    ne = pstart_ref.shape[0]
    tm = meta_ref.shape[1]
    ei = lax.broadcasted_iota(jnp.int32, (ne, tm), 0).astype(F32)
    ps = pstart_ref[:, 0:1]
    for k in range(TOP_K):
        base = jnp.sum(jnp.where(ei == meta_ref[k:k + 1, :], ps, 0.0), axis=0, keepdims=True)
        dest_ref[k:k + 1, :] = (base + meta_ref[2 * TOP_K + k:2 * TOP_K + k + 1, :]).astype(jnp.int32)


def _dest_rows(meta, pstart, tm=512):
    n = meta.shape[1]
    ne = pstart.shape[0]
    return pl.pallas_call(
        _dest_kernel,
        grid=(n // tm,),
        in_specs=[pl.BlockSpec((3 * TOP_K, tm), lambda i: (0, i)),
                  pl.BlockSpec((ne, LANES), lambda i: (0, 0))],
        out_specs=pl.BlockSpec((TOP_K, tm), lambda i: (0, i)),
        out_shape=jax.ShapeDtypeStruct((TOP_K, n), jnp.int32),
        compiler_params=_cparams(("parallel",)),
        name="moe_dest",
    )(meta, jnp.broadcast_to(pstart.astype(F32).reshape(ne, 1), (ne, LANES)))


def _dispatch_kernel(dest_ref, x_ref, xs_in_ref, xs_ref, sem):
    del xs_in_ref
    tm = x_ref.shape[0]

    def row_copy(t, k):
        return pltpu.make_async_copy(x_ref.at[pl.ds(t, 1), :], xs_ref.at[pl.ds(dest_ref[k, t], 1), :], sem)

    def body(t, _):
        for k in range(TOP_K):
            row_copy(t, k).start()
        return 0

    lax.fori_loop(0, tm, body, 0)
    done = xs_ref.at[pl.ds(0, TOP_K * tm), :]
    pltpu.make_async_copy(done, done, sem).wait()


def _dispatch(dest, x, n_rows, tm=256):
    n, d = x.shape
    return pl.pallas_call(
        _dispatch_kernel,
        grid=(n // tm,),
        in_specs=[pl.BlockSpec((TOP_K, tm), lambda i: (0, i), memory_space=pltpu.SMEM),
                  pl.BlockSpec((tm, d), lambda i: (i, 0)),
                  pl.BlockSpec(memory_space=pl.ANY)],
        out_specs=pl.BlockSpec(memory_space=pl.ANY),
        out_shape=jax.ShapeDtypeStruct((n_rows, d), F32),
        scratch_shapes=[pltpu.SemaphoreType.DMA(())],
        input_output_aliases={2: 0},
        compiler_params=_cparams(("arbitrary",)),
        name="moe_dispatch",
    )(dest, x, jnp.zeros((n_rows, d), F32))


def _combine_ln_kernel(dest_ref, w_ref, x_ref, ys_ref, wg_ref, wu_ref, wd_ref, g_ref, b_ref, o_ref,
                       buf, sem):
    tm = x_ref.shape[0]

    def row_copy(t, k):
        return pltpu.make_async_copy(ys_ref.at[pl.ds(dest_ref[k, t], 1), :], buf.at[k, pl.ds(t, 1), :], sem)

    def body(t, _):
        for k in range(TOP_K):
            row_copy(t, k).start()
        return 0

    lax.fori_loop(0, tm, body, 0)
    x = x_ref[...]
    xb = x.astype(BF16)
    h = _silu(jnp.dot(xb, wg_ref[...], preferred_element_type=F32)) * jnp.dot(
        xb, wu_ref[...], preferred_element_type=F32)
    ff = jnp.dot(h.astype(BF16), wd_ref[...], preferred_element_type=F32)
    pltpu.make_async_copy(buf, buf, sem).wait()
    for k in range(TOP_K):
        ff = ff + buf[k] * w_ref[:, k:k + 1]
    o_ref[...] = _layer_norm(DN_ALPHA * x + ff, g_ref[...], b_ref[...])


def _combine_ln(dest, wts, x, ys, ws_gate, ws_up, ws_down, g, b, tm=256):
    n, d = x.shape
    hdim = ws_gate.shape[1]
    row = pl.BlockSpec((tm, d), lambda i: (i, 0))
    return pl.pallas_call(
        _combine_ln_kernel,
        grid=(n // tm,),
        in_specs=[pl.BlockSpec((TOP_K, tm), lambda i: (0, i), memory_space=pltpu.SMEM),
                  pl.BlockSpec((tm, TOP_K), lambda i: (i, 0)),
                  row,
                  pl.BlockSpec(memory_space=pl.ANY),
                  pl.BlockSpec((d, hdim), lambda i: (0, 0)),
                  pl.BlockSpec((d, hdim), lambda i: (0, 0)),
                  pl.BlockSpec((hdim, d), lambda i: (0, 0)),
                  pl.BlockSpec((1, d), lambda i: (0, 0)),
                  pl.BlockSpec((1, d), lambda i: (0, 0))],
        out_specs=row,
        out_shape=jax.ShapeDtypeStruct((n, d), F32),
        scratch_shapes=[pltpu.VMEM((TOP_K, tm, d), F32), pltpu.SemaphoreType.DMA(())],
        compiler_params=_cparams(("arbitrary",)),
        name="moe_combine_ln",
    )(dest, wts, x, ys, ws_gate.astype(BF16), ws_up.astype(BF16), ws_down.astype(BF16),
      g.reshape(1, d), b.reshape(1, d))


def _moe_ln(x, w_router, b_router, w_gate, w_up, w_down, ws_gate, ws_up, ws_down, g, b):
    n, d = x.shape
    ne = w_router.shape[1]
    meta, counts = _router(x, w_router, b_router)
    counts = counts.astype(jnp.int32)
    padded = (counts + EXPERT_ROWS - 1) // EXPERT_ROWS * EXPERT_ROWS
    pend = jnp.cumsum(padded)
    dest = _dest_rows(meta, pend - padded)
    n_blk = (n * TOP_K) // EXPERT_ROWS + ne
    blk_e = jnp.minimum(jnp.searchsorted(pend, jnp.arange(n_blk, dtype=jnp.int32) * EXPERT_ROWS, side="right"),
                        ne - 1).astype(jnp.int32)
    n_used = (pend[-1] // EXPERT_ROWS).astype(jnp.int32).reshape(1)
    xs = _dispatch(dest, x, n_blk * EXPERT_ROWS)
    ys = _expert_ffn(blk_e, n_used, xs, w_gate, w_up, w_down)
    return _combine_ln(dest, meta[TOP_K:2 * TOP_K].T, x, ys, ws_gate, ws_up, ws_down, g, b)


def _block_diag(w):
    nb, bs, _ = w.shape
    eye = jnp.eye(nb, dtype=w.dtype)
    return (eye[:, None, :, None] * w[:, :, None, :]).reshape(nb * bs, nb * bs)


def _even_mixer_ln(x, w_in, conv_w, conv_b, wa, ba, wx, bx, lam, lb, norm_g, w_out, g, b):
    bsz, t, d = x.shape
    xf = x.reshape(bsz * t, d)
    u = _proj(xf, w_in.astype(BF16)).reshape(bsz, t, -1)
    ya = _lru(u, conv_w, conv_b, _block_diag(wa).astype(BF16), ba, _block_diag(wx).astype(BF16), bx, lam)
    yb = _hgrn2(u, lb, norm_g)
    y = jnp.concatenate([ya, yb], axis=-1).reshape(bsz * t, -1)
    return _proj_ln(y, w_out.astype(BF16), xf, g, b).reshape(bsz, t, d)


def _odd_mixer_ln(x, w_in, pe_k, pe_v, wk1, wk2, wv1, wv2, w_out, g, b):
    bsz, t, d = x.shape
    xf = x.reshape(bsz * t, d)
    nq = NSA_HEADS * HEAD_DIM
    nm = MOBA_HEADS * HEAD_DIM
    g0 = nq + 6 * NSA_KV
    ng = 3 * NSA_HEADS
    perm = np.concatenate([np.arange(0, g0), np.arange(g0 + ng, g0 + ng + 3 * nm), np.arange(g0, g0 + ng)])
    w_perm = jnp.pad(w_in[:, perm], ((0, 0), (0, LANES - ng))).astype(BF16)
    u = _proj(xf, w_perm).reshape(bsz, t, -1)

    def grp(off):
        z = u[:, :, off:off + NSA_KV].reshape(bsz, t, NSA_GROUPS, HEAD_DIM)
        return z.transpose(0, 2, 1, 3)

    qh = u[:, :, 0:nq].reshape(bsz, t, NSA_GROUPS, NSA_HPG, HEAD_DIM).transpose(0, 2, 3, 1, 4).astype(BF16)
    kc, vc, ks, vs, kw, vw = (grp(nq + i * NSA_KV) for i in range(6))
    m0 = g0

    def mh(off):
        z = u[:, :, off:off + nm].reshape(bsz, t, MOBA_HEADS, HEAD_DIM)
        return z.transpose(0, 2, 1, 3).astype(BF16)

    mq, mk, mv = mh(m0), mh(m0 + nm), mh(m0 + 2 * nm)
    gates = u[:, :, m0 + 3 * nm:m0 + 3 * nm + ng]

    n_all = NSA_HEADS + MOBA_HEADS
    s_all = 2.0 ** (-8.0 * np.arange(1, n_all + 1) / n_all)
    s_nsa = jnp.asarray(s_all[0::2], F32)
    s_moba = jnp.asarray(s_all[1::2], F32)

    k_cmp, v_cmp = _nsa_compress(kc, vc, pe_k, pe_v, wk1, wk2, wv1, wv2)
    o_c, bsel = _nsa_select(s_nsa, qh, k_cmp, v_cmp)

    ns = t // SLC_LEN
    scale = HEAD_DIM ** -0.5
    pos = np.arange(t)
    pos_cols = np.stack([pos // POS_SPLIT, pos // POS_SPLIT, pos % POS_SPLIT, pos % POS_SPLIT], axis=-1)
    onehot = (pos[:, None] // SLC_LEN) == np.arange(ns)[None, :]
    pad_k = LANES - HEAD_DIM - 4 - ns

    def slope_cols(s):
        hi = s.astype(BF16).astype(F32)
        lo = s - hi
        return jnp.stack([POS_SPLIT * hi, POS_SPLIT * lo, hi, lo], axis=-1)

    def k_aug(k, extra):
        cols = np.concatenate([pos_cols, extra, np.zeros((t, LANES - HEAD_DIM - 4 - extra.shape[1]))], axis=-1)
        cols = jnp.broadcast_to(jnp.asarray(cols, BF16), k.shape[:2] + cols.shape)
        return jnp.concatenate([k, cols], axis=-1)

    sc = slope_cols(s_nsa).reshape(1, NSA_GROUPS, NSA_HPG, 1, 4)
    mask_bias = jnp.where(bsel > 0.5, 0.0, MASKED)[:, :, None]
    hshape = (bsz, NSA_GROUPS, NSA_HPG, t)
    qa = jnp.concatenate([qh.astype(F32) * scale, jnp.broadcast_to(sc, hshape + (4,)),
                          jnp.broadcast_to(mask_bias, hshape + (ns,)),
                          jnp.zeros(hshape + (pad_k,), F32)], axis=-1).astype(BF16)
    o_s = _flash("slc", qa, k_aug(ks.astype(BF16), onehot), vs.astype(BF16), tq=64)
    o_w = _flash("win", qa, k_aug(kw.astype(BF16), np.zeros((t, 0))), vw.astype(BF16), tq=128)
    mshape = (bsz, MOBA_HEADS, 1, t)
    qm = jnp.concatenate([mq[:, :, None].astype(F32) * scale,
                          jnp.broadcast_to(slope_cols(s_moba).reshape(1, MOBA_HEADS, 1, 1, 4), mshape + (4,)),
                          jnp.zeros(mshape + (LANES - HEAD_DIM - 4,), F32)], axis=-1).astype(BF16)
    o_m = _flash("moba", qm, k_aug(mk, np.zeros((t, 0))), mv, tq=MOBA_BLOCK)

    gt = jax.nn.sigmoid(gates).reshape(bsz, t, NSA_GROUPS, NSA_HPG, 3).transpose(0, 2, 3, 1, 4)
    o = gt[..., 0:1] * o_c + gt[..., 1:2] * o_s + gt[..., 2:3] * o_w
    yc = o.transpose(0, 3, 1, 2, 4).reshape(bsz, t, nq)
    yd = o_m[:, :, 0].transpose(0, 2, 1, 3).reshape(bsz, t, nm)
    y = jnp.concatenate([yc, yd], axis=-1).reshape(bsz * t, -1)
    return _proj_ln(y, w_out.astype(BF16), xf, g, b).reshape(bsz, t, d)


def kernel(x, even_w_in, lru_conv_w, lru_conv_b, lru_wa, lru_ba, lru_wx, lru_bx, lru_lambda,
           hg_lower_bound, hg_norm_g, even_w_out, odd_w_in, nsa_pe_k, nsa_pe_v, nsa_wk1, nsa_wk2,
           nsa_wv1, nsa_wv2, odd_w_out, ln_g, ln_b, w_router, b_router, w_gate, w_up, w_down,
           ws_gate, ws_up, ws_down):
    bsz, t, d = x.shape
    lb_all = jnp.cumsum(jax.nn.softmax(hg_lower_bound.astype(F32), axis=0), axis=0)
    for layer in range(DEPTH):
        li = layer // 2
        if layer % 2 == 0:
            x = _even_mixer_ln(x, even_w_in[li], lru_conv_w[li], lru_conv_b[li], lru_wa[li], lru_ba[li],
                               lru_wx[li], lru_bx[li], lru_lambda[li], lb_all[layer], hg_norm_g[li],
                               even_w_out[li], ln_g[layer, 0], ln_b[layer, 0])
        else:
            x = _odd_mixer_ln(x, odd_w_in[li], nsa_pe_k[li], nsa_pe_v[li], nsa_wk1[li], nsa_wk2[li],
                              nsa_wv1[li], nsa_wv2[li], odd_w_out[li], ln_g[layer, 0], ln_b[layer, 0])
        x = _moe_ln(x.reshape(bsz * t, d), w_router[layer], b_router[layer], w_gate[layer], w_up[layer],
                    w_down[layer], ws_gate[layer], ws_up[layer], ws_down[layer],
                    ln_g[layer, 1], ln_b[layer, 1]).reshape(bsz, t, d)
    return x
```

```python
import functools

import numpy as np
import jax
import jax.numpy as jnp
from jax import lax
from jax.experimental import pallas as pl
from jax.experimental.pallas import tpu as pltpu

F32 = jnp.float32
BF16 = jnp.bfloat16

D_MODEL = 1024
DEPTH = 2
LRU_WIDTH = 512
LRU_BLOCKS = 8
LRU_BLOCK = LRU_WIDTH // LRU_BLOCKS
CONV_W = 4
LRU_C = 8.0
HG_HEADS = 4
HG_DK = 128
HG_FDIM = HG_HEADS * HG_DK
HG_VDIM = HG_HEADS * HG_DK
HEAD_DIM = 64
NSA_HEADS = 8
NSA_GROUPS = 2
NSA_HPG = NSA_HEADS // NSA_GROUPS
NSA_KV = NSA_GROUPS * HEAD_DIM
CMP_LEN = 32
CMP_STRIDE = 16
CMP_HIDDEN = 128
SLC_LEN = 64
SLC_TOPN = 16
WIN = 512
FORCE_BONUS = 1e4
MOBA_HEADS = 8
MOBA_BLOCK = 256
MOBA_TOPK = 3
N_EXPERTS = 256
TOP_K = 8
N_GROUPS = 8
TOPK_GROUPS = 4
EXPERT_HIDDEN = 256
ROUTED_SCALE = 2.5
DN_ALPHA = (2 * DEPTH) ** 0.25
LN_EPS = 1e-5
NEG = -1e30
NEG_INF = float("-inf")

LANES = 128
SUBLANES = 8
VMEM_LIMIT = 48 * 1024 * 1024

HG_SUB = 16
EXPERT_ROWS = 256


def _cparams(sem):
    return pltpu.CompilerParams(dimension_semantics=sem, vmem_limit_bytes=VMEM_LIMIT)


def _dot(a, b):
    return jnp.dot(a.astype(BF16), b.astype(BF16), preferred_element_type=F32)


def _dot_nt(a, b):
    return lax.dot_general(a.astype(BF16), b.astype(BF16), (((1,), (1,)), ((), ())),
                           preferred_element_type=F32)


def _dot_tn(a, b):
    return lax.dot_general(a.astype(BF16), b.astype(BF16), (((0,), (0,)), ((), ())),
                           preferred_element_type=F32)


def _split2(x):
    hi = x.astype(BF16)
    lo = (x - hi.astype(F32)).astype(BF16)
    return hi, lo


def _split3(x):
    hi = x.astype(BF16)
    r = x - hi.astype(F32)
    mid = r.astype(BF16)
    lo = (r - mid.astype(F32)).astype(BF16)
    return hi, mid, lo


def _layer_norm(v, g, b):
    mu = jnp.mean(v, axis=-1, keepdims=True)
    d = v - mu
    var = jnp.mean(d * d, axis=-1, keepdims=True)
    return d * lax.rsqrt(var + LN_EPS) * g + b


def _sigmoid(x):
    return 1.0 / (1.0 + jnp.exp(-x))


def _silu(x):
    return x * _sigmoid(x)


def _proj_kernel(x_ref, w_ref, o_ref):
    o_ref[...] = jnp.dot(x_ref[...].astype(BF16), w_ref[...], preferred_element_type=F32)


def _proj(x, w, tm=256):
    n, k = x.shape
    m = w.shape[1]
    return pl.pallas_call(
        _proj_kernel,
        grid=(n // tm,),
        in_specs=[pl.BlockSpec((tm, k), lambda i: (i, 0)),
                  pl.BlockSpec((k, m), lambda i: (0, 0))],
        out_specs=pl.BlockSpec((tm, m), lambda i: (i, 0)),
        out_shape=jax.ShapeDtypeStruct((n, m), F32),
        compiler_params=_cparams(("parallel",)),
        name="proj",
    )(x, w)


def _proj_ln_kernel(y_ref, w_ref, x_ref, g_ref, b_ref, o_ref):
    mix = jnp.dot(y_ref[...].astype(BF16), w_ref[...], preferred_element_type=F32)
    o_ref[...] = _layer_norm(DN_ALPHA * x_ref[...] + mix, g_ref[...], b_ref[...])


def _proj_ln(y, w, xres, g, b, tm=256):
    n, k = y.shape
    d = w.shape[1]
    return pl.pallas_call(
        _proj_ln_kernel,
        grid=(n // tm,),
        in_specs=[pl.BlockSpec((tm, k), lambda i: (i, 0)),
                  pl.BlockSpec((k, d), lambda i: (0, 0)),
                  pl.BlockSpec((tm, d), lambda i: (i, 0)),
                  pl.BlockSpec((1, d), lambda i: (0, 0)),
                  pl.BlockSpec((1, d), lambda i: (0, 0))],
        out_specs=pl.BlockSpec((tm, d), lambda i: (i, 0)),
        out_shape=jax.ShapeDtypeStruct((n, d), F32),
        compiler_params=_cparams(("parallel",)),
        name="proj_ln",
    )(y, w, xres, g.reshape(1, d), b.reshape(1, d))


def _lru_kernel(xb_ref, gb_ref, cw_ref, cb_ref, wa_ref, ba_ref, wx_ref, bx_ref, lam_ref,
                o_ref, xs_ref, a_ref, b_ref, h_ref):
    ti = pl.program_id(1)
    tc = xb_ref.shape[1]
    c = xb_ref.shape[2]

    @pl.when(ti == 0)
    def _():
        xs_ref[0:SUBLANES, :] = jnp.zeros((SUBLANES, c), F32)
        h_ref[...] = jnp.zeros_like(h_ref)

    xs_ref[SUBLANES:SUBLANES + tc, :] = xb_ref[0]
    xc = cb_ref[...]
    for w in range(CONV_W):
        xc = xc + cw_ref[w:w + 1, :] * xs_ref[pl.ds(SUBLANES - (CONV_W - 1) + w, tc), :]
    xs_ref[0:SUBLANES, :] = xs_ref[tc:tc + SUBLANES, :]

    r = _sigmoid(_dot(xc, wa_ref[...]) + ba_ref[...])
    i = _sigmoid(_dot(xc, wx_ref[...]) + bx_ref[...])
    nl = -lam_ref[...]
    softplus = jnp.maximum(nl, 0.0) + jnp.log(1.0 + jnp.exp(-jnp.abs(nl)))
    log_a = (-LRU_C * r) * softplus
    a = jnp.exp(log_a)
    mult = jnp.sqrt(1.0 - jnp.exp(2.0 * log_a))
    trow = ti * tc + lax.broadcasted_iota(jnp.int32, (tc, c), 0)
    mult = jnp.where(trow == 0, 1.0, mult)
    a_ref[...] = a
    b_ref[...] = mult * (i * xc)

    row = lax.broadcasted_iota(jnp.int32, (SUBLANES, c), 0)

    def body(g, carry):
        r0 = pl.multiple_of(g * SUBLANES, SUBLANES)
        av = a_ref[pl.ds(r0, SUBLANES), :]
        bv = b_ref[pl.ds(r0, SUBLANES), :]
        for d in (1, 2, 4):
            a_sh = pltpu.roll(av, d, 0)
            b_sh = pltpu.roll(bv, d, 0)
            m = row >= d
            bv = jnp.where(m, av * b_sh + bv, bv)
            av = jnp.where(m, av * a_sh, av)
        h = av * carry + bv
        b_ref[pl.ds(r0, SUBLANES), :] = h
        return h[SUBLANES - 1:SUBLANES, :]

    h_ref[...] = lax.fori_loop(0, tc // SUBLANES, body, h_ref[...])
    o_ref[0] = b_ref[...] * jax.nn.gelu(gb_ref[0])


def _lru(u3, conv_w, conv_b, wa_bd, ba, wx_bd, bx, lam, tc=256):
    b, t, _ = u3.shape
    c = LRU_WIDTH
    row = lambda v: v.reshape(1, c)
    full = lambda shp: pl.BlockSpec(shp, lambda bi, ti: (0, 0))
    return pl.pallas_call(
        _lru_kernel,
        grid=(b, t // tc),
        in_specs=[pl.BlockSpec((1, tc, c), lambda bi, ti: (bi, ti, 0)),
                  pl.BlockSpec((1, tc, c), lambda bi, ti: (bi, ti, 1)),
                  full((CONV_W, c)), full((1, c)), full((c, c)), full((1, c)),
                  full((c, c)), full((1, c)), full((1, c))],
        out_specs=pl.BlockSpec((1, tc, c), lambda bi, ti: (bi, ti, 0)),
        out_shape=jax.ShapeDtypeStruct((b, t, c), F32),
        scratch_shapes=[pltpu.VMEM((tc + SUBLANES, c), F32), pltpu.VMEM((tc, c), F32),
                        pltpu.VMEM((tc, c), F32), pltpu.VMEM((1, c), F32)],
        compiler_params=_cparams(("parallel", "arbitrary")),
        name="rg_lru",
    )(u3, u3, conv_w, row(conv_b), wa_bd, row(ba), wx_bd, row(bx), row(lam))


def _hgrn_kernel(q_ref, f_ref, v_ref, g_ref, lb_ref, ng_ref, tri_ref, ones_ref, o_ref,
                 st_ref, q_s, k_s, c_s, qe_s, kd_s, dl_s, o_s):
    ti = pl.program_id(2)
    tc = q_ref.shape[1]
    dk = q_ref.shape[2]

    @pl.when(ti == 0)
    def _():
        st_ref[...] = jnp.zeros_like(st_ref)

    lbv = lb_ref[...]
    q = _silu(q_ref[0])
    f = lbv + (1.0 - lbv) * _sigmoid(f_ref[0])
    kk = 1.0 - f
    logf = jnp.log(f)
    parts = _split3(logf)
    tri = tri_ref[...]
    ones = ones_ref[...]
    cum = sum(jnp.dot(tri, p, preferred_element_type=F32) for p in parts)
    last = sum(jnp.dot(ones, p, preferred_element_type=F32) for p in parts)
    q_s[...] = q
    k_s[...] = kk
    c_s[...] = cum
    qe_s[...] = q * jnp.exp(cum)
    kd_s[...] = kk * jnp.exp(last - cum)
    dl_s[...] = jnp.exp(last)

    s_idx = lax.broadcasted_iota(jnp.int32, (HG_SUB, dk), 0)

    def chunk(ci, _):
        r0 = pl.multiple_of(ci * HG_SUB, HG_SUB)
        rows = pl.ds(r0, HG_SUB)
        st = st_ref[...]
        vc = v_ref[0, rows, :]
        qc = q_s[rows, :]
        kc = k_s[rows, :]
        cc = c_s[rows, :]
        o = _dot_nt(qe_s[rows, :], st)
        for t in range(HG_SUB):
            d = cc[t:t + 1, :] - cc
            dec = jnp.exp(jnp.where(s_idx <= t, d, NEG))
            w = (qc[t:t + 1, :] * kc) * dec
            a_t = jnp.sum(w, axis=1, keepdims=True)
            o_t = jnp.sum(a_t * vc, axis=0, keepdims=True)
            o = o + jnp.where(s_idx == t, o_t, 0.0)
        o_s[rows, :] = o
        st_ref[...] = st * dl_s[pl.ds(r0, 1), :] + _dot_tn(vc, kd_s[rows, :])
        return 0

    lax.fori_loop(0, tc // HG_SUB, chunk, 0, unroll=4)
    o = o_s[...]
    o = o * lax.rsqrt(jnp.mean(o * o, axis=-1, keepdims=True) + 1e-6) * ng_ref[...]
    o_ref[0] = o * _sigmoid(g_ref[0])


def _hgrn2(u3, lb, norm_g, tc=256):
    b, t, _ = u3.shape
    dk = HG_DK
    base = (2 * LRU_WIDTH) // dk
    idx = np.arange(tc)
    same = (idx[:, None] // HG_SUB) == (idx[None, :] // HG_SUB)
    tri = jnp.asarray(same & (idx[None, :] <= idx[:, None]), BF16)
    ones = jnp.asarray(same, BF16)

    def col(off):
        return pl.BlockSpec((1, tc, dk), lambda bi, hi, ti: (bi, ti, base + off * HG_HEADS + hi))

    vec = pl.BlockSpec((1, dk), lambda bi, hi, ti: (0, hi))
    cst = pl.BlockSpec((tc, tc), lambda bi, hi, ti: (0, 0))
    return pl.pallas_call(
        _hgrn_kernel,
        grid=(b, HG_HEADS, t // tc),
        in_specs=[col(0), col(1), col(2), col(3), vec, vec, cst, cst],
        out_specs=pl.BlockSpec((1, tc, dk), lambda bi, hi, ti: (bi, ti, hi)),
        out_shape=jax.ShapeDtypeStruct((b, t, HG_VDIM), F32),
        scratch_shapes=[pltpu.VMEM((dk, dk), F32)] + [pltpu.VMEM((tc, dk), F32)] * 7,
        compiler_params=_cparams(("parallel", "parallel", "arbitrary")),
        name="hgrn2",
    )(u3, u3, u3, u3, lb.reshape(1, HG_FDIM), norm_g.reshape(1, HG_VDIM), tri, ones)


def _nsa_cmp_kv_kernel(zk_ref, zv_ref, pek_ref, pev_ref, wk1_ref, wk2_ref, wv1_ref, wv2_ref,
                       ok_ref, ov_ref):
    half = zk_ref.shape[3]
    nrow = zk_ref.shape[2]

    def compress(z, pe_ref, w1_ref, w2_ref):
        lo = _dot(z + pe_ref[0:1, :], w1_ref[0:half, :])
        hi = _dot(z + pe_ref[1:2, :], w1_ref[half:2 * half, :])
        pre = lo + pltpu.roll(hi, nrow - 1, 0)
        return _dot(jax.nn.gelu(pre), w2_ref[...])

    ok_ref[0, 0] = compress(zk_ref[0, 0], pek_ref, wk1_ref, wk2_ref)
    ov_ref[0, 0] = compress(zv_ref[0, 0], pev_ref, wv1_ref, wv2_ref)


def _nsa_compress(kc, vc, pe_k, pe_v, wk1, wk2, wv1, wv2):
    b, g, t, dh = kc.shape
    nrow = t // CMP_STRIDE
    half = CMP_STRIDE * dh
    zk = kc.reshape(b, g, nrow, half)
    zv = vc.reshape(b, g, nrow, half)
    zspec = pl.BlockSpec((1, 1, nrow, half), lambda bi, gi: (bi, gi, 0, 0))
    full = lambda shp: pl.BlockSpec(shp, lambda bi, gi: (0, 0))
    ospec = pl.BlockSpec((1, 1, nrow, dh), lambda bi, gi: (bi, gi, 0, 0))
    oshape = jax.ShapeDtypeStruct((b, g, nrow, dh), F32)
    return pl.pallas_call(
        _nsa_cmp_kv_kernel,
        grid=(b, g),
        in_specs=[zspec, zspec, full((2, half)), full((2, half)),
                  full((2 * half, CMP_HIDDEN)), full((CMP_HIDDEN, dh)),
                  full((2 * half, CMP_HIDDEN)), full((CMP_HIDDEN, dh))],
        out_specs=[ospec, ospec],
        out_shape=[oshape, oshape],
        compiler_params=_cparams(("parallel", "parallel")),
        name="nsa_compress",
    )(zk, zv, pe_k.reshape(2, half), pe_v.reshape(2, half),
      wk1.astype(BF16), wk2.astype(BF16), wv1.astype(BF16), wv2.astype(BF16))


def _nsa_sel_kernel(slope_ref, q_ref, kc_ref, vc_ref, ov_ref, o_ref, sel_ref):
    gi = pl.program_id(0) % NSA_GROUPS
    qi = pl.program_id(1)
    hpg = q_ref.shape[2]
    tq = q_ref.shape[3]
    dh = q_ref.shape[4]
    nc = kc_ref.shape[2]
    ns = sel_ref.shape[3]
    scale = dh ** -0.5

    q = q_ref[0, 0].reshape(hpg * tq, dh)
    khi, klo = _split2(kc_ref[0, 0])
    s = (_dot_nt(q, khi) + _dot_nt(q, klo)) * scale
    tpos = qi * tq + lax.broadcasted_iota(jnp.int32, (tq, nc), 0)
    cend = lax.broadcasted_iota(jnp.int32, (tq, nc), 1) * CMP_STRIDE + (CMP_LEN - 1)
    d_c = (tpos - cend).astype(F32)
    valid = d_c >= 0.0
    ps = []
    for i in range(hpg):
        si = s[i * tq:(i + 1) * tq, :] - slope_ref[gi * hpg + i] * d_c
        si = jnp.where(valid, si, NEG)
        m = jnp.max(si, axis=-1, keepdims=True)
        p = jnp.where(valid, jnp.exp(si - m), 0.0)
        ps.append(p / jnp.maximum(jnp.sum(p, axis=-1, keepdims=True), 1e-30))
    pall = jnp.concatenate(ps, axis=0)
    o_ref[0, 0] = _dot(pall, vc_ref[0, 0]).reshape(hpg, tq, dh)

    psum = ps[0]
    for i in range(1, hpg):
        psum = psum + ps[i]
    phi, plo = _split2(psum)
    ovl = ov_ref[...]
    imp = jnp.dot(phi, ovl, preferred_element_type=F32) + jnp.dot(plo, ovl, preferred_element_type=F32)

    j = lax.broadcasted_iota(jnp.int32, (tq, ns), 1)
    own = (qi * tq + lax.broadcasted_iota(jnp.int32, (tq, ns), 0)) // SLC_LEN
    forced = (j == 0) | (j == own) | (j == own - 1)
    imp = jnp.where(j <= own, imp + FORCE_BONUS * forced.astype(F32), NEG)
    rank = jnp.zeros((tq, ns), F32)
    for jp in range(ns):
        col = imp[:, jp:jp + 1]
        beats = (col > imp) | ((col == imp) & (j > jp))
        rank = rank + beats.astype(F32)
    sel_ref[0, 0] = (rank < float(min(SLC_TOPN, ns))).astype(F32)


def _nsa_select(slopes, qh, k_cmp, v_cmp, tq=256):
    b, g, hpg, t, dh = qh.shape
    nc = k_cmp.shape[2]
    ns = t // SLC_LEN
    ci = np.arange(nc)[:, None]
    cj = np.arange(ns)[None, :]
    ovl = ((CMP_STRIDE * ci < SLC_LEN * (cj + 1)) & (CMP_STRIDE * ci + CMP_LEN > SLC_LEN * cj))
    ovl = ovl & (ci < (t - CMP_LEN) // CMP_STRIDE + 1)
    grid_spec = pltpu.PrefetchScalarGridSpec(
        num_scalar_prefetch=1,
        grid=(b * g, t // tq),
        in_specs=[pl.BlockSpec((1, 1, hpg, tq, dh), lambda bg, qi, s: (bg // g, bg % g, 0, qi, 0)),
                  pl.BlockSpec((1, 1, nc, dh), lambda bg, qi, s: (bg // g, bg % g, 0, 0)),
                  pl.BlockSpec((1, 1, nc, dh), lambda bg, qi, s: (bg // g, bg % g, 0, 0)),
                  pl.BlockSpec((nc, ns), lambda bg, qi, s: (0, 0))],
        out_specs=[pl.BlockSpec((1, 1, hpg, tq, dh), lambda bg, qi, s: (bg // g, bg % g, 0, qi, 0)),
                   pl.BlockSpec((1, 1, tq, ns), lambda bg, qi, s: (bg // g, bg % g, qi, 0))],
    )
    return pl.pallas_call(
        _nsa_sel_kernel,
        grid_spec=grid_spec,
        out_shape=[jax.ShapeDtypeStruct((b, g, hpg, t, dh), F32),
                   jax.ShapeDtypeStruct((b, g, t, ns), F32)],
        compiler_params=_cparams(("parallel", "parallel")),
        name="nsa_select",
    )(slopes, qh, k_cmp, v_cmp, jnp.asarray(ovl, BF16))


POS_SPLIT = 64
MASKED = -2e30


MASK_COL0 = HEAD_DIM + 4
KEY_STEP = 512


def _flash_kernel(q_ref, k_ref, v_ref, o_ref, *, mode):
    qi = pl.program_id(1)
    hpg = q_ref.shape[2]
    tq = q_ref.shape[3]
    da = q_ref.shape[4]
    dh = v_ref.shape[3]
    t_all = k_ref.shape[2]
    rows = hpg * tq
    q0 = qi * tq
    q = q_ref[0, 0].reshape(rows, da)

    if mode == "moba":
        nb = t_all // MOBA_BLOCK
        kf = k_ref[0, 0].astype(F32).reshape(nb, MOBA_BLOCK, da)
        kmean = jnp.sum(kf, axis=1) / float(MOBA_BLOCK)
        kmean = jnp.where(lax.broadcasted_iota(jnp.int32, (nb, da), 1) < dh, kmean, 0.0)
        mhi, mlo = _split2(kmean)
        gate = _dot_nt(mhi, q) + _dot_nt(mlo, q)
        jb = lax.broadcasted_iota(jnp.int32, (nb, tq), 0)
        ob = q0 // MOBA_BLOCK
        gate = jnp.where(jb < ob, gate, NEG)
        rank = jnp.zeros((nb, tq), F32)
        for jp in range(nb):
            row = gate[jp:jp + 1, :]
            beats = (row > gate) | ((row == gate) & (jb > jp))
            rank = rank + beats.astype(F32)
        n_sel = float(min(MOBA_TOPK, nb - 1))
        bsel = ((rank < n_sel) & (jb < ob)) | (jb == ob)
        bias = jnp.where(bsel, 0.0, MASKED)
        place = (lax.broadcasted_iota(jnp.int32, (nb, da), 1)
                 == lax.broadcasted_iota(jnp.int32, (nb, da), 0) + MASK_COL0)
        q = q + _dot_tn(bias, place.astype(F32)).astype(BF16)

    def attend(kstart, klen, diag, off):
        s = _dot_nt(q, k_ref[0, 0, pl.ds(kstart, klen), :])
        rel = lax.broadcasted_iota(jnp.int32, (tq, diag), 0) - lax.broadcasted_iota(jnp.int32, (tq, diag), 1)
        valid = rel >= off
        if mode == "win":
            valid = valid & (rel < off + WIN)
        if hpg > 1:
            valid = jnp.concatenate([valid] * hpg, axis=0)
        sd = jnp.where(valid, s[:, klen - diag:], MASKED)
        s = sd if diag == klen else jnp.concatenate([s[:, :klen - diag], sd], axis=1)
        m = jnp.maximum(jnp.max(s, axis=-1, keepdims=True), NEG)
        p = jnp.exp(s - m)
        l = jnp.sum(p, axis=-1, keepdims=True)
        acc = _dot(p, v_ref[0, 0, pl.ds(kstart, klen), :])
        o_ref[0, 0] = (acc / jnp.maximum(l, 1e-30)).reshape(hpg, tq, dh)

    if mode == "win":
        kstart = pl.multiple_of(jnp.maximum(q0 - WIN, 0), LANES)
        attend(kstart, WIN + tq, WIN + tq, kstart - q0)
    else:
        step = MOBA_BLOCK if mode == "moba" else KEY_STEP
        var = (q0 + tq - 1) // step
        for v in range(t_all // step):
            @pl.when(var == v)
            def _(v=v):
                attend(0, step * (v + 1), step, step * v - q0)


def _flash(mode, qa, ka, vh, tq):
    b, n_kvh, hpg, t, da = qa.shape
    dh = vh.shape[3]
    assert mode != "moba" or tq == MOBA_BLOCK
    return pl.pallas_call(
        functools.partial(_flash_kernel, mode=mode),
        grid=(b * n_kvh, t // tq),
        in_specs=[pl.BlockSpec((1, 1, hpg, tq, da), lambda bh, qi: (bh // n_kvh, bh % n_kvh, 0, qi, 0)),
                  pl.BlockSpec((1, 1, t, da), lambda bh, qi: (bh // n_kvh, bh % n_kvh, 0, 0)),
                  pl.BlockSpec((1, 1, t, dh), lambda bh, qi: (bh // n_kvh, bh % n_kvh, 0, 0))],
        out_specs=pl.BlockSpec((1, 1, hpg, tq, dh), lambda bh, qi: (bh // n_kvh, bh % n_kvh, 0, qi, 0)),
        out_shape=jax.ShapeDtypeStruct((b, n_kvh, hpg, t, dh), F32),
        compiler_params=_cparams(("parallel", "parallel")),
        name="flash_" + mode,
    )(qa, ka, vh)


def _router_kernel(x_ref, whi_ref, wlo_ref, b_ref, triu_ref, meta_ref, cnt_ref, carry_ref):
    step = pl.program_id(0)
    tm = x_ref.shape[0]
    ne = whi_ref.shape[0]
    gsz = ne // N_GROUPS

    @pl.when(step == 0)
    def _():
        carry_ref[...] = jnp.zeros_like(carry_ref)

    xhi, xlo = _split2(x_ref[...])
    whi = whi_ref[...]
    logits = _dot_nt(whi, xhi) + _dot_nt(wlo_ref[...], xhi) + _dot_nt(whi, xlo)
    scores = _sigmoid(logits)
    biased = scores + b_ref[:, 0:1]

    v3 = biased.reshape(N_GROUPS, gsz, tm)
    i3 = lax.broadcasted_iota(jnp.int32, (N_GROUPS, gsz, tm), 1).astype(F32)
    m1 = jnp.max(v3, axis=1, keepdims=True)
    idx1 = jnp.min(jnp.where(v3 == m1, i3, float(gsz)), axis=1, keepdims=True)
    m2 = jnp.max(jnp.where(i3 == idx1, NEG_INF, v3), axis=1, keepdims=True)
    gs = (m1 + m2).reshape(N_GROUPS, tm)
    gi = lax.broadcasted_iota(jnp.int32, (N_GROUPS, tm), 0).astype(F32)
    gmask = jnp.zeros((N_GROUPS, tm), F32)
    for _ in range(TOPK_GROUPS):
        m = jnp.max(gs, axis=0, keepdims=True)
        pick = jnp.min(jnp.where(gs == m, gi, float(N_GROUPS)), axis=0, keepdims=True)
        hit = gi == pick
        gmask = jnp.where(hit, 1.0, gmask)
        gs = jnp.where(hit, NEG_INF, gs)
    emask = jnp.broadcast_to(gmask.reshape(N_GROUPS, 1, tm), (N_GROUPS, gsz, tm)).reshape(ne, tm)
    cur = jnp.where(emask > 0.5, biased, NEG)

    ei = lax.broadcasted_iota(jnp.int32, (ne, tm), 0).astype(F32)
    selm = jnp.zeros((ne, tm), F32)
    idxs, svals = [], []
    for _ in range(TOP_K):
        m = jnp.max(cur, axis=0, keepdims=True)
        idx = jnp.min(jnp.where(cur == m, ei, float(ne)), axis=0, keepdims=True)
        hit = ei == idx
        svals.append(jnp.sum(jnp.where(hit, scores, 0.0), axis=0, keepdims=True))
        idxs.append(idx)
        cur = jnp.where(hit, NEG_INF, cur)
        selm = jnp.where(hit, 1.0, selm)

    carry = carry_ref[:, 0:1]
    rank_full = jnp.dot(selm.astype(BF16), triu_ref[...], preferred_element_type=F32) + carry
    ssum = svals[0]
    for k in range(1, TOP_K):
        ssum = ssum + svals[k]
    for k in range(TOP_K):
        meta_ref[k:k + 1, :] = idxs[k]
        meta_ref[TOP_K + k:TOP_K + k + 1, :] = svals[k] / ssum * ROUTED_SCALE
        meta_ref[2 * TOP_K + k:2 * TOP_K + k + 1, :] = jnp.sum(
            jnp.where(ei == idxs[k], rank_full, 0.0), axis=0, keepdims=True)
    new_carry = carry + jnp.sum(selm, axis=1, keepdims=True)
    carry_ref[...] = jnp.broadcast_to(new_carry, carry_ref.shape)
    cnt_ref[...] = jnp.broadcast_to(new_carry, cnt_ref.shape)


def _router(x, w_router, b_router, tm=256):
    n, d = x.shape
    ne = w_router.shape[1]
    wt = w_router.T
    whi = wt.astype(BF16)
    wlo = (wt - whi.astype(F32)).astype(BF16)
    triu = jnp.asarray(np.arange(tm)[:, None] < np.arange(tm)[None, :], BF16)
    meta, cnt = pl.pallas_call(
        _router_kernel,
        grid=(n // tm,),
        in_specs=[pl.BlockSpec((tm, d), lambda i: (i, 0)),
                  pl.BlockSpec((ne, d), lambda i: (0, 0)),
                  pl.BlockSpec((ne, d), lambda i: (0, 0)),
                  pl.BlockSpec((ne, LANES), lambda i: (0, 0)),
                  pl.BlockSpec((tm, tm), lambda i: (0, 0))],
        out_specs=[pl.BlockSpec((3 * TOP_K, tm), lambda i: (0, i)),
                   pl.BlockSpec((ne, LANES), lambda i: (0, 0))],
        out_shape=[jax.ShapeDtypeStruct((3 * TOP_K, n), F32),
                   jax.ShapeDtypeStruct((ne, LANES), F32)],
        scratch_shapes=[pltpu.VMEM((ne, LANES), F32)],
        compiler_params=_cparams(("arbitrary",)),
        name="moe_router",
    )(x, whi, wlo, jnp.broadcast_to(b_router.reshape(ne, 1), (ne, LANES)), triu)
    return meta, cnt[:, 0]


def _expert_kernel(be_ref, nu_ref, xs_ref, wg_ref, wu_ref, wd_ref, o_ref):
    bi = pl.program_id(0)

    @pl.when(bi < nu_ref[0])
    def _():
        xa, xb = _unpack_bf16_pair(xs_ref[...])
        half = xa.shape[1]

        def xw(w_ref):
            return (jnp.dot(xa, w_ref[0, 0:half, :].astype(BF16), preferred_element_type=F32)
                    + jnp.dot(xb, w_ref[0, half:2 * half, :].astype(BF16), preferred_element_type=F32))

        h = _silu(xw(wg_ref)) * xw(wu_ref)
        o_ref[...] = _dot(h, wd_ref[0])

    @pl.when(bi >= nu_ref[0])
    def _():
        o_ref[...] = jnp.zeros_like(o_ref)


def _expert_ffn(blk_e, n_used, xs, w_gate, w_up, w_down):
    n_rows = xs.shape[0]
    d = w_gate.shape[1]
    hdim = w_gate.shape[2]
    n_blk = n_rows // EXPERT_ROWS
    grid_spec = pltpu.PrefetchScalarGridSpec(
        num_scalar_prefetch=2,
        grid=(n_blk,),
        in_specs=[pl.BlockSpec((EXPERT_ROWS, d // 2), lambda i, be, nu: (i, 0)),
                  pl.BlockSpec((1, d, hdim), lambda i, be, nu: (be[i], 0, 0)),
                  pl.BlockSpec((1, d, hdim), lambda i, be, nu: (be[i], 0, 0)),
                  pl.BlockSpec((1, hdim, d), lambda i, be, nu: (be[i], 0, 0))],
        out_specs=pl.BlockSpec((EXPERT_ROWS, d), lambda i, be, nu: (i, 0)),
    )
    return pl.pallas_call(
        _expert_kernel,
        grid_spec=grid_spec,
        out_shape=jax.ShapeDtypeStruct((n_rows, d), F32),
        compiler_params=_cparams(("arbitrary",)),
        name="moe_experts",
    )(blk_e, n_used, xs, w_gate, w_up, w_down)


def _dest_kernel(meta_ref, pstart_ref, dest_ref):
    ne = pstart_ref.shape[0]
    tm = meta_ref.shape[1]
    ei = lax.broadcasted_iota(jnp.int32, (ne, tm), 0).astype(F32)
    ps = pstart_ref[:, 0:1]
    for k in range(TOP_K):
        base = jnp.sum(jnp.where(ei == meta_ref[k:k + 1, :], ps, 0.0), axis=0, keepdims=True)
        dest_ref[k:k + 1, :] = (base + meta_ref[2 * TOP_K + k:2 * TOP_K + k + 1, :]).astype(jnp.int32)


def _dest_rows(meta, pstart, tm=512):
    n = meta.shape[1]
    ne = pstart.shape[0]
    return pl.pallas_call(
        _dest_kernel,
        grid=(n // tm,),
        in_specs=[pl.BlockSpec((3 * TOP_K, tm), lambda i: (0, i)),
                  pl.BlockSpec((ne, LANES), lambda i: (0, 0))],
        out_specs=pl.BlockSpec((TOP_K, tm), lambda i: (0, i)),
        out_shape=jax.ShapeDtypeStruct((TOP_K, n), jnp.int32),
        compiler_params=_cparams(("parallel",)),
        name="moe_dest",
    )(meta, jnp.broadcast_to(pstart.astype(F32).reshape(ne, 1), (ne, LANES)))


def _pack_bf16_pair(x):
    half = x.shape[1] // 2
    hi = pltpu.bitcast(x[:, 0:half].astype(BF16).astype(F32), jnp.uint32)
    lo = pltpu.bitcast(x[:, half:2 * half].astype(BF16).astype(F32), jnp.uint32)
    return hi | (lo >> 16)


def _unpack_bf16_pair(p):
    hi = pltpu.bitcast(p & jnp.uint32(0xFFFF0000), F32).astype(BF16)
    lo = pltpu.bitcast(p << 16, F32).astype(BF16)
    return hi, lo


def _dispatch_kernel(dest_ref, x_ref, xs_in_ref, xs_ref, pk_ref, sem):
    del xs_in_ref
    tm = x_ref.shape[0]
    pk_ref[...] = _pack_bf16_pair(x_ref[...])

    def row_copy(t, k):
        return pltpu.make_async_copy(pk_ref.at[pl.ds(t, 1), :], xs_ref.at[pl.ds(dest_ref[k, t], 1), :], sem)

    def body(t, _):
        for k in range(TOP_K):
            row_copy(t, k).start()
        return 0

    lax.fori_loop(0, tm, body, 0)
    done = xs_ref.at[pl.ds(0, TOP_K * tm), :]
    pltpu.make_async_copy(done, done, sem).wait()


def _dispatch(dest, x, n_rows, tm=256):
    n, d = x.shape
    return pl.pallas_call(
        _dispatch_kernel,
        grid=(n // tm,),
        in_specs=[pl.BlockSpec((TOP_K, tm), lambda i: (0, i), memory_space=pltpu.SMEM),
                  pl.BlockSpec((tm, d), lambda i: (i, 0)),
                  pl.BlockSpec(memory_space=pl.ANY)],
        out_specs=pl.BlockSpec(memory_space=pl.ANY),
        out_shape=jax.ShapeDtypeStruct((n_rows, d // 2), jnp.uint32),
        scratch_shapes=[pltpu.VMEM((tm, d // 2), jnp.uint32), pltpu.SemaphoreType.DMA(())],
        input_output_aliases={2: 0},
        compiler_params=_cparams(("arbitrary",)),
        name="moe_dispatch",
    )(dest, x, jnp.zeros((n_rows, d // 2), jnp.uint32))


def _combine_ln_kernel(dest_ref, dest_next_ref, w_ref, x_ref, ys_ref, wg_ref, wu_ref, wd_ref, g_ref, b_ref,
                       o_ref, buf, sem):
    step = pl.program_id(0)
    tm = x_ref.shape[0]
    slot = step % 2

    def gather_rows(idx_ref, to_slot):
        def body(t, _):
            for k in range(TOP_K):
                pltpu.make_async_copy(ys_ref.at[pl.ds(idx_ref[k, t], 1), :],
                                      buf.at[to_slot, k, pl.ds(t, 1), :], sem.at[to_slot]).start()
            return 0

        lax.fori_loop(0, tm, body, 0)

    @pl.when(step == 0)
    def _():
        gather_rows(dest_ref, 0)

    @pl.when(step + 1 < pl.num_programs(0))
    def _():
        gather_rows(dest_next_ref, 1 - slot)

    x = x_ref[...]
    xb = x.astype(BF16)
    h = _silu(jnp.dot(xb, wg_ref[...], preferred_element_type=F32)) * jnp.dot(
        xb, wu_ref[...], preferred_element_type=F32)
    ff = jnp.dot(h.astype(BF16), wd_ref[...], preferred_element_type=F32)
    pltpu.make_async_copy(buf.at[slot], buf.at[slot], sem.at[slot]).wait()
    for k in range(TOP_K):
        ff = ff + buf[slot, k] * w_ref[:, k:k + 1]
    o_ref[...] = _layer_norm(DN_ALPHA * x + ff, g_ref[...], b_ref[...])


def _combine_ln(dest, wts, x, ys, ws_gate, ws_up, ws_down, g, b, tm=256):
    n, d = x.shape
    hdim = ws_gate.shape[1]
    row = pl.BlockSpec((tm, d), lambda i: (i, 0))
    last = n // tm - 1
    return pl.pallas_call(
        _combine_ln_kernel,
        grid=(n // tm,),
        in_specs=[pl.BlockSpec((TOP_K, tm), lambda i: (0, i), memory_space=pltpu.SMEM),
                  pl.BlockSpec((TOP_K, tm), lambda i: (0, jnp.minimum(i + 1, last)), memory_space=pltpu.SMEM),
                  pl.BlockSpec((tm, TOP_K), lambda i: (i, 0)),
                  row,
                  pl.BlockSpec(memory_space=pl.ANY),
                  pl.BlockSpec((d, hdim), lambda i: (0, 0)),
                  pl.BlockSpec((d, hdim), lambda i: (0, 0)),
                  pl.BlockSpec((hdim, d), lambda i: (0, 0)),
                  pl.BlockSpec((1, d), lambda i: (0, 0)),
                  pl.BlockSpec((1, d), lambda i: (0, 0))],
        out_specs=row,
        out_shape=jax.ShapeDtypeStruct((n, d), F32),
        scratch_shapes=[pltpu.VMEM((2, TOP_K, tm, d), F32), pltpu.SemaphoreType.DMA((2,))],
        compiler_params=_cparams(("arbitrary",)),
        name="moe_combine_ln",
    )(dest, dest, wts, x, ys, ws_gate.astype(BF16), ws_up.astype(BF16), ws_down.astype(BF16),
      g.reshape(1, d), b.reshape(1, d))


def _moe_ln(x, w_router, b_router, w_gate, w_up, w_down, ws_gate, ws_up, ws_down, g, b):
    n, d = x.shape
    ne = w_router.shape[1]
    meta, counts = _router(x, w_router, b_router)
    counts = counts.astype(jnp.int32)
    padded = (counts + EXPERT_ROWS - 1) // EXPERT_ROWS * EXPERT_ROWS
    pend = jnp.cumsum(padded)
    dest = _dest_rows(meta, pend - padded)
    n_blk = (n * TOP_K) // EXPERT_ROWS + ne
    blk_e = jnp.minimum(jnp.searchsorted(pend, jnp.arange(n_blk, dtype=jnp.int32) * EXPERT_ROWS, side="right"),
                        ne - 1).astype(jnp.int32)
    n_used = (pend[-1] // EXPERT_ROWS).astype(jnp.int32).reshape(1)
    xs = _dispatch(dest, x, n_blk * EXPERT_ROWS)
    ys = _expert_ffn(blk_e, n_used, xs, w_gate, w_up, w_down)
    return _combine_ln(dest, meta[TOP_K:2 * TOP_K].T, x, ys, ws_gate, ws_up, ws_down, g, b)


def _block_diag(w):
    nb, bs, _ = w.shape
    eye = jnp.eye(nb, dtype=w.dtype)
    return (eye[:, None, :, None] * w[:, :, None, :]).reshape(nb * bs, nb * bs)


def _even_mixer_ln(x, w_in, conv_w, conv_b, wa, ba, wx, bx, lam, lb, norm_g, w_out, g, b):
    bsz, t, d = x.shape
    xf = x.reshape(bsz * t, d)
    u = _proj(xf, w_in.astype(BF16)).reshape(bsz, t, -1)
    ya = _lru(u, conv_w, conv_b, _block_diag(wa).astype(BF16), ba, _block_diag(wx).astype(BF16), bx, lam)
    yb = _hgrn2(u, lb, norm_g)
    y = jnp.concatenate([ya, yb], axis=-1).reshape(bsz * t, -1)
    return _proj_ln(y, w_out.astype(BF16), xf, g, b).reshape(bsz, t, d)


def _odd_mixer_ln(x, w_in, pe_k, pe_v, wk1, wk2, wv1, wv2, w_out, g, b):
    bsz, t, d = x.shape
    xf = x.reshape(bsz * t, d)
    nq = NSA_HEADS * HEAD_DIM
    nm = MOBA_HEADS * HEAD_DIM
    g0 = nq + 6 * NSA_KV
    ng = 3 * NSA_HEADS
    perm = np.concatenate([np.arange(0, g0), np.arange(g0 + ng, g0 + ng + 3 * nm), np.arange(g0, g0 + ng)])
    w_perm = jnp.pad(w_in[:, perm], ((0, 0), (0, LANES - ng))).astype(BF16)
    u = _proj(xf, w_perm).reshape(bsz, t, -1)

    def grp(off):
        z = u[:, :, off:off + NSA_KV].reshape(bsz, t, NSA_GROUPS, HEAD_DIM)
        return z.transpose(0, 2, 1, 3)

    qh = u[:, :, 0:nq].reshape(bsz, t, NSA_GROUPS, NSA_HPG, HEAD_DIM).transpose(0, 2, 3, 1, 4).astype(BF16)
    kc, vc, ks, vs, kw, vw = (grp(nq + i * NSA_KV) for i in range(6))
    m0 = g0

    def mh(off):
        z = u[:, :, off:off + nm].reshape(bsz, t, MOBA_HEADS, HEAD_DIM)
        return z.transpose(0, 2, 1, 3).astype(BF16)

    mq, mk, mv = mh(m0), mh(m0 + nm), mh(m0 + 2 * nm)
    gates = u[:, :, m0 + 3 * nm:m0 + 3 * nm + ng]

    n_all = NSA_HEADS + MOBA_HEADS
    s_all = 2.0 ** (-8.0 * np.arange(1, n_all + 1) / n_all)
    s_nsa = jnp.asarray(s_all[0::2], F32)
    s_moba = jnp.asarray(s_all[1::2], F32)

    k_cmp, v_cmp = _nsa_compress(kc, vc, pe_k, pe_v, wk1, wk2, wv1, wv2)
    o_c, bsel = _nsa_select(s_nsa, qh, k_cmp, v_cmp)

    ns = t // SLC_LEN
    scale = HEAD_DIM ** -0.5
    pos = np.arange(t)
    pos_cols = np.stack([pos // POS_SPLIT, pos // POS_SPLIT, pos % POS_SPLIT, pos % POS_SPLIT], axis=-1)
    onehot = (pos[:, None] // SLC_LEN) == np.arange(ns)[None, :]
    pad_k = LANES - HEAD_DIM - 4 - ns

    def slope_cols(s):
        hi = s.astype(BF16).astype(F32)
        lo = s - hi
        return jnp.stack([POS_SPLIT * hi, POS_SPLIT * lo, hi, lo], axis=-1)

    def k_aug(k, extra):
        cols = np.concatenate([pos_cols, extra, np.zeros((t, LANES - HEAD_DIM - 4 - extra.shape[1]))], axis=-1)
        cols = jnp.broadcast_to(jnp.asarray(cols, BF16), k.shape[:2] + cols.shape)
        return jnp.concatenate([k, cols], axis=-1)

    sc = slope_cols(s_nsa).reshape(1, NSA_GROUPS, NSA_HPG, 1, 4)
    mask_bias = jnp.where(bsel > 0.5, 0.0, MASKED)[:, :, None]
    hshape = (bsz, NSA_GROUPS, NSA_HPG, t)
    qa = jnp.concatenate([qh.astype(F32) * scale, jnp.broadcast_to(sc, hshape + (4,)),
                          jnp.broadcast_to(mask_bias, hshape + (ns,)),
                          jnp.zeros(hshape + (pad_k,), F32)], axis=-1).astype(BF16)
    o_s = _flash("slc", qa, k_aug(ks.astype(BF16), onehot), vs.astype(BF16), tq=128)
    o_w = _flash("win", qa, k_aug(kw.astype(BF16), np.zeros((t, 0))), vw.astype(BF16), tq=128)
    onehot_m = (pos[:, None] // MOBA_BLOCK) == np.arange(t // MOBA_BLOCK)[None, :]
    mshape = (bsz, MOBA_HEADS, 1, t)
    qm = jnp.concatenate([mq[:, :, None].astype(F32) * scale,
                          jnp.broadcast_to(slope_cols(s_moba).reshape(1, MOBA_HEADS, 1, 1, 4), mshape + (4,)),
                          jnp.zeros(mshape + (LANES - HEAD_DIM - 4,), F32)], axis=-1).astype(BF16)
    o_m = _flash("moba", qm, k_aug(mk, onehot_m), mv, tq=MOBA_BLOCK)

    gt = jax.nn.sigmoid(gates).reshape(bsz, t, NSA_GROUPS, NSA_HPG, 3).transpose(0, 2, 3, 1, 4)
    o = gt[..., 0:1] * o_c + gt[..., 1:2] * o_s + gt[..., 2:3] * o_w
    yc = o.transpose(0, 3, 1, 2, 4).reshape(bsz, t, nq)
    yd = o_m[:, :, 0].transpose(0, 2, 1, 3).reshape(bsz, t, nm)
    y = jnp.concatenate([yc, yd], axis=-1).reshape(bsz * t, -1)
    return _proj_ln(y, w_out.astype(BF16), xf, g, b).reshape(bsz, t, d)


def kernel(x, even_w_in, lru_conv_w, lru_conv_b, lru_wa, lru_ba, lru_wx, lru_bx, lru_lambda,
           hg_lower_bound, hg_norm_g, even_w_out, odd_w_in, nsa_pe_k, nsa_pe_v, nsa_wk1, nsa_wk2,
           nsa_wv1, nsa_wv2, odd_w_out, ln_g, ln_b, w_router, b_router, w_gate, w_up, w_down,
           ws_gate, ws_up, ws_down):
    bsz, t, d = x.shape
    lb_all = jnp.cumsum(jax.nn.softmax(hg_lower_bound.astype(F32), axis=0), axis=0)
    for layer in range(DEPTH):
        li = layer // 2
        if layer % 2 == 0:
            x = _even_mixer_ln(x, even_w_in[li], lru_conv_w[li], lru_conv_b[li], lru_wa[li], lru_ba[li],
                               lru_wx[li], lru_bx[li], lru_lambda[li], lb_all[layer], hg_norm_g[li],
                               even_w_out[li], ln_g[layer, 0], ln_b[layer, 0])
        else:
            x = _odd_mixer_ln(x, odd_w_in[li], nsa_pe_k[li], nsa_pe_v[li], nsa_wk1[li], nsa_wk2[li],
                              nsa_wv1[li], nsa_wv2[li], odd_w_out[li], ln_g[layer, 0], ln_b[layer, 0])
        x = _moe_ln(x.reshape(bsz * t, d), w_router[layer], b_router[layer], w_gate[layer], w_up[layer],
                    w_down[layer], ws_gate[layer], ws_up[layer], ws_down[layer],
                    ln_g[layer, 1], ln_b[layer, 1]).reshape(bsz, t, d)
    return x
```

```python
import functools

import numpy as np
import jax
import jax.numpy as jnp
from jax import lax
from jax.experimental import pallas as pl
from jax.experimental.pallas import tpu as pltpu

F32 = jnp.float32
BF16 = jnp.bfloat16

D_MODEL = 1024
DEPTH = 2
LRU_WIDTH = 512
LRU_BLOCKS = 8
LRU_BLOCK = LRU_WIDTH // LRU_BLOCKS
CONV_W = 4
LRU_C = 8.0
HG_HEADS = 4
HG_DK = 128
HG_FDIM = HG_HEADS * HG_DK
HG_VDIM = HG_HEADS * HG_DK
HEAD_DIM = 64
NSA_HEADS = 8
NSA_GROUPS = 2
NSA_HPG = NSA_HEADS // NSA_GROUPS
NSA_KV = NSA_GROUPS * HEAD_DIM
CMP_LEN = 32
CMP_STRIDE = 16
CMP_HIDDEN = 128
SLC_LEN = 64
SLC_TOPN = 16
WIN = 512
FORCE_BONUS = 1e4
MOBA_HEADS = 8
MOBA_BLOCK = 256
MOBA_TOPK = 3
N_EXPERTS = 256
TOP_K = 8
N_GROUPS = 8
TOPK_GROUPS = 4
EXPERT_HIDDEN = 256
ROUTED_SCALE = 2.5
DN_ALPHA = (2 * DEPTH) ** 0.25
LN_EPS = 1e-5
NEG = -1e30
NEG_INF = float("-inf")

LANES = 128
SUBLANES = 8
VMEM_LIMIT = 48 * 1024 * 1024

HG_SUB = 16
EXPERT_ROWS = 256


def _cparams(sem):
    return pltpu.CompilerParams(dimension_semantics=sem, vmem_limit_bytes=VMEM_LIMIT)


def _dot(a, b):
    return jnp.dot(a.astype(BF16), b.astype(BF16), preferred_element_type=F32)


def _dot_nt(a, b):
    return lax.dot_general(a.astype(BF16), b.astype(BF16), (((1,), (1,)), ((), ())),
                           preferred_element_type=F32)


def _dot_tn(a, b):
    return lax.dot_general(a.astype(BF16), b.astype(BF16), (((0,), (0,)), ((), ())),
                           preferred_element_type=F32)


def _split2(x):
    hi = x.astype(BF16)
    lo = (x - hi.astype(F32)).astype(BF16)
    return hi, lo


def _split3(x):
    hi = x.astype(BF16)
    r = x - hi.astype(F32)
    mid = r.astype(BF16)
    lo = (r - mid.astype(F32)).astype(BF16)
    return hi, mid, lo


def _layer_norm(v, g, b):
    mu = jnp.mean(v, axis=-1, keepdims=True)
    d = v - mu
    var = jnp.mean(d * d, axis=-1, keepdims=True)
    return d * lax.rsqrt(var + LN_EPS) * g + b


def _sigmoid(x):
    return 1.0 / (1.0 + jnp.exp(-x))


def _silu(x):
    return x * _sigmoid(x)


def _proj_kernel(x_ref, w_ref, o_ref):
    o_ref[...] = jnp.dot(x_ref[...].astype(BF16), w_ref[...], preferred_element_type=F32)


def _proj(x, w, tm=256):
    n, k = x.shape
    m = w.shape[1]
    return pl.pallas_call(
        _proj_kernel,
        grid=(n // tm,),
        in_specs=[pl.BlockSpec((tm, k), lambda i: (i, 0)),
                  pl.BlockSpec((k, m), lambda i: (0, 0))],
        out_specs=pl.BlockSpec((tm, m), lambda i: (i, 0)),
        out_shape=jax.ShapeDtypeStruct((n, m), F32),
        compiler_params=_cparams(("parallel",)),
        name="proj",
    )(x, w)


def _proj_ln_kernel(ya_ref, yb_ref, wa_ref, wb_ref, x_ref, g_ref, b_ref, o_ref):
    mix = (jnp.dot(ya_ref[...].astype(BF16), wa_ref[...], preferred_element_type=F32)
           + jnp.dot(yb_ref[...].astype(BF16), wb_ref[...], preferred_element_type=F32))
    o_ref[...] = _layer_norm(DN_ALPHA * x_ref[...] + mix, g_ref[...], b_ref[...])


def _proj_ln(ya, yb, w, xres, g, b, tm=256):
    n, ka = ya.shape
    kb = yb.shape[1]
    d = w.shape[1]
    wb16 = w.astype(BF16)
    full = lambda shp: pl.BlockSpec(shp, lambda i: (0, 0))
    return pl.pallas_call(
        _proj_ln_kernel,
        grid=(n // tm,),
        in_specs=[pl.BlockSpec((tm, ka), lambda i: (i, 0)),
                  pl.BlockSpec((tm, kb), lambda i: (i, 0)),
                  full((ka, d)), full((kb, d)),
                  pl.BlockSpec((tm, d), lambda i: (i, 0)),
                  full((1, d)), full((1, d))],
        out_specs=pl.BlockSpec((tm, d), lambda i: (i, 0)),
        out_shape=jax.ShapeDtypeStruct((n, d), F32),
        compiler_params=_cparams(("parallel",)),
        name="proj_ln",
    )(ya, yb, wb16[:ka], wb16[ka:], xres, g.reshape(1, d), b.reshape(1, d))


U16_QA = 0
U16_QM = U16_QA + NSA_HEADS * LANES
U16_KS = U16_QM + MOBA_HEADS * LANES
U16_KW = U16_KS + NSA_GROUPS * LANES
U16_KM = U16_KW + NSA_GROUPS * LANES
U16_VS = U16_KM + MOBA_HEADS * LANES
U16_VW = U16_VS + NSA_GROUPS * LANES
U16_VM = U16_VW + NSA_GROUPS * LANES
U16_W = U16_VM + MOBA_HEADS * LANES
U32_W = 3 * LANES


def _odd_proj_kernel(x_ref, w16_ref, w32_ref, qc_ref, kt_ref, o16_ref, o32_ref):
    xb = x_ref[...].astype(BF16)
    acc = jnp.dot(xb, w16_ref[...], preferred_element_type=F32)
    o16_ref[:, 0:U16_KS] = (acc[:, 0:U16_KS] + qc_ref[...]).astype(BF16)
    o16_ref[:, U16_KS:U16_VS] = (acc[:, U16_KS:U16_VS] + kt_ref[...]).astype(BF16)
    o16_ref[:, U16_VS:U16_W] = acc[:, U16_VS:U16_W].astype(BF16)
    o32_ref[...] = jnp.dot(xb, w32_ref[...], preferred_element_type=F32)


def _odd_proj(x, w_in, t, tm=256):
    n, d = x.shape
    dh = HEAD_DIM
    nq = NSA_HEADS * dh
    nm = MOBA_HEADS * dh
    sizes = [nq] + [NSA_KV] * 6 + [3 * NSA_HEADS] + [nm] * 3
    offs = np.concatenate([[0], np.cumsum(sizes)])
    q, kc, vc, ks, vs, kw, vw, gates, mq, mk, mv = [w_in[:, offs[i]:offs[i + 1]] for i in range(11)]
    scale = dh ** -0.5

    def heads(w, nh, mult=1.0):
        w = (w * mult).reshape(d, nh, dh)
        return jnp.pad(w, ((0, 0), (0, 0), (0, LANES - dh))).reshape(d, nh * LANES)

    w16 = jnp.concatenate([heads(q, NSA_HEADS, scale), heads(mq, MOBA_HEADS, scale), heads(ks, NSA_GROUPS),
                           heads(kw, NSA_GROUPS), heads(mk, MOBA_HEADS), heads(vs, NSA_GROUPS),
                           heads(vw, NSA_GROUPS), heads(mv, MOBA_HEADS)], axis=1).astype(BF16)
    w32 = jnp.concatenate([kc, vc, jnp.pad(gates, ((0, 0), (0, LANES - 3 * NSA_HEADS)))], axis=1).astype(BF16)

    n_all = NSA_HEADS + MOBA_HEADS
    s_all = 2.0 ** (-8.0 * np.arange(1, n_all + 1) / n_all)
    slopes = jnp.asarray(np.concatenate([s_all[0::2], s_all[1::2]]), F32)
    s_hi = slopes.astype(BF16).astype(F32)
    s_lo = slopes - s_hi
    scols = jnp.stack([POS_SPLIT * s_hi, POS_SPLIT * s_lo, s_hi, s_lo], axis=-1)
    qconst = jnp.pad(scols, ((0, 0), (dh, LANES - dh - 4))).reshape(1, U16_KS)

    pos = np.arange(t)
    pos_cols = np.stack([pos // POS_SPLIT, pos // POS_SPLIT, pos % POS_SPLIT, pos % POS_SPLIT], axis=-1)

    def key_cols(blk):
        cols = np.zeros((t, LANES), np.float32)
        cols[:, dh:dh + 4] = pos_cols
        if blk:
            cols[pos, MASK_COL0 + pos // blk] = 1.0
        return cols

    ktab = np.concatenate([key_cols(SLC_LEN)] * NSA_GROUPS + [key_cols(0)] * NSA_GROUPS
                          + [key_cols(MOBA_BLOCK)] * MOBA_HEADS, axis=1)
    tiles_per_seq = t // tm
    return pl.pallas_call(
        _odd_proj_kernel,
        grid=(n // tm,),
        in_specs=[pl.BlockSpec((tm, d), lambda i: (i, 0)),
                  pl.BlockSpec((d, U16_W), lambda i: (0, 0)),
                  pl.BlockSpec((d, U32_W), lambda i: (0, 0)),
                  pl.BlockSpec((1, U16_KS), lambda i: (0, 0)),
                  pl.BlockSpec((tm, U16_VS - U16_KS), lambda i: (i % tiles_per_seq, 0))],
        out_specs=[pl.BlockSpec((tm, U16_W), lambda i: (i, 0)),
                   pl.BlockSpec((tm, U32_W), lambda i: (i, 0))],
        out_shape=[jax.ShapeDtypeStruct((n, U16_W), BF16), jax.ShapeDtypeStruct((n, U32_W), F32)],
        compiler_params=_cparams(("parallel",)),
        name="odd_proj",
    )(x, w16, w32, qconst, jnp.asarray(ktab, F32))


def _odd_out_kernel(oc_ref, os_ref, ow_ref, om_ref, gt_ref, wa_ref, wb_ref, x_ref, g_ref, b_ref, o_ref):
    sg = _sigmoid(gt_ref[...])
    parts = []
    for h in range(NSA_HEADS):
        cols = slice(h * LANES, (h + 1) * LANES)
        parts.append(sg[:, 3 * h:3 * h + 1] * oc_ref[:, cols] + sg[:, 3 * h + 1:3 * h + 2] * os_ref[:, cols]
                     + sg[:, 3 * h + 2:3 * h + 3] * ow_ref[:, cols])
    yc = jnp.concatenate(parts, axis=1)
    mix = _dot(yc, wa_ref[...]) + _dot(om_ref[...], wb_ref[...])
    o_ref[...] = _layer_norm(DN_ALPHA * x_ref[...] + mix, g_ref[...], b_ref[...])


def _odd_out_ln(o_c, o_s, o_w, o_m, u32, w_out, xres, g, b, tm=256):
    n, d = xres.shape
    dh = HEAD_DIM

    def pad_rows(w, nh):
        return jnp.pad(w.reshape(nh, dh, d), ((0, 0), (0, LANES - dh), (0, 0))).reshape(nh * LANES, d).astype(BF16)

    wa = pad_rows(w_out[:NSA_HEADS * dh], NSA_HEADS)
    wb = pad_rows(w_out[NSA_HEADS * dh:], MOBA_HEADS)
    row = lambda width: pl.BlockSpec((tm, width), lambda i: (i, 0))
    full = lambda shp: pl.BlockSpec(shp, lambda i: (0, 0))
    ka = NSA_HEADS * LANES
    kb = MOBA_HEADS * LANES
    return pl.pallas_call(
        _odd_out_kernel,
        grid=(n // tm,),
        in_specs=[row(ka), row(ka), row(ka), row(kb),
                  pl.BlockSpec((tm, LANES), lambda i: (i, U32_W // LANES - 1)),
                  full((ka, d)), full((kb, d)), row(d), full((1, d)), full((1, d))],
        out_specs=row(d),
        out_shape=jax.ShapeDtypeStruct((n, d), F32),
        compiler_params=_cparams(("parallel",)),
        name="odd_out_ln",
    )(o_c, o_s, o_w, o_m, u32, wa, wb, xres, g.reshape(1, d), b.reshape(1, d))


def _lru_kernel(xb_ref, gb_ref, cw_ref, cb_ref, wa_ref, ba_ref, wx_ref, bx_ref, lam_ref,
                o_ref, xs_ref, a_ref, b_ref, h_ref):
    ti = pl.program_id(1)
    tc = xb_ref.shape[1]
    c = xb_ref.shape[2]

    @pl.when(ti == 0)
    def _():
        xs_ref[0:SUBLANES, :] = jnp.zeros((SUBLANES, c), F32)
        h_ref[...] = jnp.zeros_like(h_ref)

    xs_ref[SUBLANES:SUBLANES + tc, :] = xb_ref[0]
    xc = cb_ref[...]
    for w in range(CONV_W):
        xc = xc + cw_ref[w:w + 1, :] * xs_ref[pl.ds(SUBLANES - (CONV_W - 1) + w, tc), :]
    xs_ref[0:SUBLANES, :] = xs_ref[tc:tc + SUBLANES, :]

    r = _sigmoid(_dot(xc, wa_ref[...]) + ba_ref[...])
    i = _sigmoid(_dot(xc, wx_ref[...]) + bx_ref[...])
    nl = -lam_ref[...]
    softplus = jnp.maximum(nl, 0.0) + jnp.log(1.0 + jnp.exp(-jnp.abs(nl)))
    log_a = (-LRU_C * r) * softplus
    a = jnp.exp(log_a)
    mult = jnp.sqrt(1.0 - jnp.exp(2.0 * log_a))
    trow = ti * tc + lax.broadcasted_iota(jnp.int32, (tc, c), 0)
    mult = jnp.where(trow == 0, 1.0, mult)
    a_ref[...] = a
    b_ref[...] = mult * (i * xc)

    row = lax.broadcasted_iota(jnp.int32, (SUBLANES, c), 0)

    def body(g, carry):
        r0 = pl.multiple_of(g * SUBLANES, SUBLANES)
        av = a_ref[pl.ds(r0, SUBLANES), :]
        bv = b_ref[pl.ds(r0, SUBLANES), :]
        for d in (1, 2, 4):
            a_sh = pltpu.roll(av, d, 0)
            b_sh = pltpu.roll(bv, d, 0)
            m = row >= d
            bv = jnp.where(m, av * b_sh + bv, bv)
            av = jnp.where(m, av * a_sh, av)
        h = av * carry + bv
        b_ref[pl.ds(r0, SUBLANES), :] = h
        return h[SUBLANES - 1:SUBLANES, :]

    h_ref[...] = lax.fori_loop(0, tc // SUBLANES, body, h_ref[...])
    o_ref[0] = b_ref[...] * jax.nn.gelu(gb_ref[0])


def _lru(u3, conv_w, conv_b, wa_bd, ba, wx_bd, bx, lam, tc=256):
    b, t, _ = u3.shape
    c = LRU_WIDTH
    row = lambda v: v.reshape(1, c)
    full = lambda shp: pl.BlockSpec(shp, lambda bi, ti: (0, 0))
    return pl.pallas_call(
        _lru_kernel,
        grid=(b, t // tc),
        in_specs=[pl.BlockSpec((1, tc, c), lambda bi, ti: (bi, ti, 0)),
                  pl.BlockSpec((1, tc, c), lambda bi, ti: (bi, ti, 1)),
                  full((CONV_W, c)), full((1, c)), full((c, c)), full((1, c)),
                  full((c, c)), full((1, c)), full((1, c))],
        out_specs=pl.BlockSpec((1, tc, c), lambda bi, ti: (bi, ti, 0)),
        out_shape=jax.ShapeDtypeStruct((b, t, c), F32),
        scratch_shapes=[pltpu.VMEM((tc + SUBLANES, c), F32), pltpu.VMEM((tc, c), F32),
                        pltpu.VMEM((tc, c), F32), pltpu.VMEM((1, c), F32)],
        compiler_params=_cparams(("parallel", "arbitrary")),
        name="rg_lru",
    )(u3, u3, conv_w, row(conv_b), wa_bd, row(ba), wx_bd, row(bx), row(lam))


def _hgrn_kernel(q_ref, f_ref, v_ref, g_ref, lb_ref, ng_ref, tri_ref, ones_ref, o_ref,
                 st_ref, q_s, k_s, c_s, qe_s, kd_s, dl_s, o_s):
    ti = pl.program_id(2)
    tc = q_ref.shape[1]
    dk = q_ref.shape[2]

    @pl.when(ti == 0)
    def _():
        st_ref[...] = jnp.zeros_like(st_ref)

    lbv = lb_ref[...]
    q = _silu(q_ref[0])
    f = lbv + (1.0 - lbv) * _sigmoid(f_ref[0])
    kk = 1.0 - f
    logf = jnp.log(f)
    parts = _split3(logf)
    tri = tri_ref[...]
    ones = ones_ref[...]
    cum = sum(jnp.dot(tri, p, preferred_element_type=F32) for p in parts)
    last = sum(jnp.dot(ones, p, preferred_element_type=F32) for p in parts)
    q_s[...] = q
    k_s[...] = kk
    c_s[...] = cum
    qe_s[...] = q * jnp.exp(cum)
    kd_s[...] = kk * jnp.exp(last - cum)
    dl_s[...] = jnp.exp(last)

    s_idx = lax.broadcasted_iota(jnp.int32, (HG_SUB, dk), 0)

    def chunk(ci, _):
        r0 = pl.multiple_of(ci * HG_SUB, HG_SUB)
        rows = pl.ds(r0, HG_SUB)
        st = st_ref[...]
        vc = v_ref[0, rows, :]
        qc = q_s[rows, :]
        kc = k_s[rows, :]
        cc = c_s[rows, :]
        o = _dot_nt(qe_s[rows, :], st)
        for t in range(HG_SUB):
            d = cc[t:t + 1, :] - cc
            dec = jnp.exp(jnp.where(s_idx <= t, d, NEG))
            w = (qc[t:t + 1, :] * kc) * dec
            a_t = jnp.sum(w, axis=1, keepdims=True)
            o_t = jnp.sum(a_t * vc, axis=0, keepdims=True)
            o = o + jnp.where(s_idx == t, o_t, 0.0)
        o_s[rows, :] = o
        st_ref[...] = st * dl_s[pl.ds(r0, 1), :] + _dot_tn(vc, kd_s[rows, :])
        return 0

    lax.fori_loop(0, tc // HG_SUB, chunk, 0, unroll=4)
    o = o_s[...]
    o = o * lax.rsqrt(jnp.mean(o * o, axis=-1, keepdims=True) + 1e-6) * ng_ref[...]
    o_ref[0] = o * _sigmoid(g_ref[0])


def _hgrn2(u3, lb, norm_g, tc=256):
    b, t, _ = u3.shape
    dk = HG_DK
    base = (2 * LRU_WIDTH) // dk
    idx = np.arange(tc)
    same = (idx[:, None] // HG_SUB) == (idx[None, :] // HG_SUB)
    tri = jnp.asarray(same & (idx[None, :] <= idx[:, None]), BF16)
    ones = jnp.asarray(same, BF16)

    def col(off):
        return pl.BlockSpec((1, tc, dk), lambda bi, hi, ti: (bi, ti, base + off * HG_HEADS + hi))

    vec = pl.BlockSpec((1, dk), lambda bi, hi, ti: (0, hi))
    cst = pl.BlockSpec((tc, tc), lambda bi, hi, ti: (0, 0))
    return pl.pallas_call(
        _hgrn_kernel,
        grid=(b, HG_HEADS, t // tc),
        in_specs=[col(0), col(1), col(2), col(3), vec, vec, cst, cst],
        out_specs=pl.BlockSpec((1, tc, dk), lambda bi, hi, ti: (bi, ti, hi)),
        out_shape=jax.ShapeDtypeStruct((b, t, HG_VDIM), F32),
        scratch_shapes=[pltpu.VMEM((dk, dk), F32)] + [pltpu.VMEM((tc, dk), F32)] * 7,
        compiler_params=_cparams(("parallel", "parallel", "arbitrary")),
        name="hgrn2",
    )(u3, u3, u3, u3, lb.reshape(1, HG_FDIM), norm_g.reshape(1, HG_VDIM), tri, ones)


def _nsa_cmp_kv_kernel(zk_ref, zv_ref, pek_ref, pev_ref, wk1_ref, wk2_ref, wv1_ref, wv2_ref,
                       ok_ref, ov_ref):
    half = zk_ref.shape[3]
    nrow = zk_ref.shape[2]

    def compress(z, pe_ref, w1_ref, w2_ref):
        lo = _dot(z + pe_ref[0:1, :], w1_ref[0:half, :])
        hi = _dot(z + pe_ref[1:2, :], w1_ref[half:2 * half, :])
        pre = lo + pltpu.roll(hi, nrow - 1, 0)
        return _dot(jax.nn.gelu(pre), w2_ref[...])

    ok_ref[0, 0] = compress(zk_ref[0, 0], pek_ref, wk1_ref, wk2_ref)
    ov_ref[0, 0] = compress(zv_ref[0, 0], pev_ref, wv1_ref, wv2_ref)


def _nsa_compress(kc, vc, pe_k, pe_v, wk1, wk2, wv1, wv2):
    b, g, t, dh = kc.shape
    nrow = t // CMP_STRIDE
    half = CMP_STRIDE * dh
    zk = kc.reshape(b, g, nrow, half)
    zv = vc.reshape(b, g, nrow, half)
    zspec = pl.BlockSpec((1, 1, nrow, half), lambda bi, gi: (bi, gi, 0, 0))
    full = lambda shp: pl.BlockSpec(shp, lambda bi, gi: (0, 0))
    ospec = pl.BlockSpec((1, 1, nrow, LANES), lambda bi, gi: (bi, gi, 0, 0))
    oshape = jax.ShapeDtypeStruct((b, g, nrow, LANES), F32)
    pad = lambda w: jnp.pad(w, ((0, 0), (0, LANES - dh))).astype(BF16)
    return pl.pallas_call(
        _nsa_cmp_kv_kernel,
        grid=(b, g),
        in_specs=[zspec, zspec, full((2, half)), full((2, half)),
                  full((2 * half, CMP_HIDDEN)), full((CMP_HIDDEN, LANES)),
                  full((2 * half, CMP_HIDDEN)), full((CMP_HIDDEN, LANES))],
        out_specs=[ospec, ospec],
        out_shape=[oshape, oshape],
        compiler_params=_cparams(("parallel", "parallel")),
        name="nsa_compress",
    )(zk, zv, pe_k.reshape(2, half), pe_v.reshape(2, half),
      wk1.astype(BF16), pad(wk2), wv1.astype(BF16), pad(wv2))


def _nsa_sel_kernel(slope_ref, q_ref, kc_ref, vc_ref, ov_ref, o_ref, mb_ref):
    gi = pl.program_id(0) % NSA_GROUPS
    qi = pl.program_id(1)
    tq = q_ref.shape[1]
    hpg = q_ref.shape[2] // LANES
    nc = kc_ref.shape[2]
    ns = ov_ref.shape[1]

    q = jnp.concatenate([q_ref[0, :, i * LANES:(i + 1) * LANES] for i in range(hpg)], axis=0)
    khi, klo = _split2(kc_ref[0, 0])
    s = _dot_nt(q, khi) + _dot_nt(q, klo)
    tpos = qi * tq + lax.broadcasted_iota(jnp.int32, (tq, nc), 0)
    cend = lax.broadcasted_iota(jnp.int32, (tq, nc), 1) * CMP_STRIDE + (CMP_LEN - 1)
    d_c = (tpos - cend).astype(F32)
    valid = d_c >= 0.0
    ps = []
    for i in range(hpg):
        si = s[i * tq:(i + 1) * tq, :] - slope_ref[gi * hpg + i] * d_c
        si = jnp.where(valid, si, NEG)
        m = jnp.max(si, axis=-1, keepdims=True)
        p = jnp.where(valid, jnp.exp(si - m), 0.0)
        ps.append(p / jnp.maximum(jnp.sum(p, axis=-1, keepdims=True), 1e-30))
    oc = _dot(jnp.concatenate(ps, axis=0), vc_ref[0, 0])
    o_ref[0] = jnp.concatenate([oc[i * tq:(i + 1) * tq, :] for i in range(hpg)], axis=1)

    psum = ps[0]
    for i in range(1, hpg):
        psum = psum + ps[i]
    phi, plo = _split2(psum)
    ovl = ov_ref[...]
    imp = jnp.dot(phi, ovl, preferred_element_type=F32) + jnp.dot(plo, ovl, preferred_element_type=F32)

    j = lax.broadcasted_iota(jnp.int32, (tq, ns), 1)
    own = (qi * tq + lax.broadcasted_iota(jnp.int32, (tq, ns), 0)) // SLC_LEN
    forced = (j == 0) | (j == own) | (j == own - 1)
    imp = jnp.where(j <= own, imp + FORCE_BONUS * forced.astype(F32), NEG)
    rank = jnp.zeros((tq, ns), F32)
    for jp in range(ns):
        col = imp[:, jp:jp + 1]
        beats = (col > imp) | ((col == imp) & (j > jp))
        rank = rank + beats.astype(F32)
    bias = jnp.where(rank < float(min(SLC_TOPN, ns)), 0.0, MASKED)
    place = (lax.broadcasted_iota(jnp.int32, (ns, LANES), 1)
             == lax.broadcasted_iota(jnp.int32, (ns, LANES), 0) + MASK_COL0)
    mb_ref[0] = _dot(bias, place.astype(F32)).astype(BF16)


def _nsa_select(slopes, u16, k_cmp, v_cmp, tq=256):
    b, t, _ = u16.shape
    g = NSA_GROUPS
    hpg = NSA_HPG
    nc = k_cmp.shape[2]
    ns = t // SLC_LEN
    ci = np.arange(nc)[:, None]
    cj = np.arange(ns)[None, :]
    ovl = ((CMP_STRIDE * ci < SLC_LEN * (cj + 1)) & (CMP_STRIDE * ci + CMP_LEN > SLC_LEN * cj))
    ovl = ovl & (ci < (t - CMP_LEN) // CMP_STRIDE + 1)
    grid_spec = pltpu.PrefetchScalarGridSpec(
        num_scalar_prefetch=1,
        grid=(b * g, t // tq),
        in_specs=[pl.BlockSpec((1, tq, hpg * LANES), lambda bg, qi, s: (bg // g, qi, U16_QA // (hpg * LANES) + bg % g)),
                  pl.BlockSpec((1, 1, nc, LANES), lambda bg, qi, s: (bg // g, bg % g, 0, 0)),
                  pl.BlockSpec((1, 1, nc, LANES), lambda bg, qi, s: (bg // g, bg % g, 0, 0)),
                  pl.BlockSpec((nc, ns), lambda bg, qi, s: (0, 0))],
        out_specs=[pl.BlockSpec((1, tq, hpg * LANES), lambda bg, qi, s: (bg // g, qi, bg % g)),
                   pl.BlockSpec((1, tq, LANES), lambda bg, qi, s: (bg // g, qi, bg % g))],
    )
    return pl.pallas_call(
        _nsa_sel_kernel,
        grid_spec=grid_spec,
        out_shape=[jax.ShapeDtypeStruct((b, t, g * hpg * LANES), F32),
                   jax.ShapeDtypeStruct((b, t, g * LANES), BF16)],
        compiler_params=_cparams(("parallel", "parallel")),
        name="nsa_select",
    )(slopes, u16, k_cmp, v_cmp, jnp.asarray(ovl, BF16))


POS_SPLIT = 64
MASKED = -2e30


MASK_COL0 = HEAD_DIM + 4
KEY_STEP = 512


def _flash_kernel(q_ref, k_ref, v_ref, *rest, mode):
    o_ref = rest[-1]
    qi = pl.program_id(1)
    tq = q_ref.shape[1]
    hpg = q_ref.shape[2] // LANES
    da = LANES
    t_all = k_ref.shape[1]
    q0 = qi * tq
    heads = [q_ref[0, :, i * LANES:(i + 1) * LANES] for i in range(hpg)]
    if mode == "slc":
        heads = [h + rest[0][0] for h in heads]
    q = heads[0] if hpg == 1 else jnp.concatenate(heads, axis=0)

    if mode == "moba":
        nb = t_all // MOBA_BLOCK
        kf = k_ref[0].astype(F32).reshape(nb, MOBA_BLOCK, da)
        kmean = jnp.sum(kf, axis=1) / float(MOBA_BLOCK)
        kmean = jnp.where(lax.broadcasted_iota(jnp.int32, (nb, da), 1) < HEAD_DIM, kmean, 0.0)
        mhi, mlo = _split2(kmean)
        gate = _dot_nt(mhi, q) + _dot_nt(mlo, q)
        jb = lax.broadcasted_iota(jnp.int32, (nb, tq), 0)
        ob = q0 // MOBA_BLOCK
        gate = jnp.where(jb < ob, gate, NEG)
        rank = jnp.zeros((nb, tq), F32)
        for jp in range(nb):
            row = gate[jp:jp + 1, :]
            beats = (row > gate) | ((row == gate) & (jb > jp))
            rank = rank + beats.astype(F32)
        n_sel = float(min(MOBA_TOPK, nb - 1))
        bsel = ((rank < n_sel) & (jb < ob)) | (jb == ob)
        bias = jnp.where(bsel, 0.0, MASKED)
        place = (lax.broadcasted_iota(jnp.int32, (nb, da), 1)
                 == lax.broadcasted_iota(jnp.int32, (nb, da), 0) + MASK_COL0)
        q = q + _dot_tn(bias, place.astype(F32)).astype(BF16)

    def attend(kstart, klen, diag, off):
        s = _dot_nt(q, k_ref[0, pl.ds(kstart, klen), :])
        rel = lax.broadcasted_iota(jnp.int32, (tq, diag), 0) - lax.broadcasted_iota(jnp.int32, (tq, diag), 1)
        valid = rel >= off
        if mode == "win":
            valid = valid & (rel < off + WIN)
        if hpg > 1:
            valid = jnp.concatenate([valid] * hpg, axis=0)
        sd = jnp.where(valid, s[:, klen - diag:], MASKED)
        s = sd if diag == klen else jnp.concatenate([s[:, :klen - diag], sd], axis=1)
        m = jnp.maximum(jnp.max(s, axis=-1, keepdims=True), NEG)
        p = jnp.exp(s - m)
        l = jnp.sum(p, axis=-1, keepdims=True)
        o = _dot(p, v_ref[0, pl.ds(kstart, klen), :]) / jnp.maximum(l, 1e-30)
        o_ref[0] = o if hpg == 1 else jnp.concatenate([o[i * tq:(i + 1) * tq, :] for i in range(hpg)], axis=1)

    if mode == "win":
        kstart = pl.multiple_of(jnp.maximum(q0 - WIN, 0), LANES)
        attend(kstart, WIN + tq, WIN + tq, kstart - q0)
    else:
        step = MOBA_BLOCK if mode == "moba" else KEY_STEP
        var = (q0 + tq - 1) // step
        for v in range(t_all // step):
            @pl.when(var == v)
            def _(v=v):
                attend(0, step * (v + 1), step, step * v - q0)


def _flash(mode, u16, mb=None, tq=128):
    b, t, _ = u16.shape
    q_off, k_off, v_off, n_kvh, hpg = {
        "slc": (U16_QA, U16_KS, U16_VS, NSA_GROUPS, NSA_HPG),
        "win": (U16_QA, U16_KW, U16_VW, NSA_GROUPS, NSA_HPG),
        "moba": (U16_QM, U16_KM, U16_VM, MOBA_HEADS, 1)}[mode]
    assert mode != "moba" or tq == MOBA_BLOCK
    qw = hpg * LANES
    in_specs = [pl.BlockSpec((1, tq, qw), lambda bh, qi: (bh // n_kvh, qi, q_off // qw + bh % n_kvh)),
                pl.BlockSpec((1, t, LANES), lambda bh, qi: (bh // n_kvh, 0, k_off // LANES + bh % n_kvh)),
                pl.BlockSpec((1, t, LANES), lambda bh, qi: (bh // n_kvh, 0, v_off // LANES + bh % n_kvh))]
    args = [u16, u16, u16]
    if mode == "slc":
        in_specs.append(pl.BlockSpec((1, tq, LANES), lambda bh, qi: (bh // n_kvh, qi, bh % n_kvh)))
        args.append(mb)
    return pl.pallas_call(
        functools.partial(_flash_kernel, mode=mode),
        grid=(b * n_kvh, t // tq),
        in_specs=in_specs,
        out_specs=pl.BlockSpec((1, tq, qw), lambda bh, qi: (bh // n_kvh, qi, bh % n_kvh)),
        out_shape=jax.ShapeDtypeStruct((b, t, n_kvh * qw), F32),
        compiler_params=_cparams(("parallel", "parallel")),
        name="flash_" + mode,
    )(*args)


def _router_kernel(x_ref, whi_ref, wlo_ref, b_ref, triu_ref, meta_ref, cnt_ref, carry_ref):
    step = pl.program_id(0)
    tm = x_ref.shape[0]
    ne = whi_ref.shape[0]
    gsz = ne // N_GROUPS

    @pl.when(step == 0)
    def _():
        carry_ref[...] = jnp.zeros_like(carry_ref)

    xhi, xlo = _split2(x_ref[...])
    whi = whi_ref[...]
    logits = _dot_nt(whi, xhi) + _dot_nt(wlo_ref[...], xhi) + _dot_nt(whi, xlo)
    scores = _sigmoid(logits)
    biased = scores + b_ref[:, 0:1]

    v3 = biased.reshape(N_GROUPS, gsz, tm)
    i3 = lax.broadcasted_iota(jnp.int32, (N_GROUPS, gsz, tm), 1).astype(F32)
    m1 = jnp.max(v3, axis=1, keepdims=True)
    idx1 = jnp.min(jnp.where(v3 == m1, i3, float(gsz)), axis=1, keepdims=True)
    m2 = jnp.max(jnp.where(i3 == idx1, NEG_INF, v3), axis=1, keepdims=True)
    gs = (m1 + m2).reshape(N_GROUPS, tm)
    gi = lax.broadcasted_iota(jnp.int32, (N_GROUPS, tm), 0).astype(F32)
    gmask = jnp.zeros((N_GROUPS, tm), F32)
    for _ in range(TOPK_GROUPS):
        m = jnp.max(gs, axis=0, keepdims=True)
        pick = jnp.min(jnp.where(gs == m, gi, float(N_GROUPS)), axis=0, keepdims=True)
        hit = gi == pick
        gmask = jnp.where(hit, 1.0, gmask)
        gs = jnp.where(hit, NEG_INF, gs)
    emask = jnp.broadcast_to(gmask.reshape(N_GROUPS, 1, tm), (N_GROUPS, gsz, tm)).reshape(ne, tm)
    cur = jnp.where(emask > 0.5, biased, NEG)

    ei = lax.broadcasted_iota(jnp.int32, (ne, tm), 0).astype(F32)
    selm = jnp.zeros((ne, tm), F32)
    idxs, svals = [], []
    for _ in range(TOP_K):
        m = jnp.max(cur, axis=0, keepdims=True)
        idx = jnp.min(jnp.where(cur == m, ei, float(ne)), axis=0, keepdims=True)
        hit = ei == idx
        svals.append(jnp.sum(jnp.where(hit, scores, 0.0), axis=0, keepdims=True))
        idxs.append(idx)
        cur = jnp.where(hit, NEG_INF, cur)
        selm = jnp.where(hit, 1.0, selm)

    carry = carry_ref[:, 0:1]
    rank_full = jnp.dot(selm.astype(BF16), triu_ref[...], preferred_element_type=F32) + carry
    ssum = svals[0]
    for k in range(1, TOP_K):
        ssum = ssum + svals[k]
    for k in range(TOP_K):
        meta_ref[k:k + 1, :] = idxs[k]
        meta_ref[TOP_K + k:TOP_K + k + 1, :] = svals[k] / ssum * ROUTED_SCALE
        meta_ref[2 * TOP_K + k:2 * TOP_K + k + 1, :] = jnp.sum(
            jnp.where(ei == idxs[k], rank_full, 0.0), axis=0, keepdims=True)
    new_carry = carry + jnp.sum(selm, axis=1, keepdims=True)
    carry_ref[...] = jnp.broadcast_to(new_carry, carry_ref.shape)
    cnt_ref[...] = jnp.broadcast_to(new_carry, cnt_ref.shape)


def _router(x, w_router, b_router, tm=256):
    n, d = x.shape
    ne = w_router.shape[1]
    wt = w_router.T
    whi = wt.astype(BF16)
    wlo = (wt - whi.astype(F32)).astype(BF16)
    triu = jnp.asarray(np.arange(tm)[:, None] < np.arange(tm)[None, :], BF16)
    meta, cnt = pl.pallas_call(
        _router_kernel,
        grid=(n // tm,),
        in_specs=[pl.BlockSpec((tm, d), lambda i: (i, 0)),
                  pl.BlockSpec((ne, d), lambda i: (0, 0)),
                  pl.BlockSpec((ne, d), lambda i: (0, 0)),
                  pl.BlockSpec((ne, LANES), lambda i: (0, 0)),
                  pl.BlockSpec((tm, tm), lambda i: (0, 0))],
        out_specs=[pl.BlockSpec((3 * TOP_K, tm), lambda i: (0, i)),
                   pl.BlockSpec((ne, LANES), lambda i: (0, 0))],
        out_shape=[jax.ShapeDtypeStruct((3 * TOP_K, n), F32),
                   jax.ShapeDtypeStruct((ne, LANES), F32)],
        scratch_shapes=[pltpu.VMEM((ne, LANES), F32)],
        compiler_params=_cparams(("arbitrary",)),
        name="moe_router",
    )(x, whi, wlo, jnp.broadcast_to(b_router.reshape(ne, 1), (ne, LANES)), triu)
    return meta, cnt[:, 0]


def _expert_kernel(be_ref, nu_ref, xs_ref, wg_ref, wu_ref, wd_ref, o_ref):
    bi = pl.program_id(0)

    @pl.when(bi < nu_ref[0])
    def _():
        xa, xb = _unpack_bf16_pair(xs_ref[...])
        half = xa.shape[1]

        def xw(w_ref):
            return (jnp.dot(xa, w_ref[0, 0:half, :].astype(BF16), preferred_element_type=F32)
                    + jnp.dot(xb, w_ref[0, half:2 * half, :].astype(BF16), preferred_element_type=F32))

        h = _silu(xw(wg_ref)) * xw(wu_ref)
        o_ref[...] = _pack_bf16_pair(_dot(h, wd_ref[0]))

    @pl.when(bi >= nu_ref[0])
    def _():
        o_ref[...] = jnp.zeros_like(o_ref)


def _expert_ffn(blk_e, n_used, xs, w_gate, w_up, w_down):
    n_rows = xs.shape[0]
    d = w_gate.shape[1]
    hdim = w_gate.shape[2]
    n_blk = n_rows // EXPERT_ROWS
    grid_spec = pltpu.PrefetchScalarGridSpec(
        num_scalar_prefetch=2,
        grid=(n_blk,),
        in_specs=[pl.BlockSpec((EXPERT_ROWS, d // 2),
                               lambda i, be, nu: (jnp.minimum(i, jnp.maximum(nu[0] - 1, 0)), 0)),
                  pl.BlockSpec((1, d, hdim), lambda i, be, nu: (be[i], 0, 0)),
                  pl.BlockSpec((1, d, hdim), lambda i, be, nu: (be[i], 0, 0)),
                  pl.BlockSpec((1, hdim, d), lambda i, be, nu: (be[i], 0, 0))],
        out_specs=pl.BlockSpec((EXPERT_ROWS, d // 2), lambda i, be, nu: (i, 0)),
    )
    return pl.pallas_call(
        _expert_kernel,
        grid_spec=grid_spec,
        out_shape=jax.ShapeDtypeStruct((n_rows, d // 2), jnp.uint32),
        compiler_params=_cparams(("arbitrary",)),
        name="moe_experts",
    )(blk_e, n_used, xs, w_gate, w_up, w_down)


def _dest_kernel(meta_ref, pstart_ref, dest_ref):
    ne = pstart_ref.shape[0]
    tm = meta_ref.shape[1]
    ei = lax.broadcasted_iota(jnp.int32, (ne, tm), 0).astype(F32)
    ps = pstart_ref[:, 0:1]
    for k in range(TOP_K):
        base = jnp.sum(jnp.where(ei == meta_ref[k:k + 1, :], ps, 0.0), axis=0, keepdims=True)
        dest_ref[k:k + 1, :] = (base + meta_ref[2 * TOP_K + k:2 * TOP_K + k + 1, :]).astype(jnp.int32)


def _dest_rows(meta, pstart, tm=512):
    n = meta.shape[1]
    ne = pstart.shape[0]
    return pl.pallas_call(
        _dest_kernel,
        grid=(n // tm,),
        in_specs=[pl.BlockSpec((3 * TOP_K, tm), lambda i: (0, i)),
                  pl.BlockSpec((ne, LANES), lambda i: (0, 0))],
        out_specs=pl.BlockSpec((TOP_K, tm), lambda i: (0, i)),
        out_shape=jax.ShapeDtypeStruct((TOP_K, n), jnp.int32),
        compiler_params=_cparams(("parallel",)),
        name="moe_dest",
    )(meta, jnp.broadcast_to(pstart.astype(F32).reshape(ne, 1), (ne, LANES)))


def _pack_bf16_pair(x):
    half = x.shape[1] // 2
    hi = pltpu.bitcast(x[:, 0:half].astype(BF16).astype(F32), jnp.uint32)
    lo = pltpu.bitcast(x[:, half:2 * half].astype(BF16).astype(F32), jnp.uint32)
    return hi | (lo >> 16)


def _unpack_bf16_pair(p):
    hi = pltpu.bitcast(p & jnp.uint32(0xFFFF0000), F32).astype(BF16)
    lo = pltpu.bitcast(p << 16, F32).astype(BF16)
    return hi, lo


def _dispatch_kernel(dest_ref, x_ref, xs_in_ref, xs_ref, pk_ref, sem):
    del xs_in_ref
    tm = x_ref.shape[0]
    pk_ref[...] = _pack_bf16_pair(x_ref[...])

    def row_copy(t, k):
        return pltpu.make_async_copy(pk_ref.at[pl.ds(t, 1), :], xs_ref.at[pl.ds(dest_ref[k, t], 1), :], sem)

    def body(t, _):
        for k in range(TOP_K):
            row_copy(t, k).start()
        return 0

    lax.fori_loop(0, tm, body, 0)
    done = xs_ref.at[pl.ds(0, TOP_K * tm), :]
    pltpu.make_async_copy(done, done, sem).wait()


def _dispatch(dest, x, n_rows, tm=256):
    n, d = x.shape
    return pl.pallas_call(
        _dispatch_kernel,
        grid=(n // tm,),
        in_specs=[pl.BlockSpec((TOP_K, tm), lambda i: (0, i), memory_space=pltpu.SMEM),
                  pl.BlockSpec((tm, d), lambda i: (i, 0)),
                  pl.BlockSpec(memory_space=pl.ANY)],
        out_specs=pl.BlockSpec(memory_space=pl.ANY),
        out_shape=jax.ShapeDtypeStruct((n_rows, d // 2), jnp.uint32),
        scratch_shapes=[pltpu.VMEM((tm, d // 2), jnp.uint32), pltpu.SemaphoreType.DMA(())],
        input_output_aliases={2: 0},
        compiler_params=_cparams(("arbitrary",)),
        name="moe_dispatch",
    )(dest, x, jnp.zeros((n_rows, d // 2), jnp.uint32))


def _combine_ln_kernel(dest_ref, dest_next_ref, w_ref, x_ref, ys_ref, wg_ref, wu_ref, wd_ref, g_ref, b_ref,
                       o_ref, buf, sem):
    step = pl.program_id(0)
    tm = x_ref.shape[0]
    slot = step % 2

    def gather_rows(idx_ref, to_slot):
        def body(t, _):
            for k in range(TOP_K):
                pltpu.make_async_copy(ys_ref.at[pl.ds(idx_ref[k, t], 1), :],
                                      buf.at[to_slot, k, pl.ds(t, 1), :], sem.at[to_slot]).start()
            return 0

        lax.fori_loop(0, tm, body, 0)

    @pl.when(step == 0)
    def _():
        gather_rows(dest_ref, 0)

    @pl.when(step + 1 < pl.num_programs(0))
    def _():
        gather_rows(dest_next_ref, 1 - slot)

    x = x_ref[...]
    xb = x.astype(BF16)
    h = _silu(jnp.dot(xb, wg_ref[...], preferred_element_type=F32)) * jnp.dot(
        xb, wu_ref[...], preferred_element_type=F32)
    ff = jnp.dot(h.astype(BF16), wd_ref[...], preferred_element_type=F32)
    pltpu.make_async_copy(buf.at[slot], buf.at[slot], sem.at[slot]).wait()
    routed_a = jnp.zeros((tm, buf.shape[3]), F32)
    routed_b = jnp.zeros((tm, buf.shape[3]), F32)
    for k in range(TOP_K):
        ya, yb = _unpack_bf16_pair(buf[slot, k])
        routed_a = routed_a + ya.astype(F32) * w_ref[:, k:k + 1]
        routed_b = routed_b + yb.astype(F32) * w_ref[:, k:k + 1]
    ff = ff + jnp.concatenate([routed_a, routed_b], axis=1)
    o_ref[...] = _layer_norm(DN_ALPHA * x + ff, g_ref[...], b_ref[...])


def _combine_ln(dest, wts, x, ys, ws_gate, ws_up, ws_down, g, b, tm=256):
    n, d = x.shape
    hdim = ws_gate.shape[1]
    row = pl.BlockSpec((tm, d), lambda i: (i, 0))
    last = n // tm - 1
    return pl.pallas_call(
        _combine_ln_kernel,
        grid=(n // tm,),
        in_specs=[pl.BlockSpec((TOP_K, tm), lambda i: (0, i), memory_space=pltpu.SMEM),
                  pl.BlockSpec((TOP_K, tm), lambda i: (0, jnp.minimum(i + 1, last)), memory_space=pltpu.SMEM),
                  pl.BlockSpec((tm, TOP_K), lambda i: (i, 0)),
                  row,
                  pl.BlockSpec(memory_space=pl.ANY),
                  pl.BlockSpec((d, hdim), lambda i: (0, 0)),
                  pl.BlockSpec((d, hdim), lambda i: (0, 0)),
                  pl.BlockSpec((hdim, d), lambda i: (0, 0)),
                  pl.BlockSpec((1, d), lambda i: (0, 0)),
                  pl.BlockSpec((1, d), lambda i: (0, 0))],
        out_specs=row,
        out_shape=jax.ShapeDtypeStruct((n, d), F32),
        scratch_shapes=[pltpu.VMEM((2, TOP_K, tm, d // 2), jnp.uint32), pltpu.SemaphoreType.DMA((2,))],
        compiler_params=_cparams(("arbitrary",)),
        name="moe_combine_ln",
    )(dest, dest, wts, x, ys, ws_gate.astype(BF16), ws_up.astype(BF16), ws_down.astype(BF16),
      g.reshape(1, d), b.reshape(1, d))


def _moe_ln(x, w_router, b_router, w_gate, w_up, w_down, ws_gate, ws_up, ws_down, g, b):
    n, d = x.shape
    ne = w_router.shape[1]
    meta, counts = _router(x, w_router, b_router)
    counts = counts.astype(jnp.int32)
    padded = (counts + EXPERT_ROWS - 1) // EXPERT_ROWS * EXPERT_ROWS
    pend = jnp.cumsum(padded)
    dest = _dest_rows(meta, pend - padded)
    n_blk = (n * TOP_K) // EXPERT_ROWS + ne
    blk_row0 = jnp.arange(n_blk, dtype=jnp.int32) * EXPERT_ROWS
    blk_e = jnp.minimum(jnp.sum((pend[None, :] <= blk_row0[:, None]).astype(jnp.int32), axis=1), ne - 1)
    n_used = (pend[-1] // EXPERT_ROWS).astype(jnp.int32).reshape(1)
    xs = _dispatch(dest, x, n_blk * EXPERT_ROWS)
    ys = _expert_ffn(blk_e, n_used, xs, w_gate, w_up, w_down)
    return _combine_ln(dest, meta[TOP_K:2 * TOP_K].T, x, ys, ws_gate, ws_up, ws_down, g, b)


def _block_diag(w):
    nb, bs, _ = w.shape
    eye = jnp.eye(nb, dtype=w.dtype)
    return (eye[:, None, :, None] * w[:, :, None, :]).reshape(nb * bs, nb * bs)


def _even_mixer_ln(x, w_in, conv_w, conv_b, wa, ba, wx, bx, lam, lb, norm_g, w_out, g, b):
    bsz, t, d = x.shape
    xf = x.reshape(bsz * t, d)
    u = _proj(xf, w_in.astype(BF16)).reshape(bsz, t, -1)
    ya = _lru(u, conv_w, conv_b, _block_diag(wa).astype(BF16), ba, _block_diag(wx).astype(BF16), bx, lam)
    yb = _hgrn2(u, lb, norm_g)
    return _proj_ln(ya.reshape(bsz * t, -1), yb.reshape(bsz * t, -1), w_out, xf, g, b).reshape(bsz, t, d)


def _odd_mixer_ln(x, w_in, pe_k, pe_v, wk1, wk2, wv1, wv2, w_out, g, b):
    bsz, t, d = x.shape
    xf = x.reshape(bsz * t, d)
    u16, u32 = _odd_proj(xf, w_in, t)
    u16 = u16.reshape(bsz, t, U16_W)

    def grp(off):
        z = u32[:, off:off + NSA_KV].reshape(bsz, t, NSA_GROUPS, HEAD_DIM)
        return z.transpose(0, 2, 1, 3)

    n_all = NSA_HEADS + MOBA_HEADS
    s_nsa = jnp.asarray((2.0 ** (-8.0 * np.arange(1, n_all + 1) / n_all))[0::2], F32)
    k_cmp, v_cmp = _nsa_compress(grp(0), grp(NSA_KV), pe_k, pe_v, wk1, wk2, wv1, wv2)
    o_c, mask_bias = _nsa_select(s_nsa, u16, k_cmp, v_cmp)
    o_s = _flash("slc", u16, mask_bias, tq=128)
    o_w = _flash("win", u16, tq=128)
    o_m = _flash("moba", u16, tq=MOBA_BLOCK)
    flat = lambda z: z.reshape(bsz * t, -1)
    return _odd_out_ln(flat(o_c), flat(o_s), flat(o_w), flat(o_m), u32, w_out, xf, g, b).reshape(bsz, t, d)


def kernel(x, even_w_in, lru_conv_w, lru_conv_b, lru_wa, lru_ba, lru_wx, lru_bx, lru_lambda,
           hg_lower_bound, hg_norm_g, even_w_out, odd_w_in, nsa_pe_k, nsa_pe_v, nsa_wk1, nsa_wk2,
           nsa_wv1, nsa_wv2, odd_w_out, ln_g, ln_b, w_router, b_router, w_gate, w_up, w_down,
           ws_gate, ws_up, ws_down):
    bsz, t, d = x.shape
    lb_all = jnp.cumsum(jax.nn.softmax(hg_lower_bound.astype(F32), axis=0), axis=0)
    for layer in range(DEPTH):
        li = layer // 2
        if layer % 2 == 0:
            x = _even_mixer_ln(x, even_w_in[li], lru_conv_w[li], lru_conv_b[li], lru_wa[li], lru_ba[li],
                               lru_wx[li], lru_bx[li], lru_lambda[li], lb_all[layer], hg_norm_g[li],
                               even_w_out[li], ln_g[layer, 0], ln_b[layer, 0])
        else:
            x = _odd_mixer_ln(x, odd_w_in[li], nsa_pe_k[li], nsa_pe_v[li], nsa_wk1[li], nsa_wk2[li],
                              nsa_wv1[li], nsa_wv2[li], odd_w_out[li], ln_g[layer, 0], ln_b[layer, 0])
        x = _moe_ln(x.reshape(bsz * t, d), w_router[layer], b_router[layer], w_gate[layer], w_up[layer],
                    w_down[layer], ws_gate[layer], ws_up[layer], ws_down[layer],
                    ln_g[layer, 1], ln_b[layer, 1]).reshape(bsz, t, d)
    return x
```

```python
import functools

import numpy as np
import jax
import jax.numpy as jnp
from jax import lax
from jax.experimental import pallas as pl
from jax.experimental.pallas import tpu as pltpu

F32 = jnp.float32
BF16 = jnp.bfloat16

D_MODEL = 1024
DEPTH = 2
LRU_WIDTH = 512
LRU_BLOCKS = 8
LRU_BLOCK = LRU_WIDTH // LRU_BLOCKS
CONV_W = 4
LRU_C = 8.0
HG_HEADS = 4
HG_DK = 128
HG_FDIM = HG_HEADS * HG_DK
HG_VDIM = HG_HEADS * HG_DK
HEAD_DIM = 64
NSA_HEADS = 8
NSA_GROUPS = 2
NSA_HPG = NSA_HEADS // NSA_GROUPS
NSA_KV = NSA_GROUPS * HEAD_DIM
CMP_LEN = 32
CMP_STRIDE = 16
CMP_HIDDEN = 128
SLC_LEN = 64
SLC_TOPN = 16
WIN = 512
FORCE_BONUS = 1e4
MOBA_HEADS = 8
MOBA_BLOCK = 256
MOBA_TOPK = 3
N_EXPERTS = 256
TOP_K = 8
N_GROUPS = 8
TOPK_GROUPS = 4
EXPERT_HIDDEN = 256
ROUTED_SCALE = 2.5
DN_ALPHA = (2 * DEPTH) ** 0.25
LN_EPS = 1e-5
NEG = -1e30
NEG_INF = float("-inf")

LANES = 128
SUBLANES = 8
VMEM_LIMIT = 48 * 1024 * 1024

HG_SUB = 16
EXPERT_ROWS = 256


def _cparams(sem):
    return pltpu.CompilerParams(dimension_semantics=sem, vmem_limit_bytes=VMEM_LIMIT)


def _dot(a, b):
    return jnp.dot(a.astype(BF16), b.astype(BF16), preferred_element_type=F32)


def _dot_nt(a, b):
    return lax.dot_general(a.astype(BF16), b.astype(BF16), (((1,), (1,)), ((), ())),
                           preferred_element_type=F32)


def _dot_tn(a, b):
    return lax.dot_general(a.astype(BF16), b.astype(BF16), (((0,), (0,)), ((), ())),
                           preferred_element_type=F32)


def _split2(x):
    hi = x.astype(BF16)
    lo = (x - hi.astype(F32)).astype(BF16)
    return hi, lo


def _split3(x):
    hi = x.astype(BF16)
    r = x - hi.astype(F32)
    mid = r.astype(BF16)
    lo = (r - mid.astype(F32)).astype(BF16)
    return hi, mid, lo


def _layer_norm(v, g, b):
    mu = jnp.mean(v, axis=-1, keepdims=True)
    d = v - mu
    var = jnp.mean(d * d, axis=-1, keepdims=True)
    return d * lax.rsqrt(var + LN_EPS) * g + b


def _sigmoid(x):
    return 1.0 / (1.0 + jnp.exp(-x))


def _silu(x):
    return x * _sigmoid(x)


def _proj_kernel(x_ref, w_ref, o_ref):
    o_ref[...] = jnp.dot(x_ref[...].astype(BF16), w_ref[...], preferred_element_type=F32)


def _proj(x, w, tm=256):
    n, k = x.shape
    m = w.shape[1]
    return pl.pallas_call(
        _proj_kernel,
        grid=(n // tm,),
        in_specs=[pl.BlockSpec((tm, k), lambda i: (i, 0)),
                  pl.BlockSpec((k, m), lambda i: (0, 0))],
        out_specs=pl.BlockSpec((tm, m), lambda i: (i, 0)),
        out_shape=jax.ShapeDtypeStruct((n, m), F32),
        compiler_params=_cparams(("parallel",)),
        name="proj",
    )(x, w)


def _proj_ln_kernel(ya_ref, yb_ref, wa_ref, wb_ref, x_ref, g_ref, b_ref, o_ref):
    mix = (jnp.dot(ya_ref[...].astype(BF16), wa_ref[...], preferred_element_type=F32)
           + jnp.dot(yb_ref[...].astype(BF16), wb_ref[...], preferred_element_type=F32))
    o_ref[...] = _layer_norm(DN_ALPHA * x_ref[...] + mix, g_ref[...], b_ref[...])


def _proj_ln(ya, yb, w, xres, g, b, tm=256):
    n, ka = ya.shape
    kb = yb.shape[1]
    d = w.shape[1]
    wb16 = w.astype(BF16)
    full = lambda shp: pl.BlockSpec(shp, lambda i: (0, 0))
    return pl.pallas_call(
        _proj_ln_kernel,
        grid=(n // tm,),
        in_specs=[pl.BlockSpec((tm, ka), lambda i: (i, 0)),
                  pl.BlockSpec((tm, kb), lambda i: (i, 0)),
                  full((ka, d)), full((kb, d)),
                  pl.BlockSpec((tm, d), lambda i: (i, 0)),
                  full((1, d)), full((1, d))],
        out_specs=pl.BlockSpec((tm, d), lambda i: (i, 0)),
        out_shape=jax.ShapeDtypeStruct((n, d), F32),
        compiler_params=_cparams(("parallel",)),
        name="proj_ln",
    )(ya, yb, wb16[:ka], wb16[ka:], xres, g.reshape(1, d), b.reshape(1, d))


U16_QA = 0
U16_QM = U16_QA + NSA_HEADS * LANES
U16_KS = U16_QM + MOBA_HEADS * LANES
U16_KW = U16_KS + NSA_GROUPS * LANES
U16_KM = U16_KW + NSA_GROUPS * LANES
U16_VS = U16_KM + MOBA_HEADS * LANES
U16_VW = U16_VS + NSA_GROUPS * LANES
U16_VM = U16_VW + NSA_GROUPS * LANES
U16_W = U16_VM + MOBA_HEADS * LANES
U32_W = 3 * LANES


def _odd_proj_kernel(x_ref, w16_ref, w32_ref, qc_ref, kt_ref, o16_ref, o32_ref, km_ref):
    xb = x_ref[...].astype(BF16)
    acc = jnp.dot(xb, w16_ref[...], preferred_element_type=F32)
    km_ref[0] = jnp.mean(acc[:, U16_KM:U16_VS], axis=0, keepdims=True)
    o16_ref[:, 0:U16_KS] = (acc[:, 0:U16_KS] + qc_ref[...]).astype(BF16)
    o16_ref[:, U16_KS:U16_VS] = (acc[:, U16_KS:U16_VS] + kt_ref[...]).astype(BF16)
    o16_ref[:, U16_VS:U16_W] = acc[:, U16_VS:U16_W].astype(BF16)
    o32_ref[...] = jnp.dot(xb, w32_ref[...], preferred_element_type=F32)


def _odd_proj(x, w_in, t, tm=MOBA_BLOCK):
    n, d = x.shape
    assert tm == MOBA_BLOCK and t % tm == 0
    dh = HEAD_DIM
    nq = NSA_HEADS * dh
    nm = MOBA_HEADS * dh
    sizes = [nq] + [NSA_KV] * 6 + [3 * NSA_HEADS] + [nm] * 3
    offs = np.concatenate([[0], np.cumsum(sizes)])
    q, kc, vc, ks, vs, kw, vw, gates, mq, mk, mv = [w_in[:, offs[i]:offs[i + 1]] for i in range(11)]
    scale = dh ** -0.5

    def heads(w, nh, mult=1.0):
        w = (w * mult).reshape(d, nh, dh)
        return jnp.pad(w, ((0, 0), (0, 0), (0, LANES - dh))).reshape(d, nh * LANES)

    w16 = jnp.concatenate([heads(q, NSA_HEADS, scale), heads(mq, MOBA_HEADS, scale), heads(ks, NSA_GROUPS),
                           heads(kw, NSA_GROUPS), heads(mk, MOBA_HEADS), heads(vs, NSA_GROUPS),
                           heads(vw, NSA_GROUPS), heads(mv, MOBA_HEADS)], axis=1).astype(BF16)
    w32 = jnp.concatenate([kc, vc, jnp.pad(gates, ((0, 0), (0, LANES - 3 * NSA_HEADS)))], axis=1).astype(BF16)

    n_all = NSA_HEADS + MOBA_HEADS
    s_all = 2.0 ** (-8.0 * np.arange(1, n_all + 1) / n_all)
    slopes = jnp.asarray(np.concatenate([s_all[0::2], s_all[1::2]]), F32)
    s_hi = slopes.astype(BF16).astype(F32)
    s_lo = slopes - s_hi
    scols = jnp.stack([POS_SPLIT * s_hi, POS_SPLIT * s_lo, s_hi, s_lo], axis=-1)
    qconst = jnp.pad(scols, ((0, 0), (dh, LANES - dh - 4))).reshape(1, U16_KS)

    pos = np.arange(t)
    pos_cols = np.stack([pos // POS_SPLIT, pos // POS_SPLIT, pos % POS_SPLIT, pos % POS_SPLIT], axis=-1)

    def key_cols(blk):
        cols = np.zeros((t, LANES), np.float32)
        cols[:, dh:dh + 4] = pos_cols
        if blk:
            cols[pos, MASK_COL0 + pos // blk] = 1.0
        return cols

    ktab = np.concatenate([key_cols(SLC_LEN)] * NSA_GROUPS + [key_cols(0)] * NSA_GROUPS
                          + [key_cols(MOBA_BLOCK)] * MOBA_HEADS, axis=1)
    tiles_per_seq = t // tm
    return pl.pallas_call(
        _odd_proj_kernel,
        grid=(n // tm,),
        in_specs=[pl.BlockSpec((tm, d), lambda i: (i, 0)),
                  pl.BlockSpec((d, U16_W), lambda i: (0, 0)),
                  pl.BlockSpec((d, U32_W), lambda i: (0, 0)),
                  pl.BlockSpec((1, U16_KS), lambda i: (0, 0)),
                  pl.BlockSpec((tm, U16_VS - U16_KS), lambda i: (i % tiles_per_seq, 0))],
        out_specs=[pl.BlockSpec((tm, U16_W), lambda i: (i, 0)),
                   pl.BlockSpec((tm, U32_W), lambda i: (i, 0)),
                   pl.BlockSpec((1, 1, U16_VS - U16_KM), lambda i: (i, 0, 0))],
        out_shape=[jax.ShapeDtypeStruct((n, U16_W), BF16), jax.ShapeDtypeStruct((n, U32_W), F32),
                   jax.ShapeDtypeStruct((n // tm, 1, U16_VS - U16_KM), F32)],
        compiler_params=_cparams(("parallel",)),
        name="odd_proj",
    )(x, w16, w32, qconst, jnp.asarray(ktab, F32))


def _odd_out_kernel(oc_ref, os_ref, ow_ref, om_ref, gt_ref, wa_ref, wb_ref, x_ref, g_ref, b_ref, o_ref):
    sg = _sigmoid(gt_ref[...])
    parts = []
    for h in range(NSA_HEADS):
        cols = slice(h * LANES, (h + 1) * LANES)
        parts.append(sg[:, 3 * h:3 * h + 1] * oc_ref[:, cols] + sg[:, 3 * h + 1:3 * h + 2] * os_ref[:, cols]
                     + sg[:, 3 * h + 2:3 * h + 3] * ow_ref[:, cols])
    yc = jnp.concatenate(parts, axis=1)
    mix = _dot(yc, wa_ref[...]) + _dot(om_ref[...], wb_ref[...])
    o_ref[...] = _layer_norm(DN_ALPHA * x_ref[...] + mix, g_ref[...], b_ref[...])


def _odd_out_ln(o_c, o_s, o_w, o_m, u32, w_out, xres, g, b, tm=256):
    n, d = xres.shape
    dh = HEAD_DIM

    def pad_rows(w, nh):
        return jnp.pad(w.reshape(nh, dh, d), ((0, 0), (0, LANES - dh), (0, 0))).reshape(nh * LANES, d).astype(BF16)

    wa = pad_rows(w_out[:NSA_HEADS * dh], NSA_HEADS)
    wb = pad_rows(w_out[NSA_HEADS * dh:], MOBA_HEADS)
    row = lambda width: pl.BlockSpec((tm, width), lambda i: (i, 0))
    full = lambda shp: pl.BlockSpec(shp, lambda i: (0, 0))
    ka = NSA_HEADS * LANES
    kb = MOBA_HEADS * LANES
    return pl.pallas_call(
        _odd_out_kernel,
        grid=(n // tm,),
        in_specs=[row(ka), row(ka), row(ka), row(kb),
                  pl.BlockSpec((tm, LANES), lambda i: (i, U32_W // LANES - 1)),
                  full((ka, d)), full((kb, d)), row(d), full((1, d)), full((1, d))],
        out_specs=row(d),
        out_shape=jax.ShapeDtypeStruct((n, d), F32),
        compiler_params=_cparams(("parallel",)),
        name="odd_out_ln",
    )(o_c, o_s, o_w, o_m, u32, wa, wb, xres, g.reshape(1, d), b.reshape(1, d))


def _lru_kernel(xb_ref, gb_ref, cw_ref, cb_ref, wa_ref, ba_ref, wx_ref, bx_ref, lam_ref,
                o_ref, xs_ref, a_ref, b_ref, h_ref):
    ti = pl.program_id(1)
    tc = xb_ref.shape[1]
    c = xb_ref.shape[2]

    @pl.when(ti == 0)
    def _():
        xs_ref[0:SUBLANES, :] = jnp.zeros((SUBLANES, c), F32)
        h_ref[...] = jnp.zeros_like(h_ref)

    xs_ref[SUBLANES:SUBLANES + tc, :] = xb_ref[0]
    xc = cb_ref[...]
    for w in range(CONV_W):
        xc = xc + cw_ref[w:w + 1, :] * xs_ref[pl.ds(SUBLANES - (CONV_W - 1) + w, tc), :]
    xs_ref[0:SUBLANES, :] = xs_ref[tc:tc + SUBLANES, :]

    r = _sigmoid(_dot(xc, wa_ref[...]) + ba_ref[...])
    i = _sigmoid(_dot(xc, wx_ref[...]) + bx_ref[...])
    nl = -lam_ref[...]
    softplus = jnp.maximum(nl, 0.0) + jnp.log(1.0 + jnp.exp(-jnp.abs(nl)))
    log_a = (-LRU_C * r) * softplus
    a = jnp.exp(log_a)
    mult = jnp.sqrt(1.0 - jnp.exp(2.0 * log_a))
    trow = ti * tc + lax.broadcasted_iota(jnp.int32, (tc, c), 0)
    mult = jnp.where(trow == 0, 1.0, mult)
    a_ref[...] = a
    b_ref[...] = mult * (i * xc)

    row = lax.broadcasted_iota(jnp.int32, (SUBLANES, c), 0)

    def body(g, carry):
        r0 = pl.multiple_of(g * SUBLANES, SUBLANES)
        av = a_ref[pl.ds(r0, SUBLANES), :]
        bv = b_ref[pl.ds(r0, SUBLANES), :]
        for d in (1, 2, 4):
            a_sh = pltpu.roll(av, d, 0)
            b_sh = pltpu.roll(bv, d, 0)
            m = row >= d
            bv = jnp.where(m, av * b_sh + bv, bv)
            av = jnp.where(m, av * a_sh, av)
        h = av * carry + bv
        b_ref[pl.ds(r0, SUBLANES), :] = h
        return h[SUBLANES - 1:SUBLANES, :]

    h_ref[...] = lax.fori_loop(0, tc // SUBLANES, body, h_ref[...])
    o_ref[0] = b_ref[...] * jax.nn.gelu(gb_ref[0])


def _lru(u3, conv_w, conv_b, wa_bd, ba, wx_bd, bx, lam, tc=256):
    b, t, _ = u3.shape
    c = LRU_WIDTH
    row = lambda v: v.reshape(1, c)
    full = lambda shp: pl.BlockSpec(shp, lambda bi, ti: (0, 0))
    return pl.pallas_call(
        _lru_kernel,
        grid=(b, t // tc),
        in_specs=[pl.BlockSpec((1, tc, c), lambda bi, ti: (bi, ti, 0)),
                  pl.BlockSpec((1, tc, c), lambda bi, ti: (bi, ti, 1)),
                  full((CONV_W, c)), full((1, c)), full((c, c)), full((1, c)),
                  full((c, c)), full((1, c)), full((1, c))],
        out_specs=pl.BlockSpec((1, tc, c), lambda bi, ti: (bi, ti, 0)),
        out_shape=jax.ShapeDtypeStruct((b, t, c), F32),
        scratch_shapes=[pltpu.VMEM((tc + SUBLANES, c), F32), pltpu.VMEM((tc, c), F32),
                        pltpu.VMEM((tc, c), F32), pltpu.VMEM((1, c), F32)],
        compiler_params=_cparams(("parallel", "arbitrary")),
        name="rg_lru",
    )(u3, u3, conv_w, row(conv_b), wa_bd, row(ba), wx_bd, row(bx), row(lam))


def _hgrn_kernel(q_ref, f_ref, v_ref, g_ref, lb_ref, ng_ref, tri_ref, ones_ref, o_ref,
                 st_ref, q_s, k_s, c_s, qe_s, kd_s, dl_s, o_s):
    ti = pl.program_id(2)
    tc = q_ref.shape[1]
    dk = q_ref.shape[2]

    @pl.when(ti == 0)
    def _():
        st_ref[...] = jnp.zeros_like(st_ref)

    lbv = lb_ref[...]
    q = _silu(q_ref[0])
    f = lbv + (1.0 - lbv) * _sigmoid(f_ref[0])
    kk = 1.0 - f
    logf = jnp.log(f)
    parts = _split3(logf)
    tri = tri_ref[...]
    ones = ones_ref[...]
    cum = sum(jnp.dot(tri, p, preferred_element_type=F32) for p in parts)
    last = sum(jnp.dot(ones, p, preferred_element_type=F32) for p in parts)
    q_s[...] = q
    k_s[...] = kk
    c_s[...] = cum
    qe_s[...] = q * jnp.exp(cum)
    kd_s[...] = kk * jnp.exp(last - cum)
    dl_s[...] = jnp.exp(last)

    s_idx = lax.broadcasted_iota(jnp.int32, (HG_SUB, dk), 0)

    def chunk(ci, _):
        r0 = pl.multiple_of(ci * HG_SUB, HG_SUB)
        rows = pl.ds(r0, HG_SUB)
        st = st_ref[...]
        vc = v_ref[0, rows, :]
        qc = q_s[rows, :]
        kc = k_s[rows, :]
        cc = c_s[rows, :]
        o = _dot_nt(qe_s[rows, :], st)
        for t in range(HG_SUB):
            d = cc[t:t + 1, :] - cc
            dec = jnp.exp(jnp.where(s_idx <= t, d, NEG))
            w = (qc[t:t + 1, :] * kc) * dec
            a_t = jnp.sum(w, axis=1, keepdims=True)
            o_t = jnp.sum(a_t * vc, axis=0, keepdims=True)
            o = o + jnp.where(s_idx == t, o_t, 0.0)
        o_s[rows, :] = o
        st_ref[...] = st * dl_s[pl.ds(r0, 1), :] + _dot_tn(vc, kd_s[rows, :])
        return 0

    lax.fori_loop(0, tc // HG_SUB, chunk, 0, unroll=4)
    o = o_s[...]
    o = o * lax.rsqrt(jnp.mean(o * o, axis=-1, keepdims=True) + 1e-6) * ng_ref[...]
    o_ref[0] = o * _sigmoid(g_ref[0])


def _hgrn2(u3, lb, norm_g, tc=256):
    b, t, _ = u3.shape
    dk = HG_DK
    base = (2 * LRU_WIDTH) // dk
    idx = np.arange(tc)
    same = (idx[:, None] // HG_SUB) == (idx[None, :] // HG_SUB)
    tri = jnp.asarray(same & (idx[None, :] <= idx[:, None]), BF16)
    ones = jnp.asarray(same, BF16)

    def col(off):
        return pl.BlockSpec((1, tc, dk), lambda bi, hi, ti: (bi, ti, base + off * HG_HEADS + hi))

    vec = pl.BlockSpec((1, dk), lambda bi, hi, ti: (0, hi))
    cst = pl.BlockSpec((tc, tc), lambda bi, hi, ti: (0, 0))
    return pl.pallas_call(
        _hgrn_kernel,
        grid=(b, HG_HEADS, t // tc),
        in_specs=[col(0), col(1), col(2), col(3), vec, vec, cst, cst],
        out_specs=pl.BlockSpec((1, tc, dk), lambda bi, hi, ti: (bi, ti, hi)),
        out_shape=jax.ShapeDtypeStruct((b, t, HG_VDIM), F32),
        scratch_shapes=[pltpu.VMEM((dk, dk), F32)] + [pltpu.VMEM((tc, dk), F32)] * 7,
        compiler_params=_cparams(("parallel", "parallel", "arbitrary")),
        name="hgrn2",
    )(u3, u3, u3, u3, lb.reshape(1, HG_FDIM), norm_g.reshape(1, HG_VDIM), tri, ones)


def _nsa_cmp_kv_kernel(zk_ref, zv_ref, pek_ref, pev_ref, wk1_ref, wk2_ref, wv1_ref, wv2_ref,
                       ok_ref, ov_ref):
    half = zk_ref.shape[3]
    nrow = zk_ref.shape[2]

    def compress(z, pe_ref, w1_ref, w2_ref):
        lo = _dot(z + pe_ref[0:1, :], w1_ref[0:half, :])
        hi = _dot(z + pe_ref[1:2, :], w1_ref[half:2 * half, :])
        pre = lo + pltpu.roll(hi, nrow - 1, 0)
        return _dot(jax.nn.gelu(pre), w2_ref[...])

    ok_ref[0, 0] = compress(zk_ref[0, 0], pek_ref, wk1_ref, wk2_ref)
    ov_ref[0, 0] = compress(zv_ref[0, 0], pev_ref, wv1_ref, wv2_ref)


def _nsa_compress(kc, vc, pe_k, pe_v, wk1, wk2, wv1, wv2):
    b, g, t, dh = kc.shape
    nrow = t // CMP_STRIDE
    half = CMP_STRIDE * dh
    zk = kc.reshape(b, g, nrow, half)
    zv = vc.reshape(b, g, nrow, half)
    zspec = pl.BlockSpec((1, 1, nrow, half), lambda bi, gi: (bi, gi, 0, 0))
    full = lambda shp: pl.BlockSpec(shp, lambda bi, gi: (0, 0))
    ospec = pl.BlockSpec((1, 1, nrow, LANES), lambda bi, gi: (bi, gi, 0, 0))
    oshape = jax.ShapeDtypeStruct((b, g, nrow, LANES), F32)
    pad = lambda w: jnp.pad(w, ((0, 0), (0, LANES - dh))).astype(BF16)
    return pl.pallas_call(
        _nsa_cmp_kv_kernel,
        grid=(b, g),
        in_specs=[zspec, zspec, full((2, half)), full((2, half)),
                  full((2 * half, CMP_HIDDEN)), full((CMP_HIDDEN, LANES)),
                  full((2 * half, CMP_HIDDEN)), full((CMP_HIDDEN, LANES))],
        out_specs=[ospec, ospec],
        out_shape=[oshape, oshape],
        compiler_params=_cparams(("parallel", "parallel")),
        name="nsa_compress",
    )(zk, zv, pe_k.reshape(2, half), pe_v.reshape(2, half),
      wk1.astype(BF16), pad(wk2), wv1.astype(BF16), pad(wv2))


def _nsa_sel_kernel(slope_ref, q_ref, kc_ref, vc_ref, ov_ref, o_ref, mb_ref):
    gi = pl.program_id(0) % NSA_GROUPS
    qi = pl.program_id(1)
    tq = q_ref.shape[1]
    hpg = q_ref.shape[2] // LANES
    nc = kc_ref.shape[2]
    ns = ov_ref.shape[0]

    q = jnp.concatenate([q_ref[0, :, i * LANES:(i + 1) * LANES] for i in range(hpg)], axis=0)
    khi, klo = _split2(kc_ref[0, 0])
    s = _dot_nt(q, khi) + _dot_nt(q, klo)
    tpos = qi * tq + lax.broadcasted_iota(jnp.int32, (tq, nc), 0)
    cend = lax.broadcasted_iota(jnp.int32, (tq, nc), 1) * CMP_STRIDE + (CMP_LEN - 1)
    d_c = (tpos - cend).astype(F32)
    valid = d_c >= 0.0
    ps = []
    for i in range(hpg):
        si = s[i * tq:(i + 1) * tq, :] - slope_ref[gi * hpg + i] * d_c
        si = jnp.where(valid, si, NEG)
        m = jnp.max(si, axis=-1, keepdims=True)
        p = jnp.where(valid, jnp.exp(si - m), 0.0)
        ps.append(p / jnp.maximum(jnp.sum(p, axis=-1, keepdims=True), 1e-30))
    oc = _dot(jnp.concatenate(ps, axis=0), vc_ref[0, 0])
    o_ref[0] = jnp.concatenate([oc[i * tq:(i + 1) * tq, :] for i in range(hpg)], axis=1)

    psum = ps[0]
    for i in range(1, hpg):
        psum = psum + ps[i]
    phi, plo = _split2(psum)
    ovl = ov_ref[...]
    imp = _dot_nt(ovl, phi) + _dot_nt(ovl, plo)

    j = lax.broadcasted_iota(jnp.int32, (ns, tq), 0)
    own = (qi * tq + lax.broadcasted_iota(jnp.int32, (ns, tq), 1)) // SLC_LEN
    forced = (j == 0) | (j == own) | (j == own - 1)
    imp = jnp.where(j <= own, imp + FORCE_BONUS * forced.astype(F32), NEG)
    rank = jnp.zeros((ns, tq), F32)
    for jp in range(ns):
        row = imp[jp:jp + 1, :]
        beats = (row > imp) | ((row == imp) & (j > jp))
        rank = rank + beats.astype(F32)
    bias = jnp.where(rank < float(min(SLC_TOPN, ns)), 0.0, MASKED)
    place = (lax.broadcasted_iota(jnp.int32, (ns, LANES), 1)
             == lax.broadcasted_iota(jnp.int32, (ns, LANES), 0) + MASK_COL0)
    mb_ref[0] = _dot_tn(bias, place.astype(F32)).astype(BF16)


def _nsa_select(slopes, u16, k_cmp, v_cmp, tq=256):
    b, t, _ = u16.shape
    g = NSA_GROUPS
    hpg = NSA_HPG
    nc = k_cmp.shape[2]
    ns = t // SLC_LEN
    ci = np.arange(nc)[:, None]
    cj = np.arange(ns)[None, :]
    ovl = ((CMP_STRIDE * ci < SLC_LEN * (cj + 1)) & (CMP_STRIDE * ci + CMP_LEN > SLC_LEN * cj))
    ovl = ovl & (ci < (t - CMP_LEN) // CMP_STRIDE + 1)
    grid_spec = pltpu.PrefetchScalarGridSpec(
        num_scalar_prefetch=1,
        grid=(b * g, t // tq),
        in_specs=[pl.BlockSpec((1, tq, hpg * LANES), lambda bg, qi, s: (bg // g, qi, U16_QA // (hpg * LANES) + bg % g)),
                  pl.BlockSpec((1, 1, nc, LANES), lambda bg, qi, s: (bg // g, bg % g, 0, 0)),
                  pl.BlockSpec((1, 1, nc, LANES), lambda bg, qi, s: (bg // g, bg % g, 0, 0)),
                  pl.BlockSpec((ns, nc), lambda bg, qi, s: (0, 0))],
        out_specs=[pl.BlockSpec((1, tq, hpg * LANES), lambda bg, qi, s: (bg // g, qi, bg % g)),
                   pl.BlockSpec((1, tq, LANES), lambda bg, qi, s: (bg // g, qi, bg % g))],
    )
    return pl.pallas_call(
        _nsa_sel_kernel,
        grid_spec=grid_spec,
        out_shape=[jax.ShapeDtypeStruct((b, t, g * hpg * LANES), F32),
                   jax.ShapeDtypeStruct((b, t, g * LANES), BF16)],
        compiler_params=_cparams(("parallel", "parallel")),
        name="nsa_select",
    )(slopes, u16, k_cmp, v_cmp, jnp.asarray(ovl.T, BF16))


POS_SPLIT = 64
MASKED = -2e30


MASK_COL0 = HEAD_DIM + 4
KEY_STEP = 512


def _flash_kernel(q_ref, k_ref, v_ref, *rest, mode, nkv, tq):
    o_ref = rest[-1]
    hpg = q_ref.shape[2] // (LANES * nkv)
    t_all = k_ref.shape[1]
    nt = q_ref.shape[1] // tq
    whole = nt * tq == t_all
    step = MOBA_BLOCK if mode == "moba" else KEY_STEP

    def tile(ti, q0):
        rows = slice(ti * tq, (ti + 1) * tq)
        qs = []
        for j in range(nkv):
            kv_cols = slice(j * LANES, (j + 1) * LANES)
            heads = [q_ref[0, rows, (j * hpg + i) * LANES:(j * hpg + i + 1) * LANES] for i in range(hpg)]
            if mode == "slc":
                heads = [h + rest[0][0, rows, kv_cols] for h in heads]
            q = heads[0] if hpg == 1 else jnp.concatenate(heads, axis=0)
            if mode == "moba":
                kmean = rest[0][0, :, kv_cols]
                nb = kmean.shape[0]
                mhi, mlo = _split2(kmean)
                gate = _dot_nt(mhi, q) + _dot_nt(mlo, q)
                jb = lax.broadcasted_iota(jnp.int32, (nb, tq), 0)
                ob = q0 // MOBA_BLOCK
                gate = jnp.where(jb < ob, gate, NEG)
                rank = jnp.zeros((nb, tq), F32)
                for jp in range(nb):
                    row = gate[jp:jp + 1, :]
                    beats = (row > gate) | ((row == gate) & (jb > jp))
                    rank = rank + beats.astype(F32)
                n_sel = float(min(MOBA_TOPK, nb - 1))
                bsel = ((rank < n_sel) & (jb < ob)) | (jb == ob)
                bias = jnp.where(bsel, 0.0, MASKED)
                place = (lax.broadcasted_iota(jnp.int32, (nb, LANES), 1)
                         == lax.broadcasted_iota(jnp.int32, (nb, LANES), 0) + MASK_COL0)
                q = q + _dot_tn(bias, place.astype(F32)).astype(BF16)
            qs.append(q)

        def attend(kstart, klen, diag, off):
            rel = lax.broadcasted_iota(jnp.int32, (tq, diag), 0) - lax.broadcasted_iota(jnp.int32, (tq, diag), 1)
            valid = rel >= off
            if mode == "win":
                valid = valid & (rel < off + WIN)
            if hpg > 1:
                valid = jnp.concatenate([valid] * hpg, axis=0)
            for j in range(nkv):
                kv_cols = slice(j * LANES, (j + 1) * LANES)
                s = _dot_nt(qs[j], k_ref[0, pl.ds(kstart, klen), kv_cols])
                sd = jnp.where(valid, s[:, klen - diag:], MASKED)
                s = sd if diag == klen else jnp.concatenate([s[:, :klen - diag], sd], axis=1)
                m = jnp.maximum(jnp.max(s, axis=-1, keepdims=True), NEG)
                p = jnp.exp(s - m)
                l = jnp.sum(p, axis=-1, keepdims=True)
                o = _dot(p, v_ref[0, pl.ds(kstart, klen), kv_cols]) / jnp.maximum(l, 1e-30)
                for i in range(hpg):
                    o_ref[0, rows, (j * hpg + i) * LANES:(j * hpg + i + 1) * LANES] = o[i * tq:(i + 1) * tq, :]

        if mode == "win":
            kstart = max(q0 - WIN, 0) if whole else pl.multiple_of(jnp.maximum(q0 - WIN, 0), LANES)
            attend(kstart, WIN + tq, WIN + tq, kstart - q0)
        elif whole:
            v = (q0 + tq - 1) // step
            attend(0, step * (v + 1), step, step * v - q0)
        else:
            var = (q0 + tq - 1) // step
            for v in range(t_all // step):
                @pl.when(var == v)
                def _(v=v):
                    attend(0, step * (v + 1), step, step * v - q0)

    for ti in range(nt):
        tile(ti, ti * tq if whole else (pl.program_id(1) * nt + ti) * tq)


def _flash(mode, u16, extra=None, tq=128, nkv=1, tiles_per_step=1):
    b, t, _ = u16.shape
    q_off, k_off, v_off, n_kvh, hpg = {
        "slc": (U16_QA, U16_KS, U16_VS, NSA_GROUPS, NSA_HPG),
        "win": (U16_QA, U16_KW, U16_VW, NSA_GROUPS, NSA_HPG),
        "moba": (U16_QM, U16_KM, U16_VM, MOBA_HEADS, 1)}[mode]
    assert mode != "moba" or tq == MOBA_BLOCK
    steps = n_kvh // nkv
    qw = nkv * hpg * LANES
    kw = nkv * LANES
    tb = tq * tiles_per_step
    in_specs = [pl.BlockSpec((1, tb, qw), lambda bh, qi: (bh // steps, qi, q_off // qw + bh % steps)),
                pl.BlockSpec((1, t, kw), lambda bh, qi: (bh // steps, 0, k_off // kw + bh % steps)),
                pl.BlockSpec((1, t, kw), lambda bh, qi: (bh // steps, 0, v_off // kw + bh % steps))]
    args = [u16, u16, u16]
    if mode == "slc":
        in_specs.append(pl.BlockSpec((1, tb, kw), lambda bh, qi: (bh // steps, qi, bh % steps)))
        args.append(extra)
    elif mode == "moba":
        in_specs.append(pl.BlockSpec((1, t // MOBA_BLOCK, kw), lambda bh, qi: (bh // steps, 0, bh % steps)))
        args.append(extra)
    return pl.pallas_call(
        functools.partial(_flash_kernel, mode=mode, nkv=nkv, tq=tq),
        grid=(b * steps, t // tb),
        in_specs=in_specs,
        out_specs=pl.BlockSpec((1, tb, qw), lambda bh, qi: (bh // steps, qi, bh % steps)),
        out_shape=jax.ShapeDtypeStruct((b, t, n_kvh * hpg * LANES), F32),
        compiler_params=_cparams(("parallel", "parallel")),
        name="flash_" + mode,
    )(*args)


def _router_kernel(x_ref, whi_ref, wlo_ref, b_ref, triu_ref, meta_ref, cnt_ref, carry_ref):
    step = pl.program_id(0)
    tm = x_ref.shape[0]
    ne = whi_ref.shape[0]
    gsz = ne // N_GROUPS

    @pl.when(step == 0)
    def _():
        carry_ref[...] = jnp.zeros_like(carry_ref)

    xhi, xlo = _split2(x_ref[...])
    whi = whi_ref[...]
    logits = _dot_nt(whi, xhi) + _dot_nt(wlo_ref[...], xhi) + _dot_nt(whi, xlo)
    scores = _sigmoid(logits)
    biased = scores + b_ref[:, 0:1]

    v3 = biased.reshape(N_GROUPS, gsz, tm)
    i3 = lax.broadcasted_iota(jnp.int32, (N_GROUPS, gsz, tm), 1).astype(F32)
    m1 = jnp.max(v3, axis=1, keepdims=True)
    idx1 = jnp.min(jnp.where(v3 == m1, i3, float(gsz)), axis=1, keepdims=True)
    m2 = jnp.max(jnp.where(i3 == idx1, NEG_INF, v3), axis=1, keepdims=True)
    gs = (m1 + m2).reshape(N_GROUPS, tm)
    gi = lax.broadcasted_iota(jnp.int32, (N_GROUPS, tm), 0).astype(F32)
    gmask = jnp.zeros((N_GROUPS, tm), F32)
    for _ in range(TOPK_GROUPS):
        m = jnp.max(gs, axis=0, keepdims=True)
        pick = jnp.min(jnp.where(gs == m, gi, float(N_GROUPS)), axis=0, keepdims=True)
        hit = gi == pick
        gmask = jnp.where(hit, 1.0, gmask)
        gs = jnp.where(hit, NEG_INF, gs)
    emask = jnp.broadcast_to(gmask.reshape(N_GROUPS, 1, tm), (N_GROUPS, gsz, tm)).reshape(ne, tm)
    cur = jnp.where(emask > 0.5, biased, NEG)

    ei = lax.broadcasted_iota(jnp.int32, (ne, tm), 0).astype(F32)
    selm = jnp.zeros((ne, tm), F32)
    idxs, svals = [], []
    for _ in range(TOP_K):
        m = jnp.max(cur, axis=0, keepdims=True)
        idx = jnp.min(jnp.where(cur == m, ei, float(ne)), axis=0, keepdims=True)
        hit = ei == idx
        svals.append(jnp.sum(jnp.where(hit, scores, 0.0), axis=0, keepdims=True))
        idxs.append(idx)
        cur = jnp.where(hit, NEG_INF, cur)
        selm = jnp.where(hit, 1.0, selm)

    carry = carry_ref[:, 0:1]
    rank_full = jnp.dot(selm.astype(BF16), triu_ref[...], preferred_element_type=F32) + carry
    ssum = svals[0]
    for k in range(1, TOP_K):
        ssum = ssum + svals[k]
    for k in range(TOP_K):
        meta_ref[k:k + 1, :] = idxs[k]
        meta_ref[TOP_K + k:TOP_K + k + 1, :] = svals[k] / ssum * ROUTED_SCALE
        meta_ref[2 * TOP_K + k:2 * TOP_K + k + 1, :] = jnp.sum(
            jnp.where(ei == idxs[k], rank_full, 0.0), axis=0, keepdims=True)
    new_carry = carry + jnp.sum(selm, axis=1, keepdims=True)
    carry_ref[...] = jnp.broadcast_to(new_carry, carry_ref.shape)
    cnt_ref[...] = jnp.broadcast_to(new_carry, cnt_ref.shape)


def _router(x, w_router, b_router, tm=256):
    n, d = x.shape
    ne = w_router.shape[1]
    wt = w_router.T
    whi = wt.astype(BF16)
    wlo = (wt - whi.astype(F32)).astype(BF16)
    triu = jnp.asarray(np.arange(tm)[:, None] < np.arange(tm)[None, :], BF16)
    meta, cnt = pl.pallas_call(
        _router_kernel,
        grid=(n // tm,),
        in_specs=[pl.BlockSpec((tm, d), lambda i: (i, 0)),
                  pl.BlockSpec((ne, d), lambda i: (0, 0)),
                  pl.BlockSpec((ne, d), lambda i: (0, 0)),
                  pl.BlockSpec((ne, LANES), lambda i: (0, 0)),
                  pl.BlockSpec((tm, tm), lambda i: (0, 0))],
        out_specs=[pl.BlockSpec((3 * TOP_K, tm), lambda i: (0, i)),
                   pl.BlockSpec((ne, LANES), lambda i: (0, 0))],
        out_shape=[jax.ShapeDtypeStruct((3 * TOP_K, n), F32),
                   jax.ShapeDtypeStruct((ne, LANES), F32)],
        scratch_shapes=[pltpu.VMEM((ne, LANES), F32)],
        compiler_params=_cparams(("arbitrary",)),
        name="moe_router",
    )(x, whi, wlo, jnp.broadcast_to(b_router.reshape(ne, 1), (ne, LANES)), triu)
    return meta, cnt[:, 0]


def _expert_kernel(be_ref, nu_ref, xs_ref, wg_ref, wu_ref, wd_ref, o_ref):
    bi = pl.program_id(0)

    @pl.when(bi < nu_ref[0])
    def _():
        xa, xb = _unpack_bf16_pair(xs_ref[...])
        half = xa.shape[1]

        def xw(w_ref):
            return (jnp.dot(xa, w_ref[0, 0, 0:half, :].astype(BF16), preferred_element_type=F32)
                    + jnp.dot(xb, w_ref[0, 0, half:2 * half, :].astype(BF16), preferred_element_type=F32))

        h = _silu(xw(wg_ref)) * xw(wu_ref)
        o_ref[...] = _pack_bf16_pair(_dot(h, wd_ref[0, 0]))

    @pl.when(bi >= nu_ref[0])
    def _():
        o_ref[...] = jnp.zeros_like(o_ref)


def _expert_ffn(blk_e, n_used, xs, layer, w_gate, w_up, w_down):
    n_rows = xs.shape[0]
    d = w_gate.shape[2]
    hdim = w_gate.shape[3]
    n_blk = n_rows // EXPERT_ROWS
    grid_spec = pltpu.PrefetchScalarGridSpec(
        num_scalar_prefetch=2,
        grid=(n_blk,),
        in_specs=[pl.BlockSpec((EXPERT_ROWS, d // 2),
                               lambda i, be, nu: (jnp.minimum(i, jnp.maximum(nu[0] - 1, 0)), 0)),
                  pl.BlockSpec((1, 1, d, hdim), lambda i, be, nu: (layer, be[i], 0, 0)),
                  pl.BlockSpec((1, 1, d, hdim), lambda i, be, nu: (layer, be[i], 0, 0)),
                  pl.BlockSpec((1, 1, hdim, d), lambda i, be, nu: (layer, be[i], 0, 0))],
        out_specs=pl.BlockSpec((EXPERT_ROWS, d // 2), lambda i, be, nu: (i, 0)),
    )
    return pl.pallas_call(
        _expert_kernel,
        grid_spec=grid_spec,
        out_shape=jax.ShapeDtypeStruct((n_rows, d // 2), jnp.uint32),
        compiler_params=_cparams(("arbitrary",)),
        name="moe_experts",
    )(blk_e, n_used, xs, w_gate, w_up, w_down)


def _dest_kernel(meta_ref, pstart_ref, dest_ref):
    ne = pstart_ref.shape[0]
    tm = meta_ref.shape[1]
    ei = lax.broadcasted_iota(jnp.int32, (ne, tm), 0).astype(F32)
    ps = pstart_ref[:, 0:1]
    for k in range(TOP_K):
        base = jnp.sum(jnp.where(ei == meta_ref[k:k + 1, :], ps, 0.0), axis=0, keepdims=True)
        dest_ref[k:k + 1, :] = (base + meta_ref[2 * TOP_K + k:2 * TOP_K + k + 1, :]).astype(jnp.int32)


def _dest_rows(meta, pstart, tm=512):
    n = meta.shape[1]
    ne = pstart.shape[0]
    return pl.pallas_call(
        _dest_kernel,
        grid=(n // tm,),
        in_specs=[pl.BlockSpec((3 * TOP_K, tm), lambda i: (0, i)),
                  pl.BlockSpec((ne, LANES), lambda i: (0, 0))],
        out_specs=pl.BlockSpec((TOP_K, tm), lambda i: (0, i)),
        out_shape=jax.ShapeDtypeStruct((TOP_K, n), jnp.int32),
        compiler_params=_cparams(("parallel",)),
        name="moe_dest",
    )(meta, jnp.broadcast_to(pstart.astype(F32).reshape(ne, 1), (ne, LANES)))


def _pack_bf16_pair(x):
    half = x.shape[1] // 2
    hi = pltpu.bitcast(x[:, 0:half].astype(BF16).astype(F32), jnp.uint32)
    lo = pltpu.bitcast(x[:, half:2 * half].astype(BF16).astype(F32), jnp.uint32)
    return hi | (lo >> 16)


def _unpack_bf16_pair(p):
    hi = pltpu.bitcast(p & jnp.uint32(0xFFFF0000), F32).astype(BF16)
    lo = pltpu.bitcast(p << 16, F32).astype(BF16)
    return hi, lo


def _dispatch_kernel(dest_ref, x_ref, xs_in_ref, xs_ref, pk_ref, sem):
    del xs_in_ref
    tm = x_ref.shape[0]
    pk_ref[...] = _pack_bf16_pair(x_ref[...])

    def row_copy(t, k):
        return pltpu.make_async_copy(pk_ref.at[pl.ds(t, 1), :], xs_ref.at[pl.ds(dest_ref[k, t], 1), :], sem)

    def body(t, _):
        for k in range(TOP_K):
            row_copy(t, k).start()
        return 0

    lax.fori_loop(0, tm, body, 0)
    done = xs_ref.at[pl.ds(0, TOP_K * tm), :]
    pltpu.make_async_copy(done, done, sem).wait()


def _dispatch(dest, x, n_rows, tm=256):
    n, d = x.shape
    return pl.pallas_call(
        _dispatch_kernel,
        grid=(n // tm,),
        in_specs=[pl.BlockSpec((TOP_K, tm), lambda i: (0, i), memory_space=pltpu.SMEM),
                  pl.BlockSpec((tm, d), lambda i: (i, 0)),
                  pl.BlockSpec(memory_space=pl.ANY)],
        out_specs=pl.BlockSpec(memory_space=pl.ANY),
        out_shape=jax.ShapeDtypeStruct((n_rows, d // 2), jnp.uint32),
        scratch_shapes=[pltpu.VMEM((tm, d // 2), jnp.uint32), pltpu.SemaphoreType.DMA(())],
        input_output_aliases={2: 0},
        compiler_params=_cparams(("arbitrary",)),
        name="moe_dispatch",
    )(dest, x, jnp.zeros((n_rows, d // 2), jnp.uint32))


def _combine_ln_kernel(dest_ref, dest_next_ref, w_ref, x_ref, ys_ref, wg_ref, wu_ref, wd_ref, g_ref, b_ref,
                       o_ref, buf, sem):
    step = pl.program_id(0)
    tm = x_ref.shape[0]
    slot = step % 2

    def gather_rows(idx_ref, to_slot):
        def body(t, _):
            for k in range(TOP_K):
                pltpu.make_async_copy(ys_ref.at[pl.ds(idx_ref[k, t], 1), :],
                                      buf.at[to_slot, k, pl.ds(t, 1), :], sem.at[to_slot]).start()
            return 0

        lax.fori_loop(0, tm, body, 0)

    @pl.when(step == 0)
    def _():
        gather_rows(dest_ref, 0)

    @pl.when(step + 1 < pl.num_programs(0))
    def _():
        gather_rows(dest_next_ref, 1 - slot)

    x = x_ref[...]
    xb = x.astype(BF16)
    h = _silu(jnp.dot(xb, wg_ref[...], preferred_element_type=F32)) * jnp.dot(
        xb, wu_ref[...], preferred_element_type=F32)
    ff = jnp.dot(h.astype(BF16), wd_ref[...], preferred_element_type=F32)
    pltpu.make_async_copy(buf.at[slot], buf.at[slot], sem.at[slot]).wait()
    routed_a = jnp.zeros((tm, buf.shape[3]), F32)
    routed_b = jnp.zeros((tm, buf.shape[3]), F32)
    for k in range(TOP_K):
        ya, yb = _unpack_bf16_pair(buf[slot, k])
        routed_a = routed_a + ya.astype(F32) * w_ref[:, k:k + 1]
        routed_b = routed_b + yb.astype(F32) * w_ref[:, k:k + 1]
    ff = ff + jnp.concatenate([routed_a, routed_b], axis=1)
    o_ref[...] = _layer_norm(DN_ALPHA * x + ff, g_ref[...], b_ref[...])


def _combine_ln(dest, wts, x, ys, ws_gate, ws_up, ws_down, g, b, tm=256):
    n, d = x.shape
    hdim = ws_gate.shape[1]
    row = pl.BlockSpec((tm, d), lambda i: (i, 0))
    last = n // tm - 1
    return pl.pallas_call(
        _combine_ln_kernel,
        grid=(n // tm,),
        in_specs=[pl.BlockSpec((TOP_K, tm), lambda i: (0, i), memory_space=pltpu.SMEM),
                  pl.BlockSpec((TOP_K, tm), lambda i: (0, jnp.minimum(i + 1, last)), memory_space=pltpu.SMEM),
                  pl.BlockSpec((tm, TOP_K), lambda i: (i, 0)),
                  row,
                  pl.BlockSpec(memory_space=pl.ANY),
                  pl.BlockSpec((d, hdim), lambda i: (0, 0)),
                  pl.BlockSpec((d, hdim), lambda i: (0, 0)),
                  pl.BlockSpec((hdim, d), lambda i: (0, 0)),
                  pl.BlockSpec((1, d), lambda i: (0, 0)),
                  pl.BlockSpec((1, d), lambda i: (0, 0))],
        out_specs=row,
        out_shape=jax.ShapeDtypeStruct((n, d), F32),
        scratch_shapes=[pltpu.VMEM((2, TOP_K, tm, d // 2), jnp.uint32), pltpu.SemaphoreType.DMA((2,))],
        compiler_params=_cparams(("arbitrary",)),
        name="moe_combine_ln",
    )(dest, dest, wts, x, ys, ws_gate.astype(BF16), ws_up.astype(BF16), ws_down.astype(BF16),
      g.reshape(1, d), b.reshape(1, d))


def _moe_ln(x, layer, w_router, b_router, w_gate, w_up, w_down, ws_gate, ws_up, ws_down, g, b):
    n, d = x.shape
    ne = w_router.shape[1]
    meta, counts = _router(x, w_router, b_router)
    counts = counts.astype(jnp.int32)
    padded = (counts + EXPERT_ROWS - 1) // EXPERT_ROWS * EXPERT_ROWS
    pend = jnp.cumsum(padded)
    dest = _dest_rows(meta, pend - padded)
    n_blk = (n * TOP_K) // EXPERT_ROWS + ne
    blk_row0 = jnp.arange(n_blk, dtype=jnp.int32) * EXPERT_ROWS
    blk_e = jnp.minimum(jnp.sum((pend[None, :] <= blk_row0[:, None]).astype(jnp.int32), axis=1), ne - 1)
    n_used = (pend[-1] // EXPERT_ROWS).astype(jnp.int32).reshape(1)
    xs = _dispatch(dest, x, n_blk * EXPERT_ROWS)
    ys = _expert_ffn(blk_e, n_used, xs, layer, w_gate, w_up, w_down)
    return _combine_ln(dest, meta[TOP_K:2 * TOP_K].T, x, ys, ws_gate, ws_up, ws_down, g, b)


def _block_diag(w):
    nb, bs, _ = w.shape
    eye = jnp.eye(nb, dtype=w.dtype)
    return (eye[:, None, :, None] * w[:, :, None, :]).reshape(nb * bs, nb * bs)


def _even_mixer_ln(x, w_in, conv_w, conv_b, wa, ba, wx, bx, lam, lb, norm_g, w_out, g, b):
    bsz, t, d = x.shape
    xf = x.reshape(bsz * t, d)
    u = _proj(xf, w_in.astype(BF16)).reshape(bsz, t, -1)
    ya = _lru(u, conv_w, conv_b, _block_diag(wa).astype(BF16), ba, _block_diag(wx).astype(BF16), bx, lam)
    yb = _hgrn2(u, lb, norm_g)
    return _proj_ln(ya.reshape(bsz * t, -1), yb.reshape(bsz * t, -1), w_out, xf, g, b).reshape(bsz, t, d)


def _odd_mixer_ln(x, w_in, pe_k, pe_v, wk1, wk2, wv1, wv2, w_out, g, b):
    bsz, t, d = x.shape
    xf = x.reshape(bsz * t, d)
    u16, u32, kmean = _odd_proj(xf, w_in, t)
    u16 = u16.reshape(bsz, t, U16_W)
    kmean = kmean.reshape(bsz, t // MOBA_BLOCK, -1)

    def grp(off):
        z = u32[:, off:off + NSA_KV].reshape(bsz, t, NSA_GROUPS, HEAD_DIM)
        return z.transpose(0, 2, 1, 3)

    n_all = NSA_HEADS + MOBA_HEADS
    s_nsa = jnp.asarray((2.0 ** (-8.0 * np.arange(1, n_all + 1) / n_all))[0::2], F32)
    k_cmp, v_cmp = _nsa_compress(grp(0), grp(NSA_KV), pe_k, pe_v, wk1, wk2, wv1, wv2)
    o_c, mask_bias = _nsa_select(s_nsa, u16, k_cmp, v_cmp)
    o_s = _flash("slc", u16, mask_bias, tq=128)
    o_w = _flash("win", u16, tq=128)
    o_m = _flash("moba", u16, kmean, tq=MOBA_BLOCK, nkv=2, tiles_per_step=t // MOBA_BLOCK)
    flat = lambda z: z.reshape(bsz * t, -1)
    return _odd_out_ln(flat(o_c), flat(o_s), flat(o_w), flat(o_m), u32, w_out, xf, g, b).reshape(bsz, t, d)


def kernel(x, even_w_in, lru_conv_w, lru_conv_b, lru_wa, lru_ba, lru_wx, lru_bx, lru_lambda,
           hg_lower_bound, hg_norm_g, even_w_out, odd_w_in, nsa_pe_k, nsa_pe_v, nsa_wk1, nsa_wk2,
           nsa_wv1, nsa_wv2, odd_w_out, ln_g, ln_b, w_router, b_router, w_gate, w_up, w_down,
           ws_gate, ws_up, ws_down):
    bsz, t, d = x.shape
    lb_all = jnp.cumsum(jax.nn.softmax(hg_lower_bound.astype(F32), axis=0), axis=0)
    for layer in range(DEPTH):
        li = layer // 2
        if layer % 2 == 0:
            x = _even_mixer_ln(x, even_w_in[li], lru_conv_w[li], lru_conv_b[li], lru_wa[li], lru_ba[li],
                               lru_wx[li], lru_bx[li], lru_lambda[li], lb_all[layer], hg_norm_g[li],
                               even_w_out[li], ln_g[layer, 0], ln_b[layer, 0])
        else:
            x = _odd_mixer_ln(x, odd_w_in[li], nsa_pe_k[li], nsa_pe_v[li], nsa_wk1[li], nsa_wk2[li],
                              nsa_wv1[li], nsa_wv2[li], odd_w_out[li], ln_g[layer, 0], ln_b[layer, 0])
        x = _moe_ln(x.reshape(bsz * t, d), layer, w_router[layer], b_router[layer], w_gate, w_up,
                    w_down, ws_gate[layer], ws_up[layer], ws_down[layer],
                    ln_g[layer, 1], ln_b[layer, 1]).reshape(bsz, t, d)
    return x
```

```python
import functools

import numpy as np
import jax
import jax.numpy as jnp
from jax import lax
from jax.experimental import pallas as pl
from jax.experimental.pallas import tpu as pltpu

F32 = jnp.float32
BF16 = jnp.bfloat16

D_MODEL = 1024
DEPTH = 2
LRU_WIDTH = 512
LRU_BLOCKS = 8
LRU_BLOCK = LRU_WIDTH // LRU_BLOCKS
CONV_W = 4
LRU_C = 8.0
HG_HEADS = 4
HG_DK = 128
HG_FDIM = HG_HEADS * HG_DK
HG_VDIM = HG_HEADS * HG_DK
HEAD_DIM = 64
NSA_HEADS = 8
NSA_GROUPS = 2
NSA_HPG = NSA_HEADS // NSA_GROUPS
NSA_KV = NSA_GROUPS * HEAD_DIM
CMP_LEN = 32
CMP_STRIDE = 16
CMP_HIDDEN = 128
SLC_LEN = 64
SLC_TOPN = 16
WIN = 512
FORCE_BONUS = 1e4
MOBA_HEADS = 8
MOBA_BLOCK = 256
MOBA_TOPK = 3
N_EXPERTS = 256
TOP_K = 8
N_GROUPS = 8
TOPK_GROUPS = 4
EXPERT_HIDDEN = 256
ROUTED_SCALE = 2.5
DN_ALPHA = (2 * DEPTH) ** 0.25
LN_EPS = 1e-5
NEG = -1e30
NEG_INF = float("-inf")

LANES = 128
SUBLANES = 8
VMEM_LIMIT = 48 * 1024 * 1024

HG_SUB = 16
EXPERT_ROWS = 256


def _cparams(sem):
    return pltpu.CompilerParams(dimension_semantics=sem, vmem_limit_bytes=VMEM_LIMIT)


def _dot(a, b):
    return jnp.dot(a.astype(BF16), b.astype(BF16), preferred_element_type=F32)


def _dot_nt(a, b):
    return lax.dot_general(a.astype(BF16), b.astype(BF16), (((1,), (1,)), ((), ())),
                           preferred_element_type=F32)


def _dot_tn(a, b):
    return lax.dot_general(a.astype(BF16), b.astype(BF16), (((0,), (0,)), ((), ())),
                           preferred_element_type=F32)


def _split2(x):
    hi = x.astype(BF16)
    lo = (x - hi.astype(F32)).astype(BF16)
    return hi, lo


def _split3(x):
    hi = x.astype(BF16)
    r = x - hi.astype(F32)
    mid = r.astype(BF16)
    lo = (r - mid.astype(F32)).astype(BF16)
    return hi, mid, lo


def _layer_norm(v, g, b):
    mu = jnp.mean(v, axis=-1, keepdims=True)
    d = v - mu
    var = jnp.mean(d * d, axis=-1, keepdims=True)
    return d * lax.rsqrt(var + LN_EPS) * g + b


def _sigmoid(x):
    return 1.0 / (1.0 + jnp.exp(-x))


def _silu(x):
    return x * _sigmoid(x)


def _proj_kernel(x_ref, w_ref, o_ref):
    o_ref[...] = jnp.dot(x_ref[...].astype(BF16), w_ref[...], preferred_element_type=F32)


def _proj(x, w, tm=256):
    n, k = x.shape
    m = w.shape[1]
    return pl.pallas_call(
        _proj_kernel,
        grid=(n // tm,),
        in_specs=[pl.BlockSpec((tm, k), lambda i: (i, 0)),
                  pl.BlockSpec((k, m), lambda i: (0, 0))],
        out_specs=pl.BlockSpec((tm, m), lambda i: (i, 0)),
        out_shape=jax.ShapeDtypeStruct((n, m), F32),
        compiler_params=_cparams(("parallel",)),
        name="proj",
    )(x, w)


def _proj_ln_kernel(ya_ref, yb_ref, wa_ref, wb_ref, x_ref, g_ref, b_ref, o_ref):
    mix = (jnp.dot(ya_ref[...].astype(BF16), wa_ref[...], preferred_element_type=F32)
           + jnp.dot(yb_ref[...].astype(BF16), wb_ref[...], preferred_element_type=F32))
    o_ref[...] = _layer_norm(DN_ALPHA * x_ref[...] + mix, g_ref[...], b_ref[...])


def _proj_ln(ya, yb, w, xres, g, b, tm=256):
    n, ka = ya.shape
    kb = yb.shape[1]
    d = w.shape[1]
    wb16 = w.astype(BF16)
    full = lambda shp: pl.BlockSpec(shp, lambda i: (0, 0))
    return pl.pallas_call(
        _proj_ln_kernel,
        grid=(n // tm,),
        in_specs=[pl.BlockSpec((tm, ka), lambda i: (i, 0)),
                  pl.BlockSpec((tm, kb), lambda i: (i, 0)),
                  full((ka, d)), full((kb, d)),
                  pl.BlockSpec((tm, d), lambda i: (i, 0)),
                  full((1, d)), full((1, d))],
        out_specs=pl.BlockSpec((tm, d), lambda i: (i, 0)),
        out_shape=jax.ShapeDtypeStruct((n, d), F32),
        compiler_params=_cparams(("parallel",)),
        name="proj_ln",
    )(ya, yb, wb16[:ka], wb16[ka:], xres, g.reshape(1, d), b.reshape(1, d))


U16_QA = 0
U16_QM = U16_QA + NSA_HEADS * LANES
U16_KS = U16_QM + MOBA_HEADS * LANES
U16_KW = U16_KS + NSA_GROUPS * LANES
U16_KM = U16_KW + NSA_GROUPS * LANES
U16_VS = U16_KM + MOBA_HEADS * LANES
U16_VW = U16_VS + NSA_GROUPS * LANES
U16_VM = U16_VW + NSA_GROUPS * LANES
U16_W = U16_VM + MOBA_HEADS * LANES
U32_W = 3 * LANES


def _odd_proj_kernel(x_ref, w16_ref, w32_ref, qc_ref, kt_ref, o16_ref, o32_ref, km_ref):
    xb = x_ref[...].astype(BF16)
    acc = jnp.dot(xb, w16_ref[...], preferred_element_type=F32)
    km_ref[0] = jnp.mean(acc[:, U16_KM:U16_VS], axis=0, keepdims=True)
    o16_ref[:, 0:U16_KS] = (acc[:, 0:U16_KS] + qc_ref[...]).astype(BF16)
    o16_ref[:, U16_KS:U16_VS] = (acc[:, U16_KS:U16_VS] + kt_ref[...]).astype(BF16)
    o16_ref[:, U16_VS:U16_W] = acc[:, U16_VS:U16_W].astype(BF16)
    o32_ref[...] = jnp.dot(xb, w32_ref[...], preferred_element_type=F32)


def _odd_proj(x, w_in, t, tm=MOBA_BLOCK):
    n, d = x.shape
    assert tm == MOBA_BLOCK and t % tm == 0
    dh = HEAD_DIM
    nq = NSA_HEADS * dh
    nm = MOBA_HEADS * dh
    sizes = [nq] + [NSA_KV] * 6 + [3 * NSA_HEADS] + [nm] * 3
    offs = np.concatenate([[0], np.cumsum(sizes)])
    q, kc, vc, ks, vs, kw, vw, gates, mq, mk, mv = [w_in[:, offs[i]:offs[i + 1]] for i in range(11)]
    scale = dh ** -0.5

    def heads(w, nh, mult=1.0):
        w = (w * mult).reshape(d, nh, dh)
        return jnp.pad(w, ((0, 0), (0, 0), (0, LANES - dh))).reshape(d, nh * LANES)

    w16 = jnp.concatenate([heads(q, NSA_HEADS, scale), heads(mq, MOBA_HEADS, scale), heads(ks, NSA_GROUPS),
                           heads(kw, NSA_GROUPS), heads(mk, MOBA_HEADS), heads(vs, NSA_GROUPS),
                           heads(vw, NSA_GROUPS), heads(mv, MOBA_HEADS)], axis=1).astype(BF16)
    w32 = jnp.concatenate([kc, vc, jnp.pad(gates, ((0, 0), (0, LANES - 3 * NSA_HEADS)))], axis=1).astype(BF16)

    n_all = NSA_HEADS + MOBA_HEADS
    s_all = 2.0 ** (-8.0 * np.arange(1, n_all + 1) / n_all)
    slopes = jnp.asarray(np.concatenate([s_all[0::2], s_all[1::2]]), F32)
    s_hi = slopes.astype(BF16).astype(F32)
    s_lo = slopes - s_hi
    scols = jnp.stack([POS_SPLIT * s_hi, POS_SPLIT * s_lo, s_hi, s_lo], axis=-1)
    qconst = jnp.pad(scols, ((0, 0), (dh, LANES - dh - 4))).reshape(1, U16_KS)

    pos = np.arange(t)
    pos_cols = np.stack([pos // POS_SPLIT, pos // POS_SPLIT, pos % POS_SPLIT, pos % POS_SPLIT], axis=-1)

    def key_cols(blk):
        cols = np.zeros((t, LANES), np.float32)
        cols[:, dh:dh + 4] = pos_cols
        if blk:
            cols[pos, MASK_COL0 + pos // blk] = 1.0
        return cols

    ktab = np.concatenate([key_cols(SLC_LEN)] * NSA_GROUPS + [key_cols(0)] * NSA_GROUPS
                          + [key_cols(MOBA_BLOCK)] * MOBA_HEADS, axis=1)
    tiles_per_seq = t // tm
    return pl.pallas_call(
        _odd_proj_kernel,
        grid=(n // tm,),
        in_specs=[pl.BlockSpec((tm, d), lambda i: (i, 0)),
                  pl.BlockSpec((d, U16_W), lambda i: (0, 0)),
                  pl.BlockSpec((d, U32_W), lambda i: (0, 0)),
                  pl.BlockSpec((1, U16_KS), lambda i: (0, 0)),
                  pl.BlockSpec((tm, U16_VS - U16_KS), lambda i: (i % tiles_per_seq, 0))],
        out_specs=[pl.BlockSpec((tm, U16_W), lambda i: (i, 0)),
                   pl.BlockSpec((tm, U32_W), lambda i: (i, 0)),
                   pl.BlockSpec((1, 1, U16_VS - U16_KM), lambda i: (i, 0, 0))],
        out_shape=[jax.ShapeDtypeStruct((n, U16_W), BF16), jax.ShapeDtypeStruct((n, U32_W), F32),
                   jax.ShapeDtypeStruct((n // tm, 1, U16_VS - U16_KM), F32)],
        compiler_params=_cparams(("parallel",)),
        name="odd_proj",
    )(x, w16, w32, qconst, jnp.asarray(ktab, F32))


def _odd_out_kernel(oc_ref, os_ref, ow_ref, om_ref, gt_ref, wa_ref, wb_ref, x_ref, g_ref, b_ref, o_ref):
    sg = _sigmoid(gt_ref[...])
    parts = []
    for h in range(NSA_HEADS):
        cols = slice(h * LANES, (h + 1) * LANES)
        parts.append(sg[:, 3 * h:3 * h + 1] * oc_ref[:, cols] + sg[:, 3 * h + 1:3 * h + 2] * os_ref[:, cols]
                     + sg[:, 3 * h + 2:3 * h + 3] * ow_ref[:, cols])
    yc = jnp.concatenate(parts, axis=1)
    mix = _dot(yc, wa_ref[...]) + _dot(om_ref[...], wb_ref[...])
    o_ref[...] = _layer_norm(DN_ALPHA * x_ref[...] + mix, g_ref[...], b_ref[...])


def _odd_out_ln(o_c, o_s, o_w, o_m, u32, w_out, xres, g, b, tm=256):
    n, d = xres.shape
    dh = HEAD_DIM

    def pad_rows(w, nh):
        return jnp.pad(w.reshape(nh, dh, d), ((0, 0), (0, LANES - dh), (0, 0))).reshape(nh * LANES, d).astype(BF16)

    wa = pad_rows(w_out[:NSA_HEADS * dh], NSA_HEADS)
    wb = pad_rows(w_out[NSA_HEADS * dh:], MOBA_HEADS)
    row = lambda width: pl.BlockSpec((tm, width), lambda i: (i, 0))
    full = lambda shp: pl.BlockSpec(shp, lambda i: (0, 0))
    ka = NSA_HEADS * LANES
    kb = MOBA_HEADS * LANES
    return pl.pallas_call(
        _odd_out_kernel,
        grid=(n // tm,),
        in_specs=[row(ka), row(ka), row(ka), row(kb),
                  pl.BlockSpec((tm, LANES), lambda i: (i, U32_W // LANES - 1)),
                  full((ka, d)), full((kb, d)), row(d), full((1, d)), full((1, d))],
        out_specs=row(d),
        out_shape=jax.ShapeDtypeStruct((n, d), F32),
        compiler_params=_cparams(("parallel",)),
        name="odd_out_ln",
    )(o_c, o_s, o_w, o_m, u32, wa, wb, xres, g.reshape(1, d), b.reshape(1, d))


def _lru_kernel(xb_ref, gb_ref, cw_ref, cb_ref, wa_ref, ba_ref, wx_ref, bx_ref, lam_ref,
                o_ref, xs_ref, a_ref, b_ref, h_ref):
    ti = pl.program_id(1)
    tc = xb_ref.shape[1]
    c = xb_ref.shape[2]

    @pl.when(ti == 0)
    def _():
        xs_ref[0:SUBLANES, :] = jnp.zeros((SUBLANES, c), F32)
        h_ref[...] = jnp.zeros_like(h_ref)

    xs_ref[SUBLANES:SUBLANES + tc, :] = xb_ref[0]
    xc = cb_ref[...]
    for w in range(CONV_W):
        xc = xc + cw_ref[w:w + 1, :] * xs_ref[pl.ds(SUBLANES - (CONV_W - 1) + w, tc), :]
    xs_ref[0:SUBLANES, :] = xs_ref[tc:tc + SUBLANES, :]

    r = _sigmoid(_dot(xc, wa_ref[...]) + ba_ref[...])
    i = _sigmoid(_dot(xc, wx_ref[...]) + bx_ref[...])
    nl = -lam_ref[...]
    softplus = jnp.maximum(nl, 0.0) + jnp.log(1.0 + jnp.exp(-jnp.abs(nl)))
    log_a = (-LRU_C * r) * softplus
    a = jnp.exp(log_a)
    mult = jnp.sqrt(1.0 - jnp.exp(2.0 * log_a))
    trow = ti * tc + lax.broadcasted_iota(jnp.int32, (tc, c), 0)
    mult = jnp.where(trow == 0, 1.0, mult)
    a_ref[...] = a
    b_ref[...] = mult * (i * xc)

    row = lax.broadcasted_iota(jnp.int32, (SUBLANES, c), 0)

    def body(g, carry):
        r0 = pl.multiple_of(g * SUBLANES, SUBLANES)
        av = a_ref[pl.ds(r0, SUBLANES), :]
        bv = b_ref[pl.ds(r0, SUBLANES), :]
        for d in (1, 2, 4):
            a_sh = pltpu.roll(av, d, 0)
            b_sh = pltpu.roll(bv, d, 0)
            m = row >= d
            bv = jnp.where(m, av * b_sh + bv, bv)
            av = jnp.where(m, av * a_sh, av)
        h = av * carry + bv
        b_ref[pl.ds(r0, SUBLANES), :] = h
        return h[SUBLANES - 1:SUBLANES, :]

    h_ref[...] = lax.fori_loop(0, tc // SUBLANES, body, h_ref[...])
    o_ref[0] = b_ref[...] * jax.nn.gelu(gb_ref[0])


def _lru(u3, conv_w, conv_b, wa_bd, ba, wx_bd, bx, lam, tc=256):
    b, t, _ = u3.shape
    c = LRU_WIDTH
    row = lambda v: v.reshape(1, c)
    full = lambda shp: pl.BlockSpec(shp, lambda bi, ti: (0, 0))
    return pl.pallas_call(
        _lru_kernel,
        grid=(b, t // tc),
        in_specs=[pl.BlockSpec((1, tc, c), lambda bi, ti: (bi, ti, 0)),
                  pl.BlockSpec((1, tc, c), lambda bi, ti: (bi, ti, 1)),
                  full((CONV_W, c)), full((1, c)), full((c, c)), full((1, c)),
                  full((c, c)), full((1, c)), full((1, c))],
        out_specs=pl.BlockSpec((1, tc, c), lambda bi, ti: (bi, ti, 0)),
        out_shape=jax.ShapeDtypeStruct((b, t, c), F32),
        scratch_shapes=[pltpu.VMEM((tc + SUBLANES, c), F32), pltpu.VMEM((tc, c), F32),
                        pltpu.VMEM((tc, c), F32), pltpu.VMEM((1, c), F32)],
        compiler_params=_cparams(("parallel", "arbitrary")),
        name="rg_lru",
    )(u3, u3, conv_w, row(conv_b), wa_bd, row(ba), wx_bd, row(bx), row(lam))


def _hgrn_kernel(q_ref, f_ref, v_ref, g_ref, lb_ref, ng_ref, tri_ref, ones_ref, o_ref,
                 st_ref, q_s, k_s, c_s, qe_s, kd_s, dl_s, o_s):
    ti = pl.program_id(2)
    tc = q_ref.shape[1]
    dk = q_ref.shape[2]

    @pl.when(ti == 0)
    def _():
        st_ref[...] = jnp.zeros_like(st_ref)

    lbv = lb_ref[...]
    q = _silu(q_ref[0])
    f = lbv + (1.0 - lbv) * _sigmoid(f_ref[0])
    kk = 1.0 - f
    logf = jnp.log(f)
    parts = _split3(logf)
    tri = tri_ref[...]
    ones = ones_ref[...]
    cum = sum(jnp.dot(tri, p, preferred_element_type=F32) for p in parts)
    last = sum(jnp.dot(ones, p, preferred_element_type=F32) for p in parts)
    q_s[...] = q
    k_s[...] = kk
    c_s[...] = cum
    qe_s[...] = q * jnp.exp(cum)
    kd_s[...] = kk * jnp.exp(last - cum)
    dl_s[...] = jnp.exp(last)

    s_idx = lax.broadcasted_iota(jnp.int32, (HG_SUB, dk), 0)

    def chunk(ci, _):
        r0 = pl.multiple_of(ci * HG_SUB, HG_SUB)
        rows = pl.ds(r0, HG_SUB)
        st = st_ref[...]
        vc = v_ref[0, rows, :]
        qc = q_s[rows, :]
        kc = k_s[rows, :]
        cc = c_s[rows, :]
        o = _dot_nt(qe_s[rows, :], st)
        for t in range(HG_SUB):
            n = (t // SUBLANES + 1) * SUBLANES
            d = cc[t:t + 1, :] - cc[0:n, :]
            dec = jnp.exp(jnp.where(s_idx[0:n, :] <= t, d, NEG))
            w = (qc[t:t + 1, :] * kc[0:n, :]) * dec
            a_t = jnp.sum(w, axis=1, keepdims=True)
            o_t = jnp.sum(a_t * vc[0:n, :], axis=0, keepdims=True)
            o = o + jnp.where(s_idx == t, o_t, 0.0)
        o_s[rows, :] = o
        st_ref[...] = st * dl_s[pl.ds(r0, 1), :] + _dot_tn(vc, kd_s[rows, :])
        return 0

    lax.fori_loop(0, tc // HG_SUB, chunk, 0, unroll=4)
    o = o_s[...]
    o = o * lax.rsqrt(jnp.mean(o * o, axis=-1, keepdims=True) + 1e-6) * ng_ref[...]
    o_ref[0] = o * _sigmoid(g_ref[0])


def _hgrn2(u3, lb, norm_g, tc=256):
    b, t, _ = u3.shape
    dk = HG_DK
    base = (2 * LRU_WIDTH) // dk
    idx = np.arange(tc)
    same = (idx[:, None] // HG_SUB) == (idx[None, :] // HG_SUB)
    tri = jnp.asarray(same & (idx[None, :] <= idx[:, None]), BF16)
    ones = jnp.asarray(same, BF16)

    def col(off):
        return pl.BlockSpec((1, tc, dk), lambda bi, hi, ti: (bi, ti, base + off * HG_HEADS + hi))

    vec = pl.BlockSpec((1, dk), lambda bi, hi, ti: (0, hi))
    cst = pl.BlockSpec((tc, tc), lambda bi, hi, ti: (0, 0))
    return pl.pallas_call(
        _hgrn_kernel,
        grid=(b, HG_HEADS, t // tc),
        in_specs=[col(0), col(1), col(2), col(3), vec, vec, cst, cst],
        out_specs=pl.BlockSpec((1, tc, dk), lambda bi, hi, ti: (bi, ti, hi)),
        out_shape=jax.ShapeDtypeStruct((b, t, HG_VDIM), F32),
        scratch_shapes=[pltpu.VMEM((dk, dk), F32)] + [pltpu.VMEM((tc, dk), F32)] * 7,
        compiler_params=_cparams(("parallel", "parallel", "arbitrary")),
        name="hgrn2",
    )(u3, u3, u3, u3, lb.reshape(1, HG_FDIM), norm_g.reshape(1, HG_VDIM), tri, ones)


def _nsa_cmp_kv_kernel(zk_ref, zv_ref, pek_ref, pev_ref, wk1_ref, wk2_ref, wv1_ref, wv2_ref,
                       ok_ref, ov_ref):
    half = zk_ref.shape[3]
    nrow = zk_ref.shape[2]

    def compress(z, pe_ref, w1_ref, w2_ref):
        lo = _dot(z + pe_ref[0:1, :], w1_ref[0:half, :])
        hi = _dot(z + pe_ref[1:2, :], w1_ref[half:2 * half, :])
        pre = lo + pltpu.roll(hi, nrow - 1, 0)
        return _dot(jax.nn.gelu(pre), w2_ref[...])

    ok_ref[0, 0] = compress(zk_ref[0, 0], pek_ref, wk1_ref, wk2_ref)
    ov_ref[0, 0] = compress(zv_ref[0, 0], pev_ref, wv1_ref, wv2_ref)


def _nsa_compress(kc, vc, pe_k, pe_v, wk1, wk2, wv1, wv2):
    b, g, t, dh = kc.shape
    nrow = t // CMP_STRIDE
    half = CMP_STRIDE * dh
    zk = kc.reshape(b, g, nrow, half)
    zv = vc.reshape(b, g, nrow, half)
    zspec = pl.BlockSpec((1, 1, nrow, half), lambda bi, gi: (bi, gi, 0, 0))
    full = lambda shp: pl.BlockSpec(shp, lambda bi, gi: (0, 0))
    ospec = pl.BlockSpec((1, 1, nrow, LANES), lambda bi, gi: (bi, gi, 0, 0))
    oshape = jax.ShapeDtypeStruct((b, g, nrow, LANES), F32)
    pad = lambda w: jnp.pad(w, ((0, 0), (0, LANES - dh))).astype(BF16)
    return pl.pallas_call(
        _nsa_cmp_kv_kernel,
        grid=(b, g),
        in_specs=[zspec, zspec, full((2, half)), full((2, half)),
                  full((2 * half, CMP_HIDDEN)), full((CMP_HIDDEN, LANES)),
                  full((2 * half, CMP_HIDDEN)), full((CMP_HIDDEN, LANES))],
        out_specs=[ospec, ospec],
        out_shape=[oshape, oshape],
        compiler_params=_cparams(("parallel", "parallel")),
        name="nsa_compress",
    )(zk, zv, pe_k.reshape(2, half), pe_v.reshape(2, half),
      wk1.astype(BF16), pad(wk2), wv1.astype(BF16), pad(wv2))


def _nsa_sel_kernel(slope_ref, q_ref, kc_ref, vc_ref, ov_ref, o_ref, mb_ref):
    gi = pl.program_id(0) % NSA_GROUPS
    qi = pl.program_id(1)
    tq = q_ref.shape[1]
    hpg = q_ref.shape[2] // LANES
    nc = kc_ref.shape[2]
    ns = ov_ref.shape[0]

    q = jnp.concatenate([q_ref[0, :, i * LANES:(i + 1) * LANES] for i in range(hpg)], axis=0)
    khi, klo = _split2(kc_ref[0, 0])
    s = _dot_nt(q, khi) + _dot_nt(q, klo)
    tpos = qi * tq + lax.broadcasted_iota(jnp.int32, (tq, nc), 0)
    cend = lax.broadcasted_iota(jnp.int32, (tq, nc), 1) * CMP_STRIDE + (CMP_LEN - 1)
    d_c = (tpos - cend).astype(F32)
    valid = d_c >= 0.0
    ps = []
    for i in range(hpg):
        si = s[i * tq:(i + 1) * tq, :] - slope_ref[gi * hpg + i] * d_c
        si = jnp.where(valid, si, NEG)
        m = jnp.max(si, axis=-1, keepdims=True)
        p = jnp.where(valid, jnp.exp(si - m), 0.0)
        ps.append(p / jnp.maximum(jnp.sum(p, axis=-1, keepdims=True), 1e-30))
    oc = _dot(jnp.concatenate(ps, axis=0), vc_ref[0, 0])
    o_ref[0] = jnp.concatenate([oc[i * tq:(i + 1) * tq, :] for i in range(hpg)], axis=1)

    psum = ps[0]
    for i in range(1, hpg):
        psum = psum + ps[i]
    phi, plo = _split2(psum)
    ovl = ov_ref[...]
    imp = _dot_nt(ovl, phi) + _dot_nt(ovl, plo)

    j = lax.broadcasted_iota(jnp.int32, (ns, tq), 0)
    own = (qi * tq + lax.broadcasted_iota(jnp.int32, (ns, tq), 1)) // SLC_LEN
    forced = (j == 0) | (j == own) | (j == own - 1)
    imp = jnp.where(j <= own, imp + FORCE_BONUS * forced.astype(F32), NEG)
    rank = jnp.zeros((ns, tq), F32)
    for jp in range(ns):
        row = imp[jp:jp + 1, :]
        beats = (row > imp) | ((row == imp) & (j > jp))
        rank = rank + beats.astype(F32)
    bias = jnp.where(rank < float(min(SLC_TOPN, ns)), 0.0, MASKED)
    place = (lax.broadcasted_iota(jnp.int32, (ns, LANES), 1)
             == lax.broadcasted_iota(jnp.int32, (ns, LANES), 0) + MASK_COL0)
    mb_ref[0] = _dot_tn(bias, place.astype(F32)).astype(BF16)


def _nsa_select(slopes, u16, k_cmp, v_cmp, tq=256):
    b, t, _ = u16.shape
    g = NSA_GROUPS
    hpg = NSA_HPG
    nc = k_cmp.shape[2]
    ns = t // SLC_LEN
    ci = np.arange(nc)[:, None]
    cj = np.arange(ns)[None, :]
    ovl = ((CMP_STRIDE * ci < SLC_LEN * (cj + 1)) & (CMP_STRIDE * ci + CMP_LEN > SLC_LEN * cj))
    ovl = ovl & (ci < (t - CMP_LEN) // CMP_STRIDE + 1)
    grid_spec = pltpu.PrefetchScalarGridSpec(
        num_scalar_prefetch=1,
        grid=(b * g, t // tq),
        in_specs=[pl.BlockSpec((1, tq, hpg * LANES), lambda bg, qi, s: (bg // g, qi, U16_QA // (hpg * LANES) + bg % g)),
                  pl.BlockSpec((1, 1, nc, LANES), lambda bg, qi, s: (bg // g, bg % g, 0, 0)),
                  pl.BlockSpec((1, 1, nc, LANES), lambda bg, qi, s: (bg // g, bg % g, 0, 0)),
                  pl.BlockSpec((ns, nc), lambda bg, qi, s: (0, 0))],
        out_specs=[pl.BlockSpec((1, tq, hpg * LANES), lambda bg, qi, s: (bg // g, qi, bg % g)),
                   pl.BlockSpec((1, tq, LANES), lambda bg, qi, s: (bg // g, qi, bg % g))],
    )
    return pl.pallas_call(
        _nsa_sel_kernel,
        grid_spec=grid_spec,
        out_shape=[jax.ShapeDtypeStruct((b, t, g * hpg * LANES), F32),
                   jax.ShapeDtypeStruct((b, t, g * LANES), BF16)],
        compiler_params=_cparams(("parallel", "parallel")),
        name="nsa_select",
    )(slopes, u16, k_cmp, v_cmp, jnp.asarray(ovl.T, BF16))


POS_SPLIT = 64
MASKED = -2e30


MASK_COL0 = HEAD_DIM + 4
KEY_STEP = 512


def _flash_kernel(q_ref, k_ref, v_ref, *rest, mode, nkv, tq):
    o_ref = rest[-1]
    hpg = q_ref.shape[2] // (LANES * nkv)
    t_all = k_ref.shape[1]
    nt = q_ref.shape[1] // tq
    whole = nt * tq == t_all
    step = MOBA_BLOCK if mode == "moba" else KEY_STEP

    def tile(ti, q0):
        rows = slice(ti * tq, (ti + 1) * tq)
        qs = []
        for j in range(nkv):
            kv_cols = slice(j * LANES, (j + 1) * LANES)
            heads = [q_ref[0, rows, (j * hpg + i) * LANES:(j * hpg + i + 1) * LANES] for i in range(hpg)]
            if mode == "slc":
                heads = [h + rest[0][0, rows, kv_cols] for h in heads]
            q = heads[0] if hpg == 1 else jnp.concatenate(heads, axis=0)
            if mode == "moba":
                kmean = rest[0][0, :, kv_cols]
                nb = kmean.shape[0]
                mhi, mlo = _split2(kmean)
                gate = _dot_nt(mhi, q) + _dot_nt(mlo, q)
                jb = lax.broadcasted_iota(jnp.int32, (nb, tq), 0)
                ob = q0 // MOBA_BLOCK
                gate = jnp.where(jb < ob, gate, NEG)
                rank = jnp.zeros((nb, tq), F32)
                for jp in range(nb):
                    row = gate[jp:jp + 1, :]
                    beats = (row > gate) | ((row == gate) & (jb > jp))
                    rank = rank + beats.astype(F32)
                n_sel = float(min(MOBA_TOPK, nb - 1))
                bsel = ((rank < n_sel) & (jb < ob)) | (jb == ob)
                bias = jnp.where(bsel, 0.0, MASKED)
                place = (lax.broadcasted_iota(jnp.int32, (nb, LANES), 1)
                         == lax.broadcasted_iota(jnp.int32, (nb, LANES), 0) + MASK_COL0)
                q = q + _dot_tn(bias, place.astype(F32)).astype(BF16)
            qs.append(q)

        def attend(kstart, klen, diag, off):
            rel = lax.broadcasted_iota(jnp.int32, (tq, diag), 0) - lax.broadcasted_iota(jnp.int32, (tq, diag), 1)
            valid = rel >= off
            if mode == "win":
                valid = valid & (rel < off + WIN)
            if hpg > 1:
                valid = jnp.concatenate([valid] * hpg, axis=0)
            for j in range(nkv):
                kv_cols = slice(j * LANES, (j + 1) * LANES)
                s = _dot_nt(qs[j], k_ref[0, pl.ds(kstart, klen), kv_cols])
                sd = jnp.where(valid, s[:, klen - diag:], MASKED)
                s = sd if diag == klen else jnp.concatenate([s[:, :klen - diag], sd], axis=1)
                m = jnp.maximum(jnp.max(s, axis=-1, keepdims=True), NEG)
                p = jnp.exp(s - m)
                l = jnp.sum(p, axis=-1, keepdims=True)
                o = _dot(p, v_ref[0, pl.ds(kstart, klen), kv_cols]) / jnp.maximum(l, 1e-30)
                for i in range(hpg):
                    o_ref[0, rows, (j * hpg + i) * LANES:(j * hpg + i + 1) * LANES] = o[i * tq:(i + 1) * tq, :]

        if mode == "win":
            kstart = max(q0 - WIN, 0) if whole else pl.multiple_of(jnp.maximum(q0 - WIN, 0), LANES)
            attend(kstart, WIN + tq, WIN + tq, kstart - q0)
        elif whole:
            v = (q0 + tq - 1) // step
            attend(0, step * (v + 1), step, step * v - q0)
        else:
            var = (q0 + tq - 1) // step
            for v in range(t_all // step):
                @pl.when(var == v)
                def _(v=v):
                    attend(0, step * (v + 1), step, step * v - q0)

    for ti in range(nt):
        tile(ti, ti * tq if whole else (pl.program_id(1) * nt + ti) * tq)


def _flash(mode, u16, extra=None, tq=128, nkv=1, tiles_per_step=1):
    b, t, _ = u16.shape
    q_off, k_off, v_off, n_kvh, hpg = {
        "slc": (U16_QA, U16_KS, U16_VS, NSA_GROUPS, NSA_HPG),
        "win": (U16_QA, U16_KW, U16_VW, NSA_GROUPS, NSA_HPG),
        "moba": (U16_QM, U16_KM, U16_VM, MOBA_HEADS, 1)}[mode]
    assert mode != "moba" or tq == MOBA_BLOCK
    steps = n_kvh // nkv
    qw = nkv * hpg * LANES
    kw = nkv * LANES
    tb = tq * tiles_per_step
    in_specs = [pl.BlockSpec((1, tb, qw), lambda bh, qi: (bh // steps, qi, q_off // qw + bh % steps)),
                pl.BlockSpec((1, t, kw), lambda bh, qi: (bh // steps, 0, k_off // kw + bh % steps)),
                pl.BlockSpec((1, t, kw), lambda bh, qi: (bh // steps, 0, v_off // kw + bh % steps))]
    args = [u16, u16, u16]
    if mode == "slc":
        in_specs.append(pl.BlockSpec((1, tb, kw), lambda bh, qi: (bh // steps, qi, bh % steps)))
        args.append(extra)
    elif mode == "moba":
        in_specs.append(pl.BlockSpec((1, t // MOBA_BLOCK, kw), lambda bh, qi: (bh // steps, 0, bh % steps)))
        args.append(extra)
    return pl.pallas_call(
        functools.partial(_flash_kernel, mode=mode, nkv=nkv, tq=tq),
        grid=(b * steps, t // tb),
        in_specs=in_specs,
        out_specs=pl.BlockSpec((1, tb, qw), lambda bh, qi: (bh // steps, qi, bh % steps)),
        out_shape=jax.ShapeDtypeStruct((b, t, n_kvh * hpg * LANES), F32),
        compiler_params=_cparams(("parallel", "parallel")),
        name="flash_" + mode,
    )(*args)


def _router_kernel(x_ref, whi_ref, wlo_ref, b_ref, triu_ref, meta_ref, cnt_ref, carry_ref):
    step = pl.program_id(0)
    tm = x_ref.shape[0]
    ne = whi_ref.shape[0]
    gsz = ne // N_GROUPS

    @pl.when(step == 0)
    def _():
        carry_ref[...] = jnp.zeros_like(carry_ref)

    xhi, xlo = _split2(x_ref[...])
    whi = whi_ref[...]
    logits = _dot_nt(whi, xhi) + _dot_nt(wlo_ref[...], xhi) + _dot_nt(whi, xlo)
    scores = _sigmoid(logits)
    biased = scores + b_ref[:, 0:1]

    v3 = biased.reshape(N_GROUPS, gsz, tm)
    i3 = lax.broadcasted_iota(jnp.int32, (N_GROUPS, gsz, tm), 1).astype(F32)
    m1 = jnp.max(v3, axis=1, keepdims=True)
    idx1 = jnp.min(jnp.where(v3 == m1, i3, float(gsz)), axis=1, keepdims=True)
    m2 = jnp.max(jnp.where(i3 == idx1, NEG_INF, v3), axis=1, keepdims=True)
    gs = (m1 + m2).reshape(N_GROUPS, tm)
    gi = lax.broadcasted_iota(jnp.int32, (N_GROUPS, tm), 0).astype(F32)
    gmask = jnp.zeros((N_GROUPS, tm), F32)
    for _ in range(TOPK_GROUPS):
        m = jnp.max(gs, axis=0, keepdims=True)
        pick = jnp.min(jnp.where(gs == m, gi, float(N_GROUPS)), axis=0, keepdims=True)
        hit = gi == pick
        gmask = jnp.where(hit, 1.0, gmask)
        gs = jnp.where(hit, NEG_INF, gs)
    emask = jnp.broadcast_to(gmask.reshape(N_GROUPS, 1, tm), (N_GROUPS, gsz, tm)).reshape(ne, tm)
    cur = jnp.where(emask > 0.5, biased, NEG)

    ei = lax.broadcasted_iota(jnp.int32, (ne, tm), 0).astype(F32)
    selm = jnp.zeros((ne, tm), F32)
    idxs, svals = [], []
    for _ in range(TOP_K):
        m = jnp.max(cur, axis=0, keepdims=True)
        idx = jnp.min(jnp.where(cur == m, ei, float(ne)), axis=0, keepdims=True)
        hit = ei == idx
        svals.append(jnp.sum(jnp.where(hit, scores, 0.0), axis=0, keepdims=True))
        idxs.append(idx)
        cur = jnp.where(hit, NEG_INF, cur)
        selm = jnp.where(hit, 1.0, selm)

    carry = carry_ref[:, 0:1]
    rank_full = jnp.dot(selm.astype(BF16), triu_ref[...], preferred_element_type=F32) + carry
    ssum = svals[0]
    for k in range(1, TOP_K):
        ssum = ssum + svals[k]
    for k in range(TOP_K):
        meta_ref[k:k + 1, :] = idxs[k]
        meta_ref[TOP_K + k:TOP_K + k + 1, :] = svals[k] / ssum * ROUTED_SCALE
        meta_ref[2 * TOP_K + k:2 * TOP_K + k + 1, :] = jnp.sum(
            jnp.where(ei == idxs[k], rank_full, 0.0), axis=0, keepdims=True)
    new_carry = carry + jnp.sum(selm, axis=1, keepdims=True)
    carry_ref[...] = jnp.broadcast_to(new_carry, carry_ref.shape)
    cnt_ref[...] = jnp.broadcast_to(new_carry, cnt_ref.shape)


def _router(x, w_router, b_router, tm=256):
    n, d = x.shape
    ne = w_router.shape[1]
    wt = w_router.T
    whi = wt.astype(BF16)
    wlo = (wt - whi.astype(F32)).astype(BF16)
    triu = jnp.asarray(np.arange(tm)[:, None] < np.arange(tm)[None, :], BF16)
    meta, cnt = pl.pallas_call(
        _router_kernel,
        grid=(n // tm,),
        in_specs=[pl.BlockSpec((tm, d), lambda i: (i, 0)),
                  pl.BlockSpec((ne, d), lambda i: (0, 0)),
                  pl.BlockSpec((ne, d), lambda i: (0, 0)),
                  pl.BlockSpec((ne, LANES), lambda i: (0, 0)),
                  pl.BlockSpec((tm, tm), lambda i: (0, 0))],
        out_specs=[pl.BlockSpec((3 * TOP_K, tm), lambda i: (0, i)),
                   pl.BlockSpec((ne, LANES), lambda i: (0, 0))],
        out_shape=[jax.ShapeDtypeStruct((3 * TOP_K, n), F32),
                   jax.ShapeDtypeStruct((ne, LANES), F32)],
        scratch_shapes=[pltpu.VMEM((ne, LANES), F32)],
        compiler_params=_cparams(("arbitrary",)),
        name="moe_router",
    )(x, whi, wlo, jnp.broadcast_to(b_router.reshape(ne, 1), (ne, LANES)), triu)
    return meta, cnt[:, 0]


def _expert_kernel(be_ref, first_ref, nxt_ref, slot_ref, nu_ref, xs_ref, wg_hbm, wu_hbm, wd_hbm, o_ref,
                   wg_buf, wu_buf, wd_buf, wg16, wu16, wd16, sem, *, layer):
    bi = pl.program_id(0)

    def weight_copies(e, s):
        return [pltpu.make_async_copy(hbm.at[layer, e], buf.at[s], sem.at[s])
                for hbm, buf in ((wg_hbm, wg_buf), (wu_hbm, wu_buf), (wd_hbm, wd_buf))]

    @pl.when(bi < nu_ref[0])
    def _():
        s = slot_ref[bi]

        @pl.when(first_ref[bi] == 1)
        def _():
            @pl.when(bi == 0)
            def _():
                for c in weight_copies(be_ref[0], s):
                    c.start()

            @pl.when(nxt_ref[bi] >= 0)
            def _():
                for c in weight_copies(nxt_ref[bi], 1 - s):
                    c.start()

            for c in weight_copies(be_ref[bi], s):
                c.wait()
            wg16[...] = wg_buf[s].astype(BF16)
            wu16[...] = wu_buf[s].astype(BF16)
            wd16[...] = wd_buf[s].astype(BF16)

        xa, xb = _unpack_bf16_pair(xs_ref[...])
        half = xa.shape[1]

        def xw(w_ref):
            return (jnp.dot(xa, w_ref[0:half, :], preferred_element_type=F32)
                    + jnp.dot(xb, w_ref[half:2 * half, :], preferred_element_type=F32))

        h = _silu(xw(wg16)) * xw(wu16)
        o_ref[...] = _pack_bf16_pair(jnp.dot(h.astype(BF16), wd16[...], preferred_element_type=F32))

    @pl.when(bi >= nu_ref[0])
    def _():
        o_ref[...] = jnp.zeros_like(o_ref)


def _expert_ffn(blk_e, n_used, xs, layer, w_gate, w_up, w_down):
    n_rows = xs.shape[0]
    d = w_gate.shape[2]
    hdim = w_gate.shape[3]
    n_blk = n_rows // EXPERT_ROWS
    idx = jnp.arange(n_blk, dtype=jnp.int32)
    first = jnp.concatenate([jnp.ones((1,), bool), blk_e[1:] != blk_e[:-1]]) & (idx < n_used[0])
    slot = ((jnp.cumsum(first.astype(jnp.int32)) - 1) % 2).astype(jnp.int32)
    nxt_first = lax.cummin(jnp.where(first, idx, n_blk), reverse=True)
    nxt_first = jnp.concatenate([nxt_first[1:], jnp.full((1,), n_blk, jnp.int32)])
    nxt_e = jnp.where(nxt_first < n_blk, blk_e[jnp.minimum(nxt_first, n_blk - 1)], -1).astype(jnp.int32)
    grid_spec = pltpu.PrefetchScalarGridSpec(
        num_scalar_prefetch=5,
        grid=(n_blk,),
        in_specs=[pl.BlockSpec((EXPERT_ROWS, d // 2),
                               lambda i, be, fi, nx, sl, nu: (jnp.minimum(i, jnp.maximum(nu[0] - 1, 0)), 0)),
                  pl.BlockSpec(memory_space=pl.ANY),
                  pl.BlockSpec(memory_space=pl.ANY),
                  pl.BlockSpec(memory_space=pl.ANY)],
        out_specs=pl.BlockSpec((EXPERT_ROWS, d // 2), lambda i, be, fi, nx, sl, nu: (i, 0)),
        scratch_shapes=[pltpu.VMEM((2, d, hdim), F32), pltpu.VMEM((2, d, hdim), F32), pltpu.VMEM((2, hdim, d), F32),
                        pltpu.VMEM((d, hdim), BF16), pltpu.VMEM((d, hdim), BF16), pltpu.VMEM((hdim, d), BF16),
                        pltpu.SemaphoreType.DMA((2,))],
    )
    return pl.pallas_call(
        functools.partial(_expert_kernel, layer=layer),
        grid_spec=grid_spec,
        out_shape=jax.ShapeDtypeStruct((n_rows, d // 2), jnp.uint32),
        compiler_params=_cparams(("arbitrary",)),
        name="moe_experts",
    )(blk_e, first.astype(jnp.int32), nxt_e, slot, n_used, xs, w_gate, w_up, w_down)


def _dest_kernel(meta_ref, pstart_ref, dest_ref):
    ne = pstart_ref.shape[0]
    tm = meta_ref.shape[1]
    ei = lax.broadcasted_iota(jnp.int32, (ne, tm), 0).astype(F32)
    ps = pstart_ref[:, 0:1]
    for k in range(TOP_K):
        base = jnp.sum(jnp.where(ei == meta_ref[k:k + 1, :], ps, 0.0), axis=0, keepdims=True)
        dest_ref[k:k + 1, :] = (base + meta_ref[2 * TOP_K + k:2 * TOP_K + k + 1, :]).astype(jnp.int32)


def _dest_rows(meta, pstart, tm=512):
    n = meta.shape[1]
    ne = pstart.shape[0]
    return pl.pallas_call(
        _dest_kernel,
        grid=(n // tm,),
        in_specs=[pl.BlockSpec((3 * TOP_K, tm), lambda i: (0, i)),
                  pl.BlockSpec((ne, LANES), lambda i: (0, 0))],
        out_specs=pl.BlockSpec((TOP_K, tm), lambda i: (0, i)),
        out_shape=jax.ShapeDtypeStruct((TOP_K, n), jnp.int32),
        compiler_params=_cparams(("parallel",)),
        name="moe_dest",
    )(meta, jnp.broadcast_to(pstart.astype(F32).reshape(ne, 1), (ne, LANES)))


def _pack_bf16_pair(x):
    half = x.shape[1] // 2
    hi = pltpu.bitcast(x[:, 0:half].astype(BF16).astype(F32), jnp.uint32)
    lo = pltpu.bitcast(x[:, half:2 * half].astype(BF16).astype(F32), jnp.uint32)
    return hi | (lo >> 16)


def _unpack_bf16_pair(p):
    hi = pltpu.bitcast(p & jnp.uint32(0xFFFF0000), F32).astype(BF16)
    lo = pltpu.bitcast(p << 16, F32).astype(BF16)
    return hi, lo


def _dispatch_kernel(dest_ref, x_ref, xs_in_ref, xs_ref, pk_ref, sem):
    del xs_in_ref
    tm = x_ref.shape[0]
    pk_ref[...] = _pack_bf16_pair(x_ref[...])

    def row_copy(t, k):
        return pltpu.make_async_copy(pk_ref.at[pl.ds(t, 1), :], xs_ref.at[pl.ds(dest_ref[k, t], 1), :], sem)

    def body(t, _):
        for k in range(TOP_K):
            row_copy(t, k).start()
        return 0

    lax.fori_loop(0, tm, body, 0)
    done = xs_ref.at[pl.ds(0, TOP_K * tm), :]
    pltpu.make_async_copy(done, done, sem).wait()


def _dispatch(dest, x, n_rows, tm=256):
    n, d = x.shape
    return pl.pallas_call(
        _dispatch_kernel,
        grid=(n // tm,),
        in_specs=[pl.BlockSpec((TOP_K, tm), lambda i: (0, i), memory_space=pltpu.SMEM),
                  pl.BlockSpec((tm, d), lambda i: (i, 0)),
                  pl.BlockSpec(memory_space=pl.ANY)],
        out_specs=pl.BlockSpec(memory_space=pl.ANY),
        out_shape=jax.ShapeDtypeStruct((n_rows, d // 2), jnp.uint32),
        scratch_shapes=[pltpu.VMEM((tm, d // 2), jnp.uint32), pltpu.SemaphoreType.DMA(())],
        input_output_aliases={2: 0},
        compiler_params=_cparams(("arbitrary",)),
        name="moe_dispatch",
    )(dest, x, jnp.zeros((n_rows, d // 2), jnp.uint32))


def _combine_ln_kernel(dest_ref, dest_next_ref, w_ref, x_ref, ys_ref, wg_ref, wu_ref, wd_ref, g_ref, b_ref,
                       o_ref, buf, sem):
    step = pl.program_id(0)
    tm = x_ref.shape[0]
    slot = step % 2

    def gather_rows(idx_ref, to_slot):
        def body(t, _):
            for k in range(TOP_K):
                pltpu.make_async_copy(ys_ref.at[pl.ds(idx_ref[k, t], 1), :],
                                      buf.at[to_slot, k, pl.ds(t, 1), :], sem.at[to_slot]).start()
            return 0

        lax.fori_loop(0, tm, body, 0)

    @pl.when(step == 0)
    def _():
        gather_rows(dest_ref, 0)

    @pl.when(step + 1 < pl.num_programs(0))
    def _():
        gather_rows(dest_next_ref, 1 - slot)

    x = x_ref[...]
    xb = x.astype(BF16)
    h = _silu(jnp.dot(xb, wg_ref[...], preferred_element_type=F32)) * jnp.dot(
        xb, wu_ref[...], preferred_element_type=F32)
    ff = jnp.dot(h.astype(BF16), wd_ref[...], preferred_element_type=F32)
    pltpu.make_async_copy(buf.at[slot], buf.at[slot], sem.at[slot]).wait()
    routed_a = jnp.zeros((tm, buf.shape[3]), F32)
    routed_b = jnp.zeros((tm, buf.shape[3]), F32)
    for k in range(TOP_K):
        ya, yb = _unpack_bf16_pair(buf[slot, k])
        routed_a = routed_a + ya.astype(F32) * w_ref[:, k:k + 1]
        routed_b = routed_b + yb.astype(F32) * w_ref[:, k:k + 1]
    ff = ff + jnp.concatenate([routed_a, routed_b], axis=1)
    o_ref[...] = _layer_norm(DN_ALPHA * x + ff, g_ref[...], b_ref[...])


def _combine_ln(dest, wts, x, ys, ws_gate, ws_up, ws_down, g, b, tm=256):
    n, d = x.shape
    hdim = ws_gate.shape[1]
    row = pl.BlockSpec((tm, d), lambda i: (i, 0))
    last = n // tm - 1
    return pl.pallas_call(
        _combine_ln_kernel,
        grid=(n // tm,),
        in_specs=[pl.BlockSpec((TOP_K, tm), lambda i: (0, i), memory_space=pltpu.SMEM),
                  pl.BlockSpec((TOP_K, tm), lambda i: (0, jnp.minimum(i + 1, last)), memory_space=pltpu.SMEM),
                  pl.BlockSpec((tm, TOP_K), lambda i: (i, 0)),
                  row,
                  pl.BlockSpec(memory_space=pl.ANY),
                  pl.BlockSpec((d, hdim), lambda i: (0, 0)),
                  pl.BlockSpec((d, hdim), lambda i: (0, 0)),
                  pl.BlockSpec((hdim, d), lambda i: (0, 0)),
                  pl.BlockSpec((1, d), lambda i: (0, 0)),
                  pl.BlockSpec((1, d), lambda i: (0, 0))],
        out_specs=row,
        out_shape=jax.ShapeDtypeStruct((n, d), F32),
        scratch_shapes=[pltpu.VMEM((2, TOP_K, tm, d // 2), jnp.uint32), pltpu.SemaphoreType.DMA((2,))],
        compiler_params=_cparams(("arbitrary",)),
        name="moe_combine_ln",
    )(dest, dest, wts, x, ys, ws_gate.astype(BF16), ws_up.astype(BF16), ws_down.astype(BF16),
      g.reshape(1, d), b.reshape(1, d))


def _moe_ln(x, layer, w_router, b_router, w_gate, w_up, w_down, ws_gate, ws_up, ws_down, g, b):
    n, d = x.shape
    ne = w_router.shape[1]
    meta, counts = _router(x, w_router, b_router)
    counts = counts.astype(jnp.int32)
    padded = (counts + EXPERT_ROWS - 1) // EXPERT_ROWS * EXPERT_ROWS
    pend = jnp.cumsum(padded)
    dest = _dest_rows(meta, pend - padded)
    n_blk = (n * TOP_K) // EXPERT_ROWS + ne
    blk_row0 = jnp.arange(n_blk, dtype=jnp.int32) * EXPERT_ROWS
    blk_e = jnp.minimum(jnp.sum((pend[None, :] <= blk_row0[:, None]).astype(jnp.int32), axis=1), ne - 1)
    n_used = (pend[-1] // EXPERT_ROWS).astype(jnp.int32).reshape(1)
    xs = _dispatch(dest, x, n_blk * EXPERT_ROWS)
    ys = _expert_ffn(blk_e, n_used, xs, layer, w_gate, w_up, w_down)
    return _combine_ln(dest, meta[TOP_K:2 * TOP_K].T, x, ys, ws_gate, ws_up, ws_down, g, b)


def _block_diag(w):
    nb, bs, _ = w.shape
    eye = jnp.eye(nb, dtype=w.dtype)
    return (eye[:, None, :, None] * w[:, :, None, :]).reshape(nb * bs, nb * bs)


def _even_mixer_ln(x, w_in, conv_w, conv_b, wa, ba, wx, bx, lam, lb, norm_g, w_out, g, b):
    bsz, t, d = x.shape
    xf = x.reshape(bsz * t, d)
    u = _proj(xf, w_in.astype(BF16)).reshape(bsz, t, -1)
    ya = _lru(u, conv_w, conv_b, _block_diag(wa).astype(BF16), ba, _block_diag(wx).astype(BF16), bx, lam)
    yb = _hgrn2(u, lb, norm_g)
    return _proj_ln(ya.reshape(bsz * t, -1), yb.reshape(bsz * t, -1), w_out, xf, g, b).reshape(bsz, t, d)


def _odd_mixer_ln(x, w_in, pe_k, pe_v, wk1, wk2, wv1, wv2, w_out, g, b):
    bsz, t, d = x.shape
    xf = x.reshape(bsz * t, d)
    u16, u32, kmean = _odd_proj(xf, w_in, t)
    u16 = u16.reshape(bsz, t, U16_W)
    kmean = kmean.reshape(bsz, t // MOBA_BLOCK, -1)

    def grp(off):
        z = u32[:, off:off + NSA_KV].reshape(bsz, t, NSA_GROUPS, HEAD_DIM)
        return z.transpose(0, 2, 1, 3)

    n_all = NSA_HEADS + MOBA_HEADS
    s_nsa = jnp.asarray((2.0 ** (-8.0 * np.arange(1, n_all + 1) / n_all))[0::2], F32)
    k_cmp, v_cmp = _nsa_compress(grp(0), grp(NSA_KV), pe_k, pe_v, wk1, wk2, wv1, wv2)
    o_c, mask_bias = _nsa_select(s_nsa, u16, k_cmp, v_cmp)
    o_s = _flash("slc", u16, mask_bias, tq=128)
    o_w = _flash("win", u16, tq=128)
    o_m = _flash("moba", u16, kmean, tq=MOBA_BLOCK, nkv=2, tiles_per_step=t // MOBA_BLOCK)
    flat = lambda z: z.reshape(bsz * t, -1)
    return _odd_out_ln(flat(o_c), flat(o_s), flat(o_w), flat(o_m), u32, w_out, xf, g, b).reshape(bsz, t, d)


def kernel(x, even_w_in, lru_conv_w, lru_conv_b, lru_wa, lru_ba, lru_wx, lru_bx, lru_lambda,
           hg_lower_bound, hg_norm_g, even_w_out, odd_w_in, nsa_pe_k, nsa_pe_v, nsa_wk1, nsa_wk2,
           nsa_wv1, nsa_wv2, odd_w_out, ln_g, ln_b, w_router, b_router, w_gate, w_up, w_down,
           ws_gate, ws_up, ws_down):
    bsz, t, d = x.shape
    lb_all = jnp.cumsum(jax.nn.softmax(hg_lower_bound.astype(F32), axis=0), axis=0)
    for layer in range(DEPTH):
        li = layer // 2
        if layer % 2 == 0:
            x = _even_mixer_ln(x, even_w_in[li], lru_conv_w[li], lru_conv_b[li], lru_wa[li], lru_ba[li],
                               lru_wx[li], lru_bx[li], lru_lambda[li], lb_all[layer], hg_norm_g[li],
                               even_w_out[li], ln_g[layer, 0], ln_b[layer, 0])
        else:
            x = _odd_mixer_ln(x, odd_w_in[li], nsa_pe_k[li], nsa_pe_v[li], nsa_wk1[li], nsa_wk2[li],
                              nsa_wv1[li], nsa_wv2[li], odd_w_out[li], ln_g[layer, 0], ln_b[layer, 0])
        x = _moe_ln(x.reshape(bsz * t, d), layer, w_router[layer], b_router[layer], w_gate, w_up,
                    w_down, ws_gate[layer], ws_up[layer], ws_down[layer],
                    ln_g[layer, 1], ln_b[layer, 1]).reshape(bsz, t, d)
    return x
```

```python
import functools

import numpy as np
import jax
import jax.numpy as jnp
from jax import lax
from jax.experimental import pallas as pl
from jax.experimental.pallas import tpu as pltpu

F32 = jnp.float32
BF16 = jnp.bfloat16

D_MODEL = 1024
DEPTH = 2
LRU_WIDTH = 512
LRU_BLOCKS = 8
LRU_BLOCK = LRU_WIDTH // LRU_BLOCKS
CONV_W = 4
LRU_C = 8.0
HG_HEADS = 4
HG_DK = 128
HG_FDIM = HG_HEADS * HG_DK
HG_VDIM = HG_HEADS * HG_DK
HEAD_DIM = 64
NSA_HEADS = 8
NSA_GROUPS = 2
NSA_HPG = NSA_HEADS // NSA_GROUPS
NSA_KV = NSA_GROUPS * HEAD_DIM
CMP_LEN = 32
CMP_STRIDE = 16
CMP_HIDDEN = 128
SLC_LEN = 64
SLC_TOPN = 16
WIN = 512
FORCE_BONUS = 1e4
MOBA_HEADS = 8
MOBA_BLOCK = 256
MOBA_TOPK = 3
N_EXPERTS = 256
TOP_K = 8
N_GROUPS = 8
TOPK_GROUPS = 4
EXPERT_HIDDEN = 256
ROUTED_SCALE = 2.5
DN_ALPHA = (2 * DEPTH) ** 0.25
LN_EPS = 1e-5
NEG = -1e30
NEG_INF = float("-inf")

LANES = 128
SUBLANES = 8
VMEM_LIMIT = 48 * 1024 * 1024

HG_SUB = 16
EXPERT_ROWS = 256


def _cparams(sem):
    return pltpu.CompilerParams(dimension_semantics=sem, vmem_limit_bytes=VMEM_LIMIT)


def _dot(a, b):
    return jnp.dot(a.astype(BF16), b.astype(BF16), preferred_element_type=F32)


def _dot_nt(a, b):
    return lax.dot_general(a.astype(BF16), b.astype(BF16), (((1,), (1,)), ((), ())),
                           preferred_element_type=F32)


def _dot_tn(a, b):
    return lax.dot_general(a.astype(BF16), b.astype(BF16), (((0,), (0,)), ((), ())),
                           preferred_element_type=F32)


def _split2(x):
    hi = x.astype(BF16)
    lo = (x - hi.astype(F32)).astype(BF16)
    return hi, lo


def _split3(x):
    hi = x.astype(BF16)
    r = x - hi.astype(F32)
    mid = r.astype(BF16)
    lo = (r - mid.astype(F32)).astype(BF16)
    return hi, mid, lo


def _layer_norm(v, g, b):
    mu = jnp.mean(v, axis=-1, keepdims=True)
    d = v - mu
    var = jnp.mean(d * d, axis=-1, keepdims=True)
    return d * lax.rsqrt(var + LN_EPS) * g + b


def _sigmoid(x):
    return 1.0 / (1.0 + jnp.exp(-x))


def _silu(x):
    return x * _sigmoid(x)


def _proj_kernel(x_ref, w_ref, o_ref):
    o_ref[...] = jnp.dot(x_ref[...].astype(BF16), w_ref[...], preferred_element_type=F32)


def _proj(x, w, tm=256):
    n, k = x.shape
    m = w.shape[1]
    return pl.pallas_call(
        _proj_kernel,
        grid=(n // tm,),
        in_specs=[pl.BlockSpec((tm, k), lambda i: (i, 0)),
                  pl.BlockSpec((k, m), lambda i: (0, 0))],
        out_specs=pl.BlockSpec((tm, m), lambda i: (i, 0)),
        out_shape=jax.ShapeDtypeStruct((n, m), F32),
        compiler_params=_cparams(("parallel",)),
        name="proj",
    )(x, w)


def _proj_ln_kernel(ya_ref, yb_ref, wa_ref, wb_ref, x_ref, g_ref, b_ref, o_ref):
    mix = (jnp.dot(ya_ref[...].astype(BF16), wa_ref[...], preferred_element_type=F32)
           + jnp.dot(yb_ref[...].astype(BF16), wb_ref[...], preferred_element_type=F32))
    o_ref[...] = _layer_norm(DN_ALPHA * x_ref[...] + mix, g_ref[...], b_ref[...])


def _proj_ln(ya, yb, w, xres, g, b, tm=256):
    n, ka = ya.shape
    kb = yb.shape[1]
    d = w.shape[1]
    wb16 = w.astype(BF16)
    full = lambda shp: pl.BlockSpec(shp, lambda i: (0, 0))
    return pl.pallas_call(
        _proj_ln_kernel,
        grid=(n // tm,),
        in_specs=[pl.BlockSpec((tm, ka), lambda i: (i, 0)),
                  pl.BlockSpec((tm, kb), lambda i: (i, 0)),
                  full((ka, d)), full((kb, d)),
                  pl.BlockSpec((tm, d), lambda i: (i, 0)),
                  full((1, d)), full((1, d))],
        out_specs=pl.BlockSpec((tm, d), lambda i: (i, 0)),
        out_shape=jax.ShapeDtypeStruct((n, d), F32),
        compiler_params=_cparams(("parallel",)),
        name="proj_ln",
    )(ya, yb, wb16[:ka], wb16[ka:], xres, g.reshape(1, d), b.reshape(1, d))


U16_QA = 0
U16_QM = U16_QA + NSA_HEADS * LANES
U16_KS = U16_QM + MOBA_HEADS * LANES
U16_KW = U16_KS + NSA_GROUPS * LANES
U16_KM = U16_KW + NSA_GROUPS * LANES
U16_VS = U16_KM + MOBA_HEADS * LANES
U16_VW = U16_VS + NSA_GROUPS * LANES
U16_VM = U16_VW + NSA_GROUPS * LANES
U16_W = U16_VM + MOBA_HEADS * LANES
U32_W = 3 * LANES


def _odd_proj_kernel(x_ref, w16_ref, w32_ref, qc_ref, kt_ref, o16_ref, o32_ref, km_ref):
    xb = x_ref[...].astype(BF16)
    acc = jnp.dot(xb, w16_ref[...], preferred_element_type=F32)
    km_ref[0] = jnp.mean(acc[:, U16_KM:U16_VS], axis=0, keepdims=True)
    o16_ref[:, 0:U16_KS] = (acc[:, 0:U16_KS] + qc_ref[...]).astype(BF16)
    o16_ref[:, U16_KS:U16_VS] = (acc[:, U16_KS:U16_VS] + kt_ref[...]).astype(BF16)
    o16_ref[:, U16_VS:U16_W] = acc[:, U16_VS:U16_W].astype(BF16)
    o32_ref[...] = jnp.dot(xb, w32_ref[...], preferred_element_type=F32)


def _odd_proj(x, w_in, t, tm=MOBA_BLOCK):
    n, d = x.shape
    assert tm == MOBA_BLOCK and t % tm == 0
    dh = HEAD_DIM
    nq = NSA_HEADS * dh
    nm = MOBA_HEADS * dh
    sizes = [nq] + [NSA_KV] * 6 + [3 * NSA_HEADS] + [nm] * 3
    offs = np.concatenate([[0], np.cumsum(sizes)])
    q, kc, vc, ks, vs, kw, vw, gates, mq, mk, mv = [w_in[:, offs[i]:offs[i + 1]] for i in range(11)]
    scale = dh ** -0.5

    def heads(w, nh, mult=1.0):
        w = (w * mult).reshape(d, nh, dh)
        return jnp.pad(w, ((0, 0), (0, 0), (0, LANES - dh))).reshape(d, nh * LANES)

    w16 = jnp.concatenate([heads(q, NSA_HEADS, scale), heads(mq, MOBA_HEADS, scale), heads(ks, NSA_GROUPS),
                           heads(kw, NSA_GROUPS), heads(mk, MOBA_HEADS), heads(vs, NSA_GROUPS),
                           heads(vw, NSA_GROUPS), heads(mv, MOBA_HEADS)], axis=1).astype(BF16)
    w32 = jnp.concatenate([kc, vc, jnp.pad(gates, ((0, 0), (0, LANES - 3 * NSA_HEADS)))], axis=1).astype(BF16)

    n_all = NSA_HEADS + MOBA_HEADS
    s_all = 2.0 ** (-8.0 * np.arange(1, n_all + 1) / n_all)
    slopes = jnp.asarray(np.concatenate([s_all[0::2], s_all[1::2]]), F32)
    s_hi = slopes.astype(BF16).astype(F32)
    s_lo = slopes - s_hi
    scols = jnp.stack([POS_SPLIT * s_hi, POS_SPLIT * s_lo, s_hi, s_lo], axis=-1)
    qconst = jnp.pad(scols, ((0, 0), (dh, LANES - dh - 4))).reshape(1, U16_KS)

    pos = np.arange(t)
    pos_cols = np.stack([pos // POS_SPLIT, pos // POS_SPLIT, pos % POS_SPLIT, pos % POS_SPLIT], axis=-1)

    def key_cols(blk):
        cols = np.zeros((t, LANES), np.float32)
        cols[:, dh:dh + 4] = pos_cols
        if blk:
            cols[pos, MASK_COL0 + pos // blk] = 1.0
        return cols

    ktab = np.concatenate([key_cols(SLC_LEN)] * NSA_GROUPS + [key_cols(0)] * NSA_GROUPS
                          + [key_cols(MOBA_BLOCK)] * MOBA_HEADS, axis=1)
    tiles_per_seq = t // tm
    return pl.pallas_call(
        _odd_proj_kernel,
        grid=(n // tm,),
        in_specs=[pl.BlockSpec((tm, d), lambda i: (i, 0)),
                  pl.BlockSpec((d, U16_W), lambda i: (0, 0)),
                  pl.BlockSpec((d, U32_W), lambda i: (0, 0)),
                  pl.BlockSpec((1, U16_KS), lambda i: (0, 0)),
                  pl.BlockSpec((tm, U16_VS - U16_KS), lambda i: (i % tiles_per_seq, 0))],
        out_specs=[pl.BlockSpec((tm, U16_W), lambda i: (i, 0)),
                   pl.BlockSpec((tm, U32_W), lambda i: (i, 0)),
                   pl.BlockSpec((1, 1, U16_VS - U16_KM), lambda i: (i, 0, 0))],
        out_shape=[jax.ShapeDtypeStruct((n, U16_W), BF16), jax.ShapeDtypeStruct((n, U32_W), F32),
                   jax.ShapeDtypeStruct((n // tm, 1, U16_VS - U16_KM), F32)],
        compiler_params=_cparams(("parallel",)),
        name="odd_proj",
    )(x, w16, w32, qconst, jnp.asarray(ktab, F32))


def _odd_out_kernel(oc_ref, os_ref, ow_ref, om_ref, gt_ref, wa_ref, wb_ref, x_ref, g_ref, b_ref, o_ref):
    sg = _sigmoid(gt_ref[...])
    parts = []
    for h in range(NSA_HEADS):
        cols = slice(h * LANES, (h + 1) * LANES)
        parts.append(sg[:, 3 * h:3 * h + 1] * oc_ref[:, cols] + sg[:, 3 * h + 1:3 * h + 2] * os_ref[:, cols]
                     + sg[:, 3 * h + 2:3 * h + 3] * ow_ref[:, cols])
    yc = jnp.concatenate(parts, axis=1)
    mix = _dot(yc, wa_ref[...]) + _dot(om_ref[...], wb_ref[...])
    o_ref[...] = _layer_norm(DN_ALPHA * x_ref[...] + mix, g_ref[...], b_ref[...])


def _odd_out_ln(o_c, o_s, o_w, o_m, u32, w_out, xres, g, b, tm=256):
    n, d = xres.shape
    dh = HEAD_DIM

    def pad_rows(w, nh):
        return jnp.pad(w.reshape(nh, dh, d), ((0, 0), (0, LANES - dh), (0, 0))).reshape(nh * LANES, d).astype(BF16)

    wa = pad_rows(w_out[:NSA_HEADS * dh], NSA_HEADS)
    wb = pad_rows(w_out[NSA_HEADS * dh:], MOBA_HEADS)
    row = lambda width: pl.BlockSpec((tm, width), lambda i: (i, 0))
    full = lambda shp: pl.BlockSpec(shp, lambda i: (0, 0))
    ka = NSA_HEADS * LANES
    kb = MOBA_HEADS * LANES
    return pl.pallas_call(
        _odd_out_kernel,
        grid=(n // tm,),
        in_specs=[row(ka), row(ka), row(ka), row(kb),
                  pl.BlockSpec((tm, LANES), lambda i: (i, U32_W // LANES - 1)),
                  full((ka, d)), full((kb, d)), row(d), full((1, d)), full((1, d))],
        out_specs=row(d),
        out_shape=jax.ShapeDtypeStruct((n, d), F32),
        compiler_params=_cparams(("parallel",)),
        name="odd_out_ln",
    )(o_c, o_s, o_w, o_m, u32, wa, wb, xres, g.reshape(1, d), b.reshape(1, d))


def _lru_kernel(xb_ref, gb_ref, cw_ref, cb_ref, wa_ref, ba_ref, wx_ref, bx_ref, lam_ref,
                o_ref, xs_ref, a_ref, b_ref, h_ref):
    ti = pl.program_id(1)
    tc = xb_ref.shape[1]
    c = xb_ref.shape[2]

    @pl.when(ti == 0)
    def _():
        xs_ref[0:SUBLANES, :] = jnp.zeros((SUBLANES, c), F32)
        h_ref[...] = jnp.zeros_like(h_ref)

    xs_ref[SUBLANES:SUBLANES + tc, :] = xb_ref[0]
    xc = cb_ref[...]
    for w in range(CONV_W):
        xc = xc + cw_ref[w:w + 1, :] * xs_ref[pl.ds(SUBLANES - (CONV_W - 1) + w, tc), :]
    xs_ref[0:SUBLANES, :] = xs_ref[tc:tc + SUBLANES, :]

    r = _sigmoid(_dot(xc, wa_ref[...]) + ba_ref[...])
    i = _sigmoid(_dot(xc, wx_ref[...]) + bx_ref[...])
    nl = -lam_ref[...]
    softplus = jnp.maximum(nl, 0.0) + jnp.log(1.0 + jnp.exp(-jnp.abs(nl)))
    log_a = (-LRU_C * r) * softplus
    a = jnp.exp(log_a)
    mult = jnp.sqrt(1.0 - jnp.exp(2.0 * log_a))
    trow = ti * tc + lax.broadcasted_iota(jnp.int32, (tc, c), 0)
    mult = jnp.where(trow == 0, 1.0, mult)
    a_ref[...] = a
    b_ref[...] = mult * (i * xc)

    row = lax.broadcasted_iota(jnp.int32, (SUBLANES, c), 0)

    def body(g, carry):
        r0 = pl.multiple_of(g * SUBLANES, SUBLANES)
        av = a_ref[pl.ds(r0, SUBLANES), :]
        bv = b_ref[pl.ds(r0, SUBLANES), :]
        for d in (1, 2, 4):
            a_sh = pltpu.roll(av, d, 0)
            b_sh = pltpu.roll(bv, d, 0)
            m = row >= d
            bv = jnp.where(m, av * b_sh + bv, bv)
            av = jnp.where(m, av * a_sh, av)
        h = av * carry + bv
        b_ref[pl.ds(r0, SUBLANES), :] = h
        return h[SUBLANES - 1:SUBLANES, :]

    h_ref[...] = lax.fori_loop(0, tc // SUBLANES, body, h_ref[...])
    o_ref[0] = b_ref[...] * jax.nn.gelu(gb_ref[0])


def _lru(u3, conv_w, conv_b, wa_bd, ba, wx_bd, bx, lam, tc=256):
    b, t, _ = u3.shape
    c = LRU_WIDTH
    row = lambda v: v.reshape(1, c)
    full = lambda shp: pl.BlockSpec(shp, lambda bi, ti: (0, 0))
    return pl.pallas_call(
        _lru_kernel,
        grid=(b, t // tc),
        in_specs=[pl.BlockSpec((1, tc, c), lambda bi, ti: (bi, ti, 0)),
                  pl.BlockSpec((1, tc, c), lambda bi, ti: (bi, ti, 1)),
                  full((CONV_W, c)), full((1, c)), full((c, c)), full((1, c)),
                  full((c, c)), full((1, c)), full((1, c))],
        out_specs=pl.BlockSpec((1, tc, c), lambda bi, ti: (bi, ti, 0)),
        out_shape=jax.ShapeDtypeStruct((b, t, c), F32),
        scratch_shapes=[pltpu.VMEM((tc + SUBLANES, c), F32), pltpu.VMEM((tc, c), F32),
                        pltpu.VMEM((tc, c), F32), pltpu.VMEM((1, c), F32)],
        compiler_params=_cparams(("parallel", "arbitrary")),
        name="rg_lru",
    )(u3, u3, conv_w, row(conv_b), wa_bd, row(ba), wx_bd, row(bx), row(lam))


def _hgrn_kernel(q_ref, f_ref, v_ref, g_ref, lb_ref, ng_ref, tri_ref, ones_ref, o_ref,
                 st_ref, q_s, k_s, c_s, qe_s, kd_s, dl_s, o_s):
    ti = pl.program_id(2)
    tc = q_ref.shape[1]
    dk = HG_DK
    nh = q_ref.shape[2] // dk

    @pl.when(ti == 0)
    def _():
        st_ref[...] = jnp.zeros_like(st_ref)

    lbv = lb_ref[...]
    q = _silu(q_ref[0])
    f = lbv + (1.0 - lbv) * _sigmoid(f_ref[0])
    kk = 1.0 - f
    logf = jnp.log(f)
    parts = _split3(logf)
    tri = tri_ref[...]
    ones = ones_ref[...]
    cum = sum(jnp.dot(tri, p, preferred_element_type=F32) for p in parts)
    last = sum(jnp.dot(ones, p, preferred_element_type=F32) for p in parts)
    q_s[...] = q
    k_s[...] = kk
    c_s[...] = cum
    qe_s[...] = q * jnp.exp(cum)
    kd_s[...] = kk * jnp.exp(last - cum)
    dl_s[...] = jnp.exp(last)

    s_idx = lax.broadcasted_iota(jnp.int32, (HG_SUB, dk), 0)

    def chunk(ci, _):
        r0 = pl.multiple_of(ci * HG_SUB, HG_SUB)
        rows = pl.ds(r0, HG_SUB)
        for h in range(nh):
            cols = slice(h * dk, (h + 1) * dk)
            st = st_ref[h]
            vc = v_ref[0, rows, cols]
            qc = q_s[rows, cols]
            kc = k_s[rows, cols]
            cc = c_s[rows, cols]
            o = _dot_nt(qe_s[rows, cols], st)
            for t in range(HG_SUB):
                n = (t // SUBLANES + 1) * SUBLANES
                d = cc[t:t + 1, :] - cc[0:n, :]
                dec = jnp.exp(jnp.where(s_idx[0:n, :] <= t, d, NEG))
                w = (qc[t:t + 1, :] * kc[0:n, :]) * dec
                a_t = jnp.sum(w, axis=1, keepdims=True)
                o_t = jnp.sum(a_t * vc[0:n, :], axis=0, keepdims=True)
                o = o + jnp.where(s_idx == t, o_t, 0.0)
            o_s[rows, cols] = o
            st_ref[h] = st * dl_s[pl.ds(r0, 1), cols] + _dot_tn(vc, kd_s[rows, cols])
        return 0

    lax.fori_loop(0, tc // HG_SUB, chunk, 0, unroll=2)
    gate = _sigmoid(g_ref[0])
    for h in range(nh):
        cols = slice(h * dk, (h + 1) * dk)
        o = o_s[:, cols]
        o = o * lax.rsqrt(jnp.mean(o * o, axis=-1, keepdims=True) + 1e-6) * ng_ref[:, cols]
        o_ref[0, :, cols] = o * gate[:, cols]


def _hgrn2(u3, lb, norm_g, tc=256, nh=4):
    b, t, _ = u3.shape
    w = nh * HG_DK
    base = (2 * LRU_WIDTH) // w
    per = HG_FDIM // w
    idx = np.arange(tc)
    same = (idx[:, None] // HG_SUB) == (idx[None, :] // HG_SUB)
    tri = jnp.asarray(same & (idx[None, :] <= idx[:, None]), BF16)
    ones = jnp.asarray(same, BF16)

    def col(off):
        return pl.BlockSpec((1, tc, w), lambda bi, hi, ti: (bi, ti, base + off * per + hi))

    vec = pl.BlockSpec((1, w), lambda bi, hi, ti: (0, hi))
    cst = pl.BlockSpec((tc, tc), lambda bi, hi, ti: (0, 0))
    return pl.pallas_call(
        _hgrn_kernel,
        grid=(b, per, t // tc),
        in_specs=[col(0), col(1), col(2), col(3), vec, vec, cst, cst],
        out_specs=pl.BlockSpec((1, tc, w), lambda bi, hi, ti: (bi, ti, hi)),
        out_shape=jax.ShapeDtypeStruct((b, t, HG_VDIM), F32),
        scratch_shapes=[pltpu.VMEM((nh, HG_DK, HG_DK), F32)] + [pltpu.VMEM((tc, w), F32)] * 7,
        compiler_params=_cparams(("parallel", "parallel", "arbitrary")),
        name="hgrn2",
    )(u3, u3, u3, u3, lb.reshape(1, HG_FDIM), norm_g.reshape(1, HG_VDIM), tri, ones)


def _nsa_cmp_kv_kernel(zk_ref, zv_ref, pek_ref, pev_ref, wk1_ref, wk2_ref, wv1_ref, wv2_ref,
                       ok_ref, ov_ref):
    half = zk_ref.shape[3]
    nrow = zk_ref.shape[2]

    def compress(z, pe_ref, w1_ref, w2_ref):
        lo = _dot(z + pe_ref[0:1, :], w1_ref[0:half, :])
        hi = _dot(z + pe_ref[1:2, :], w1_ref[half:2 * half, :])
        pre = lo + pltpu.roll(hi, nrow - 1, 0)
        return _dot(jax.nn.gelu(pre), w2_ref[...])

    ok_ref[0, 0] = compress(zk_ref[0, 0], pek_ref, wk1_ref, wk2_ref)
    ov_ref[0, 0] = compress(zv_ref[0, 0], pev_ref, wv1_ref, wv2_ref)


def _nsa_compress(kc, vc, pe_k, pe_v, wk1, wk2, wv1, wv2):
    b, g, t, dh = kc.shape
    nrow = t // CMP_STRIDE
    half = CMP_STRIDE * dh
    zk = kc.reshape(b, g, nrow, half)
    zv = vc.reshape(b, g, nrow, half)
    zspec = pl.BlockSpec((1, 1, nrow, half), lambda bi, gi: (bi, gi, 0, 0))
    full = lambda shp: pl.BlockSpec(shp, lambda bi, gi: (0, 0))
    ospec = pl.BlockSpec((1, 1, nrow, LANES), lambda bi, gi: (bi, gi, 0, 0))
    oshape = jax.ShapeDtypeStruct((b, g, nrow, LANES), F32)
    pad = lambda w: jnp.pad(w, ((0, 0), (0, LANES - dh))).astype(BF16)
    return pl.pallas_call(
        _nsa_cmp_kv_kernel,
        grid=(b, g),
        in_specs=[zspec, zspec, full((2, half)), full((2, half)),
                  full((2 * half, CMP_HIDDEN)), full((CMP_HIDDEN, LANES)),
                  full((2 * half, CMP_HIDDEN)), full((CMP_HIDDEN, LANES))],
        out_specs=[ospec, ospec],
        out_shape=[oshape, oshape],
        compiler_params=_cparams(("parallel", "parallel")),
        name="nsa_compress",
    )(zk, zv, pe_k.reshape(2, half), pe_v.reshape(2, half),
      wk1.astype(BF16), pad(wk2), wv1.astype(BF16), pad(wv2))


def _nsa_sel_kernel(slope_ref, q_ref, kc_ref, vc_ref, ov_ref, o_ref, mb_ref):
    gi = pl.program_id(0) % NSA_GROUPS
    qi = pl.program_id(1)
    tq = q_ref.shape[1]
    hpg = q_ref.shape[2] // LANES
    nc = kc_ref.shape[2]
    ns = ov_ref.shape[0]

    q = jnp.concatenate([q_ref[0, :, i * LANES:(i + 1) * LANES] for i in range(hpg)], axis=0)
    khi, klo = _split2(kc_ref[0, 0])
    s = _dot_nt(q, khi) + _dot_nt(q, klo)
    tpos = qi * tq + lax.broadcasted_iota(jnp.int32, (tq, nc), 0)
    cend = lax.broadcasted_iota(jnp.int32, (tq, nc), 1) * CMP_STRIDE + (CMP_LEN - 1)
    d_c = (tpos - cend).astype(F32)
    valid = d_c >= 0.0
    ps = []
    for i in range(hpg):
        si = s[i * tq:(i + 1) * tq, :] - slope_ref[gi * hpg + i] * d_c
        si = jnp.where(valid, si, NEG)
        m = jnp.max(si, axis=-1, keepdims=True)
        p = jnp.where(valid, jnp.exp(si - m), 0.0)
        ps.append(p / jnp.maximum(jnp.sum(p, axis=-1, keepdims=True), 1e-30))
    oc = _dot(jnp.concatenate(ps, axis=0), vc_ref[0, 0])
    o_ref[0] = jnp.concatenate([oc[i * tq:(i + 1) * tq, :] for i in range(hpg)], axis=1)

    psum = ps[0]
    for i in range(1, hpg):
        psum = psum + ps[i]
    phi, plo = _split2(psum)
    ovl = ov_ref[...]
    imp = _dot_nt(ovl, phi) + _dot_nt(ovl, plo)

    j = lax.broadcasted_iota(jnp.int32, (ns, tq), 0)
    own = (qi * tq + lax.broadcasted_iota(jnp.int32, (ns, tq), 1)) // SLC_LEN
    forced = (j == 0) | (j == own) | (j == own - 1)
    imp = jnp.where(j <= own, imp + FORCE_BONUS * forced.astype(F32), NEG)
    rank = jnp.zeros((ns, tq), F32)
    for jp in range(ns):
        row = imp[jp:jp + 1, :]
        beats = (row > imp) | ((row == imp) & (j > jp))
        rank = rank + beats.astype(F32)
    bias = jnp.where(rank < float(min(SLC_TOPN, ns)), 0.0, MASKED)
    place = (lax.broadcasted_iota(jnp.int32, (ns, LANES), 1)
             == lax.broadcasted_iota(jnp.int32, (ns, LANES), 0) + MASK_COL0)
    mb_ref[0] = _dot_tn(bias, place.astype(F32)).astype(BF16)


def _nsa_select(slopes, u16, k_cmp, v_cmp, tq=256):
    b, t, _ = u16.shape
    g = NSA_GROUPS
    hpg = NSA_HPG
    nc = k_cmp.shape[2]
    ns = t // SLC_LEN
    ci = np.arange(nc)[:, None]
    cj = np.arange(ns)[None, :]
    ovl = ((CMP_STRIDE * ci < SLC_LEN * (cj + 1)) & (CMP_STRIDE * ci + CMP_LEN > SLC_LEN * cj))
    ovl = ovl & (ci < (t - CMP_LEN) // CMP_STRIDE + 1)
    grid_spec = pltpu.PrefetchScalarGridSpec(
        num_scalar_prefetch=1,
        grid=(b * g, t // tq),
        in_specs=[pl.BlockSpec((1, tq, hpg * LANES), lambda bg, qi, s: (bg // g, qi, U16_QA // (hpg * LANES) + bg % g)),
                  pl.BlockSpec((1, 1, nc, LANES), lambda bg, qi, s: (bg // g, bg % g, 0, 0)),
                  pl.BlockSpec((1, 1, nc, LANES), lambda bg, qi, s: (bg // g, bg % g, 0, 0)),
                  pl.BlockSpec((ns, nc), lambda bg, qi, s: (0, 0))],
        out_specs=[pl.BlockSpec((1, tq, hpg * LANES), lambda bg, qi, s: (bg // g, qi, bg % g)),
                   pl.BlockSpec((1, tq, LANES), lambda bg, qi, s: (bg // g, qi, bg % g))],
    )
    return pl.pallas_call(
        _nsa_sel_kernel,
        grid_spec=grid_spec,
        out_shape=[jax.ShapeDtypeStruct((b, t, g * hpg * LANES), F32),
                   jax.ShapeDtypeStruct((b, t, g * LANES), BF16)],
        compiler_params=_cparams(("parallel", "parallel")),
        name="nsa_select",
    )(slopes, u16, k_cmp, v_cmp, jnp.asarray(ovl.T, BF16))


POS_SPLIT = 64
MASKED = -2e30


MASK_COL0 = HEAD_DIM + 4
KEY_STEP = 512


def _flash_kernel(q_ref, k_ref, v_ref, *rest, mode, nkv, tq):
    o_ref = rest[-1]
    hpg = q_ref.shape[2] // (LANES * nkv)
    t_all = k_ref.shape[1]
    nt = q_ref.shape[1] // tq
    whole = nt * tq == t_all
    step = MOBA_BLOCK if mode == "moba" else KEY_STEP

    def tile(ti, q0):
        rows = slice(ti * tq, (ti + 1) * tq)
        qs = []
        for j in range(nkv):
            kv_cols = slice(j * LANES, (j + 1) * LANES)
            heads = [q_ref[0, rows, (j * hpg + i) * LANES:(j * hpg + i + 1) * LANES] for i in range(hpg)]
            if mode == "slc":
                heads = [h + rest[0][0, rows, kv_cols] for h in heads]
            q = heads[0] if hpg == 1 else jnp.concatenate(heads, axis=0)
            if mode == "moba":
                kmean = rest[0][0, :, kv_cols]
                nb = kmean.shape[0]
                mhi, mlo = _split2(kmean)
                gate = _dot_nt(mhi, q) + _dot_nt(mlo, q)
                jb = lax.broadcasted_iota(jnp.int32, (nb, tq), 0)
                ob = q0 // MOBA_BLOCK
                gate = jnp.where(jb < ob, gate, NEG)
                rank = jnp.zeros((nb, tq), F32)
                for jp in range(nb):
                    row = gate[jp:jp + 1, :]
                    beats = (row > gate) | ((row == gate) & (jb > jp))
                    rank = rank + beats.astype(F32)
                n_sel = float(min(MOBA_TOPK, nb - 1))
                bsel = ((rank < n_sel) & (jb < ob)) | (jb == ob)
                bias = jnp.where(bsel, 0.0, MASKED)
                place = (lax.broadcasted_iota(jnp.int32, (nb, LANES), 1)
                         == lax.broadcasted_iota(jnp.int32, (nb, LANES), 0) + MASK_COL0)
                q = q + _dot_tn(bias, place.astype(F32)).astype(BF16)
            qs.append(q)

        def attend(kstart, klen, diag, off):
            rel = lax.broadcasted_iota(jnp.int32, (tq, diag), 0) - lax.broadcasted_iota(jnp.int32, (tq, diag), 1)
            valid = rel >= off
            if mode == "win":
                valid = valid & (rel < off + WIN)
            if hpg > 1:
                valid = jnp.concatenate([valid] * hpg, axis=0)
            for j in range(nkv):
                kv_cols = slice(j * LANES, (j + 1) * LANES)
                s = _dot_nt(qs[j], k_ref[0, pl.ds(kstart, klen), kv_cols])
                sd = jnp.where(valid, s[:, klen - diag:], MASKED)
                s = sd if diag == klen else jnp.concatenate([s[:, :klen - diag], sd], axis=1)
                m = jnp.maximum(jnp.max(s, axis=-1, keepdims=True), NEG)
                p = jnp.exp(s - m)
                l = jnp.sum(p, axis=-1, keepdims=True)
                o = _dot(p, v_ref[0, pl.ds(kstart, klen), kv_cols]) / jnp.maximum(l, 1e-30)
                for i in range(hpg):
                    o_ref[0, rows, (j * hpg + i) * LANES:(j * hpg + i + 1) * LANES] = o[i * tq:(i + 1) * tq, :]

        if mode == "win":
            kstart = max(q0 - WIN, 0) if whole else pl.multiple_of(jnp.maximum(q0 - WIN, 0), LANES)
            attend(kstart, WIN + tq, WIN + tq, kstart - q0)
        elif whole:
            v = (q0 + tq - 1) // step
            attend(0, step * (v + 1), step, step * v - q0)
        else:
            var = (q0 + tq - 1) // step
            for v in range(t_all // step):
                @pl.when(var == v)
                def _(v=v):
                    attend(0, step * (v + 1), step, step * v - q0)

    for ti in range(nt):
        tile(ti, ti * tq if whole else (pl.program_id(1) * nt + ti) * tq)


def _flash(mode, u16, extra=None, tq=128, nkv=1, tiles_per_step=1):
    b, t, _ = u16.shape
    q_off, k_off, v_off, n_kvh, hpg = {
        "slc": (U16_QA, U16_KS, U16_VS, NSA_GROUPS, NSA_HPG),
        "win": (U16_QA, U16_KW, U16_VW, NSA_GROUPS, NSA_HPG),
        "moba": (U16_QM, U16_KM, U16_VM, MOBA_HEADS, 1)}[mode]
    assert mode != "moba" or tq == MOBA_BLOCK
    steps = n_kvh // nkv
    qw = nkv * hpg * LANES
    kw = nkv * LANES
    tb = tq * tiles_per_step
    in_specs = [pl.BlockSpec((1, tb, qw), lambda bh, qi: (bh // steps, qi, q_off // qw + bh % steps)),
                pl.BlockSpec((1, t, kw), lambda bh, qi: (bh // steps, 0, k_off // kw + bh % steps)),
                pl.BlockSpec((1, t, kw), lambda bh, qi: (bh // steps, 0, v_off // kw + bh % steps))]
    args = [u16, u16, u16]
    if mode == "slc":
        in_specs.append(pl.BlockSpec((1, tb, kw), lambda bh, qi: (bh // steps, qi, bh % steps)))
        args.append(extra)
    elif mode == "moba":
        in_specs.append(pl.BlockSpec((1, t // MOBA_BLOCK, kw), lambda bh, qi: (bh // steps, 0, bh % steps)))
        args.append(extra)
    return pl.pallas_call(
        functools.partial(_flash_kernel, mode=mode, nkv=nkv, tq=tq),
        grid=(b * steps, t // tb),
        in_specs=in_specs,
        out_specs=pl.BlockSpec((1, tb, qw), lambda bh, qi: (bh // steps, qi, bh % steps)),
        out_shape=jax.ShapeDtypeStruct((b, t, n_kvh * hpg * LANES), F32),
        compiler_params=_cparams(("parallel", "parallel")),
        name="flash_" + mode,
    )(*args)


def _router_kernel(x_ref, whi_ref, wlo_ref, b_ref, triu_ref, meta_ref, cnt_ref, carry_ref):
    step = pl.program_id(0)
    tm = x_ref.shape[0]
    ne = whi_ref.shape[0]
    gsz = ne // N_GROUPS

    @pl.when(step == 0)
    def _():
        carry_ref[...] = jnp.zeros_like(carry_ref)

    xhi, xlo = _split2(x_ref[...])
    whi = whi_ref[...]
    logits = _dot_nt(whi, xhi) + _dot_nt(wlo_ref[...], xhi) + _dot_nt(whi, xlo)
    scores = _sigmoid(logits)
    biased = scores + b_ref[:, 0:1]

    v3 = biased.reshape(N_GROUPS, gsz, tm)
    i3 = lax.broadcasted_iota(jnp.int32, (N_GROUPS, gsz, tm), 1).astype(F32)
    m1 = jnp.max(v3, axis=1, keepdims=True)
    idx1 = jnp.min(jnp.where(v3 == m1, i3, float(gsz)), axis=1, keepdims=True)
    m2 = jnp.max(jnp.where(i3 == idx1, NEG_INF, v3), axis=1, keepdims=True)
    gs = (m1 + m2).reshape(N_GROUPS, tm)
    gi = lax.broadcasted_iota(jnp.int32, (N_GROUPS, tm), 0).astype(F32)
    gmask = jnp.zeros((N_GROUPS, tm), F32)
    for _ in range(TOPK_GROUPS):
        m = jnp.max(gs, axis=0, keepdims=True)
        pick = jnp.min(jnp.where(gs == m, gi, float(N_GROUPS)), axis=0, keepdims=True)
        hit = gi == pick
        gmask = jnp.where(hit, 1.0, gmask)
        gs = jnp.where(hit, NEG_INF, gs)
    emask = jnp.broadcast_to(gmask.reshape(N_GROUPS, 1, tm), (N_GROUPS, gsz, tm)).reshape(ne, tm)
    cur = jnp.where(emask > 0.5, biased, NEG)

    ei = lax.broadcasted_iota(jnp.int32, (ne, tm), 0).astype(F32)
    selm = jnp.zeros((ne, tm), F32)
    idxs, svals = [], []
    for _ in range(TOP_K):
        m = jnp.max(cur, axis=0, keepdims=True)
        idx = jnp.min(jnp.where(cur == m, ei, float(ne)), axis=0, keepdims=True)
        hit = ei == idx
        svals.append(jnp.sum(jnp.where(hit, scores, 0.0), axis=0, keepdims=True))
        idxs.append(idx)
        cur = jnp.where(hit, NEG_INF, cur)
        selm = jnp.where(hit, 1.0, selm)

    carry = carry_ref[:, 0:1]
    rank_full = jnp.dot(selm.astype(BF16), triu_ref[...], preferred_element_type=F32) + carry
    ssum = svals[0]
    for k in range(1, TOP_K):
        ssum = ssum + svals[k]
    for k in range(TOP_K):
        meta_ref[k:k + 1, :] = idxs[k]
        meta_ref[TOP_K + k:TOP_K + k + 1, :] = svals[k] / ssum * ROUTED_SCALE
        meta_ref[2 * TOP_K + k:2 * TOP_K + k + 1, :] = jnp.sum(
            jnp.where(ei == idxs[k], rank_full, 0.0), axis=0, keepdims=True)
    new_carry = carry + jnp.sum(selm, axis=1, keepdims=True)
    carry_ref[...] = jnp.broadcast_to(new_carry, carry_ref.shape)
    cnt_ref[...] = jnp.broadcast_to(new_carry, cnt_ref.shape)


def _router(x, w_router, b_router, tm=256):
    n, d = x.shape
    ne = w_router.shape[1]
    wt = w_router.T
    whi = wt.astype(BF16)
    wlo = (wt - whi.astype(F32)).astype(BF16)
    triu = jnp.asarray(np.arange(tm)[:, None] < np.arange(tm)[None, :], BF16)
    meta, cnt = pl.pallas_call(
        _router_kernel,
        grid=(n // tm,),
        in_specs=[pl.BlockSpec((tm, d), lambda i: (i, 0)),
                  pl.BlockSpec((ne, d), lambda i: (0, 0)),
                  pl.BlockSpec((ne, d), lambda i: (0, 0)),
                  pl.BlockSpec((ne, LANES), lambda i: (0, 0)),
                  pl.BlockSpec((tm, tm), lambda i: (0, 0))],
        out_specs=[pl.BlockSpec((3 * TOP_K, tm), lambda i: (0, i)),
                   pl.BlockSpec((ne, LANES), lambda i: (0, 0))],
        out_shape=[jax.ShapeDtypeStruct((3 * TOP_K, n), F32),
                   jax.ShapeDtypeStruct((ne, LANES), F32)],
        scratch_shapes=[pltpu.VMEM((ne, LANES), F32)],
        compiler_params=_cparams(("arbitrary",)),
        name="moe_router",
    )(x, whi, wlo, jnp.broadcast_to(b_router.reshape(ne, 1), (ne, LANES)), triu)
    return meta, cnt[:, 0]


def _expert_kernel(be_ref, first_ref, nxt_ref, slot_ref, nu_ref, xs_ref, wg_hbm, wu_hbm, wd_hbm, o_ref,
                   wg_buf, wu_buf, wd_buf, wgu16, wd16, sem, *, layer):
    bi = pl.program_id(0)

    def weight_copies(e, s):
        return [pltpu.make_async_copy(hbm.at[layer, e], buf.at[s], sem.at[s])
                for hbm, buf in ((wg_hbm, wg_buf), (wu_hbm, wu_buf), (wd_hbm, wd_buf))]

    @pl.when(bi < nu_ref[0])
    def _():
        s = slot_ref[bi]

        @pl.when(first_ref[bi] == 1)
        def _():
            @pl.when(bi == 0)
            def _():
                for c in weight_copies(be_ref[0], s):
                    c.start()

            @pl.when(nxt_ref[bi] >= 0)
            def _():
                for c in weight_copies(nxt_ref[bi], 1 - s):
                    c.start()

            for c in weight_copies(be_ref[bi], s):
                c.wait()
            hd = wd16.shape[0]
            wgu16[:, 0:hd] = wg_buf[s].astype(BF16)
            wgu16[:, hd:2 * hd] = wu_buf[s].astype(BF16)
            wd16[...] = wd_buf[s].astype(BF16)

        xa, xb = _unpack_bf16_pair(xs_ref[...])
        half = xa.shape[1]

        def xw(w_ref):
            return (jnp.dot(xa, w_ref[0:half, :], preferred_element_type=F32)
                    + jnp.dot(xb, w_ref[half:2 * half, :], preferred_element_type=F32))

        gu = xw(wgu16)
        hd = wd16.shape[0]
        h = _silu(gu[:, 0:hd]) * gu[:, hd:2 * hd]
        o_ref[...] = _pack_bf16_pair(jnp.dot(h.astype(BF16), wd16[...], preferred_element_type=F32))

    @pl.when(bi >= nu_ref[0])
    def _():
        o_ref[...] = jnp.zeros_like(o_ref)


def _expert_ffn(blk_e, n_used, xs, layer, w_gate, w_up, w_down):
    n_rows = xs.shape[0]
    d = w_gate.shape[2]
    hdim = w_gate.shape[3]
    n_blk = n_rows // EXPERT_ROWS
    idx = jnp.arange(n_blk, dtype=jnp.int32)
    first = jnp.concatenate([jnp.ones((1,), bool), blk_e[1:] != blk_e[:-1]]) & (idx < n_used[0])
    slot = ((jnp.cumsum(first.astype(jnp.int32)) - 1) % 2).astype(jnp.int32)
    nxt_first = lax.cummin(jnp.where(first, idx, n_blk), reverse=True)
    nxt_first = jnp.concatenate([nxt_first[1:], jnp.full((1,), n_blk, jnp.int32)])
    nxt_e = jnp.where(nxt_first < n_blk, blk_e[jnp.minimum(nxt_first, n_blk - 1)], -1).astype(jnp.int32)
    grid_spec = pltpu.PrefetchScalarGridSpec(
        num_scalar_prefetch=5,
        grid=(n_blk,),
        in_specs=[pl.BlockSpec((EXPERT_ROWS, d // 2),
                               lambda i, be, fi, nx, sl, nu: (jnp.minimum(i, jnp.maximum(nu[0] - 1, 0)), 0)),
                  pl.BlockSpec(memory_space=pl.ANY),
                  pl.BlockSpec(memory_space=pl.ANY),
                  pl.BlockSpec(memory_space=pl.ANY)],
        out_specs=pl.BlockSpec((EXPERT_ROWS, d // 2), lambda i, be, fi, nx, sl, nu: (i, 0)),
        scratch_shapes=[pltpu.VMEM((2, d, hdim), F32), pltpu.VMEM((2, d, hdim), F32), pltpu.VMEM((2, hdim, d), F32),
                        pltpu.VMEM((d, 2 * hdim), BF16), pltpu.VMEM((hdim, d), BF16),
                        pltpu.SemaphoreType.DMA((2,))],
    )
    return pl.pallas_call(
        functools.partial(_expert_kernel, layer=layer),
        grid_spec=grid_spec,
        out_shape=jax.ShapeDtypeStruct((n_rows, d // 2), jnp.uint32),
        compiler_params=_cparams(("arbitrary",)),
        name="moe_experts",
    )(blk_e, first.astype(jnp.int32), nxt_e, slot, n_used, xs, w_gate, w_up, w_down)


def _dest_kernel(meta_ref, pstart_ref, dest_ref):
    ne = pstart_ref.shape[0]
    tm = meta_ref.shape[1]
    ei = lax.broadcasted_iota(jnp.int32, (ne, tm), 0).astype(F32)
    ps = pstart_ref[:, 0:1]
    for k in range(TOP_K):
        base = jnp.sum(jnp.where(ei == meta_ref[k:k + 1, :], ps, 0.0), axis=0, keepdims=True)
        dest_ref[k:k + 1, :] = (base + meta_ref[2 * TOP_K + k:2 * TOP_K + k + 1, :]).astype(jnp.int32)


def _dest_rows(meta, pstart, tm=512):
    n = meta.shape[1]
    ne = pstart.shape[0]
    return pl.pallas_call(
        _dest_kernel,
        grid=(n // tm,),
        in_specs=[pl.BlockSpec((3 * TOP_K, tm), lambda i: (0, i)),
                  pl.BlockSpec((ne, LANES), lambda i: (0, 0))],
        out_specs=pl.BlockSpec((TOP_K, tm), lambda i: (0, i)),
        out_shape=jax.ShapeDtypeStruct((TOP_K, n), jnp.int32),
        compiler_params=_cparams(("parallel",)),
        name="moe_dest",
    )(meta, jnp.broadcast_to(pstart.astype(F32).reshape(ne, 1), (ne, LANES)))


def _pack_bf16_pair(x):
    half = x.shape[1] // 2
    hi = pltpu.bitcast(x[:, 0:half].astype(BF16).astype(F32), jnp.uint32)
    lo = pltpu.bitcast(x[:, half:2 * half].astype(BF16).astype(F32), jnp.uint32)
    return hi | (lo >> 16)


def _unpack_bf16_pair(p):
    hi = pltpu.bitcast(p & jnp.uint32(0xFFFF0000), F32).astype(BF16)
    lo = pltpu.bitcast(p << 16, F32).astype(BF16)
    return hi, lo


def _dispatch_kernel(dest_ref, x_ref, xs_in_ref, xs_ref, pk_ref, sem):
    del xs_in_ref
    tm = x_ref.shape[0]
    pk_ref[...] = _pack_bf16_pair(x_ref[...])

    def row_copy(t, k):
        return pltpu.make_async_copy(pk_ref.at[pl.ds(t, 1), :], xs_ref.at[pl.ds(dest_ref[t * TOP_K + k], 1), :], sem)

    def body(t, _):
        for k in range(TOP_K):
            row_copy(t, k).start()
        return 0

    lax.fori_loop(0, tm, body, 0, unroll=4)
    done = xs_ref.at[pl.ds(0, TOP_K * tm), :]
    pltpu.make_async_copy(done, done, sem).wait()


def _dispatch(dest, x, n_rows, tm=256):
    n, d = x.shape
    return pl.pallas_call(
        _dispatch_kernel,
        grid=(n // tm,),
        in_specs=[pl.BlockSpec((TOP_K * tm,), lambda i: (i,), memory_space=pltpu.SMEM),
                  pl.BlockSpec((tm, d), lambda i: (i, 0)),
                  pl.BlockSpec(memory_space=pl.ANY)],
        out_specs=pl.BlockSpec(memory_space=pl.ANY),
        out_shape=jax.ShapeDtypeStruct((n_rows, d // 2), jnp.uint32),
        scratch_shapes=[pltpu.VMEM((tm, d // 2), jnp.uint32), pltpu.SemaphoreType.DMA(())],
        input_output_aliases={2: 0},
        compiler_params=_cparams(("arbitrary",)),
        name="moe_dispatch",
    )(dest, x, jnp.zeros((n_rows, d // 2), jnp.uint32))


def _combine_ln_kernel(dest_ref, dest_next_ref, w_ref, x_ref, ys_ref, wg_ref, wu_ref, wd_ref, g_ref, b_ref,
                       o_ref, buf, sem):
    step = pl.program_id(0)
    tm = x_ref.shape[0]
    slot = step % 2

    def gather_rows(idx_ref, to_slot):
        def body(t, _):
            for k in range(TOP_K):
                pltpu.make_async_copy(ys_ref.at[pl.ds(idx_ref[t * TOP_K + k], 1), :],
                                      buf.at[to_slot, k, pl.ds(t, 1), :], sem.at[to_slot]).start()
            return 0

        lax.fori_loop(0, tm, body, 0, unroll=4)

    @pl.when(step == 0)
    def _():
        gather_rows(dest_ref, 0)

    @pl.when(step + 1 < pl.num_programs(0))
    def _():
        gather_rows(dest_next_ref, 1 - slot)

    x = x_ref[...]
    xb = x.astype(BF16)
    h = _silu(jnp.dot(xb, wg_ref[...], preferred_element_type=F32)) * jnp.dot(
        xb, wu_ref[...], preferred_element_type=F32)
    ff = jnp.dot(h.astype(BF16), wd_ref[...], preferred_element_type=F32)
    pltpu.make_async_copy(buf.at[slot], buf.at[slot], sem.at[slot]).wait()
    routed_a = jnp.zeros((tm, buf.shape[3]), F32)
    routed_b = jnp.zeros((tm, buf.shape[3]), F32)
    for k in range(TOP_K):
        ya, yb = _unpack_bf16_pair(buf[slot, k])
        routed_a = routed_a + ya.astype(F32) * w_ref[:, k:k + 1]
        routed_b = routed_b + yb.astype(F32) * w_ref[:, k:k + 1]
    ff = ff + jnp.concatenate([routed_a, routed_b], axis=1)
    o_ref[...] = _layer_norm(DN_ALPHA * x + ff, g_ref[...], b_ref[...])


def _combine_ln(dest, wts, x, ys, ws_gate, ws_up, ws_down, g, b, tm=256):
    n, d = x.shape
    hdim = ws_gate.shape[1]
    row = pl.BlockSpec((tm, d), lambda i: (i, 0))
    last = n // tm - 1
    return pl.pallas_call(
        _combine_ln_kernel,
        grid=(n // tm,),
        in_specs=[pl.BlockSpec((TOP_K * tm,), lambda i: (i,), memory_space=pltpu.SMEM),
                  pl.BlockSpec((TOP_K * tm,), lambda i: (jnp.minimum(i + 1, last),), memory_space=pltpu.SMEM),
                  pl.BlockSpec((tm, TOP_K), lambda i: (i, 0)),
                  row,
                  pl.BlockSpec(memory_space=pl.ANY),
                  pl.BlockSpec((d, hdim), lambda i: (0, 0)),
                  pl.BlockSpec((d, hdim), lambda i: (0, 0)),
                  pl.BlockSpec((hdim, d), lambda i: (0, 0)),
                  pl.BlockSpec((1, d), lambda i: (0, 0)),
                  pl.BlockSpec((1, d), lambda i: (0, 0))],
        out_specs=row,
        out_shape=jax.ShapeDtypeStruct((n, d), F32),
        scratch_shapes=[pltpu.VMEM((2, TOP_K, tm, d // 2), jnp.uint32), pltpu.SemaphoreType.DMA((2,))],
        compiler_params=_cparams(("arbitrary",)),
        name="moe_combine_ln",
    )(dest, dest, wts, x, ys, ws_gate.astype(BF16), ws_up.astype(BF16), ws_down.astype(BF16),
      g.reshape(1, d), b.reshape(1, d))


def _moe_ln(x, layer, w_router, b_router, w_gate, w_up, w_down, ws_gate, ws_up, ws_down, g, b):
    n, d = x.shape
    ne = w_router.shape[1]
    meta, counts = _router(x, w_router, b_router)
    counts = counts.astype(jnp.int32)
    padded = (counts + EXPERT_ROWS - 1) // EXPERT_ROWS * EXPERT_ROWS
    pend = jnp.cumsum(padded)
    dest = _dest_rows(meta, pend - padded).T.reshape(-1)
    n_blk = (n * TOP_K) // EXPERT_ROWS + ne
    blk_row0 = jnp.arange(n_blk, dtype=jnp.int32) * EXPERT_ROWS
    blk_e = jnp.minimum(jnp.sum((pend[None, :] <= blk_row0[:, None]).astype(jnp.int32), axis=1), ne - 1)
    n_used = (pend[-1] // EXPERT_ROWS).astype(jnp.int32).reshape(1)
    xs = _dispatch(dest, x, n_blk * EXPERT_ROWS)
    ys = _expert_ffn(blk_e, n_used, xs, layer, w_gate, w_up, w_down)
    return _combine_ln(dest, meta[TOP_K:2 * TOP_K].T, x, ys, ws_gate, ws_up, ws_down, g, b)


def _block_diag(w):
    nb, bs, _ = w.shape
    eye = jnp.eye(nb, dtype=w.dtype)
    return (eye[:, None, :, None] * w[:, :, None, :]).reshape(nb * bs, nb * bs)


def _even_mixer_ln(x, w_in, conv_w, conv_b, wa, ba, wx, bx, lam, lb, norm_g, w_out, g, b):
    bsz, t, d = x.shape
    xf = x.reshape(bsz * t, d)
    u = _proj(xf, w_in.astype(BF16)).reshape(bsz, t, -1)
    ya = _lru(u, conv_w, conv_b, _block_diag(wa).astype(BF16), ba, _block_diag(wx).astype(BF16), bx, lam)
    yb = _hgrn2(u, lb, norm_g)
    return _proj_ln(ya.reshape(bsz * t, -1), yb.reshape(bsz * t, -1), w_out, xf, g, b).reshape(bsz, t, d)


def _odd_mixer_ln(x, w_in, pe_k, pe_v, wk1, wk2, wv1, wv2, w_out, g, b):
    bsz, t, d = x.shape
    xf = x.reshape(bsz * t, d)
    u16, u32, kmean = _odd_proj(xf, w_in, t)
    u16 = u16.reshape(bsz, t, U16_W)
    kmean = kmean.reshape(bsz, t // MOBA_BLOCK, -1)

    def grp(off):
        z = u32[:, off:off + NSA_KV].reshape(bsz, t, NSA_GROUPS, HEAD_DIM)
        return z.transpose(0, 2, 1, 3)

    n_all = NSA_HEADS + MOBA_HEADS
    s_nsa = jnp.asarray((2.0 ** (-8.0 * np.arange(1, n_all + 1) / n_all))[0::2], F32)
    k_cmp, v_cmp = _nsa_compress(grp(0), grp(NSA_KV), pe_k, pe_v, wk1, wk2, wv1, wv2)
    o_c, mask_bias = _nsa_select(s_nsa, u16, k_cmp, v_cmp)
    o_s = _flash("slc", u16, mask_bias, tq=128)
    o_w = _flash("win", u16, tq=128)
    o_m = _flash("moba", u16, kmean, tq=MOBA_BLOCK, nkv=2, tiles_per_step=t // MOBA_BLOCK)
    flat = lambda z: z.reshape(bsz * t, -1)
    return _odd_out_ln(flat(o_c), flat(o_s), flat(o_w), flat(o_m), u32, w_out, xf, g, b).reshape(bsz, t, d)


def kernel(x, even_w_in, lru_conv_w, lru_conv_b, lru_wa, lru_ba, lru_wx, lru_bx, lru_lambda,
           hg_lower_bound, hg_norm_g, even_w_out, odd_w_in, nsa_pe_k, nsa_pe_v, nsa_wk1, nsa_wk2,
           nsa_wv1, nsa_wv2, odd_w_out, ln_g, ln_b, w_router, b_router, w_gate, w_up, w_down,
           ws_gate, ws_up, ws_down):
    bsz, t, d = x.shape
    lb_all = jnp.cumsum(jax.nn.softmax(hg_lower_bound.astype(F32), axis=0), axis=0)
    for layer in range(DEPTH):
        li = layer // 2
        if layer % 2 == 0:
            x = _even_mixer_ln(x, even_w_in[li], lru_conv_w[li], lru_conv_b[li], lru_wa[li], lru_ba[li],
                               lru_wx[li], lru_bx[li], lru_lambda[li], lb_all[layer], hg_norm_g[li],
                               even_w_out[li], ln_g[layer, 0], ln_b[layer, 0])
        else:
            x = _odd_mixer_ln(x, odd_w_in[li], nsa_pe_k[li], nsa_pe_v[li], nsa_wk1[li], nsa_wk2[li],
                              nsa_wv1[li], nsa_wv2[li], odd_w_out[li], ln_g[layer, 0], ln_b[layer, 0])
        x = _moe_ln(x.reshape(bsz * t, d), layer, w_router[layer], b_router[layer], w_gate, w_up,
                    w_down, ws_gate[layer], ws_up[layer], ws_down[layer],
                    ln_g[layer, 1], ln_b[layer, 1]).reshape(bsz, t, d)
    return x
```

```python
import functools

import numpy as np
import jax
import jax.numpy as jnp
from jax import lax
from jax.experimental import pallas as pl
from jax.experimental.pallas import tpu as pltpu

F32 = jnp.float32
BF16 = jnp.bfloat16

D_MODEL = 1024
DEPTH = 2
LRU_WIDTH = 512
LRU_BLOCKS = 8
LRU_BLOCK = LRU_WIDTH // LRU_BLOCKS
CONV_W = 4
LRU_C = 8.0
HG_HEADS = 4
HG_DK = 128
HG_FDIM = HG_HEADS * HG_DK
HG_VDIM = HG_HEADS * HG_DK
HEAD_DIM = 64
NSA_HEADS = 8
NSA_GROUPS = 2
NSA_HPG = NSA_HEADS // NSA_GROUPS
NSA_KV = NSA_GROUPS * HEAD_DIM
CMP_LEN = 32
CMP_STRIDE = 16
CMP_HIDDEN = 128
SLC_LEN = 64
SLC_TOPN = 16
WIN = 512
FORCE_BONUS = 1e4
MOBA_HEADS = 8
MOBA_BLOCK = 256
MOBA_TOPK = 3
N_EXPERTS = 256
TOP_K = 8
N_GROUPS = 8
TOPK_GROUPS = 4
EXPERT_HIDDEN = 256
ROUTED_SCALE = 2.5
DN_ALPHA = (2 * DEPTH) ** 0.25
LN_EPS = 1e-5
NEG = -1e30
NEG_INF = float("-inf")

LANES = 128
SUBLANES = 8
VMEM_LIMIT = 48 * 1024 * 1024

HG_SUB = 16
EXPERT_ROWS = 256


def _cparams(sem):
    return pltpu.CompilerParams(dimension_semantics=sem, vmem_limit_bytes=VMEM_LIMIT)


def _dot(a, b):
    return jnp.dot(a.astype(BF16), b.astype(BF16), preferred_element_type=F32)


def _dot_nt(a, b):
    return lax.dot_general(a.astype(BF16), b.astype(BF16), (((1,), (1,)), ((), ())),
                           preferred_element_type=F32)


def _dot_tn(a, b):
    return lax.dot_general(a.astype(BF16), b.astype(BF16), (((0,), (0,)), ((), ())),
                           preferred_element_type=F32)


def _split2(x):
    hi = x.astype(BF16)
    lo = (x - hi.astype(F32)).astype(BF16)
    return hi, lo


def _split3(x):
    hi = x.astype(BF16)
    r = x - hi.astype(F32)
    mid = r.astype(BF16)
    lo = (r - mid.astype(F32)).astype(BF16)
    return hi, mid, lo


def _layer_norm(v, g, b):
    mu = jnp.mean(v, axis=-1, keepdims=True)
    d = v - mu
    var = jnp.mean(d * d, axis=-1, keepdims=True)
    return d * lax.rsqrt(var + LN_EPS) * g + b


def _sigmoid(x):
    return 1.0 / (1.0 + jnp.exp(-x))


def _silu(x):
    return x * _sigmoid(x)


def _proj_kernel(x_ref, w_ref, o_ref):
    o_ref[...] = jnp.dot(x_ref[...].astype(BF16), w_ref[...], preferred_element_type=F32)


def _proj(x, w, tm=256):
    n, k = x.shape
    m = w.shape[1]
    return pl.pallas_call(
        _proj_kernel,
        grid=(n // tm,),
        in_specs=[pl.BlockSpec((tm, k), lambda i: (i, 0)),
                  pl.BlockSpec((k, m), lambda i: (0, 0))],
        out_specs=pl.BlockSpec((tm, m), lambda i: (i, 0)),
        out_shape=jax.ShapeDtypeStruct((n, m), F32),
        compiler_params=_cparams(("parallel",)),
        name="proj",
    )(x, w)


def _proj_ln_kernel(ya_ref, yb_ref, wa_ref, wb_ref, x_ref, g_ref, b_ref, o_ref):
    mix = (jnp.dot(ya_ref[...].astype(BF16), wa_ref[...], preferred_element_type=F32)
           + jnp.dot(yb_ref[...].astype(BF16), wb_ref[...], preferred_element_type=F32))
    o_ref[...] = _layer_norm(DN_ALPHA * x_ref[...] + mix, g_ref[...], b_ref[...])


def _proj_ln(ya, yb, w, xres, g, b, tm=256):
    n, ka = ya.shape
    kb = yb.shape[1]
    d = w.shape[1]
    wb16 = w.astype(BF16)
    full = lambda shp: pl.BlockSpec(shp, lambda i: (0, 0))
    return pl.pallas_call(
        _proj_ln_kernel,
        grid=(n // tm,),
        in_specs=[pl.BlockSpec((tm, ka), lambda i: (i, 0)),
                  pl.BlockSpec((tm, kb), lambda i: (i, 0)),
                  full((ka, d)), full((kb, d)),
                  pl.BlockSpec((tm, d), lambda i: (i, 0)),
                  full((1, d)), full((1, d))],
        out_specs=pl.BlockSpec((tm, d), lambda i: (i, 0)),
        out_shape=jax.ShapeDtypeStruct((n, d), F32),
        compiler_params=_cparams(("parallel",)),
        name="proj_ln",
    )(ya, yb, wb16[:ka], wb16[ka:], xres, g.reshape(1, d), b.reshape(1, d))


U16_QA = 0
U16_QM = U16_QA + NSA_HEADS * LANES
U16_KS = U16_QM + MOBA_HEADS * LANES
U16_KW = U16_KS + NSA_GROUPS * LANES
U16_KM = U16_KW + NSA_GROUPS * LANES
U16_VS = U16_KM + MOBA_HEADS * LANES
U16_VW = U16_VS + NSA_GROUPS * LANES
U16_VM = U16_VW + NSA_GROUPS * LANES
U16_W = U16_VM + MOBA_HEADS * LANES
U32_W = 3 * LANES


def _odd_proj_kernel(x_ref, w16_ref, w32_ref, qc_ref, kt_ref, o16_ref, o32_ref, km_ref):
    xb = x_ref[...].astype(BF16)
    acc = jnp.dot(xb, w16_ref[...], preferred_element_type=F32)
    km_ref[0] = jnp.mean(acc[:, U16_KM:U16_VS], axis=0, keepdims=True)
    o16_ref[:, 0:U16_KS] = (acc[:, 0:U16_KS] + qc_ref[...]).astype(BF16)
    o16_ref[:, U16_KS:U16_VS] = (acc[:, U16_KS:U16_VS] + kt_ref[...]).astype(BF16)
    o16_ref[:, U16_VS:U16_W] = acc[:, U16_VS:U16_W].astype(BF16)
    o32_ref[...] = jnp.dot(xb, w32_ref[...], preferred_element_type=F32)


def _odd_proj(x, w_in, t, tm=MOBA_BLOCK):
    n, d = x.shape
    assert tm == MOBA_BLOCK and t % tm == 0
    dh = HEAD_DIM
    nq = NSA_HEADS * dh
    nm = MOBA_HEADS * dh
    sizes = [nq] + [NSA_KV] * 6 + [3 * NSA_HEADS] + [nm] * 3
    offs = np.concatenate([[0], np.cumsum(sizes)])
    q, kc, vc, ks, vs, kw, vw, gates, mq, mk, mv = [w_in[:, offs[i]:offs[i + 1]] for i in range(11)]
    scale = dh ** -0.5

    def heads(w, nh, mult=1.0):
        w = (w * mult).reshape(d, nh, dh)
        return jnp.pad(w, ((0, 0), (0, 0), (0, LANES - dh))).reshape(d, nh * LANES)

    w16 = jnp.concatenate([heads(q, NSA_HEADS, scale), heads(mq, MOBA_HEADS, scale), heads(ks, NSA_GROUPS),
                           heads(kw, NSA_GROUPS), heads(mk, MOBA_HEADS), heads(vs, NSA_GROUPS),
                           heads(vw, NSA_GROUPS), heads(mv, MOBA_HEADS)], axis=1).astype(BF16)
    w32 = jnp.concatenate([kc, vc, jnp.pad(gates, ((0, 0), (0, LANES - 3 * NSA_HEADS)))], axis=1).astype(BF16)

    n_all = NSA_HEADS + MOBA_HEADS
    s_all = 2.0 ** (-8.0 * np.arange(1, n_all + 1) / n_all)
    slopes = jnp.asarray(np.concatenate([s_all[0::2], s_all[1::2]]), F32)
    s_hi = slopes.astype(BF16).astype(F32)
    s_lo = slopes - s_hi
    scols = jnp.stack([POS_SPLIT * s_hi, POS_SPLIT * s_lo, s_hi, s_lo], axis=-1)
    qconst = jnp.pad(scols, ((0, 0), (dh, LANES - dh - 4))).reshape(1, U16_KS)

    pos = np.arange(t)
    pos_cols = np.stack([pos // POS_SPLIT, pos // POS_SPLIT, pos % POS_SPLIT, pos % POS_SPLIT], axis=-1)

    def key_cols(blk):
        cols = np.zeros((t, LANES), np.float32)
        cols[:, dh:dh + 4] = pos_cols
        if blk:
            cols[pos, MASK_COL0 + pos // blk] = 1.0
        return cols

    ktab = np.concatenate([key_cols(SLC_LEN)] * NSA_GROUPS + [key_cols(0)] * NSA_GROUPS
                          + [key_cols(MOBA_BLOCK)] * MOBA_HEADS, axis=1)
    tiles_per_seq = t // tm
    return pl.pallas_call(
        _odd_proj_kernel,
        grid=(n // tm,),
        in_specs=[pl.BlockSpec((tm, d), lambda i: (i, 0)),
                  pl.BlockSpec((d, U16_W), lambda i: (0, 0)),
                  pl.BlockSpec((d, U32_W), lambda i: (0, 0)),
                  pl.BlockSpec((1, U16_KS), lambda i: (0, 0)),
                  pl.BlockSpec((tm, U16_VS - U16_KS), lambda i: (i % tiles_per_seq, 0))],
        out_specs=[pl.BlockSpec((tm, U16_W), lambda i: (i, 0)),
                   pl.BlockSpec((tm, U32_W), lambda i: (i, 0)),
                   pl.BlockSpec((1, 1, U16_VS - U16_KM), lambda i: (i, 0, 0))],
        out_shape=[jax.ShapeDtypeStruct((n, U16_W), BF16), jax.ShapeDtypeStruct((n, U32_W), F32),
                   jax.ShapeDtypeStruct((n // tm, 1, U16_VS - U16_KM), F32)],
        compiler_params=_cparams(("parallel",)),
        name="odd_proj",
    )(x, w16, w32, qconst, jnp.asarray(ktab, F32))


def _odd_out_kernel(oc_ref, os_ref, ow_ref, om_ref, gt_ref, wa_ref, wb_ref, x_ref, g_ref, b_ref, o_ref):
    sg = _sigmoid(gt_ref[...])
    parts = []
    for h in range(NSA_HEADS):
        cols = slice(h * LANES, (h + 1) * LANES)
        parts.append(sg[:, 3 * h:3 * h + 1] * oc_ref[:, cols] + sg[:, 3 * h + 1:3 * h + 2] * os_ref[:, cols]
                     + sg[:, 3 * h + 2:3 * h + 3] * ow_ref[:, cols])
    yc = jnp.concatenate(parts, axis=1)
    mix = _dot(yc, wa_ref[...]) + _dot(om_ref[...], wb_ref[...])
    o_ref[...] = _layer_norm(DN_ALPHA * x_ref[...] + mix, g_ref[...], b_ref[...])


def _odd_out_ln(o_c, o_s, o_w, o_m, u32, w_out, xres, g, b, tm=256):
    n, d = xres.shape
    dh = HEAD_DIM

    def pad_rows(w, nh):
        return jnp.pad(w.reshape(nh, dh, d), ((0, 0), (0, LANES - dh), (0, 0))).reshape(nh * LANES, d).astype(BF16)

    wa = pad_rows(w_out[:NSA_HEADS * dh], NSA_HEADS)
    wb = pad_rows(w_out[NSA_HEADS * dh:], MOBA_HEADS)
    row = lambda width: pl.BlockSpec((tm, width), lambda i: (i, 0))
    full = lambda shp: pl.BlockSpec(shp, lambda i: (0, 0))
    ka = NSA_HEADS * LANES
    kb = MOBA_HEADS * LANES
    return pl.pallas_call(
        _odd_out_kernel,
        grid=(n // tm,),
        in_specs=[row(ka), row(ka), row(ka), row(kb),
                  pl.BlockSpec((tm, LANES), lambda i: (i, U32_W // LANES - 1)),
                  full((ka, d)), full((kb, d)), row(d), full((1, d)), full((1, d))],
        out_specs=row(d),
        out_shape=jax.ShapeDtypeStruct((n, d), F32),
        compiler_params=_cparams(("parallel",)),
        name="odd_out_ln",
    )(o_c, o_s, o_w, o_m, u32, wa, wb, xres, g.reshape(1, d), b.reshape(1, d))


def _lru_kernel(xb_ref, gb_ref, cw_ref, cb_ref, wa_ref, ba_ref, wx_ref, bx_ref, lam_ref,
                o_ref, xs_ref, a_ref, b_ref, h_ref):
    ti = pl.program_id(1)
    tc = xb_ref.shape[1]
    c = xb_ref.shape[2]

    @pl.when(ti == 0)
    def _():
        xs_ref[0:SUBLANES, :] = jnp.zeros((SUBLANES, c), F32)
        h_ref[...] = jnp.zeros_like(h_ref)

    xs_ref[SUBLANES:SUBLANES + tc, :] = xb_ref[0]
    xc = cb_ref[...]
    for w in range(CONV_W):
        xc = xc + cw_ref[w:w + 1, :] * xs_ref[pl.ds(SUBLANES - (CONV_W - 1) + w, tc), :]
    xs_ref[0:SUBLANES, :] = xs_ref[tc:tc + SUBLANES, :]

    r = _sigmoid(_dot(xc, wa_ref[...]) + ba_ref[...])
    i = _sigmoid(_dot(xc, wx_ref[...]) + bx_ref[...])
    nl = -lam_ref[...]
    softplus = jnp.maximum(nl, 0.0) + jnp.log(1.0 + jnp.exp(-jnp.abs(nl)))
    log_a = (-LRU_C * r) * softplus
    a = jnp.exp(log_a)
    mult = jnp.sqrt(1.0 - jnp.exp(2.0 * log_a))
    trow = ti * tc + lax.broadcasted_iota(jnp.int32, (tc, c), 0)
    mult = jnp.where(trow == 0, 1.0, mult)
    a_ref[...] = a
    b_ref[...] = mult * (i * xc)

    row = lax.broadcasted_iota(jnp.int32, (SUBLANES, c), 0)

    def body(g, carry):
        r0 = pl.multiple_of(g * SUBLANES, SUBLANES)
        av = a_ref[pl.ds(r0, SUBLANES), :]
        bv = b_ref[pl.ds(r0, SUBLANES), :]
        for d in (1, 2, 4):
            a_sh = pltpu.roll(av, d, 0)
            b_sh = pltpu.roll(bv, d, 0)
            m = row >= d
            bv = jnp.where(m, av * b_sh + bv, bv)
            av = jnp.where(m, av * a_sh, av)
        h = av * carry + bv
        b_ref[pl.ds(r0, SUBLANES), :] = h
        return h[SUBLANES - 1:SUBLANES, :]

    h_ref[...] = lax.fori_loop(0, tc // SUBLANES, body, h_ref[...])
    o_ref[0] = b_ref[...] * jax.nn.gelu(gb_ref[0])


def _lru(u3, conv_w, conv_b, wa_bd, ba, wx_bd, bx, lam, tc=256):
    b, t, _ = u3.shape
    c = LRU_WIDTH
    row = lambda v: v.reshape(1, c)
    full = lambda shp: pl.BlockSpec(shp, lambda bi, ti: (0, 0))
    return pl.pallas_call(
        _lru_kernel,
        grid=(b, t // tc),
        in_specs=[pl.BlockSpec((1, tc, c), lambda bi, ti: (bi, ti, 0)),
                  pl.BlockSpec((1, tc, c), lambda bi, ti: (bi, ti, 1)),
                  full((CONV_W, c)), full((1, c)), full((c, c)), full((1, c)),
                  full((c, c)), full((1, c)), full((1, c))],
        out_specs=pl.BlockSpec((1, tc, c), lambda bi, ti: (bi, ti, 0)),
        out_shape=jax.ShapeDtypeStruct((b, t, c), F32),
        scratch_shapes=[pltpu.VMEM((tc + SUBLANES, c), F32), pltpu.VMEM((tc, c), F32),
                        pltpu.VMEM((tc, c), F32), pltpu.VMEM((1, c), F32)],
        compiler_params=_cparams(("parallel", "arbitrary")),
        name="rg_lru",
    )(u3, u3, conv_w, row(conv_b), wa_bd, row(ba), wx_bd, row(bx), row(lam))


def _hgrn_kernel(q_ref, f_ref, v_ref, g_ref, lb_ref, ng_ref, tri_ref, ones_ref, o_ref,
                 st_ref, q_s, k_s, c_s, qe_s, kd_s, dl_s, o_s):
    ti = pl.program_id(2)
    tc = q_ref.shape[1]
    dk = HG_DK
    nh = q_ref.shape[2] // dk

    @pl.when(ti == 0)
    def _():
        st_ref[...] = jnp.zeros_like(st_ref)

    lbv = lb_ref[...]
    q = _silu(q_ref[0])
    f = lbv + (1.0 - lbv) * _sigmoid(f_ref[0])
    kk = 1.0 - f
    logf = jnp.log(f)
    parts = _split3(logf)
    tri = tri_ref[...]
    ones = ones_ref[...]
    cum = sum(jnp.dot(tri, p, preferred_element_type=F32) for p in parts)
    last = sum(jnp.dot(ones, p, preferred_element_type=F32) for p in parts)
    q_s[...] = q
    k_s[...] = kk
    c_s[...] = cum
    qe_s[...] = q * jnp.exp(cum)
    kd_s[...] = kk * jnp.exp(last - cum)
    dl_s[...] = jnp.exp(last)

    s_idx = lax.broadcasted_iota(jnp.int32, (HG_SUB, dk), 0)

    def chunk(ci, _):
        r0 = pl.multiple_of(ci * HG_SUB, HG_SUB)
        rows = pl.ds(r0, HG_SUB)
        for h in range(nh):
            cols = slice(h * dk, (h + 1) * dk)
            st = st_ref[h]
            vc = v_ref[0, rows, cols]
            qc = q_s[rows, cols]
            kc = k_s[rows, cols]
            cc = c_s[rows, cols]
            o = _dot_nt(qe_s[rows, cols], st)
            for t in range(HG_SUB):
                n = (t // SUBLANES + 1) * SUBLANES
                d = cc[t:t + 1, :] - cc[0:n, :]
                dec = jnp.exp(jnp.where(s_idx[0:n, :] <= t, d, NEG))
                w = (qc[t:t + 1, :] * kc[0:n, :]) * dec
                a_t = jnp.sum(w, axis=1, keepdims=True)
                o_t = jnp.sum(a_t * vc[0:n, :], axis=0, keepdims=True)
                o = o + jnp.where(s_idx == t, o_t, 0.0)
            o_s[rows, cols] = o
            st_ref[h] = st * dl_s[pl.ds(r0, 1), cols] + _dot_tn(vc, kd_s[rows, cols])
        return 0

    lax.fori_loop(0, tc // HG_SUB, chunk, 0, unroll=2)
    gate = _sigmoid(g_ref[0])
    for h in range(nh):
        cols = slice(h * dk, (h + 1) * dk)
        o = o_s[:, cols]
        o = o * lax.rsqrt(jnp.mean(o * o, axis=-1, keepdims=True) + 1e-6) * ng_ref[:, cols]
        o_ref[0, :, cols] = o * gate[:, cols]


def _hgrn2(u3, lb, norm_g, tc=256, nh=4):
    b, t, _ = u3.shape
    w = nh * HG_DK
    base = (2 * LRU_WIDTH) // w
    per = HG_FDIM // w
    idx = np.arange(tc)
    same = (idx[:, None] // HG_SUB) == (idx[None, :] // HG_SUB)
    tri = jnp.asarray(same & (idx[None, :] <= idx[:, None]), BF16)
    ones = jnp.asarray(same, BF16)

    def col(off):
        return pl.BlockSpec((1, tc, w), lambda bi, hi, ti: (bi, ti, base + off * per + hi))

    vec = pl.BlockSpec((1, w), lambda bi, hi, ti: (0, hi))
    cst = pl.BlockSpec((tc, tc), lambda bi, hi, ti: (0, 0))
    return pl.pallas_call(
        _hgrn_kernel,
        grid=(b, per, t // tc),
        in_specs=[col(0), col(1), col(2), col(3), vec, vec, cst, cst],
        out_specs=pl.BlockSpec((1, tc, w), lambda bi, hi, ti: (bi, ti, hi)),
        out_shape=jax.ShapeDtypeStruct((b, t, HG_VDIM), F32),
        scratch_shapes=[pltpu.VMEM((nh, HG_DK, HG_DK), F32)] + [pltpu.VMEM((tc, w), F32)] * 7,
        compiler_params=_cparams(("parallel", "parallel", "arbitrary")),
        name="hgrn2",
    )(u3, u3, u3, u3, lb.reshape(1, HG_FDIM), norm_g.reshape(1, HG_VDIM), tri, ones)


def _nsa_cmp_kv_kernel(zk_ref, zv_ref, pek_ref, pev_ref, wk1_ref, wk2_ref, wv1_ref, wv2_ref,
                       ok_ref, ov_ref):
    half = zk_ref.shape[3]
    nrow = zk_ref.shape[2]

    def compress(z, pe_ref, w1_ref, w2_ref):
        lo = _dot(z + pe_ref[0:1, :], w1_ref[0:half, :])
        hi = _dot(z + pe_ref[1:2, :], w1_ref[half:2 * half, :])
        pre = lo + pltpu.roll(hi, nrow - 1, 0)
        return _dot(jax.nn.gelu(pre), w2_ref[...])

    ok_ref[0, 0] = compress(zk_ref[0, 0], pek_ref, wk1_ref, wk2_ref)
    ov_ref[0, 0] = compress(zv_ref[0, 0], pev_ref, wv1_ref, wv2_ref)


def _nsa_compress(kc, vc, pe_k, pe_v, wk1, wk2, wv1, wv2):
    b, g, t, dh = kc.shape
    nrow = t // CMP_STRIDE
    half = CMP_STRIDE * dh
    zk = kc.reshape(b, g, nrow, half)
    zv = vc.reshape(b, g, nrow, half)
    zspec = pl.BlockSpec((1, 1, nrow, half), lambda bi, gi: (bi, gi, 0, 0))
    full = lambda shp: pl.BlockSpec(shp, lambda bi, gi: (0, 0))
    ospec = pl.BlockSpec((1, 1, nrow, LANES), lambda bi, gi: (bi, gi, 0, 0))
    oshape = jax.ShapeDtypeStruct((b, g, nrow, LANES), F32)
    pad = lambda w: jnp.pad(w, ((0, 0), (0, LANES - dh))).astype(BF16)
    return pl.pallas_call(
        _nsa_cmp_kv_kernel,
        grid=(b, g),
        in_specs=[zspec, zspec, full((2, half)), full((2, half)),
                  full((2 * half, CMP_HIDDEN)), full((CMP_HIDDEN, LANES)),
                  full((2 * half, CMP_HIDDEN)), full((CMP_HIDDEN, LANES))],
        out_specs=[ospec, ospec],
        out_shape=[oshape, oshape],
        compiler_params=_cparams(("parallel", "parallel")),
        name="nsa_compress",
    )(zk, zv, pe_k.reshape(2, half), pe_v.reshape(2, half),
      wk1.astype(BF16), pad(wk2), wv1.astype(BF16), pad(wv2))


def _nsa_sel_kernel(slope_ref, q_ref, kc_ref, vc_ref, ov_ref, o_ref, mb_ref):
    gi = pl.program_id(0) % NSA_GROUPS
    qi = pl.program_id(1)
    tq = q_ref.shape[1]
    hpg = q_ref.shape[2] // LANES
    nc = kc_ref.shape[2]
    ns = ov_ref.shape[0]

    q = jnp.concatenate([q_ref[0, :, i * LANES:(i + 1) * LANES] for i in range(hpg)], axis=0)
    khi, klo = _split2(kc_ref[0, 0])
    s = _dot_nt(q, khi) + _dot_nt(q, klo)
    tpos = qi * tq + lax.broadcasted_iota(jnp.int32, (tq, nc), 0)
    cend = lax.broadcasted_iota(jnp.int32, (tq, nc), 1) * CMP_STRIDE + (CMP_LEN - 1)
    d_c = (tpos - cend).astype(F32)
    valid = d_c >= 0.0
    ps = []
    for i in range(hpg):
        si = s[i * tq:(i + 1) * tq, :] - slope_ref[gi * hpg + i] * d_c
        si = jnp.where(valid, si, NEG)
        m = jnp.max(si, axis=-1, keepdims=True)
        p = jnp.where(valid, jnp.exp(si - m), 0.0)
        ps.append(p / jnp.maximum(jnp.sum(p, axis=-1, keepdims=True), 1e-30))
    oc = _dot(jnp.concatenate(ps, axis=0), vc_ref[0, 0])
    o_ref[0] = jnp.concatenate([oc[i * tq:(i + 1) * tq, :] for i in range(hpg)], axis=1)

    psum = ps[0]
    for i in range(1, hpg):
        psum = psum + ps[i]
    phi, plo = _split2(psum)
    ovl = ov_ref[...]
    imp = _dot_nt(ovl, phi) + _dot_nt(ovl, plo)

    j = lax.broadcasted_iota(jnp.int32, (ns, tq), 0)
    own = (qi * tq + lax.broadcasted_iota(jnp.int32, (ns, tq), 1)) // SLC_LEN
    forced = (j == 0) | (j == own) | (j == own - 1)
    imp = jnp.where(j <= own, imp + FORCE_BONUS * forced.astype(F32), NEG)
    rank = jnp.zeros((ns, tq), F32)
    for jp in range(ns):
        row = imp[jp:jp + 1, :]
        beats = (row > imp) | ((row == imp) & (j > jp))
        rank = rank + beats.astype(F32)
    bias = jnp.where(rank < float(min(SLC_TOPN, ns)), 0.0, MASKED)
    place = (lax.broadcasted_iota(jnp.int32, (ns, LANES), 1)
             == lax.broadcasted_iota(jnp.int32, (ns, LANES), 0) + MASK_COL0)
    mb_ref[0] = _dot_tn(bias, place.astype(F32)).astype(BF16)


def _nsa_select(slopes, u16, k_cmp, v_cmp, tq=256):
    b, t, _ = u16.shape
    g = NSA_GROUPS
    hpg = NSA_HPG
    nc = k_cmp.shape[2]
    ns = t // SLC_LEN
    ci = np.arange(nc)[:, None]
    cj = np.arange(ns)[None, :]
    ovl = ((CMP_STRIDE * ci < SLC_LEN * (cj + 1)) & (CMP_STRIDE * ci + CMP_LEN > SLC_LEN * cj))
    ovl = ovl & (ci < (t - CMP_LEN) // CMP_STRIDE + 1)
    grid_spec = pltpu.PrefetchScalarGridSpec(
        num_scalar_prefetch=1,
        grid=(b * g, t // tq),
        in_specs=[pl.BlockSpec((1, tq, hpg * LANES), lambda bg, qi, s: (bg // g, qi, U16_QA // (hpg * LANES) + bg % g)),
                  pl.BlockSpec((1, 1, nc, LANES), lambda bg, qi, s: (bg // g, bg % g, 0, 0)),
                  pl.BlockSpec((1, 1, nc, LANES), lambda bg, qi, s: (bg // g, bg % g, 0, 0)),
                  pl.BlockSpec((ns, nc), lambda bg, qi, s: (0, 0))],
        out_specs=[pl.BlockSpec((1, tq, hpg * LANES), lambda bg, qi, s: (bg // g, qi, bg % g)),
                   pl.BlockSpec((1, tq, LANES), lambda bg, qi, s: (bg // g, qi, bg % g))],
    )
    return pl.pallas_call(
        _nsa_sel_kernel,
        grid_spec=grid_spec,
        out_shape=[jax.ShapeDtypeStruct((b, t, g * hpg * LANES), F32),
                   jax.ShapeDtypeStruct((b, t, g * LANES), BF16)],
        compiler_params=_cparams(("parallel", "parallel")),
        name="nsa_select",
    )(slopes, u16, k_cmp, v_cmp, jnp.asarray(ovl.T, BF16))


POS_SPLIT = 64
MASKED = -2e30


MASK_COL0 = HEAD_DIM + 4
KEY_STEP = 512


def _flash_kernel(q_ref, k_ref, v_ref, *rest, mode, nkv, tq):
    o_ref = rest[-1]
    hpg = q_ref.shape[2] // (LANES * nkv)
    t_all = k_ref.shape[1]
    nt = q_ref.shape[1] // tq
    whole = nt * tq == t_all
    step = MOBA_BLOCK if mode == "moba" else KEY_STEP

    def tile(ti, q0):
        rows = slice(ti * tq, (ti + 1) * tq)
        qs = []
        for j in range(nkv):
            kv_cols = slice(j * LANES, (j + 1) * LANES)
            heads = [q_ref[0, rows, (j * hpg + i) * LANES:(j * hpg + i + 1) * LANES] for i in range(hpg)]
            if mode == "slc":
                heads = [h + rest[0][0, rows, kv_cols] for h in heads]
            q = heads[0] if hpg == 1 else jnp.concatenate(heads, axis=0)
            if mode == "moba":
                kmean = rest[0][0, :, kv_cols]
                nb = kmean.shape[0]
                mhi, mlo = _split2(kmean)
                gate = _dot_nt(mhi, q) + _dot_nt(mlo, q)
                jb = lax.broadcasted_iota(jnp.int32, (nb, tq), 0)
                ob = q0 // MOBA_BLOCK
                gate = jnp.where(jb < ob, gate, NEG)
                rank = jnp.zeros((nb, tq), F32)
                for jp in range(nb):
                    row = gate[jp:jp + 1, :]
                    beats = (row > gate) | ((row == gate) & (jb > jp))
                    rank = rank + beats.astype(F32)
                n_sel = float(min(MOBA_TOPK, nb - 1))
                bsel = ((rank < n_sel) & (jb < ob)) | (jb == ob)
                bias = jnp.where(bsel, 0.0, MASKED)
                place = (lax.broadcasted_iota(jnp.int32, (nb, LANES), 1)
                         == lax.broadcasted_iota(jnp.int32, (nb, LANES), 0) + MASK_COL0)
                q = q + _dot_tn(bias, place.astype(F32)).astype(BF16)
            qs.append(q)

        def attend(kstart, klen, diag, off):
            rel = lax.broadcasted_iota(jnp.int32, (tq, diag), 0) - lax.broadcasted_iota(jnp.int32, (tq, diag), 1)
            valid = rel >= off
            if mode == "win":
                valid = valid & (rel < off + WIN)
            if hpg > 1:
                valid = jnp.concatenate([valid] * hpg, axis=0)
            for j in range(nkv):
                kv_cols = slice(j * LANES, (j + 1) * LANES)
                s = _dot_nt(qs[j], k_ref[0, pl.ds(kstart, klen), kv_cols])
                sd = jnp.where(valid, s[:, klen - diag:], MASKED)
                s = sd if diag == klen else jnp.concatenate([s[:, :klen - diag], sd], axis=1)
                m = jnp.maximum(jnp.max(s, axis=-1, keepdims=True), NEG)
                p = jnp.exp(s - m)
                l = jnp.sum(p, axis=-1, keepdims=True)
                o = _dot(p, v_ref[0, pl.ds(kstart, klen), kv_cols]) / jnp.maximum(l, 1e-30)
                for i in range(hpg):
                    o_ref[0, rows, (j * hpg + i) * LANES:(j * hpg + i + 1) * LANES] = o[i * tq:(i + 1) * tq, :]

        if mode == "win":
            kstart = max(q0 - WIN, 0) if whole else pl.multiple_of(jnp.maximum(q0 - WIN, 0), LANES)
            attend(kstart, WIN + tq, WIN + tq, kstart - q0)
        elif whole:
            v = (q0 + tq - 1) // step
            attend(0, step * (v + 1), step, step * v - q0)
        else:
            var = (q0 + tq - 1) // step
            for v in range(t_all // step):
                @pl.when(var == v)
                def _(v=v):
                    attend(0, step * (v + 1), step, step * v - q0)

    for ti in range(nt):
        tile(ti, ti * tq if whole else (pl.program_id(1) * nt + ti) * tq)


def _flash(mode, u16, extra=None, tq=128, nkv=1, tiles_per_step=1):
    b, t, _ = u16.shape
    q_off, k_off, v_off, n_kvh, hpg = {
        "slc": (U16_QA, U16_KS, U16_VS, NSA_GROUPS, NSA_HPG),
        "win": (U16_QA, U16_KW, U16_VW, NSA_GROUPS, NSA_HPG),
        "moba": (U16_QM, U16_KM, U16_VM, MOBA_HEADS, 1)}[mode]
    assert mode != "moba" or tq == MOBA_BLOCK
    steps = n_kvh // nkv
    qw = nkv * hpg * LANES
    kw = nkv * LANES
    tb = tq * tiles_per_step
    in_specs = [pl.BlockSpec((1, tb, qw), lambda bh, qi: (bh // steps, qi, q_off // qw + bh % steps)),
                pl.BlockSpec((1, t, kw), lambda bh, qi: (bh // steps, 0, k_off // kw + bh % steps)),
                pl.BlockSpec((1, t, kw), lambda bh, qi: (bh // steps, 0, v_off // kw + bh % steps))]
    args = [u16, u16, u16]
    if mode == "slc":
        in_specs.append(pl.BlockSpec((1, tb, kw), lambda bh, qi: (bh // steps, qi, bh % steps)))
        args.append(extra)
    elif mode == "moba":
        in_specs.append(pl.BlockSpec((1, t // MOBA_BLOCK, kw), lambda bh, qi: (bh // steps, 0, bh % steps)))
        args.append(extra)
    return pl.pallas_call(
        functools.partial(_flash_kernel, mode=mode, nkv=nkv, tq=tq),
        grid=(b * steps, t // tb),
        in_specs=in_specs,
        out_specs=pl.BlockSpec((1, tb, qw), lambda bh, qi: (bh // steps, qi, bh % steps)),
        out_shape=jax.ShapeDtypeStruct((b, t, n_kvh * hpg * LANES), F32),
        compiler_params=_cparams(("parallel", "parallel")),
        name="flash_" + mode,
    )(*args)


def _router_kernel(x_ref, whi_ref, wlo_ref, b_ref, triu_ref, meta_ref, cnt_ref, carry_ref):
    step = pl.program_id(0)
    tm = x_ref.shape[0]
    ne = whi_ref.shape[0]
    gsz = ne // N_GROUPS

    @pl.when(step == 0)
    def _():
        carry_ref[...] = jnp.zeros_like(carry_ref)

    xhi, xlo = _split2(x_ref[...])
    whi = whi_ref[...]
    logits = _dot_nt(whi, xhi) + _dot_nt(wlo_ref[...], xhi) + _dot_nt(whi, xlo)
    scores = _sigmoid(logits)
    biased = scores + b_ref[:, 0:1]

    v3 = biased.reshape(N_GROUPS, gsz, tm)
    i3 = lax.broadcasted_iota(jnp.int32, (N_GROUPS, gsz, tm), 1).astype(F32)
    m1 = jnp.max(v3, axis=1, keepdims=True)
    idx1 = jnp.min(jnp.where(v3 == m1, i3, float(gsz)), axis=1, keepdims=True)
    m2 = jnp.max(jnp.where(i3 == idx1, NEG_INF, v3), axis=1, keepdims=True)
    gs = (m1 + m2).reshape(N_GROUPS, tm)
    gi = lax.broadcasted_iota(jnp.int32, (N_GROUPS, tm), 0).astype(F32)
    gmask = jnp.zeros((N_GROUPS, tm), F32)
    for _ in range(TOPK_GROUPS):
        m = jnp.max(gs, axis=0, keepdims=True)
        pick = jnp.min(jnp.where(gs == m, gi, float(N_GROUPS)), axis=0, keepdims=True)
        hit = gi == pick
        gmask = jnp.where(hit, 1.0, gmask)
        gs = jnp.where(hit, NEG_INF, gs)
    emask = jnp.broadcast_to(gmask.reshape(N_GROUPS, 1, tm), (N_GROUPS, gsz, tm)).reshape(ne, tm)
    cur = jnp.where(emask > 0.5, biased, NEG)

    ei = lax.broadcasted_iota(jnp.int32, (ne, tm), 0).astype(F32)
    selm = jnp.zeros((ne, tm), F32)
    idxs, svals = [], []
    for _ in range(TOP_K):
        m = jnp.max(cur, axis=0, keepdims=True)
        idx = jnp.min(jnp.where(cur == m, ei, float(ne)), axis=0, keepdims=True)
        hit = ei == idx
        svals.append(jnp.sum(jnp.where(hit, scores, 0.0), axis=0, keepdims=True))
        idxs.append(idx)
        cur = jnp.where(hit, NEG_INF, cur)
        selm = jnp.where(hit, 1.0, selm)

    carry = carry_ref[:, 0:1]
    rank_full = jnp.dot(selm.astype(BF16), triu_ref[...], preferred_element_type=F32) + carry
    ssum = svals[0]
    for k in range(1, TOP_K):
        ssum = ssum + svals[k]
    for k in range(TOP_K):
        meta_ref[k:k + 1, :] = idxs[k]
        meta_ref[TOP_K + k:TOP_K + k + 1, :] = svals[k] / ssum * ROUTED_SCALE
        meta_ref[2 * TOP_K + k:2 * TOP_K + k + 1, :] = jnp.sum(
            jnp.where(ei == idxs[k], rank_full, 0.0), axis=0, keepdims=True)
    new_carry = carry + jnp.sum(selm, axis=1, keepdims=True)
    carry_ref[...] = jnp.broadcast_to(new_carry, carry_ref.shape)
    cnt_ref[...] = jnp.broadcast_to(new_carry, cnt_ref.shape)


def _router(x, w_router, b_router, tm=256):
    n, d = x.shape
    ne = w_router.shape[1]
    wt = w_router.T
    whi = wt.astype(BF16)
    wlo = (wt - whi.astype(F32)).astype(BF16)
    triu = jnp.asarray(np.arange(tm)[:, None] < np.arange(tm)[None, :], BF16)
    meta, cnt = pl.pallas_call(
        _router_kernel,
        grid=(n // tm,),
        in_specs=[pl.BlockSpec((tm, d), lambda i: (i, 0)),
                  pl.BlockSpec((ne, d), lambda i: (0, 0)),
                  pl.BlockSpec((ne, d), lambda i: (0, 0)),
                  pl.BlockSpec((ne, LANES), lambda i: (0, 0)),
                  pl.BlockSpec((tm, tm), lambda i: (0, 0))],
        out_specs=[pl.BlockSpec((3 * TOP_K, tm), lambda i: (0, i)),
                   pl.BlockSpec((ne, LANES), lambda i: (0, 0))],
        out_shape=[jax.ShapeDtypeStruct((3 * TOP_K, n), F32),
                   jax.ShapeDtypeStruct((ne, LANES), F32)],
        scratch_shapes=[pltpu.VMEM((ne, LANES), F32)],
        compiler_params=_cparams(("arbitrary",)),
        name="moe_router",
    )(x, whi, wlo, jnp.broadcast_to(b_router.reshape(ne, 1), (ne, LANES)), triu)
    return meta, cnt[:, 0]


def _expert_kernel(be_ref, first_ref, nxt_ref, slot_ref, nu_ref, xs_ref, wg_hbm, wu_hbm, wd_hbm, o_ref,
                   wg_buf, wu_buf, wd_buf, wgu16, wd16, sem, *, layer):
    bi = pl.program_id(0)

    def weight_copies(e, s):
        return [pltpu.make_async_copy(hbm.at[layer, e], buf.at[s], sem.at[s])
                for hbm, buf in ((wg_hbm, wg_buf), (wu_hbm, wu_buf), (wd_hbm, wd_buf))]

    @pl.when(bi < nu_ref[0])
    def _():
        s = slot_ref[bi]

        @pl.when(first_ref[bi] == 1)
        def _():
            @pl.when(bi == 0)
            def _():
                for c in weight_copies(be_ref[0], s):
                    c.start()

            @pl.when(nxt_ref[bi] >= 0)
            def _():
                for c in weight_copies(nxt_ref[bi], 1 - s):
                    c.start()

            for c in weight_copies(be_ref[bi], s):
                c.wait()
            hd = wd16.shape[0]
            wgu16[:, 0:hd] = wg_buf[s].astype(BF16)
            wgu16[:, hd:2 * hd] = wu_buf[s].astype(BF16)
            wd16[...] = wd_buf[s].astype(BF16)

        xa, xb = _unpack_bf16_pair(xs_ref[...])
        half = xa.shape[1]

        def xw(w_ref):
            return (jnp.dot(xa, w_ref[0:half, :], preferred_element_type=F32)
                    + jnp.dot(xb, w_ref[half:2 * half, :], preferred_element_type=F32))

        gu = xw(wgu16)
        hd = wd16.shape[0]
        h = _silu(gu[:, 0:hd]) * gu[:, hd:2 * hd]
        o_ref[...] = _pack_bf16_pair(jnp.dot(h.astype(BF16), wd16[...], preferred_element_type=F32))

    @pl.when(bi >= nu_ref[0])
    def _():
        o_ref[...] = jnp.zeros_like(o_ref)


def _expert_ffn(blk_e, n_used, xs, layer, w_gate, w_up, w_down):
    n_rows = xs.shape[0]
    d = w_gate.shape[2]
    hdim = w_gate.shape[3]
    n_blk = n_rows // EXPERT_ROWS
    idx = jnp.arange(n_blk, dtype=jnp.int32)
    first = jnp.concatenate([jnp.ones((1,), bool), blk_e[1:] != blk_e[:-1]]) & (idx < n_used[0])
    slot = ((jnp.cumsum(first.astype(jnp.int32)) - 1) % 2).astype(jnp.int32)
    nxt_first = lax.cummin(jnp.where(first, idx, n_blk), reverse=True)
    nxt_first = jnp.concatenate([nxt_first[1:], jnp.full((1,), n_blk, jnp.int32)])
    nxt_e = jnp.where(nxt_first < n_blk, blk_e[jnp.minimum(nxt_first, n_blk - 1)], -1).astype(jnp.int32)
    grid_spec = pltpu.PrefetchScalarGridSpec(
        num_scalar_prefetch=5,
        grid=(n_blk,),
        in_specs=[pl.BlockSpec((EXPERT_ROWS, d // 2),
                               lambda i, be, fi, nx, sl, nu: (jnp.minimum(i, jnp.maximum(nu[0] - 1, 0)), 0)),
                  pl.BlockSpec(memory_space=pl.ANY),
                  pl.BlockSpec(memory_space=pl.ANY),
                  pl.BlockSpec(memory_space=pl.ANY)],
        out_specs=pl.BlockSpec((EXPERT_ROWS, d // 2), lambda i, be, fi, nx, sl, nu: (i, 0)),
        scratch_shapes=[pltpu.VMEM((2, d, hdim), F32), pltpu.VMEM((2, d, hdim), F32), pltpu.VMEM((2, hdim, d), F32),
                        pltpu.VMEM((d, 2 * hdim), BF16), pltpu.VMEM((hdim, d), BF16),
                        pltpu.SemaphoreType.DMA((2,))],
    )
    return pl.pallas_call(
        functools.partial(_expert_kernel, layer=layer),
        grid_spec=grid_spec,
        out_shape=jax.ShapeDtypeStruct((n_rows, d // 2), jnp.uint32),
        compiler_params=_cparams(("arbitrary",)),
        name="moe_experts",
    )(blk_e, first.astype(jnp.int32), nxt_e, slot, n_used, xs, w_gate, w_up, w_down)


def _dest_kernel(meta_ref, pstart_ref, dest_ref):
    ne = pstart_ref.shape[0]
    tm = meta_ref.shape[1]
    ei = lax.broadcasted_iota(jnp.int32, (ne, tm), 0).astype(F32)
    ps = pstart_ref[:, 0:1]
    for k in range(TOP_K):
        base = jnp.sum(jnp.where(ei == meta_ref[k:k + 1, :], ps, 0.0), axis=0, keepdims=True)
        dest_ref[k:k + 1, :] = (base + meta_ref[2 * TOP_K + k:2 * TOP_K + k + 1, :]).astype(jnp.int32)


def _dest_rows(meta, pstart, tm=512):
    n = meta.shape[1]
    ne = pstart.shape[0]
    return pl.pallas_call(
        _dest_kernel,
        grid=(n // tm,),
        in_specs=[pl.BlockSpec((3 * TOP_K, tm), lambda i: (0, i)),
                  pl.BlockSpec((ne, LANES), lambda i: (0, 0))],
        out_specs=pl.BlockSpec((TOP_K, tm), lambda i: (0, i)),
        out_shape=jax.ShapeDtypeStruct((TOP_K, n), jnp.int32),
        compiler_params=_cparams(("parallel",)),
        name="moe_dest",
    )(meta, jnp.broadcast_to(pstart.astype(F32).reshape(ne, 1), (ne, LANES)))


def _pack_bf16_pair(x):
    half = x.shape[1] // 2
    hi = pltpu.bitcast(x[:, 0:half].astype(BF16).astype(F32), jnp.uint32)
    lo = pltpu.bitcast(x[:, half:2 * half].astype(BF16).astype(F32), jnp.uint32)
    return hi | (lo >> 16)


def _unpack_bf16_pair(p):
    hi = pltpu.bitcast(p & jnp.uint32(0xFFFF0000), F32).astype(BF16)
    lo = pltpu.bitcast(p << 16, F32).astype(BF16)
    return hi, lo


def _dispatch_kernel(dest_ref, x_ref, xs_in_ref, xs_ref, pk_ref, sem):
    del xs_in_ref
    tm = x_ref.shape[0]
    pk_ref[...] = _pack_bf16_pair(x_ref[...])

    def row_copy(t, k):
        return pltpu.make_async_copy(pk_ref.at[pl.ds(t, 1), :], xs_ref.at[pl.ds(dest_ref[t * TOP_K + k], 1), :], sem)

    def body(t, _):
        for k in range(TOP_K):
            row_copy(t, k).start(priority=k % 2)
        return 0

    lax.fori_loop(0, tm, body, 0, unroll=4)
    done = xs_ref.at[pl.ds(0, TOP_K * tm), :]
    pltpu.make_async_copy(done, done, sem).wait()


def _dispatch(dest, x, n_rows, tm=256):
    n, d = x.shape
    return pl.pallas_call(
        _dispatch_kernel,
        grid=(n // tm,),
        in_specs=[pl.BlockSpec((TOP_K * tm,), lambda i: (i,), memory_space=pltpu.SMEM),
                  pl.BlockSpec((tm, d), lambda i: (i, 0)),
                  pl.BlockSpec(memory_space=pl.ANY)],
        out_specs=pl.BlockSpec(memory_space=pl.ANY),
        out_shape=jax.ShapeDtypeStruct((n_rows, d // 2), jnp.uint32),
        scratch_shapes=[pltpu.VMEM((tm, d // 2), jnp.uint32), pltpu.SemaphoreType.DMA(())],
        input_output_aliases={2: 0},
        compiler_params=_cparams(("arbitrary",)),
        name="moe_dispatch",
    )(dest, x, jnp.zeros((n_rows, d // 2), jnp.uint32))


def _combine_ln_kernel(dest_ref, dest_next_ref, w_ref, x_ref, ys_ref, wg_ref, wu_ref, wd_ref, g_ref, b_ref,
                       o_ref, buf, sem):
    step = pl.program_id(0)
    tm = x_ref.shape[0]
    slot = step % 2

    def gather_rows(idx_ref, to_slot):
        def body(t, _):
            for k in range(TOP_K):
                pltpu.make_async_copy(ys_ref.at[pl.ds(idx_ref[t * TOP_K + k], 1), :],
                                      buf.at[to_slot, k, pl.ds(t, 1), :], sem.at[to_slot]).start(priority=k % 2)
            return 0

        lax.fori_loop(0, tm, body, 0, unroll=4)

    @pl.when(step == 0)
    def _():
        gather_rows(dest_ref, 0)

    @pl.when(step + 1 < pl.num_programs(0))
    def _():
        gather_rows(dest_next_ref, 1 - slot)

    x = x_ref[...]
    xb = x.astype(BF16)
    h = _silu(jnp.dot(xb, wg_ref[...], preferred_element_type=F32)) * jnp.dot(
        xb, wu_ref[...], preferred_element_type=F32)
    ff = jnp.dot(h.astype(BF16), wd_ref[...], preferred_element_type=F32)
    pltpu.make_async_copy(buf.at[slot], buf.at[slot], sem.at[slot]).wait()
    routed_a = jnp.zeros((tm, buf.shape[3]), F32)
    routed_b = jnp.zeros((tm, buf.shape[3]), F32)
    for k in range(TOP_K):
        ya, yb = _unpack_bf16_pair(buf[slot, k])
        routed_a = routed_a + ya.astype(F32) * w_ref[:, k:k + 1]
        routed_b = routed_b + yb.astype(F32) * w_ref[:, k:k + 1]
    ff = ff + jnp.concatenate([routed_a, routed_b], axis=1)
    o_ref[...] = _layer_norm(DN_ALPHA * x + ff, g_ref[...], b_ref[...])


def _combine_ln(dest, wts, x, ys, ws_gate, ws_up, ws_down, g, b, tm=256):
    n, d = x.shape
    hdim = ws_gate.shape[1]
    row = pl.BlockSpec((tm, d), lambda i: (i, 0))
    last = n // tm - 1
    return pl.pallas_call(
        _combine_ln_kernel,
        grid=(n // tm,),
        in_specs=[pl.BlockSpec((TOP_K * tm,), lambda i: (i,), memory_space=pltpu.SMEM),
                  pl.BlockSpec((TOP_K * tm,), lambda i: (jnp.minimum(i + 1, last),), memory_space=pltpu.SMEM),
                  pl.BlockSpec((tm, TOP_K), lambda i: (i, 0)),
                  row,
                  pl.BlockSpec(memory_space=pl.ANY),
                  pl.BlockSpec((d, hdim), lambda i: (0, 0)),
                  pl.BlockSpec((d, hdim), lambda i: (0, 0)),
                  pl.BlockSpec((hdim, d), lambda i: (0, 0)),
                  pl.BlockSpec((1, d), lambda i: (0, 0)),
                  pl.BlockSpec((1, d), lambda i: (0, 0))],
        out_specs=row,
        out_shape=jax.ShapeDtypeStruct((n, d), F32),
        scratch_shapes=[pltpu.VMEM((2, TOP_K, tm, d // 2), jnp.uint32), pltpu.SemaphoreType.DMA((2,))],
        compiler_params=_cparams(("arbitrary",)),
        name="moe_combine_ln",
    )(dest, dest, wts, x, ys, ws_gate.astype(BF16), ws_up.astype(BF16), ws_down.astype(BF16),
      g.reshape(1, d), b.reshape(1, d))


def _moe_ln(x, layer, w_router, b_router, w_gate, w_up, w_down, ws_gate, ws_up, ws_down, g, b):
    n, d = x.shape
    ne = w_router.shape[1]
    meta, counts = _router(x, w_router, b_router)
    counts = counts.astype(jnp.int32)
    padded = (counts + EXPERT_ROWS - 1) // EXPERT_ROWS * EXPERT_ROWS
    pend = jnp.cumsum(padded)
    dest = _dest_rows(meta, pend - padded).T.reshape(-1)
    n_blk = (n * TOP_K) // EXPERT_ROWS + ne
    blk_row0 = jnp.arange(n_blk, dtype=jnp.int32) * EXPERT_ROWS
    blk_e = jnp.minimum(jnp.sum((pend[None, :] <= blk_row0[:, None]).astype(jnp.int32), axis=1), ne - 1)
    n_used = (pend[-1] // EXPERT_ROWS).astype(jnp.int32).reshape(1)
    xs = _dispatch(dest, x, n_blk * EXPERT_ROWS)
    ys = _expert_ffn(blk_e, n_used, xs, layer, w_gate, w_up, w_down)
    return _combine_ln(dest, meta[TOP_K:2 * TOP_K].T, x, ys, ws_gate, ws_up, ws_down, g, b)


def _block_diag(w):
    nb, bs, _ = w.shape
    eye = jnp.eye(nb, dtype=w.dtype)
    return (eye[:, None, :, None] * w[:, :, None, :]).reshape(nb * bs, nb * bs)


def _even_mixer_ln(x, w_in, conv_w, conv_b, wa, ba, wx, bx, lam, lb, norm_g, w_out, g, b):
    bsz, t, d = x.shape
    xf = x.reshape(bsz * t, d)
    u = _proj(xf, w_in.astype(BF16)).reshape(bsz, t, -1)
    ya = _lru(u, conv_w, conv_b, _block_diag(wa).astype(BF16), ba, _block_diag(wx).astype(BF16), bx, lam)
    yb = _hgrn2(u, lb, norm_g)
    return _proj_ln(ya.reshape(bsz * t, -1), yb.reshape(bsz * t, -1), w_out, xf, g, b).reshape(bsz, t, d)


def _odd_mixer_ln(x, w_in, pe_k, pe_v, wk1, wk2, wv1, wv2, w_out, g, b):
    bsz, t, d = x.shape
    xf = x.reshape(bsz * t, d)
    u16, u32, kmean = _odd_proj(xf, w_in, t)
    u16 = u16.reshape(bsz, t, U16_W)
    kmean = kmean.reshape(bsz, t // MOBA_BLOCK, -1)

    def grp(off):
        z = u32[:, off:off + NSA_KV].reshape(bsz, t, NSA_GROUPS, HEAD_DIM)
        return z.transpose(0, 2, 1, 3)

    n_all = NSA_HEADS + MOBA_HEADS
    s_nsa = jnp.asarray((2.0 ** (-8.0 * np.arange(1, n_all + 1) / n_all))[0::2], F32)
    k_cmp, v_cmp = _nsa_compress(grp(0), grp(NSA_KV), pe_k, pe_v, wk1, wk2, wv1, wv2)
    o_c, mask_bias = _nsa_select(s_nsa, u16, k_cmp, v_cmp)
    o_s = _flash("slc", u16, mask_bias, tq=128)
    o_w = _flash("win", u16, tq=128)
    o_m = _flash("moba", u16, kmean, tq=MOBA_BLOCK, nkv=2, tiles_per_step=t // MOBA_BLOCK)
    flat = lambda z: z.reshape(bsz * t, -1)
    return _odd_out_ln(flat(o_c), flat(o_s), flat(o_w), flat(o_m), u32, w_out, xf, g, b).reshape(bsz, t, d)


def kernel(x, even_w_in, lru_conv_w, lru_conv_b, lru_wa, lru_ba, lru_wx, lru_bx, lru_lambda,
           hg_lower_bound, hg_norm_g, even_w_out, odd_w_in, nsa_pe_k, nsa_pe_v, nsa_wk1, nsa_wk2,
           nsa_wv1, nsa_wv2, odd_w_out, ln_g, ln_b, w_router, b_router, w_gate, w_up, w_down,
           ws_gate, ws_up, ws_down):
    bsz, t, d = x.shape
    lb_all = jnp.cumsum(jax.nn.softmax(hg_lower_bound.astype(F32), axis=0), axis=0)
    for layer in range(DEPTH):
        li = layer // 2
        if layer % 2 == 0:
            x = _even_mixer_ln(x, even_w_in[li], lru_conv_w[li], lru_conv_b[li], lru_wa[li], lru_ba[li],
                               lru_wx[li], lru_bx[li], lru_lambda[li], lb_all[layer], hg_norm_g[li],
                               even_w_out[li], ln_g[layer, 0], ln_b[layer, 0])
        else:
            x = _odd_mixer_ln(x, odd_w_in[li], nsa_pe_k[li], nsa_pe_v[li], nsa_wk1[li], nsa_wk2[li],
                              nsa_wv1[li], nsa_wv2[li], odd_w_out[li], ln_g[layer, 0], ln_b[layer, 0])
        x = _moe_ln(x.reshape(bsz * t, d), layer, w_router[layer], b_router[layer], w_gate, w_up,
                    w_down, ws_gate[layer], ws_up[layer], ws_down[layer],
                    ln_g[layer, 1], ln_b[layer, 1]).reshape(bsz, t, d)
    return x
```

```python
import functools

import numpy as np
import jax
import jax.numpy as jnp
from jax import lax
from jax.experimental import pallas as pl
from jax.experimental.pallas import tpu as pltpu

F32 = jnp.float32
BF16 = jnp.bfloat16

D_MODEL = 1024
DEPTH = 2
LRU_WIDTH = 512
LRU_BLOCKS = 8
LRU_BLOCK = LRU_WIDTH // LRU_BLOCKS
CONV_W = 4
LRU_C = 8.0
HG_HEADS = 4
HG_DK = 128
HG_FDIM = HG_HEADS * HG_DK
HG_VDIM = HG_HEADS * HG_DK
HEAD_DIM = 64
NSA_HEADS = 8
NSA_GROUPS = 2
NSA_HPG = NSA_HEADS // NSA_GROUPS
NSA_KV = NSA_GROUPS * HEAD_DIM
CMP_LEN = 32
CMP_STRIDE = 16
CMP_HIDDEN = 128
SLC_LEN = 64
SLC_TOPN = 16
WIN = 512
FORCE_BONUS = 1e4
MOBA_HEADS = 8
MOBA_BLOCK = 256
MOBA_TOPK = 3
N_EXPERTS = 256
TOP_K = 8
N_GROUPS = 8
TOPK_GROUPS = 4
EXPERT_HIDDEN = 256
ROUTED_SCALE = 2.5
DN_ALPHA = (2 * DEPTH) ** 0.25
LN_EPS = 1e-5
NEG = -1e30
NEG_INF = float("-inf")

LANES = 128
SUBLANES = 8
VMEM_LIMIT = 48 * 1024 * 1024

HG_SUB = 16
EXPERT_ROWS = 256


def _cparams(sem):
    return pltpu.CompilerParams(dimension_semantics=sem, vmem_limit_bytes=VMEM_LIMIT)


def _dot(a, b):
    return jnp.dot(a.astype(BF16), b.astype(BF16), preferred_element_type=F32)


def _dot_nt(a, b):
    return lax.dot_general(a.astype(BF16), b.astype(BF16), (((1,), (1,)), ((), ())),
                           preferred_element_type=F32)


def _dot_tn(a, b):
    return lax.dot_general(a.astype(BF16), b.astype(BF16), (((0,), (0,)), ((), ())),
                           preferred_element_type=F32)


def _split2(x):
    hi = x.astype(BF16)
    lo = (x - hi.astype(F32)).astype(BF16)
    return hi, lo


def _split3(x):
    hi = x.astype(BF16)
    r = x - hi.astype(F32)
    mid = r.astype(BF16)
    lo = (r - mid.astype(F32)).astype(BF16)
    return hi, mid, lo


def _layer_norm(v, g, b):
    mu = jnp.mean(v, axis=-1, keepdims=True)
    d = v - mu
    var = jnp.mean(d * d, axis=-1, keepdims=True)
    return d * lax.rsqrt(var + LN_EPS) * g + b


def _sigmoid(x):
    return 1.0 / (1.0 + jnp.exp(-x))


def _silu(x):
    return x * _sigmoid(x)


def _proj_kernel(x_ref, w_ref, o_ref):
    o_ref[...] = jnp.dot(x_ref[...].astype(BF16), w_ref[...], preferred_element_type=F32)


def _proj(x, w, tm=256):
    n, k = x.shape
    m = w.shape[1]
    return pl.pallas_call(
        _proj_kernel,
        grid=(n // tm,),
        in_specs=[pl.BlockSpec((tm, k), lambda i: (i, 0)),
                  pl.BlockSpec((k, m), lambda i: (0, 0))],
        out_specs=pl.BlockSpec((tm, m), lambda i: (i, 0)),
        out_shape=jax.ShapeDtypeStruct((n, m), F32),
        compiler_params=_cparams(("parallel",)),
        name="proj",
    )(x, w)


def _proj_ln_kernel(ya_ref, yb_ref, wa_ref, wb_ref, x_ref, g_ref, b_ref, o_ref):
    mix = (jnp.dot(ya_ref[...].astype(BF16), wa_ref[...], preferred_element_type=F32)
           + jnp.dot(yb_ref[...].astype(BF16), wb_ref[...], preferred_element_type=F32))
    o_ref[...] = _layer_norm(DN_ALPHA * x_ref[...] + mix, g_ref[...], b_ref[...])


def _proj_ln(ya, yb, w, xres, g, b, tm=256):
    n, ka = ya.shape
    kb = yb.shape[1]
    d = w.shape[1]
    wb16 = w.astype(BF16)
    full = lambda shp: pl.BlockSpec(shp, lambda i: (0, 0))
    return pl.pallas_call(
        _proj_ln_kernel,
        grid=(n // tm,),
        in_specs=[pl.BlockSpec((tm, ka), lambda i: (i, 0)),
                  pl.BlockSpec((tm, kb), lambda i: (i, 0)),
                  full((ka, d)), full((kb, d)),
                  pl.BlockSpec((tm, d), lambda i: (i, 0)),
                  full((1, d)), full((1, d))],
        out_specs=pl.BlockSpec((tm, d), lambda i: (i, 0)),
        out_shape=jax.ShapeDtypeStruct((n, d), F32),
        compiler_params=_cparams(("parallel",)),
        name="proj_ln",
    )(ya, yb, wb16[:ka], wb16[ka:], xres, g.reshape(1, d), b.reshape(1, d))


U16_QA = 0
U16_QM = U16_QA + NSA_HEADS * LANES
U16_KS = U16_QM + MOBA_HEADS * LANES
U16_KW = U16_KS + NSA_GROUPS * LANES
U16_KM = U16_KW + NSA_GROUPS * LANES
U16_VS = U16_KM + MOBA_HEADS * LANES
U16_VW = U16_VS + NSA_GROUPS * LANES
U16_VM = U16_VW + NSA_GROUPS * LANES
U16_W = U16_VM + MOBA_HEADS * LANES
U32_W = 3 * LANES


def _odd_proj_kernel(x_ref, w16_ref, w32_ref, qc_ref, kt_ref, o16_ref, o32_ref, km_ref):
    xb = x_ref[...].astype(BF16)
    acc = jnp.dot(xb, w16_ref[...], preferred_element_type=F32)
    km_ref[0] = jnp.mean(acc[:, U16_KM:U16_VS], axis=0, keepdims=True)
    o16_ref[:, 0:U16_KS] = (acc[:, 0:U16_KS] + qc_ref[...]).astype(BF16)
    o16_ref[:, U16_KS:U16_VS] = (acc[:, U16_KS:U16_VS] + kt_ref[...]).astype(BF16)
    o16_ref[:, U16_VS:U16_W] = acc[:, U16_VS:U16_W].astype(BF16)
    o32_ref[...] = jnp.dot(xb, w32_ref[...], preferred_element_type=F32)


def _odd_proj(x, w_in, t, tm=MOBA_BLOCK):
    n, d = x.shape
    assert tm == MOBA_BLOCK and t % tm == 0
    dh = HEAD_DIM
    nq = NSA_HEADS * dh
    nm = MOBA_HEADS * dh
    sizes = [nq] + [NSA_KV] * 6 + [3 * NSA_HEADS] + [nm] * 3
    offs = np.concatenate([[0], np.cumsum(sizes)])
    q, kc, vc, ks, vs, kw, vw, gates, mq, mk, mv = [w_in[:, offs[i]:offs[i + 1]] for i in range(11)]
    scale = dh ** -0.5

    def heads(w, nh, mult=1.0):
        w = (w * mult).reshape(d, nh, dh)
        return jnp.pad(w, ((0, 0), (0, 0), (0, LANES - dh))).reshape(d, nh * LANES)

    w16 = jnp.concatenate([heads(q, NSA_HEADS, scale), heads(mq, MOBA_HEADS, scale), heads(ks, NSA_GROUPS),
                           heads(kw, NSA_GROUPS), heads(mk, MOBA_HEADS), heads(vs, NSA_GROUPS),
                           heads(vw, NSA_GROUPS), heads(mv, MOBA_HEADS)], axis=1).astype(BF16)
    w32 = jnp.concatenate([kc, vc, jnp.pad(gates, ((0, 0), (0, LANES - 3 * NSA_HEADS)))], axis=1).astype(BF16)

    n_all = NSA_HEADS + MOBA_HEADS
    s_all = 2.0 ** (-8.0 * np.arange(1, n_all + 1) / n_all)
    slopes = jnp.asarray(np.concatenate([s_all[0::2], s_all[1::2]]), F32)
    s_hi = slopes.astype(BF16).astype(F32)
    s_lo = slopes - s_hi
    scols = jnp.stack([POS_SPLIT * s_hi, POS_SPLIT * s_lo, s_hi, s_lo], axis=-1)
    qconst = jnp.pad(scols, ((0, 0), (dh, LANES - dh - 4))).reshape(1, U16_KS)

    pos = np.arange(t)
    pos_cols = np.stack([pos // POS_SPLIT, pos // POS_SPLIT, pos % POS_SPLIT, pos % POS_SPLIT], axis=-1)

    def key_cols(blk):
        cols = np.zeros((t, LANES), np.float32)
        cols[:, dh:dh + 4] = pos_cols
        if blk:
            cols[pos, MASK_COL0 + pos // blk] = 1.0
        return cols

    ktab = np.concatenate([key_cols(SLC_LEN)] * NSA_GROUPS + [key_cols(0)] * NSA_GROUPS
                          + [key_cols(MOBA_BLOCK)] * MOBA_HEADS, axis=1)
    tiles_per_seq = t // tm
    return pl.pallas_call(
        _odd_proj_kernel,
        grid=(n // tm,),
        in_specs=[pl.BlockSpec((tm, d), lambda i: (i, 0)),
                  pl.BlockSpec((d, U16_W), lambda i: (0, 0)),
                  pl.BlockSpec((d, U32_W), lambda i: (0, 0)),
                  pl.BlockSpec((1, U16_KS), lambda i: (0, 0)),
                  pl.BlockSpec((tm, U16_VS - U16_KS), lambda i: (i % tiles_per_seq, 0))],
        out_specs=[pl.BlockSpec((tm, U16_W), lambda i: (i, 0)),
                   pl.BlockSpec((tm, U32_W), lambda i: (i, 0)),
                   pl.BlockSpec((1, 1, U16_VS - U16_KM), lambda i: (i, 0, 0))],
        out_shape=[jax.ShapeDtypeStruct((n, U16_W), BF16), jax.ShapeDtypeStruct((n, U32_W), F32),
                   jax.ShapeDtypeStruct((n // tm, 1, U16_VS - U16_KM), F32)],
        compiler_params=_cparams(("parallel",)),
        name="odd_proj",
    )(x, w16, w32, qconst, jnp.asarray(ktab, F32))


def _odd_out_kernel(oc_ref, os_ref, ow_ref, om_ref, gt_ref, wa_ref, wb_ref, x_ref, g_ref, b_ref, o_ref):
    sg = _sigmoid(gt_ref[...])
    parts = []
    for h in range(NSA_HEADS):
        cols = slice(h * LANES, (h + 1) * LANES)
        parts.append(sg[:, 3 * h:3 * h + 1] * oc_ref[:, cols] + sg[:, 3 * h + 1:3 * h + 2] * os_ref[:, cols]
                     + sg[:, 3 * h + 2:3 * h + 3] * ow_ref[:, cols])
    yc = jnp.concatenate(parts, axis=1)
    mix = _dot(yc, wa_ref[...]) + _dot(om_ref[...], wb_ref[...])
    o_ref[...] = _layer_norm(DN_ALPHA * x_ref[...] + mix, g_ref[...], b_ref[...])


def _odd_out_ln(o_c, o_s, o_w, o_m, u32, w_out, xres, g, b, tm=256):
    n, d = xres.shape
    dh = HEAD_DIM

    def pad_rows(w, nh):
        return jnp.pad(w.reshape(nh, dh, d), ((0, 0), (0, LANES - dh), (0, 0))).reshape(nh * LANES, d).astype(BF16)

    wa = pad_rows(w_out[:NSA_HEADS * dh], NSA_HEADS)
    wb = pad_rows(w_out[NSA_HEADS * dh:], MOBA_HEADS)
    row = lambda width: pl.BlockSpec((tm, width), lambda i: (i, 0))
    full = lambda shp: pl.BlockSpec(shp, lambda i: (0, 0))
    ka = NSA_HEADS * LANES
    kb = MOBA_HEADS * LANES
    return pl.pallas_call(
        _odd_out_kernel,
        grid=(n // tm,),
        in_specs=[row(ka), row(ka), row(ka), row(kb),
                  pl.BlockSpec((tm, LANES), lambda i: (i, U32_W // LANES - 1)),
                  full((ka, d)), full((kb, d)), row(d), full((1, d)), full((1, d))],
        out_specs=row(d),
        out_shape=jax.ShapeDtypeStruct((n, d), F32),
        compiler_params=_cparams(("parallel",)),
        name="odd_out_ln",
    )(o_c, o_s, o_w, o_m, u32, wa, wb, xres, g.reshape(1, d), b.reshape(1, d))


def _lru_kernel(xb_ref, gb_ref, cw_ref, cb_ref, wa_ref, ba_ref, wx_ref, bx_ref, lam_ref,
                o_ref, xs_ref, a_ref, b_ref, h_ref):
    ti = pl.program_id(1)
    tc = xb_ref.shape[1]
    c = xb_ref.shape[2]

    @pl.when(ti == 0)
    def _():
        xs_ref[0:SUBLANES, :] = jnp.zeros((SUBLANES, c), F32)
        h_ref[...] = jnp.zeros_like(h_ref)

    xs_ref[SUBLANES:SUBLANES + tc, :] = xb_ref[0]
    xc = cb_ref[...]
    for w in range(CONV_W):
        xc = xc + cw_ref[w:w + 1, :] * xs_ref[pl.ds(SUBLANES - (CONV_W - 1) + w, tc), :]
    xs_ref[0:SUBLANES, :] = xs_ref[tc:tc + SUBLANES, :]

    r = _sigmoid(_dot(xc, wa_ref[...]) + ba_ref[...])
    i = _sigmoid(_dot(xc, wx_ref[...]) + bx_ref[...])
    nl = -lam_ref[...]
    softplus = jnp.maximum(nl, 0.0) + jnp.log(1.0 + jnp.exp(-jnp.abs(nl)))
    log_a = (-LRU_C * r) * softplus
    a = jnp.exp(log_a)
    mult = jnp.sqrt(1.0 - jnp.exp(2.0 * log_a))
    trow = ti * tc + lax.broadcasted_iota(jnp.int32, (tc, c), 0)
    mult = jnp.where(trow == 0, 1.0, mult)
    a_ref[...] = a
    b_ref[...] = mult * (i * xc)

    row = lax.broadcasted_iota(jnp.int32, (SUBLANES, c), 0)

    def body(g, carry):
        r0 = pl.multiple_of(g * SUBLANES, SUBLANES)
        av = a_ref[pl.ds(r0, SUBLANES), :]
        bv = b_ref[pl.ds(r0, SUBLANES), :]
        for d in (1, 2, 4):
            a_sh = pltpu.roll(av, d, 0)
            b_sh = pltpu.roll(bv, d, 0)
            m = row >= d
            bv = jnp.where(m, av * b_sh + bv, bv)
            av = jnp.where(m, av * a_sh, av)
        h = av * carry + bv
        b_ref[pl.ds(r0, SUBLANES), :] = h
        return h[SUBLANES - 1:SUBLANES, :]

    h_ref[...] = lax.fori_loop(0, tc // SUBLANES, body, h_ref[...])
    o_ref[0] = b_ref[...] * jax.nn.gelu(gb_ref[0])


def _lru(u3, conv_w, conv_b, wa_bd, ba, wx_bd, bx, lam, tc=256):
    b, t, _ = u3.shape
    c = LRU_WIDTH
    row = lambda v: v.reshape(1, c)
    full = lambda shp: pl.BlockSpec(shp, lambda bi, ti: (0, 0))
    return pl.pallas_call(
        _lru_kernel,
        grid=(b, t // tc),
        in_specs=[pl.BlockSpec((1, tc, c), lambda bi, ti: (bi, ti, 0)),
                  pl.BlockSpec((1, tc, c), lambda bi, ti: (bi, ti, 1)),
                  full((CONV_W, c)), full((1, c)), full((c, c)), full((1, c)),
                  full((c, c)), full((1, c)), full((1, c))],
        out_specs=pl.BlockSpec((1, tc, c), lambda bi, ti: (bi, ti, 0)),
        out_shape=jax.ShapeDtypeStruct((b, t, c), F32),
        scratch_shapes=[pltpu.VMEM((tc + SUBLANES, c), F32), pltpu.VMEM((tc, c), F32),
                        pltpu.VMEM((tc, c), F32), pltpu.VMEM((1, c), F32)],
        compiler_params=_cparams(("parallel", "arbitrary")),
        name="rg_lru",
    )(u3, u3, conv_w, row(conv_b), wa_bd, row(ba), wx_bd, row(bx), row(lam))


def _hgrn_kernel(q_ref, f_ref, v_ref, g_ref, lb_ref, ng_ref, tri_ref, ones_ref, o_ref,
                 st_ref, q_s, k_s, c_s, qe_s, kd_s, dl_s, o_s):
    ti = pl.program_id(2)
    tc = q_ref.shape[1]
    dk = HG_DK
    nh = q_ref.shape[2] // dk

    @pl.when(ti == 0)
    def _():
        st_ref[...] = jnp.zeros_like(st_ref)

    lbv = lb_ref[...]
    q = _silu(q_ref[0])
    f = lbv + (1.0 - lbv) * _sigmoid(f_ref[0])
    kk = 1.0 - f
    logf = jnp.log(f)
    parts = _split3(logf)
    tri = tri_ref[...]
    ones = ones_ref[...]
    cum = sum(jnp.dot(tri, p, preferred_element_type=F32) for p in parts)
    last = sum(jnp.dot(ones, p, preferred_element_type=F32) for p in parts)
    q_s[...] = q
    k_s[...] = kk
    c_s[...] = cum
    qe_s[...] = q * jnp.exp(cum)
    kd_s[...] = kk * jnp.exp(last - cum)
    dl_s[...] = jnp.exp(last)

    s_idx = lax.broadcasted_iota(jnp.int32, (HG_SUB, dk), 0)

    def chunk(ci, _):
        r0 = pl.multiple_of(ci * HG_SUB, HG_SUB)
        rows = pl.ds(r0, HG_SUB)
        for h in range(nh):
            cols = slice(h * dk, (h + 1) * dk)
            st = st_ref[h]
            vc = v_ref[0, rows, cols]
            qc = q_s[rows, cols]
            kc = k_s[rows, cols]
            cc = c_s[rows, cols]
            o = _dot_nt(qe_s[rows, cols], st)
            for t in range(HG_SUB):
                n = (t // SUBLANES + 1) * SUBLANES
                d = cc[t:t + 1, :] - cc[0:n, :]
                dec = jnp.exp(jnp.where(s_idx[0:n, :] <= t, d, NEG))
                w = (qc[t:t + 1, :] * kc[0:n, :]) * dec
                a_t = jnp.sum(w, axis=1, keepdims=True)
                o_t = jnp.sum(a_t * vc[0:n, :], axis=0, keepdims=True)
                o = o + jnp.where(s_idx == t, o_t, 0.0)
            o_s[rows, cols] = o
            st_ref[h] = st * dl_s[pl.ds(r0, 1), cols] + _dot_tn(vc, kd_s[rows, cols])
        return 0

    lax.fori_loop(0, tc // HG_SUB, chunk, 0, unroll=2)
    gate = _sigmoid(g_ref[0])
    for h in range(nh):
        cols = slice(h * dk, (h + 1) * dk)
        o = o_s[:, cols]
        o = o * lax.rsqrt(jnp.mean(o * o, axis=-1, keepdims=True) + 1e-6) * ng_ref[:, cols]
        o_ref[0, :, cols] = o * gate[:, cols]


def _hgrn2(u3, lb, norm_g, tc=256, nh=4):
    b, t, _ = u3.shape
    w = nh * HG_DK
    base = (2 * LRU_WIDTH) // w
    per = HG_FDIM // w
    idx = np.arange(tc)
    same = (idx[:, None] // HG_SUB) == (idx[None, :] // HG_SUB)
    tri = jnp.asarray(same & (idx[None, :] <= idx[:, None]), BF16)
    ones = jnp.asarray(same, BF16)

    def col(off):
        return pl.BlockSpec((1, tc, w), lambda bi, hi, ti: (bi, ti, base + off * per + hi))

    vec = pl.BlockSpec((1, w), lambda bi, hi, ti: (0, hi))
    cst = pl.BlockSpec((tc, tc), lambda bi, hi, ti: (0, 0))
    return pl.pallas_call(
        _hgrn_kernel,
        grid=(b, per, t // tc),
        in_specs=[col(0), col(1), col(2), col(3), vec, vec, cst, cst],
        out_specs=pl.BlockSpec((1, tc, w), lambda bi, hi, ti: (bi, ti, hi)),
        out_shape=jax.ShapeDtypeStruct((b, t, HG_VDIM), F32),
        scratch_shapes=[pltpu.VMEM((nh, HG_DK, HG_DK), F32)] + [pltpu.VMEM((tc, w), F32)] * 7,
        compiler_params=_cparams(("parallel", "parallel", "arbitrary")),
        name="hgrn2",
    )(u3, u3, u3, u3, lb.reshape(1, HG_FDIM), norm_g.reshape(1, HG_VDIM), tri, ones)


def _nsa_cmp_kv_kernel(zk_ref, zv_ref, pek_ref, pev_ref, wk1_ref, wk2_ref, wv1_ref, wv2_ref,
                       ok_ref, ov_ref):
    half = zk_ref.shape[3]
    nrow = zk_ref.shape[2]

    def compress(z, pe_ref, w1_ref, w2_ref):
        lo = _dot(z + pe_ref[0:1, :], w1_ref[0:half, :])
        hi = _dot(z + pe_ref[1:2, :], w1_ref[half:2 * half, :])
        pre = lo + pltpu.roll(hi, nrow - 1, 0)
        return _dot(jax.nn.gelu(pre), w2_ref[...])

    ok_ref[0, 0] = compress(zk_ref[0, 0], pek_ref, wk1_ref, wk2_ref)
    ov_ref[0, 0] = compress(zv_ref[0, 0], pev_ref, wv1_ref, wv2_ref)


def _nsa_compress(kc, vc, pe_k, pe_v, wk1, wk2, wv1, wv2):
    b, g, t, dh = kc.shape
    nrow = t // CMP_STRIDE
    half = CMP_STRIDE * dh
    zk = kc.reshape(b, g, nrow, half)
    zv = vc.reshape(b, g, nrow, half)
    zspec = pl.BlockSpec((1, 1, nrow, half), lambda bi, gi: (bi, gi, 0, 0))
    full = lambda shp: pl.BlockSpec(shp, lambda bi, gi: (0, 0))
    ospec = pl.BlockSpec((1, 1, nrow, LANES), lambda bi, gi: (bi, gi, 0, 0))
    oshape = jax.ShapeDtypeStruct((b, g, nrow, LANES), F32)
    pad = lambda w: jnp.pad(w, ((0, 0), (0, LANES - dh))).astype(BF16)
    return pl.pallas_call(
        _nsa_cmp_kv_kernel,
        grid=(b, g),
        in_specs=[zspec, zspec, full((2, half)), full((2, half)),
                  full((2 * half, CMP_HIDDEN)), full((CMP_HIDDEN, LANES)),
                  full((2 * half, CMP_HIDDEN)), full((CMP_HIDDEN, LANES))],
        out_specs=[ospec, ospec],
        out_shape=[oshape, oshape],
        compiler_params=_cparams(("parallel", "parallel")),
        name="nsa_compress",
    )(zk, zv, pe_k.reshape(2, half), pe_v.reshape(2, half),
      wk1.astype(BF16), pad(wk2), wv1.astype(BF16), pad(wv2))


def _nsa_sel_kernel(slope_ref, q_ref, kc_ref, vc_ref, ov_ref, o_ref, mb_ref):
    gi = pl.program_id(0) % NSA_GROUPS
    qi = pl.program_id(1)
    tq = q_ref.shape[1]
    hpg = q_ref.shape[2] // LANES
    nc = kc_ref.shape[2]
    ns = ov_ref.shape[0]

    q = jnp.concatenate([q_ref[0, :, i * LANES:(i + 1) * LANES] for i in range(hpg)], axis=0)
    khi, klo = _split2(kc_ref[0, 0])
    s = _dot_nt(q, khi) + _dot_nt(q, klo)
    tpos = qi * tq + lax.broadcasted_iota(jnp.int32, (tq, nc), 0)
    cend = lax.broadcasted_iota(jnp.int32, (tq, nc), 1) * CMP_STRIDE + (CMP_LEN - 1)
    d_c = (tpos - cend).astype(F32)
    valid = d_c >= 0.0
    ps = []
    for i in range(hpg):
        si = s[i * tq:(i + 1) * tq, :] - slope_ref[gi * hpg + i] * d_c
        si = jnp.where(valid, si, NEG)
        m = jnp.max(si, axis=-1, keepdims=True)
        p = jnp.where(valid, jnp.exp(si - m), 0.0)
        ps.append(p / jnp.maximum(jnp.sum(p, axis=-1, keepdims=True), 1e-30))
    oc = _dot(jnp.concatenate(ps, axis=0), vc_ref[0, 0])
    o_ref[0] = jnp.concatenate([oc[i * tq:(i + 1) * tq, :] for i in range(hpg)], axis=1)

    psum = ps[0]
    for i in range(1, hpg):
        psum = psum + ps[i]
    phi, plo = _split2(psum)
    ovl = ov_ref[...]
    imp = _dot_nt(ovl, phi) + _dot_nt(ovl, plo)

    j = lax.broadcasted_iota(jnp.int32, (ns, tq), 0)
    own = (qi * tq + lax.broadcasted_iota(jnp.int32, (ns, tq), 1)) // SLC_LEN
    forced = (j == 0) | (j == own) | (j == own - 1)
    imp = jnp.where(j <= own, imp + FORCE_BONUS * forced.astype(F32), NEG)
    rank = jnp.zeros((ns, tq), F32)
    for jp in range(ns):
        row = imp[jp:jp + 1, :]
        beats = (row > imp) | ((row == imp) & (j > jp))
        rank = rank + beats.astype(F32)
    bias = jnp.where(rank < float(min(SLC_TOPN, ns)), 0.0, MASKED)
    place = (lax.broadcasted_iota(jnp.int32, (ns, LANES), 1)
             == lax.broadcasted_iota(jnp.int32, (ns, LANES), 0) + MASK_COL0)
    mb_ref[0] = _dot_tn(bias, place.astype(F32)).astype(BF16)


def _nsa_select(slopes, u16, k_cmp, v_cmp, tq=512):
    b, t, _ = u16.shape
    g = NSA_GROUPS
    hpg = NSA_HPG
    nc = k_cmp.shape[2]
    ns = t // SLC_LEN
    ci = np.arange(nc)[:, None]
    cj = np.arange(ns)[None, :]
    ovl = ((CMP_STRIDE * ci < SLC_LEN * (cj + 1)) & (CMP_STRIDE * ci + CMP_LEN > SLC_LEN * cj))
    ovl = ovl & (ci < (t - CMP_LEN) // CMP_STRIDE + 1)
    grid_spec = pltpu.PrefetchScalarGridSpec(
        num_scalar_prefetch=1,
        grid=(b * g, t // tq),
        in_specs=[pl.BlockSpec((1, tq, hpg * LANES), lambda bg, qi, s: (bg // g, qi, U16_QA // (hpg * LANES) + bg % g)),
                  pl.BlockSpec((1, 1, nc, LANES), lambda bg, qi, s: (bg // g, bg % g, 0, 0)),
                  pl.BlockSpec((1, 1, nc, LANES), lambda bg, qi, s: (bg // g, bg % g, 0, 0)),
                  pl.BlockSpec((ns, nc), lambda bg, qi, s: (0, 0))],
        out_specs=[pl.BlockSpec((1, tq, hpg * LANES), lambda bg, qi, s: (bg // g, qi, bg % g)),
                   pl.BlockSpec((1, tq, LANES), lambda bg, qi, s: (bg // g, qi, bg % g))],
    )
    return pl.pallas_call(
        _nsa_sel_kernel,
        grid_spec=grid_spec,
        out_shape=[jax.ShapeDtypeStruct((b, t, g * hpg * LANES), F32),
                   jax.ShapeDtypeStruct((b, t, g * LANES), BF16)],
        compiler_params=_cparams(("parallel", "parallel")),
        name="nsa_select",
    )(slopes, u16, k_cmp, v_cmp, jnp.asarray(ovl.T, BF16))


POS_SPLIT = 64
MASKED = -2e30


MASK_COL0 = HEAD_DIM + 4
KEY_STEP = 512


def _flash_kernel(q_ref, k_ref, v_ref, *rest, mode, nkv, tq):
    o_ref = rest[-1]
    hpg = q_ref.shape[2] // (LANES * nkv)
    t_all = k_ref.shape[1]
    nt = q_ref.shape[1] // tq
    whole = nt * tq == t_all
    step = MOBA_BLOCK if mode == "moba" else KEY_STEP

    def tile(ti, q0):
        rows = slice(ti * tq, (ti + 1) * tq)
        qs = []
        for j in range(nkv):
            kv_cols = slice(j * LANES, (j + 1) * LANES)
            heads = [q_ref[0, rows, (j * hpg + i) * LANES:(j * hpg + i + 1) * LANES] for i in range(hpg)]
            if mode == "slc":
                heads = [h + rest[0][0, rows, kv_cols] for h in heads]
            q = heads[0] if hpg == 1 else jnp.concatenate(heads, axis=0)
            if mode == "moba":
                kmean = rest[0][0, :, kv_cols]
                nb = kmean.shape[0]
                mhi, mlo = _split2(kmean)
                gate = _dot_nt(mhi, q) + _dot_nt(mlo, q)
                jb = lax.broadcasted_iota(jnp.int32, (nb, tq), 0)
                ob = q0 // MOBA_BLOCK
                gate = jnp.where(jb < ob, gate, NEG)
                rank = jnp.zeros((nb, tq), F32)
                for jp in range(nb):
                    row = gate[jp:jp + 1, :]
                    beats = (row > gate) | ((row == gate) & (jb > jp))
                    rank = rank + beats.astype(F32)
                n_sel = float(min(MOBA_TOPK, nb - 1))
                bsel = ((rank < n_sel) & (jb < ob)) | (jb == ob)
                bias = jnp.where(bsel, 0.0, MASKED)
                place = (lax.broadcasted_iota(jnp.int32, (nb, LANES), 1)
                         == lax.broadcasted_iota(jnp.int32, (nb, LANES), 0) + MASK_COL0)
                q = q + _dot_tn(bias, place.astype(F32)).astype(BF16)
            qs.append(q)

        def attend(kstart, klen, diag, off):
            rel = lax.broadcasted_iota(jnp.int32, (tq, diag), 0) - lax.broadcasted_iota(jnp.int32, (tq, diag), 1)
            valid = rel >= off
            if mode == "win":
                valid = valid & (rel < off + WIN)
            if hpg > 1:
                valid = jnp.concatenate([valid] * hpg, axis=0)
            for j in range(nkv):
                kv_cols = slice(j * LANES, (j + 1) * LANES)
                s = _dot_nt(qs[j], k_ref[0, pl.ds(kstart, klen), kv_cols])
                sd = jnp.where(valid, s[:, klen - diag:], MASKED)
                s = sd if diag == klen else jnp.concatenate([s[:, :klen - diag], sd], axis=1)
                m = jnp.maximum(jnp.max(s, axis=-1, keepdims=True), NEG)
                p = jnp.exp(s - m)
                l = jnp.sum(p, axis=-1, keepdims=True)
                o = _dot(p, v_ref[0, pl.ds(kstart, klen), kv_cols]) / jnp.maximum(l, 1e-30)
                for i in range(hpg):
                    o_ref[0, rows, (j * hpg + i) * LANES:(j * hpg + i + 1) * LANES] = o[i * tq:(i + 1) * tq, :]

        if mode == "win":
            kstart = max(q0 - WIN, 0) if whole else pl.multiple_of(jnp.maximum(q0 - WIN, 0), LANES)
            attend(kstart, WIN + tq, WIN + tq, kstart - q0)
        elif whole:
            v = (q0 + tq - 1) // step
            attend(0, step * (v + 1), step, step * v - q0)
        else:
            var = (q0 + tq - 1) // step
            for v in range(t_all // step):
                @pl.when(var == v)
                def _(v=v):
                    attend(0, step * (v + 1), step, step * v - q0)

    for ti in range(nt):
        tile(ti, ti * tq if whole else (pl.program_id(1) * nt + ti) * tq)


def _flash(mode, u16, extra=None, tq=128, nkv=1, tiles_per_step=1):
    b, t, _ = u16.shape
    q_off, k_off, v_off, n_kvh, hpg = {
        "slc": (U16_QA, U16_KS, U16_VS, NSA_GROUPS, NSA_HPG),
        "win": (U16_QA, U16_KW, U16_VW, NSA_GROUPS, NSA_HPG),
        "moba": (U16_QM, U16_KM, U16_VM, MOBA_HEADS, 1)}[mode]
    assert mode != "moba" or tq == MOBA_BLOCK
    steps = n_kvh // nkv
    qw = nkv * hpg * LANES
    kw = nkv * LANES
    tb = tq * tiles_per_step
    in_specs = [pl.BlockSpec((1, tb, qw), lambda bh, qi: (bh // steps, qi, q_off // qw + bh % steps)),
                pl.BlockSpec((1, t, kw), lambda bh, qi: (bh // steps, 0, k_off // kw + bh % steps)),
                pl.BlockSpec((1, t, kw), lambda bh, qi: (bh // steps, 0, v_off // kw + bh % steps))]
    args = [u16, u16, u16]
    if mode == "slc":
        in_specs.append(pl.BlockSpec((1, tb, kw), lambda bh, qi: (bh // steps, qi, bh % steps)))
        args.append(extra)
    elif mode == "moba":
        in_specs.append(pl.BlockSpec((1, t // MOBA_BLOCK, kw), lambda bh, qi: (bh // steps, 0, bh % steps)))
        args.append(extra)
    return pl.pallas_call(
        functools.partial(_flash_kernel, mode=mode, nkv=nkv, tq=tq),
        grid=(b * steps, t // tb),
        in_specs=in_specs,
        out_specs=pl.BlockSpec((1, tb, qw), lambda bh, qi: (bh // steps, qi, bh % steps)),
        out_shape=jax.ShapeDtypeStruct((b, t, n_kvh * hpg * LANES), F32),
        compiler_params=_cparams(("parallel", "parallel")),
        name="flash_" + mode,
    )(*args)


def _router_kernel(x_ref, whi_ref, wlo_ref, b_ref, triu_ref, meta_ref, cnt_ref, carry_ref):
    step = pl.program_id(0)
    tm = x_ref.shape[0]
    ne = whi_ref.shape[0]
    gsz = ne // N_GROUPS

    @pl.when(step == 0)
    def _():
        carry_ref[...] = jnp.zeros_like(carry_ref)

    xhi, xlo = _split2(x_ref[...])
    whi = whi_ref[...]
    logits = _dot_nt(whi, xhi) + _dot_nt(wlo_ref[...], xhi) + _dot_nt(whi, xlo)
    scores = _sigmoid(logits)
    biased = scores + b_ref[:, 0:1]

    v3 = biased.reshape(N_GROUPS, gsz, tm)
    i3 = lax.broadcasted_iota(jnp.int32, (N_GROUPS, gsz, tm), 1).astype(F32)
    m1 = jnp.max(v3, axis=1, keepdims=True)
    idx1 = jnp.min(jnp.where(v3 == m1, i3, float(gsz)), axis=1, keepdims=True)
    m2 = jnp.max(jnp.where(i3 == idx1, NEG_INF, v3), axis=1, keepdims=True)
    gs = (m1 + m2).reshape(N_GROUPS, tm)
    gi = lax.broadcasted_iota(jnp.int32, (N_GROUPS, tm), 0).astype(F32)
    gmask = jnp.zeros((N_GROUPS, tm), F32)
    for _ in range(TOPK_GROUPS):
        m = jnp.max(gs, axis=0, keepdims=True)
        pick = jnp.min(jnp.where(gs == m, gi, float(N_GROUPS)), axis=0, keepdims=True)
        hit = gi == pick
        gmask = jnp.where(hit, 1.0, gmask)
        gs = jnp.where(hit, NEG_INF, gs)
    emask = jnp.broadcast_to(gmask.reshape(N_GROUPS, 1, tm), (N_GROUPS, gsz, tm)).reshape(ne, tm)
    cur = jnp.where(emask > 0.5, biased, NEG)

    ei = lax.broadcasted_iota(jnp.int32, (ne, tm), 0).astype(F32)
    selm = jnp.zeros((ne, tm), F32)
    idxs, svals = [], []
    for _ in range(TOP_K):
        m = jnp.max(cur, axis=0, keepdims=True)
        idx = jnp.min(jnp.where(cur == m, ei, float(ne)), axis=0, keepdims=True)
        hit = ei == idx
        svals.append(jnp.sum(jnp.where(hit, scores, 0.0), axis=0, keepdims=True))
        idxs.append(idx)
        cur = jnp.where(hit, NEG_INF, cur)
        selm = jnp.where(hit, 1.0, selm)

    carry = carry_ref[:, 0:1]
    rank_full = jnp.dot(selm.astype(BF16), triu_ref[...], preferred_element_type=F32) + carry
    ssum = svals[0]
    for k in range(1, TOP_K):
        ssum = ssum + svals[k]
    for k in range(TOP_K):
        meta_ref[k:k + 1, :] = idxs[k]
        meta_ref[TOP_K + k:TOP_K + k + 1, :] = svals[k] / ssum * ROUTED_SCALE
        meta_ref[2 * TOP_K + k:2 * TOP_K + k + 1, :] = jnp.sum(
            jnp.where(ei == idxs[k], rank_full, 0.0), axis=0, keepdims=True)
    new_carry = carry + jnp.sum(selm, axis=1, keepdims=True)
    carry_ref[...] = jnp.broadcast_to(new_carry, carry_ref.shape)
    cnt_ref[...] = jnp.broadcast_to(new_carry, cnt_ref.shape)


def _router(x, w_router, b_router, tm=512):
    n, d = x.shape
    ne = w_router.shape[1]
    wt = w_router.T
    whi = wt.astype(BF16)
    wlo = (wt - whi.astype(F32)).astype(BF16)
    triu = jnp.asarray(np.arange(tm)[:, None] < np.arange(tm)[None, :], BF16)
    meta, cnt = pl.pallas_call(
        _router_kernel,
        grid=(n // tm,),
        in_specs=[pl.BlockSpec((tm, d), lambda i: (i, 0)),
                  pl.BlockSpec((ne, d), lambda i: (0, 0)),
                  pl.BlockSpec((ne, d), lambda i: (0, 0)),
                  pl.BlockSpec((ne, LANES), lambda i: (0, 0)),
                  pl.BlockSpec((tm, tm), lambda i: (0, 0))],
        out_specs=[pl.BlockSpec((3 * TOP_K, tm), lambda i: (0, i)),
                   pl.BlockSpec((ne, LANES), lambda i: (0, 0))],
        out_shape=[jax.ShapeDtypeStruct((3 * TOP_K, n), F32),
                   jax.ShapeDtypeStruct((ne, LANES), F32)],
        scratch_shapes=[pltpu.VMEM((ne, LANES), F32)],
        compiler_params=_cparams(("arbitrary",)),
        name="moe_router",
    )(x, whi, wlo, jnp.broadcast_to(b_router.reshape(ne, 1), (ne, LANES)), triu)
    return meta, cnt[:, 0]


def _expert_kernel(be_ref, first_ref, nxt_ref, slot_ref, nu_ref, xs_ref, wg_hbm, wu_hbm, wd_hbm, o_ref,
                   wg_buf, wu_buf, wd_buf, wgu16, wd16, sem, *, layer):
    bi = pl.program_id(0)

    def weight_copies(e, s):
        return [pltpu.make_async_copy(hbm.at[layer, e], buf.at[s], sem.at[s])
                for hbm, buf in ((wg_hbm, wg_buf), (wu_hbm, wu_buf), (wd_hbm, wd_buf))]

    @pl.when(bi < nu_ref[0])
    def _():
        s = slot_ref[bi]

        @pl.when(first_ref[bi] == 1)
        def _():
            @pl.when(bi == 0)
            def _():
                for c in weight_copies(be_ref[0], s):
                    c.start()

            @pl.when(nxt_ref[bi] >= 0)
            def _():
                for c in weight_copies(nxt_ref[bi], 1 - s):
                    c.start()

            for c in weight_copies(be_ref[bi], s):
                c.wait()
            hd = wd16.shape[0]
            wgu16[:, 0:hd] = wg_buf[s].astype(BF16)
            wgu16[:, hd:2 * hd] = wu_buf[s].astype(BF16)
            wd16[...] = wd_buf[s].astype(BF16)

        xa, xb = _unpack_bf16_pair(xs_ref[...])
        half = xa.shape[1]

        def xw(w_ref):
            return (jnp.dot(xa, w_ref[0:half, :], preferred_element_type=F32)
                    + jnp.dot(xb, w_ref[half:2 * half, :], preferred_element_type=F32))

        gu = xw(wgu16)
        hd = wd16.shape[0]
        h = _silu(gu[:, 0:hd]) * gu[:, hd:2 * hd]
        o_ref[...] = _pack_bf16_pair(jnp.dot(h.astype(BF16), wd16[...], preferred_element_type=F32))

    @pl.when(bi >= nu_ref[0])
    def _():
        o_ref[...] = jnp.zeros_like(o_ref)


def _expert_ffn(blk_e, n_used, xs, layer, w_gate, w_up, w_down):
    n_rows = xs.shape[0]
    d = w_gate.shape[2]
    hdim = w_gate.shape[3]
    n_blk = n_rows // EXPERT_ROWS
    idx = jnp.arange(n_blk, dtype=jnp.int32)
    first = jnp.concatenate([jnp.ones((1,), bool), blk_e[1:] != blk_e[:-1]]) & (idx < n_used[0])
    slot = ((jnp.cumsum(first.astype(jnp.int32)) - 1) % 2).astype(jnp.int32)
    nxt_first = lax.cummin(jnp.where(first, idx, n_blk), reverse=True)
    nxt_first = jnp.concatenate([nxt_first[1:], jnp.full((1,), n_blk, jnp.int32)])
    nxt_e = jnp.where(nxt_first < n_blk, blk_e[jnp.minimum(nxt_first, n_blk - 1)], -1).astype(jnp.int32)
    grid_spec = pltpu.PrefetchScalarGridSpec(
        num_scalar_prefetch=5,
        grid=(n_blk,),
        in_specs=[pl.BlockSpec((EXPERT_ROWS, d // 2),
                               lambda i, be, fi, nx, sl, nu: (jnp.minimum(i, jnp.maximum(nu[0] - 1, 0)), 0)),
                  pl.BlockSpec(memory_space=pl.ANY),
                  pl.BlockSpec(memory_space=pl.ANY),
                  pl.BlockSpec(memory_space=pl.ANY)],
        out_specs=pl.BlockSpec((EXPERT_ROWS, d // 2), lambda i, be, fi, nx, sl, nu: (i, 0)),
        scratch_shapes=[pltpu.VMEM((2, d, hdim), F32), pltpu.VMEM((2, d, hdim), F32), pltpu.VMEM((2, hdim, d), F32),
                        pltpu.VMEM((d, 2 * hdim), BF16), pltpu.VMEM((hdim, d), BF16),
                        pltpu.SemaphoreType.DMA((2,))],
    )
    return pl.pallas_call(
        functools.partial(_expert_kernel, layer=layer),
        grid_spec=grid_spec,
        out_shape=jax.ShapeDtypeStruct((n_rows, d // 2), jnp.uint32),
        compiler_params=_cparams(("arbitrary",)),
        name="moe_experts",
    )(blk_e, first.astype(jnp.int32), nxt_e, slot, n_used, xs, w_gate, w_up, w_down)


def _dest_kernel(meta_ref, pstart_ref, dest_ref):
    ne = pstart_ref.shape[0]
    tm = meta_ref.shape[1]
    ei = lax.broadcasted_iota(jnp.int32, (ne, tm), 0).astype(F32)
    ps = pstart_ref[:, 0:1]
    for k in range(TOP_K):
        base = jnp.sum(jnp.where(ei == meta_ref[k:k + 1, :], ps, 0.0), axis=0, keepdims=True)
        dest_ref[k:k + 1, :] = (base + meta_ref[2 * TOP_K + k:2 * TOP_K + k + 1, :]).astype(jnp.int32)


def _dest_rows(meta, pstart, tm=512):
    n = meta.shape[1]
    ne = pstart.shape[0]
    return pl.pallas_call(
        _dest_kernel,
        grid=(n // tm,),
        in_specs=[pl.BlockSpec((3 * TOP_K, tm), lambda i: (0, i)),
                  pl.BlockSpec((ne, LANES), lambda i: (0, 0))],
        out_specs=pl.BlockSpec((TOP_K, tm), lambda i: (0, i)),
        out_shape=jax.ShapeDtypeStruct((TOP_K, n), jnp.int32),
        compiler_params=_cparams(("parallel",)),
        name="moe_dest",
    )(meta, jnp.broadcast_to(pstart.astype(F32).reshape(ne, 1), (ne, LANES)))


def _pack_bf16_pair(x):
    half = x.shape[1] // 2
    hi = pltpu.bitcast(x[:, 0:half].astype(BF16).astype(F32), jnp.uint32)
    lo = pltpu.bitcast(x[:, half:2 * half].astype(BF16).astype(F32), jnp.uint32)
    return hi | (lo >> 16)


def _unpack_bf16_pair(p):
    hi = pltpu.bitcast(p & jnp.uint32(0xFFFF0000), F32).astype(BF16)
    lo = pltpu.bitcast(p << 16, F32).astype(BF16)
    return hi, lo


def _dispatch_kernel(dest_ref, x_ref, xs_in_ref, xs_ref, pk_ref, sem):
    del xs_in_ref
    tm = x_ref.shape[0]
    pk_ref[...] = _pack_bf16_pair(x_ref[...])

    def row_copy(t, k):
        return pltpu.make_async_copy(pk_ref.at[pl.ds(t, 1), :], xs_ref.at[pl.ds(dest_ref[t * TOP_K + k], 1), :], sem)

    def body(t, _):
        for k in range(TOP_K):
            row_copy(t, k).start(priority=k % 2)
        return 0

    lax.fori_loop(0, tm, body, 0, unroll=4)
    done = xs_ref.at[pl.ds(0, TOP_K * tm), :]
    pltpu.make_async_copy(done, done, sem).wait()


def _dispatch(dest, x, n_rows, tm=256):
    n, d = x.shape
    return pl.pallas_call(
        _dispatch_kernel,
        grid=(n // tm,),
        in_specs=[pl.BlockSpec((TOP_K * tm,), lambda i: (i,), memory_space=pltpu.SMEM),
                  pl.BlockSpec((tm, d), lambda i: (i, 0)),
                  pl.BlockSpec(memory_space=pl.ANY)],
        out_specs=pl.BlockSpec(memory_space=pl.ANY),
        out_shape=jax.ShapeDtypeStruct((n_rows, d // 2), jnp.uint32),
        scratch_shapes=[pltpu.VMEM((tm, d // 2), jnp.uint32), pltpu.SemaphoreType.DMA(())],
        input_output_aliases={2: 0},
        compiler_params=_cparams(("arbitrary",)),
        name="moe_dispatch",
    )(dest, x, jnp.zeros((n_rows, d // 2), jnp.uint32))


def _combine_ln_kernel(dest_ref, dest_next_ref, w_ref, x_ref, ys_ref, wg_ref, wu_ref, wd_ref, g_ref, b_ref,
                       o_ref, buf, sem):
    step = pl.program_id(0)
    tm = x_ref.shape[0]
    slot = step % 2

    def gather_rows(idx_ref, to_slot):
        def body(t, _):
            for k in range(TOP_K):
                pltpu.make_async_copy(ys_ref.at[pl.ds(idx_ref[t * TOP_K + k], 1), :],
                                      buf.at[to_slot, k, pl.ds(t, 1), :], sem.at[to_slot]).start(priority=k % 2)
            return 0

        lax.fori_loop(0, tm, body, 0, unroll=4)

    @pl.when(step == 0)
    def _():
        gather_rows(dest_ref, 0)

    @pl.when(step + 1 < pl.num_programs(0))
    def _():
        gather_rows(dest_next_ref, 1 - slot)

    x = x_ref[...]
    xb = x.astype(BF16)
    h = _silu(jnp.dot(xb, wg_ref[...], preferred_element_type=F32)) * jnp.dot(
        xb, wu_ref[...], preferred_element_type=F32)
    ff = jnp.dot(h.astype(BF16), wd_ref[...], preferred_element_type=F32)
    pltpu.make_async_copy(buf.at[slot], buf.at[slot], sem.at[slot]).wait()
    routed_a = jnp.zeros((tm, buf.shape[3]), F32)
    routed_b = jnp.zeros((tm, buf.shape[3]), F32)
    for k in range(TOP_K):
        ya, yb = _unpack_bf16_pair(buf[slot, k])
        routed_a = routed_a + ya.astype(F32) * w_ref[:, k:k + 1]
        routed_b = routed_b + yb.astype(F32) * w_ref[:, k:k + 1]
    ff = ff + jnp.concatenate([routed_a, routed_b], axis=1)
    o_ref[...] = _layer_norm(DN_ALPHA * x + ff, g_ref[...], b_ref[...])


def _combine_ln(dest, wts, x, ys, ws_gate, ws_up, ws_down, g, b, tm=256):
    n, d = x.shape
    hdim = ws_gate.shape[1]
    row = pl.BlockSpec((tm, d), lambda i: (i, 0))
    last = n // tm - 1
    return pl.pallas_call(
        _combine_ln_kernel,
        grid=(n // tm,),
        in_specs=[pl.BlockSpec((TOP_K * tm,), lambda i: (i,), memory_space=pltpu.SMEM),
                  pl.BlockSpec((TOP_K * tm,), lambda i: (jnp.minimum(i + 1, last),), memory_space=pltpu.SMEM),
                  pl.BlockSpec((tm, TOP_K), lambda i: (i, 0)),
                  row,
                  pl.BlockSpec(memory_space=pl.ANY),
                  pl.BlockSpec((d, hdim), lambda i: (0, 0)),
                  pl.BlockSpec((d, hdim), lambda i: (0, 0)),
                  pl.BlockSpec((hdim, d), lambda i: (0, 0)),
                  pl.BlockSpec((1, d), lambda i: (0, 0)),
                  pl.BlockSpec((1, d), lambda i: (0, 0))],
        out_specs=row,
        out_shape=jax.ShapeDtypeStruct((n, d), F32),
        scratch_shapes=[pltpu.VMEM((2, TOP_K, tm, d // 2), jnp.uint32), pltpu.SemaphoreType.DMA((2,))],
        compiler_params=_cparams(("arbitrary",)),
        name="moe_combine_ln",
    )(dest, dest, wts, x, ys, ws_gate.astype(BF16), ws_up.astype(BF16), ws_down.astype(BF16),
      g.reshape(1, d), b.reshape(1, d))


def _moe_ln(x, layer, w_router, b_router, w_gate, w_up, w_down, ws_gate, ws_up, ws_down, g, b):
    n, d = x.shape
    ne = w_router.shape[1]
    meta, counts = _router(x, w_router, b_router)
    counts = counts.astype(jnp.int32)
    padded = (counts + EXPERT_ROWS - 1) // EXPERT_ROWS * EXPERT_ROWS
    pend = jnp.cumsum(padded)
    dest = _dest_rows(meta, pend - padded).T.reshape(-1)
    n_blk = (n * TOP_K) // EXPERT_ROWS + ne
    blk_row0 = jnp.arange(n_blk, dtype=jnp.int32) * EXPERT_ROWS
    blk_e = jnp.minimum(jnp.sum((pend[None, :] <= blk_row0[:, None]).astype(jnp.int32), axis=1), ne - 1)
    n_used = (pend[-1] // EXPERT_ROWS).astype(jnp.int32).reshape(1)
    xs = _dispatch(dest, x, n_blk * EXPERT_ROWS)
    ys = _expert_ffn(blk_e, n_used, xs, layer, w_gate, w_up, w_down)
    return _combine_ln(dest, meta[TOP_K:2 * TOP_K].T, x, ys, ws_gate, ws_up, ws_down, g, b)


def _block_diag(w):
    nb, bs, _ = w.shape
    eye = jnp.eye(nb, dtype=w.dtype)
    return (eye[:, None, :, None] * w[:, :, None, :]).reshape(nb * bs, nb * bs)


def _even_mixer_ln(x, w_in, conv_w, conv_b, wa, ba, wx, bx, lam, lb, norm_g, w_out, g, b):
    bsz, t, d = x.shape
    xf = x.reshape(bsz * t, d)
    u = _proj(xf, w_in.astype(BF16)).reshape(bsz, t, -1)
    ya = _lru(u, conv_w, conv_b, _block_diag(wa).astype(BF16), ba, _block_diag(wx).astype(BF16), bx, lam)
    yb = _hgrn2(u, lb, norm_g)
    return _proj_ln(ya.reshape(bsz * t, -1), yb.reshape(bsz * t, -1), w_out, xf, g, b).reshape(bsz, t, d)


def _odd_mixer_ln(x, w_in, pe_k, pe_v, wk1, wk2, wv1, wv2, w_out, g, b):
    bsz, t, d = x.shape
    xf = x.reshape(bsz * t, d)
    u16, u32, kmean = _odd_proj(xf, w_in, t)
    u16 = u16.reshape(bsz, t, U16_W)
    kmean = kmean.reshape(bsz, t // MOBA_BLOCK, -1)

    def grp(off):
        z = u32[:, off:off + NSA_KV].reshape(bsz, t, NSA_GROUPS, HEAD_DIM)
        return z.transpose(0, 2, 1, 3)

    n_all = NSA_HEADS + MOBA_HEADS
    s_nsa = jnp.asarray((2.0 ** (-8.0 * np.arange(1, n_all + 1) / n_all))[0::2], F32)
    k_cmp, v_cmp = _nsa_compress(grp(0), grp(NSA_KV), pe_k, pe_v, wk1, wk2, wv1, wv2)
    o_c, mask_bias = _nsa_select(s_nsa, u16, k_cmp, v_cmp)
    o_s = _flash("slc", u16, mask_bias, tq=128, tiles_per_step=t // 128)
    o_w = _flash("win", u16, tq=128, tiles_per_step=t // 128)
    o_m = _flash("moba", u16, kmean, tq=MOBA_BLOCK, nkv=2, tiles_per_step=t // MOBA_BLOCK)
    flat = lambda z: z.reshape(bsz * t, -1)
    return _odd_out_ln(flat(o_c), flat(o_s), flat(o_w), flat(o_m), u32, w_out, xf, g, b).reshape(bsz, t, d)


def kernel(x, even_w_in, lru_conv_w, lru_conv_b, lru_wa, lru_ba, lru_wx, lru_bx, lru_lambda,
           hg_lower_bound, hg_norm_g, even_w_out, odd_w_in, nsa_pe_k, nsa_pe_v, nsa_wk1, nsa_wk2,
           nsa_wv1, nsa_wv2, odd_w_out, ln_g, ln_b, w_router, b_router, w_gate, w_up, w_down,
           ws_gate, ws_up, ws_down):
    bsz, t, d = x.shape
    lb_all = jnp.cumsum(jax.nn.softmax(hg_lower_bound.astype(F32), axis=0), axis=0)
    for layer in range(DEPTH):
        li = layer // 2
        if layer % 2 == 0:
            x = _even_mixer_ln(x, even_w_in[li], lru_conv_w[li], lru_conv_b[li], lru_wa[li], lru_ba[li],
                               lru_wx[li], lru_bx[li], lru_lambda[li], lb_all[layer], hg_norm_g[li],
                               even_w_out[li], ln_g[layer, 0], ln_b[layer, 0])
        else:
            x = _odd_mixer_ln(x, odd_w_in[li], nsa_pe_k[li], nsa_pe_v[li], nsa_wk1[li], nsa_wk2[li],
                              nsa_wv1[li], nsa_wv2[li], odd_w_out[li], ln_g[layer, 0], ln_b[layer, 0])
        x = _moe_ln(x.reshape(bsz * t, d), layer, w_router[layer], b_router[layer], w_gate, w_up,
                    w_down, ws_gate[layer], ws_up[layer], ws_down[layer],
                    ln_g[layer, 1], ln_b[layer, 1]).reshape(bsz, t, d)
    return x
```

```python
import functools

import numpy as np
import jax
import jax.numpy as jnp
from jax import lax
from jax.experimental import pallas as pl
from jax.experimental.pallas import tpu as pltpu

F32 = jnp.float32
BF16 = jnp.bfloat16

D_MODEL = 1024
DEPTH = 2
LRU_WIDTH = 512
LRU_BLOCKS = 8
LRU_BLOCK = LRU_WIDTH // LRU_BLOCKS
CONV_W = 4
LRU_C = 8.0
HG_HEADS = 4
HG_DK = 128
HG_FDIM = HG_HEADS * HG_DK
HG_VDIM = HG_HEADS * HG_DK
HEAD_DIM = 64
NSA_HEADS = 8
NSA_GROUPS = 2
NSA_HPG = NSA_HEADS // NSA_GROUPS
NSA_KV = NSA_GROUPS * HEAD_DIM
CMP_LEN = 32
CMP_STRIDE = 16
CMP_HIDDEN = 128
SLC_LEN = 64
SLC_TOPN = 16
WIN = 512
FORCE_BONUS = 1e4
MOBA_HEADS = 8
MOBA_BLOCK = 256
MOBA_TOPK = 3
N_EXPERTS = 256
TOP_K = 8
N_GROUPS = 8
TOPK_GROUPS = 4
EXPERT_HIDDEN = 256
ROUTED_SCALE = 2.5
DN_ALPHA = (2 * DEPTH) ** 0.25
LN_EPS = 1e-5
NEG = -1e30
NEG_INF = float("-inf")

LANES = 128
SUBLANES = 8
VMEM_LIMIT = 48 * 1024 * 1024

HG_SUB = 16
EXPERT_ROWS = 512


def _cparams(sem):
    return pltpu.CompilerParams(dimension_semantics=sem, vmem_limit_bytes=VMEM_LIMIT)


def _dot(a, b):
    return jnp.dot(a.astype(BF16), b.astype(BF16), preferred_element_type=F32)


def _dot_nt(a, b):
    return lax.dot_general(a.astype(BF16), b.astype(BF16), (((1,), (1,)), ((), ())),
                           preferred_element_type=F32)


def _dot_tn(a, b):
    return lax.dot_general(a.astype(BF16), b.astype(BF16), (((0,), (0,)), ((), ())),
                           preferred_element_type=F32)


def _split2(x):
    hi = x.astype(BF16)
    lo = (x - hi.astype(F32)).astype(BF16)
    return hi, lo


def _split3(x):
    hi = x.astype(BF16)
    r = x - hi.astype(F32)
    mid = r.astype(BF16)
    lo = (r - mid.astype(F32)).astype(BF16)
    return hi, mid, lo


def _layer_norm(v, g, b):
    mu = jnp.mean(v, axis=-1, keepdims=True)
    d = v - mu
    var = jnp.mean(d * d, axis=-1, keepdims=True)
    return d * lax.rsqrt(var + LN_EPS) * g + b


def _sigmoid(x):
    return 1.0 / (1.0 + jnp.exp(-x))


def _silu(x):
    return x * _sigmoid(x)


def _proj_kernel(x_ref, w_ref, o_ref):
    o_ref[...] = jnp.dot(x_ref[...].astype(BF16), w_ref[...], preferred_element_type=F32)


def _proj(x, w, tm=256):
    n, k = x.shape
    m = w.shape[1]
    return pl.pallas_call(
        _proj_kernel,
        grid=(n // tm,),
        in_specs=[pl.BlockSpec((tm, k), lambda i: (i, 0)),
                  pl.BlockSpec((k, m), lambda i: (0, 0))],
        out_specs=pl.BlockSpec((tm, m), lambda i: (i, 0)),
        out_shape=jax.ShapeDtypeStruct((n, m), F32),
        compiler_params=_cparams(("parallel",)),
        name="proj",
    )(x, w)


def _proj_ln_kernel(ya_ref, yb_ref, wa_ref, wb_ref, x_ref, g_ref, b_ref, o_ref):
    mix = (jnp.dot(ya_ref[...].astype(BF16), wa_ref[...], preferred_element_type=F32)
           + jnp.dot(yb_ref[...].astype(BF16), wb_ref[...], preferred_element_type=F32))
    o_ref[...] = _layer_norm(DN_ALPHA * x_ref[...] + mix, g_ref[...], b_ref[...])


def _proj_ln(ya, yb, w, xres, g, b, tm=256):
    n, ka = ya.shape
    kb = yb.shape[1]
    d = w.shape[1]
    wb16 = w.astype(BF16)
    full = lambda shp: pl.BlockSpec(shp, lambda i: (0, 0))
    return pl.pallas_call(
        _proj_ln_kernel,
        grid=(n // tm,),
        in_specs=[pl.BlockSpec((tm, ka), lambda i: (i, 0)),
                  pl.BlockSpec((tm, kb), lambda i: (i, 0)),
                  full((ka, d)), full((kb, d)),
                  pl.BlockSpec((tm, d), lambda i: (i, 0)),
                  full((1, d)), full((1, d))],
        out_specs=pl.BlockSpec((tm, d), lambda i: (i, 0)),
        out_shape=jax.ShapeDtypeStruct((n, d), F32),
        compiler_params=_cparams(("parallel",)),
        name="proj_ln",
    )(ya, yb, wb16[:ka], wb16[ka:], xres, g.reshape(1, d), b.reshape(1, d))


U16_QA = 0
U16_QM = U16_QA + NSA_HEADS * LANES
U16_KS = U16_QM + MOBA_HEADS * LANES
U16_KW = U16_KS + NSA_GROUPS * LANES
U16_KM = U16_KW + NSA_GROUPS * LANES
U16_VS = U16_KM + MOBA_HEADS * LANES
U16_VW = U16_VS + NSA_GROUPS * LANES
U16_VM = U16_VW + NSA_GROUPS * LANES
U16_W = U16_VM + MOBA_HEADS * LANES
U32_W = 3 * LANES


def _odd_proj_kernel(x_ref, w16_ref, w32_ref, qc_ref, kt_ref, o16_ref, o32_ref, km_ref):
    xb = x_ref[...].astype(BF16)
    acc = jnp.dot(xb, w16_ref[...], preferred_element_type=F32)
    km_ref[0] = jnp.mean(acc[:, U16_KM:U16_VS], axis=0, keepdims=True)
    o16_ref[:, 0:U16_KS] = (acc[:, 0:U16_KS] + qc_ref[...]).astype(BF16)
    o16_ref[:, U16_KS:U16_VS] = (acc[:, U16_KS:U16_VS] + kt_ref[...]).astype(BF16)
    o16_ref[:, U16_VS:U16_W] = acc[:, U16_VS:U16_W].astype(BF16)
    o32_ref[...] = jnp.dot(xb, w32_ref[...], preferred_element_type=F32)


def _odd_proj(x, w_in, t, tm=MOBA_BLOCK):
    n, d = x.shape
    assert tm == MOBA_BLOCK and t % tm == 0
    dh = HEAD_DIM
    nq = NSA_HEADS * dh
    nm = MOBA_HEADS * dh
    sizes = [nq] + [NSA_KV] * 6 + [3 * NSA_HEADS] + [nm] * 3
    offs = np.concatenate([[0], np.cumsum(sizes)])
    q, kc, vc, ks, vs, kw, vw, gates, mq, mk, mv = [w_in[:, offs[i]:offs[i + 1]] for i in range(11)]
    scale = dh ** -0.5

    def heads(w, nh, mult=1.0):
        w = (w * mult).reshape(d, nh, dh)
        return jnp.pad(w, ((0, 0), (0, 0), (0, LANES - dh))).reshape(d, nh * LANES)

    w16 = jnp.concatenate([heads(q, NSA_HEADS, scale), heads(mq, MOBA_HEADS, scale), heads(ks, NSA_GROUPS),
                           heads(kw, NSA_GROUPS), heads(mk, MOBA_HEADS), heads(vs, NSA_GROUPS),
                           heads(vw, NSA_GROUPS), heads(mv, MOBA_HEADS)], axis=1).astype(BF16)
    w32 = jnp.concatenate([kc, vc, jnp.pad(gates, ((0, 0), (0, LANES - 3 * NSA_HEADS)))], axis=1).astype(BF16)

    n_all = NSA_HEADS + MOBA_HEADS
    s_all = 2.0 ** (-8.0 * np.arange(1, n_all + 1) / n_all)
    slopes = jnp.asarray(np.concatenate([s_all[0::2], s_all[1::2]]), F32)
    s_hi = slopes.astype(BF16).astype(F32)
    s_lo = slopes - s_hi
    scols = jnp.stack([POS_SPLIT * s_hi, POS_SPLIT * s_lo, s_hi, s_lo], axis=-1)
    qconst = jnp.pad(scols, ((0, 0), (dh, LANES - dh - 4))).reshape(1, U16_KS)

    pos = np.arange(t)
    pos_cols = np.stack([pos // POS_SPLIT, pos // POS_SPLIT, pos % POS_SPLIT, pos % POS_SPLIT], axis=-1)

    def key_cols(blk):
        cols = np.zeros((t, LANES), np.float32)
        cols[:, dh:dh + 4] = pos_cols
        if blk:
            cols[pos, MASK_COL0 + pos // blk] = 1.0
        return cols

    ktab = np.concatenate([key_cols(SLC_LEN)] * NSA_GROUPS + [key_cols(0)] * NSA_GROUPS
                          + [key_cols(MOBA_BLOCK)] * MOBA_HEADS, axis=1)
    tiles_per_seq = t // tm
    return pl.pallas_call(
        _odd_proj_kernel,
        grid=(n // tm,),
        in_specs=[pl.BlockSpec((tm, d), lambda i: (i, 0)),
                  pl.BlockSpec((d, U16_W), lambda i: (0, 0)),
                  pl.BlockSpec((d, U32_W), lambda i: (0, 0)),
                  pl.BlockSpec((1, U16_KS), lambda i: (0, 0)),
                  pl.BlockSpec((tm, U16_VS - U16_KS), lambda i: (i % tiles_per_seq, 0))],
        out_specs=[pl.BlockSpec((tm, U16_W), lambda i: (i, 0)),
                   pl.BlockSpec((tm, U32_W), lambda i: (i, 0)),
                   pl.BlockSpec((1, 1, U16_VS - U16_KM), lambda i: (i, 0, 0))],
        out_shape=[jax.ShapeDtypeStruct((n, U16_W), BF16), jax.ShapeDtypeStruct((n, U32_W), F32),
                   jax.ShapeDtypeStruct((n // tm, 1, U16_VS - U16_KM), F32)],
        compiler_params=_cparams(("parallel",)),
        name="odd_proj",
    )(x, w16, w32, qconst, jnp.asarray(ktab, F32))


def _odd_out_kernel(oc_ref, os_ref, ow_ref, om_ref, gt_ref, wa_ref, wb_ref, x_ref, g_ref, b_ref, o_ref):
    sg = _sigmoid(gt_ref[...])
    parts = []
    for h in range(NSA_HEADS):
        cols = slice(h * LANES, (h + 1) * LANES)
        parts.append(sg[:, 3 * h:3 * h + 1] * oc_ref[:, cols] + sg[:, 3 * h + 1:3 * h + 2] * os_ref[:, cols]
                     + sg[:, 3 * h + 2:3 * h + 3] * ow_ref[:, cols])
    yc = jnp.concatenate(parts, axis=1)
    mix = _dot(yc, wa_ref[...]) + _dot(om_ref[...], wb_ref[...])
    o_ref[...] = _layer_norm(DN_ALPHA * x_ref[...] + mix, g_ref[...], b_ref[...])


def _odd_out_ln(o_c, o_s, o_w, o_m, u32, w_out, xres, g, b, tm=256):
    n, d = xres.shape
    dh = HEAD_DIM

    def pad_rows(w, nh):
        return jnp.pad(w.reshape(nh, dh, d), ((0, 0), (0, LANES - dh), (0, 0))).reshape(nh * LANES, d).astype(BF16)

    wa = pad_rows(w_out[:NSA_HEADS * dh], NSA_HEADS)
    wb = pad_rows(w_out[NSA_HEADS * dh:], MOBA_HEADS)
    row = lambda width: pl.BlockSpec((tm, width), lambda i: (i, 0))
    full = lambda shp: pl.BlockSpec(shp, lambda i: (0, 0))
    ka = NSA_HEADS * LANES
    kb = MOBA_HEADS * LANES
    return pl.pallas_call(
        _odd_out_kernel,
        grid=(n // tm,),
        in_specs=[row(ka), row(ka), row(ka), row(kb),
                  pl.BlockSpec((tm, LANES), lambda i: (i, U32_W // LANES - 1)),
                  full((ka, d)), full((kb, d)), row(d), full((1, d)), full((1, d))],
        out_specs=row(d),
        out_shape=jax.ShapeDtypeStruct((n, d), F32),
        compiler_params=_cparams(("parallel",)),
        name="odd_out_ln",
    )(o_c, o_s, o_w, o_m, u32, wa, wb, xres, g.reshape(1, d), b.reshape(1, d))


def _lru_kernel(xb_ref, gb_ref, cw_ref, cb_ref, wa_ref, ba_ref, wx_ref, bx_ref, lam_ref,
                o_ref, xs_ref, a_ref, b_ref, h_ref):
    ti = pl.program_id(1)
    tc = xb_ref.shape[1]
    c = xb_ref.shape[2]

    @pl.when(ti == 0)
    def _():
        xs_ref[0:SUBLANES, :] = jnp.zeros((SUBLANES, c), F32)
        h_ref[...] = jnp.zeros_like(h_ref)

    xs_ref[SUBLANES:SUBLANES + tc, :] = xb_ref[0]
    xc = cb_ref[...]
    for w in range(CONV_W):
        xc = xc + cw_ref[w:w + 1, :] * xs_ref[pl.ds(SUBLANES - (CONV_W - 1) + w, tc), :]
    xs_ref[0:SUBLANES, :] = xs_ref[tc:tc + SUBLANES, :]

    r = _sigmoid(_dot(xc, wa_ref[...]) + ba_ref[...])
    i = _sigmoid(_dot(xc, wx_ref[...]) + bx_ref[...])
    nl = -lam_ref[...]
    softplus = jnp.maximum(nl, 0.0) + jnp.log(1.0 + jnp.exp(-jnp.abs(nl)))
    log_a = (-LRU_C * r) * softplus
    a = jnp.exp(log_a)
    mult = jnp.sqrt(1.0 - jnp.exp(2.0 * log_a))
    trow = ti * tc + lax.broadcasted_iota(jnp.int32, (tc, c), 0)
    mult = jnp.where(trow == 0, 1.0, mult)
    a_ref[...] = a
    b_ref[...] = mult * (i * xc)

    row = lax.broadcasted_iota(jnp.int32, (SUBLANES, c), 0)

    def body(g, carry):
        r0 = pl.multiple_of(g * SUBLANES, SUBLANES)
        av = a_ref[pl.ds(r0, SUBLANES), :]
        bv = b_ref[pl.ds(r0, SUBLANES), :]
        for d in (1, 2, 4):
            a_sh = pltpu.roll(av, d, 0)
            b_sh = pltpu.roll(bv, d, 0)
            m = row >= d
            bv = jnp.where(m, av * b_sh + bv, bv)
            av = jnp.where(m, av * a_sh, av)
        h = av * carry + bv
        b_ref[pl.ds(r0, SUBLANES), :] = h
        return h[SUBLANES - 1:SUBLANES, :]

    h_ref[...] = lax.fori_loop(0, tc // SUBLANES, body, h_ref[...])
    o_ref[0] = b_ref[...] * jax.nn.gelu(gb_ref[0])


def _lru(u3, conv_w, conv_b, wa_bd, ba, wx_bd, bx, lam, tc=256):
    b, t, _ = u3.shape
    c = LRU_WIDTH
    row = lambda v: v.reshape(1, c)
    full = lambda shp: pl.BlockSpec(shp, lambda bi, ti: (0, 0))
    return pl.pallas_call(
        _lru_kernel,
        grid=(b, t // tc),
        in_specs=[pl.BlockSpec((1, tc, c), lambda bi, ti: (bi, ti, 0)),
                  pl.BlockSpec((1, tc, c), lambda bi, ti: (bi, ti, 1)),
                  full((CONV_W, c)), full((1, c)), full((c, c)), full((1, c)),
                  full((c, c)), full((1, c)), full((1, c))],
        out_specs=pl.BlockSpec((1, tc, c), lambda bi, ti: (bi, ti, 0)),
        out_shape=jax.ShapeDtypeStruct((b, t, c), F32),
        scratch_shapes=[pltpu.VMEM((tc + SUBLANES, c), F32), pltpu.VMEM((tc, c), F32),
                        pltpu.VMEM((tc, c), F32), pltpu.VMEM((1, c), F32)],
        compiler_params=_cparams(("parallel", "arbitrary")),
        name="rg_lru",
    )(u3, u3, conv_w, row(conv_b), wa_bd, row(ba), wx_bd, row(bx), row(lam))


def _hgrn_kernel(q_ref, f_ref, v_ref, g_ref, lb_ref, ng_ref, tri_ref, ones_ref, o_ref,
                 st_ref, q_s, k_s, c_s, qe_s, kd_s, dl_s, o_s):
    ti = pl.program_id(2)
    tc = q_ref.shape[1]
    dk = HG_DK
    nh = q_ref.shape[2] // dk

    @pl.when(ti == 0)
    def _():
        st_ref[...] = jnp.zeros_like(st_ref)

    lbv = lb_ref[...]
    q = _silu(q_ref[0])
    f = lbv + (1.0 - lbv) * _sigmoid(f_ref[0])
    kk = 1.0 - f
    logf = jnp.log(f)
    parts = _split3(logf)
    tri = tri_ref[...]
    ones = ones_ref[...]
    cum = sum(jnp.dot(tri, p, preferred_element_type=F32) for p in parts)
    last = sum(jnp.dot(ones, p, preferred_element_type=F32) for p in parts)
    q_s[...] = q
    k_s[...] = kk
    c_s[...] = cum
    qe_s[...] = q * jnp.exp(cum)
    kd_s[...] = kk * jnp.exp(last - cum)
    dl_s[...] = jnp.exp(last)

    s_idx = lax.broadcasted_iota(jnp.int32, (HG_SUB, dk), 0)

    def chunk(ci, _):
        r0 = pl.multiple_of(ci * HG_SUB, HG_SUB)
        rows = pl.ds(r0, HG_SUB)
        for h in range(nh):
            cols = slice(h * dk, (h + 1) * dk)
            st = st_ref[h]
            vc = v_ref[0, rows, cols]
            qc = q_s[rows, cols]
            kc = k_s[rows, cols]
            cc = c_s[rows, cols]
            o = _dot_nt(qe_s[rows, cols], st)
            for t in range(HG_SUB):
                n = (t // SUBLANES + 1) * SUBLANES
                d = cc[t:t + 1, :] - cc[0:n, :]
                dec = jnp.exp(jnp.where(s_idx[0:n, :] <= t, d, NEG))
                w = (qc[t:t + 1, :] * kc[0:n, :]) * dec
                a_t = jnp.sum(w, axis=1, keepdims=True)
                o_t = jnp.sum(a_t * vc[0:n, :], axis=0, keepdims=True)
                o = o + jnp.where(s_idx == t, o_t, 0.0)
            o_s[rows, cols] = o
            st_ref[h] = st * dl_s[pl.ds(r0, 1), cols] + _dot_tn(vc, kd_s[rows, cols])
        return 0

    lax.fori_loop(0, tc // HG_SUB, chunk, 0, unroll=2)
    gate = _sigmoid(g_ref[0])
    for h in range(nh):
        cols = slice(h * dk, (h + 1) * dk)
        o = o_s[:, cols]
        o = o * lax.rsqrt(jnp.mean(o * o, axis=-1, keepdims=True) + 1e-6) * ng_ref[:, cols]
        o_ref[0, :, cols] = o * gate[:, cols]


def _hgrn2(u3, lb, norm_g, tc=256, nh=4):
    b, t, _ = u3.shape
    w = nh * HG_DK
    base = (2 * LRU_WIDTH) // w
    per = HG_FDIM // w
    idx = np.arange(tc)
    same = (idx[:, None] // HG_SUB) == (idx[None, :] // HG_SUB)
    tri = jnp.asarray(same & (idx[None, :] <= idx[:, None]), BF16)
    ones = jnp.asarray(same, BF16)

    def col(off):
        return pl.BlockSpec((1, tc, w), lambda bi, hi, ti: (bi, ti, base + off * per + hi))

    vec = pl.BlockSpec((1, w), lambda bi, hi, ti: (0, hi))
    cst = pl.BlockSpec((tc, tc), lambda bi, hi, ti: (0, 0))
    return pl.pallas_call(
        _hgrn_kernel,
        grid=(b, per, t // tc),
        in_specs=[col(0), col(1), col(2), col(3), vec, vec, cst, cst],
        out_specs=pl.BlockSpec((1, tc, w), lambda bi, hi, ti: (bi, ti, hi)),
        out_shape=jax.ShapeDtypeStruct((b, t, HG_VDIM), F32),
        scratch_shapes=[pltpu.VMEM((nh, HG_DK, HG_DK), F32)] + [pltpu.VMEM((tc, w), F32)] * 7,
        compiler_params=_cparams(("parallel", "parallel", "arbitrary")),
        name="hgrn2",
    )(u3, u3, u3, u3, lb.reshape(1, HG_FDIM), norm_g.reshape(1, HG_VDIM), tri, ones)


def _nsa_cmp_kv_kernel(zk_ref, zv_ref, pek_ref, pev_ref, wk1_ref, wk2_ref, wv1_ref, wv2_ref,
                       ok_ref, ov_ref):
    half = zk_ref.shape[3]
    nrow = zk_ref.shape[2]

    def compress(z, pe_ref, w1_ref, w2_ref):
        lo = _dot(z + pe_ref[0:1, :], w1_ref[0:half, :])
        hi = _dot(z + pe_ref[1:2, :], w1_ref[half:2 * half, :])
        pre = lo + pltpu.roll(hi, nrow - 1, 0)
        return _dot(jax.nn.gelu(pre), w2_ref[...])

    ok_ref[0, 0] = compress(zk_ref[0, 0], pek_ref, wk1_ref, wk2_ref)
    ov_ref[0, 0] = compress(zv_ref[0, 0], pev_ref, wv1_ref, wv2_ref)


def _nsa_compress(kc, vc, pe_k, pe_v, wk1, wk2, wv1, wv2):
    b, g, t, dh = kc.shape
    nrow = t // CMP_STRIDE
    half = CMP_STRIDE * dh
    zk = kc.reshape(b, g, nrow, half)
    zv = vc.reshape(b, g, nrow, half)
    zspec = pl.BlockSpec((1, 1, nrow, half), lambda bi, gi: (bi, gi, 0, 0))
    full = lambda shp: pl.BlockSpec(shp, lambda bi, gi: (0, 0))
    ospec = pl.BlockSpec((1, 1, nrow, LANES), lambda bi, gi: (bi, gi, 0, 0))
    oshape = jax.ShapeDtypeStruct((b, g, nrow, LANES), F32)
    pad = lambda w: jnp.pad(w, ((0, 0), (0, LANES - dh))).astype(BF16)
    return pl.pallas_call(
        _nsa_cmp_kv_kernel,
        grid=(b, g),
        in_specs=[zspec, zspec, full((2, half)), full((2, half)),
                  full((2 * half, CMP_HIDDEN)), full((CMP_HIDDEN, LANES)),
                  full((2 * half, CMP_HIDDEN)), full((CMP_HIDDEN, LANES))],
        out_specs=[ospec, ospec],
        out_shape=[oshape, oshape],
        compiler_params=_cparams(("parallel", "parallel")),
        name="nsa_compress",
    )(zk, zv, pe_k.reshape(2, half), pe_v.reshape(2, half),
      wk1.astype(BF16), pad(wk2), wv1.astype(BF16), pad(wv2))


def _nsa_sel_kernel(slope_ref, q_ref, kc_ref, vc_ref, ov_ref, o_ref, mb_ref):
    gi = pl.program_id(0) % NSA_GROUPS
    qi = pl.program_id(1)
    tq = q_ref.shape[1]
    hpg = q_ref.shape[2] // LANES
    nc = kc_ref.shape[2]
    ns = ov_ref.shape[0]

    q = jnp.concatenate([q_ref[0, :, i * LANES:(i + 1) * LANES] for i in range(hpg)], axis=0)
    khi, klo = _split2(kc_ref[0, 0])
    s = _dot_nt(q, khi) + _dot_nt(q, klo)
    tpos = qi * tq + lax.broadcasted_iota(jnp.int32, (tq, nc), 0)
    cend = lax.broadcasted_iota(jnp.int32, (tq, nc), 1) * CMP_STRIDE + (CMP_LEN - 1)
    d_c = (tpos - cend).astype(F32)
    valid = d_c >= 0.0
    ps = []
    for i in range(hpg):
        si = s[i * tq:(i + 1) * tq, :] - slope_ref[gi * hpg + i] * d_c
        si = jnp.where(valid, si, NEG)
        m = jnp.max(si, axis=-1, keepdims=True)
        p = jnp.where(valid, jnp.exp(si - m), 0.0)
        ps.append(p / jnp.maximum(jnp.sum(p, axis=-1, keepdims=True), 1e-30))
    oc = _dot(jnp.concatenate(ps, axis=0), vc_ref[0, 0])
    o_ref[0] = jnp.concatenate([oc[i * tq:(i + 1) * tq, :] for i in range(hpg)], axis=1)

    psum = ps[0]
    for i in range(1, hpg):
        psum = psum + ps[i]
    phi, plo = _split2(psum)
    ovl = ov_ref[...]
    imp = _dot_nt(ovl, phi) + _dot_nt(ovl, plo)

    j = lax.broadcasted_iota(jnp.int32, (ns, tq), 0)
    own = (qi * tq + lax.broadcasted_iota(jnp.int32, (ns, tq), 1)) // SLC_LEN
    forced = (j == 0) | (j == own) | (j == own - 1)
    imp = jnp.where(j <= own, imp + FORCE_BONUS * forced.astype(F32), NEG)
    rank = jnp.zeros((ns, tq), F32)
    for jp in range(ns):
        row = imp[jp:jp + 1, :]
        beats = (row > imp) | ((row == imp) & (j > jp))
        rank = rank + beats.astype(F32)
    bias = jnp.where(rank < float(min(SLC_TOPN, ns)), 0.0, MASKED)
    place = (lax.broadcasted_iota(jnp.int32, (ns, LANES), 1)
             == lax.broadcasted_iota(jnp.int32, (ns, LANES), 0) + MASK_COL0)
    mb_ref[0] = _dot_tn(bias, place.astype(F32)).astype(BF16)


def _nsa_select(slopes, u16, k_cmp, v_cmp, tq=512):
    b, t, _ = u16.shape
    g = NSA_GROUPS
    hpg = NSA_HPG
    nc = k_cmp.shape[2]
    ns = t // SLC_LEN
    ci = np.arange(nc)[:, None]
    cj = np.arange(ns)[None, :]
    ovl = ((CMP_STRIDE * ci < SLC_LEN * (cj + 1)) & (CMP_STRIDE * ci + CMP_LEN > SLC_LEN * cj))
    ovl = ovl & (ci < (t - CMP_LEN) // CMP_STRIDE + 1)
    grid_spec = pltpu.PrefetchScalarGridSpec(
        num_scalar_prefetch=1,
        grid=(b * g, t // tq),
        in_specs=[pl.BlockSpec((1, tq, hpg * LANES), lambda bg, qi, s: (bg // g, qi, U16_QA // (hpg * LANES) + bg % g)),
                  pl.BlockSpec((1, 1, nc, LANES), lambda bg, qi, s: (bg // g, bg % g, 0, 0)),
                  pl.BlockSpec((1, 1, nc, LANES), lambda bg, qi, s: (bg // g, bg % g, 0, 0)),
                  pl.BlockSpec((ns, nc), lambda bg, qi, s: (0, 0))],
        out_specs=[pl.BlockSpec((1, tq, hpg * LANES), lambda bg, qi, s: (bg // g, qi, bg % g)),
                   pl.BlockSpec((1, tq, LANES), lambda bg, qi, s: (bg // g, qi, bg % g))],
    )
    return pl.pallas_call(
        _nsa_sel_kernel,
        grid_spec=grid_spec,
        out_shape=[jax.ShapeDtypeStruct((b, t, g * hpg * LANES), F32),
                   jax.ShapeDtypeStruct((b, t, g * LANES), BF16)],
        compiler_params=_cparams(("parallel", "parallel")),
        name="nsa_select",
    )(slopes, u16, k_cmp, v_cmp, jnp.asarray(ovl.T, BF16))


POS_SPLIT = 64
MASKED = -2e30


MASK_COL0 = HEAD_DIM + 4
KEY_STEP = 512


def _flash_kernel(q_ref, k_ref, v_ref, *rest, mode, nkv, tq):
    o_ref = rest[-1]
    hpg = q_ref.shape[2] // (LANES * nkv)
    t_all = k_ref.shape[1]
    nt = q_ref.shape[1] // tq
    whole = nt * tq == t_all
    step = MOBA_BLOCK if mode == "moba" else KEY_STEP

    def tile(ti, q0):
        rows = slice(ti * tq, (ti + 1) * tq)
        qs = []
        for j in range(nkv):
            kv_cols = slice(j * LANES, (j + 1) * LANES)
            heads = [q_ref[0, rows, (j * hpg + i) * LANES:(j * hpg + i + 1) * LANES] for i in range(hpg)]
            if mode == "slc":
                heads = [h + rest[0][0, rows, kv_cols] for h in heads]
            q = heads[0] if hpg == 1 else jnp.concatenate(heads, axis=0)
            if mode == "moba":
                kmean = rest[0][0, :, kv_cols]
                nb = kmean.shape[0]
                mhi, mlo = _split2(kmean)
                gate = _dot_nt(mhi, q) + _dot_nt(mlo, q)
                jb = lax.broadcasted_iota(jnp.int32, (nb, tq), 0)
                ob = q0 // MOBA_BLOCK
                gate = jnp.where(jb < ob, gate, NEG)
                rank = jnp.zeros((nb, tq), F32)
                for jp in range(nb):
                    row = gate[jp:jp + 1, :]
                    beats = (row > gate) | ((row == gate) & (jb > jp))
                    rank = rank + beats.astype(F32)
                n_sel = float(min(MOBA_TOPK, nb - 1))
                bsel = ((rank < n_sel) & (jb < ob)) | (jb == ob)
                bias = jnp.where(bsel, 0.0, MASKED)
                place = (lax.broadcasted_iota(jnp.int32, (nb, LANES), 1)
                         == lax.broadcasted_iota(jnp.int32, (nb, LANES), 0) + MASK_COL0)
                q = q + _dot_tn(bias, place.astype(F32)).astype(BF16)
            qs.append(q)

        def attend(kstart, klen, diag, off):
            rel = lax.broadcasted_iota(jnp.int32, (tq, diag), 0) - lax.broadcasted_iota(jnp.int32, (tq, diag), 1)
            valid = rel >= off
            if mode == "win":
                valid = valid & (rel < off + WIN)
            if hpg > 1:
                valid = jnp.concatenate([valid] * hpg, axis=0)
            for j in range(nkv):
                kv_cols = slice(j * LANES, (j + 1) * LANES)
                s = _dot_nt(qs[j], k_ref[0, pl.ds(kstart, klen), kv_cols])
                sd = jnp.where(valid, s[:, klen - diag:], MASKED)
                s = sd if diag == klen else jnp.concatenate([s[:, :klen - diag], sd], axis=1)
                m = jnp.maximum(jnp.max(s, axis=-1, keepdims=True), NEG)
                p = jnp.exp(s - m)
                l = jnp.sum(p, axis=-1, keepdims=True)
                o = _dot(p, v_ref[0, pl.ds(kstart, klen), kv_cols]) / jnp.maximum(l, 1e-30)
                for i in range(hpg):
                    o_ref[0, rows, (j * hpg + i) * LANES:(j * hpg + i + 1) * LANES] = o[i * tq:(i + 1) * tq, :]

        if mode == "win":
            kstart = max(q0 - WIN, 0) if whole else pl.multiple_of(jnp.maximum(q0 - WIN, 0), LANES)
            attend(kstart, WIN + tq, WIN + tq, kstart - q0)
        elif whole:
            v = (q0 + tq - 1) // step
            attend(0, step * (v + 1), step, step * v - q0)
        else:
            var = (q0 + tq - 1) // step
            for v in range(t_all // step):
                @pl.when(var == v)
                def _(v=v):
                    attend(0, step * (v + 1), step, step * v - q0)

    for ti in range(nt):
        tile(ti, ti * tq if whole else (pl.program_id(1) * nt + ti) * tq)


def _flash(mode, u16, extra=None, tq=128, nkv=1, tiles_per_step=1):
    b, t, _ = u16.shape
    q_off, k_off, v_off, n_kvh, hpg = {
        "slc": (U16_QA, U16_KS, U16_VS, NSA_GROUPS, NSA_HPG),
        "win": (U16_QA, U16_KW, U16_VW, NSA_GROUPS, NSA_HPG),
        "moba": (U16_QM, U16_KM, U16_VM, MOBA_HEADS, 1)}[mode]
    assert mode != "moba" or tq == MOBA_BLOCK
    steps = n_kvh // nkv
    qw = nkv * hpg * LANES
    kw = nkv * LANES
    tb = tq * tiles_per_step
    in_specs = [pl.BlockSpec((1, tb, qw), lambda bh, qi: (bh // steps, qi, q_off // qw + bh % steps)),
                pl.BlockSpec((1, t, kw), lambda bh, qi: (bh // steps, 0, k_off // kw + bh % steps)),
                pl.BlockSpec((1, t, kw), lambda bh, qi: (bh // steps, 0, v_off // kw + bh % steps))]
    args = [u16, u16, u16]
    if mode == "slc":
        in_specs.append(pl.BlockSpec((1, tb, kw), lambda bh, qi: (bh // steps, qi, bh % steps)))
        args.append(extra)
    elif mode == "moba":
        in_specs.append(pl.BlockSpec((1, t // MOBA_BLOCK, kw), lambda bh, qi: (bh // steps, 0, bh % steps)))
        args.append(extra)
    return pl.pallas_call(
        functools.partial(_flash_kernel, mode=mode, nkv=nkv, tq=tq),
        grid=(b * steps, t // tb),
        in_specs=in_specs,
        out_specs=pl.BlockSpec((1, tb, qw), lambda bh, qi: (bh // steps, qi, bh % steps)),
        out_shape=jax.ShapeDtypeStruct((b, t, n_kvh * hpg * LANES), F32),
        compiler_params=_cparams(("parallel", "parallel")),
        name="flash_" + mode,
    )(*args)


def _router_kernel(x_ref, whi_ref, wlo_ref, b_ref, triu_ref, meta_ref, cnt_ref, carry_ref):
    step = pl.program_id(0)
    tm = x_ref.shape[0]
    ne = whi_ref.shape[0]
    gsz = ne // N_GROUPS

    @pl.when(step == 0)
    def _():
        carry_ref[...] = jnp.zeros_like(carry_ref)

    xhi, xlo = _split2(x_ref[...])
    whi = whi_ref[...]
    logits = _dot_nt(whi, xhi) + _dot_nt(wlo_ref[...], xhi) + _dot_nt(whi, xlo)
    scores = _sigmoid(logits)
    biased = scores + b_ref[:, 0:1]

    v3 = biased.reshape(N_GROUPS, gsz, tm)
    i3 = lax.broadcasted_iota(jnp.int32, (N_GROUPS, gsz, tm), 1).astype(F32)
    m1 = jnp.max(v3, axis=1, keepdims=True)
    idx1 = jnp.min(jnp.where(v3 == m1, i3, float(gsz)), axis=1, keepdims=True)
    m2 = jnp.max(jnp.where(i3 == idx1, NEG_INF, v3), axis=1, keepdims=True)
    gs = (m1 + m2).reshape(N_GROUPS, tm)
    gi = lax.broadcasted_iota(jnp.int32, (N_GROUPS, tm), 0).astype(F32)
    gmask = jnp.zeros((N_GROUPS, tm), F32)
    for _ in range(TOPK_GROUPS):
        m = jnp.max(gs, axis=0, keepdims=True)
        pick = jnp.min(jnp.where(gs == m, gi, float(N_GROUPS)), axis=0, keepdims=True)
        hit = gi == pick
        gmask = jnp.where(hit, 1.0, gmask)
        gs = jnp.where(hit, NEG_INF, gs)
    emask = jnp.broadcast_to(gmask.reshape(N_GROUPS, 1, tm), (N_GROUPS, gsz, tm)).reshape(ne, tm)
    cur = jnp.where(emask > 0.5, biased, NEG)

    ei = lax.broadcasted_iota(jnp.int32, (ne, tm), 0).astype(F32)
    selm = jnp.zeros((ne, tm), F32)
    idxs, svals = [], []
    for _ in range(TOP_K):
        m = jnp.max(cur, axis=0, keepdims=True)
        idx = jnp.min(jnp.where(cur == m, ei, float(ne)), axis=0, keepdims=True)
        hit = ei == idx
        svals.append(jnp.sum(jnp.where(hit, scores, 0.0), axis=0, keepdims=True))
        idxs.append(idx)
        cur = jnp.where(hit, NEG_INF, cur)
        selm = jnp.where(hit, 1.0, selm)

    carry = carry_ref[:, 0:1]
    rank_full = jnp.dot(selm.astype(BF16), triu_ref[...], preferred_element_type=F32) + carry
    ssum = svals[0]
    for k in range(1, TOP_K):
        ssum = ssum + svals[k]
    for k in range(TOP_K):
        meta_ref[k:k + 1, :] = idxs[k]
        meta_ref[TOP_K + k:TOP_K + k + 1, :] = svals[k] / ssum * ROUTED_SCALE
        meta_ref[2 * TOP_K + k:2 * TOP_K + k + 1, :] = jnp.sum(
            jnp.where(ei == idxs[k], rank_full, 0.0), axis=0, keepdims=True)
    new_carry = carry + jnp.sum(selm, axis=1, keepdims=True)
    carry_ref[...] = jnp.broadcast_to(new_carry, carry_ref.shape)
    cnt_ref[...] = jnp.broadcast_to(new_carry, cnt_ref.shape)


def _router(x, w_router, b_router, tm=512):
    n, d = x.shape
    ne = w_router.shape[1]
    wt = w_router.T
    whi = wt.astype(BF16)
    wlo = (wt - whi.astype(F32)).astype(BF16)
    triu = jnp.asarray(np.arange(tm)[:, None] < np.arange(tm)[None, :], BF16)
    meta, cnt = pl.pallas_call(
        _router_kernel,
        grid=(n // tm,),
        in_specs=[pl.BlockSpec((tm, d), lambda i: (i, 0)),
                  pl.BlockSpec((ne, d), lambda i: (0, 0)),
                  pl.BlockSpec((ne, d), lambda i: (0, 0)),
                  pl.BlockSpec((ne, LANES), lambda i: (0, 0)),
                  pl.BlockSpec((tm, tm), lambda i: (0, 0))],
        out_specs=[pl.BlockSpec((3 * TOP_K, tm), lambda i: (0, i)),
                   pl.BlockSpec((ne, LANES), lambda i: (0, 0))],
        out_shape=[jax.ShapeDtypeStruct((3 * TOP_K, n), F32),
                   jax.ShapeDtypeStruct((ne, LANES), F32)],
        scratch_shapes=[pltpu.VMEM((ne, LANES), F32)],
        compiler_params=_cparams(("arbitrary",)),
        name="moe_router",
    )(x, whi, wlo, jnp.broadcast_to(b_router.reshape(ne, 1), (ne, LANES)), triu)
    return meta, cnt[:, 0]


def _expert_kernel(be_ref, first_ref, nxt_ref, slot_ref, nu_ref, xs_ref, wg_hbm, wu_hbm, wd_hbm, o_ref,
                   wg_buf, wu_buf, wd_buf, wgu16, wd16, sem, *, layer):
    bi = pl.program_id(0)

    def weight_copies(e, s):
        return [pltpu.make_async_copy(hbm.at[layer, e], buf.at[s], sem.at[s])
                for hbm, buf in ((wg_hbm, wg_buf), (wu_hbm, wu_buf), (wd_hbm, wd_buf))]

    @pl.when(bi < nu_ref[0])
    def _():
        s = slot_ref[bi]

        @pl.when(first_ref[bi] == 1)
        def _():
            @pl.when(bi == 0)
            def _():
                for c in weight_copies(be_ref[0], s):
                    c.start()

            @pl.when(nxt_ref[bi] >= 0)
            def _():
                for c in weight_copies(nxt_ref[bi], 1 - s):
                    c.start()

            for c in weight_copies(be_ref[bi], s):
                c.wait()
            hd = wd16.shape[0]
            wgu16[:, 0:hd] = wg_buf[s].astype(BF16)
            wgu16[:, hd:2 * hd] = wu_buf[s].astype(BF16)
            wd16[...] = wd_buf[s].astype(BF16)

        xa, xb = _unpack_bf16_pair(xs_ref[...])
        half = xa.shape[1]

        def xw(w_ref):
            return (jnp.dot(xa, w_ref[0:half, :], preferred_element_type=F32)
                    + jnp.dot(xb, w_ref[half:2 * half, :], preferred_element_type=F32))

        gu = xw(wgu16)
        hd = wd16.shape[0]
        h = _silu(gu[:, 0:hd]) * gu[:, hd:2 * hd]
        o_ref[...] = _pack_bf16_pair(jnp.dot(h.astype(BF16), wd16[...], preferred_element_type=F32))

    @pl.when(bi >= nu_ref[0])
    def _():
        o_ref[...] = jnp.zeros_like(o_ref)


def _expert_ffn(blk_e, n_used, xs, layer, w_gate, w_up, w_down):
    n_rows = xs.shape[0]
    d = w_gate.shape[2]
    hdim = w_gate.shape[3]
    n_blk = n_rows // EXPERT_ROWS
    idx = jnp.arange(n_blk, dtype=jnp.int32)
    first = jnp.concatenate([jnp.ones((1,), bool), blk_e[1:] != blk_e[:-1]]) & (idx < n_used[0])
    slot = ((jnp.cumsum(first.astype(jnp.int32)) - 1) % 2).astype(jnp.int32)
    nxt_first = lax.cummin(jnp.where(first, idx, n_blk), reverse=True)
    nxt_first = jnp.concatenate([nxt_first[1:], jnp.full((1,), n_blk, jnp.int32)])
    nxt_e = jnp.where(nxt_first < n_blk, blk_e[jnp.minimum(nxt_first, n_blk - 1)], -1).astype(jnp.int32)
    grid_spec = pltpu.PrefetchScalarGridSpec(
        num_scalar_prefetch=5,
        grid=(n_blk,),
        in_specs=[pl.BlockSpec((EXPERT_ROWS, d // 2),
                               lambda i, be, fi, nx, sl, nu: (jnp.minimum(i, jnp.maximum(nu[0] - 1, 0)), 0)),
                  pl.BlockSpec(memory_space=pl.ANY),
                  pl.BlockSpec(memory_space=pl.ANY),
                  pl.BlockSpec(memory_space=pl.ANY)],
        out_specs=pl.BlockSpec((EXPERT_ROWS, d // 2), lambda i, be, fi, nx, sl, nu: (i, 0)),
        scratch_shapes=[pltpu.VMEM((2, d, hdim), F32), pltpu.VMEM((2, d, hdim), F32), pltpu.VMEM((2, hdim, d), F32),
                        pltpu.VMEM((d, 2 * hdim), BF16), pltpu.VMEM((hdim, d), BF16),
                        pltpu.SemaphoreType.DMA((2,))],
    )
    return pl.pallas_call(
        functools.partial(_expert_kernel, layer=layer),
        grid_spec=grid_spec,
        out_shape=jax.ShapeDtypeStruct((n_rows, d // 2), jnp.uint32),
        compiler_params=_cparams(("arbitrary",)),
        name="moe_experts",
    )(blk_e, first.astype(jnp.int32), nxt_e, slot, n_used, xs, w_gate, w_up, w_down)


def _dest_kernel(meta_ref, pstart_ref, dest_ref):
    ne = pstart_ref.shape[0]
    tm = meta_ref.shape[1]
    ei = lax.broadcasted_iota(jnp.int32, (ne, tm), 0).astype(F32)
    ps = pstart_ref[:, 0:1]
    for k in range(TOP_K):
        base = jnp.sum(jnp.where(ei == meta_ref[k:k + 1, :], ps, 0.0), axis=0, keepdims=True)
        dest_ref[k:k + 1, :] = (base + meta_ref[2 * TOP_K + k:2 * TOP_K + k + 1, :]).astype(jnp.int32)


def _dest_rows(meta, pstart, tm=512):
    n = meta.shape[1]
    ne = pstart.shape[0]
    return pl.pallas_call(
        _dest_kernel,
        grid=(n // tm,),
        in_specs=[pl.BlockSpec((3 * TOP_K, tm), lambda i: (0, i)),
                  pl.BlockSpec((ne, LANES), lambda i: (0, 0))],
        out_specs=pl.BlockSpec((TOP_K, tm), lambda i: (0, i)),
        out_shape=jax.ShapeDtypeStruct((TOP_K, n), jnp.int32),
        compiler_params=_cparams(("parallel",)),
        name="moe_dest",
    )(meta, jnp.broadcast_to(pstart.astype(F32).reshape(ne, 1), (ne, LANES)))


def _pack_bf16_pair(x):
    half = x.shape[1] // 2
    hi = pltpu.bitcast(x[:, 0:half].astype(BF16).astype(F32), jnp.uint32)
    lo = pltpu.bitcast(x[:, half:2 * half].astype(BF16).astype(F32), jnp.uint32)
    return hi | (lo >> 16)


def _unpack_bf16_pair(p):
    hi = pltpu.bitcast(p & jnp.uint32(0xFFFF0000), F32).astype(BF16)
    lo = pltpu.bitcast(p << 16, F32).astype(BF16)
    return hi, lo


def _dispatch_kernel(dest_ref, x_ref, xs_in_ref, xs_ref, pk_ref, sem):
    del xs_in_ref
    tm = x_ref.shape[0]
    pk_ref[...] = _pack_bf16_pair(x_ref[...])

    def row_copy(t, k):
        return pltpu.make_async_copy(pk_ref.at[pl.ds(t, 1), :], xs_ref.at[pl.ds(dest_ref[t * TOP_K + k], 1), :], sem)

    def body(t, _):
        for k in range(TOP_K):
            row_copy(t, k).start(priority=k % 2)
        return 0

    lax.fori_loop(0, tm, body, 0, unroll=4)
    done = xs_ref.at[pl.ds(0, TOP_K * tm), :]
    pltpu.make_async_copy(done, done, sem).wait()


def _dispatch(dest, x, n_rows, tm=256):
    n, d = x.shape
    return pl.pallas_call(
        _dispatch_kernel,
        grid=(n // tm,),
        in_specs=[pl.BlockSpec((TOP_K * tm,), lambda i: (i,), memory_space=pltpu.SMEM),
                  pl.BlockSpec((tm, d), lambda i: (i, 0)),
                  pl.BlockSpec(memory_space=pl.ANY)],
        out_specs=pl.BlockSpec(memory_space=pl.ANY),
        out_shape=jax.ShapeDtypeStruct((n_rows, d // 2), jnp.uint32),
        scratch_shapes=[pltpu.VMEM((tm, d // 2), jnp.uint32), pltpu.SemaphoreType.DMA(())],
        input_output_aliases={2: 0},
        compiler_params=_cparams(("arbitrary",)),
        name="moe_dispatch",
    )(dest, x, jnp.zeros((n_rows, d // 2), jnp.uint32))


def _combine_ln_kernel(dest_ref, dest_next_ref, w_ref, x_ref, ys_ref, wg_ref, wu_ref, wd_ref, g_ref, b_ref,
                       o_ref, buf, sem):
    step = pl.program_id(0)
    tm = x_ref.shape[0]
    slot = step % 2

    def gather_rows(idx_ref, to_slot):
        def body(t, _):
            for k in range(TOP_K):
                pltpu.make_async_copy(ys_ref.at[pl.ds(idx_ref[t * TOP_K + k], 1), :],
                                      buf.at[to_slot, k, pl.ds(t, 1), :], sem.at[to_slot]).start(priority=k % 2)
            return 0

        lax.fori_loop(0, tm, body, 0, unroll=4)

    @pl.when(step == 0)
    def _():
        gather_rows(dest_ref, 0)

    @pl.when(step + 1 < pl.num_programs(0))
    def _():
        gather_rows(dest_next_ref, 1 - slot)

    x = x_ref[...]
    xb = x.astype(BF16)
    h = _silu(jnp.dot(xb, wg_ref[...], preferred_element_type=F32)) * jnp.dot(
        xb, wu_ref[...], preferred_element_type=F32)
    ff = jnp.dot(h.astype(BF16), wd_ref[...], preferred_element_type=F32)
    pltpu.make_async_copy(buf.at[slot], buf.at[slot], sem.at[slot]).wait()
    routed_a = jnp.zeros((tm, buf.shape[3]), F32)
    routed_b = jnp.zeros((tm, buf.shape[3]), F32)
    for k in range(TOP_K):
        ya, yb = _unpack_bf16_pair(buf[slot, k])
        routed_a = routed_a + ya.astype(F32) * w_ref[:, k:k + 1]
        routed_b = routed_b + yb.astype(F32) * w_ref[:, k:k + 1]
    ff = ff + jnp.concatenate([routed_a, routed_b], axis=1)
    o_ref[...] = _layer_norm(DN_ALPHA * x + ff, g_ref[...], b_ref[...])


def _combine_ln(dest, wts, x, ys, ws_gate, ws_up, ws_down, g, b, tm=256):
    n, d = x.shape
    hdim = ws_gate.shape[1]
    row = pl.BlockSpec((tm, d), lambda i: (i, 0))
    last = n // tm - 1
    return pl.pallas_call(
        _combine_ln_kernel,
        grid=(n // tm,),
        in_specs=[pl.BlockSpec((TOP_K * tm,), lambda i: (i,), memory_space=pltpu.SMEM),
                  pl.BlockSpec((TOP_K * tm,), lambda i: (jnp.minimum(i + 1, last),), memory_space=pltpu.SMEM),
                  pl.BlockSpec((tm, TOP_K), lambda i: (i, 0)),
                  row,
                  pl.BlockSpec(memory_space=pl.ANY),
                  pl.BlockSpec((d, hdim), lambda i: (0, 0)),
                  pl.BlockSpec((d, hdim), lambda i: (0, 0)),
                  pl.BlockSpec((hdim, d), lambda i: (0, 0)),
                  pl.BlockSpec((1, d), lambda i: (0, 0)),
                  pl.BlockSpec((1, d), lambda i: (0, 0))],
        out_specs=row,
        out_shape=jax.ShapeDtypeStruct((n, d), F32),
        scratch_shapes=[pltpu.VMEM((2, TOP_K, tm, d // 2), jnp.uint32), pltpu.SemaphoreType.DMA((2,))],
        compiler_params=_cparams(("arbitrary",)),
        name="moe_combine_ln",
    )(dest, dest, wts, x, ys, ws_gate.astype(BF16), ws_up.astype(BF16), ws_down.astype(BF16),
      g.reshape(1, d), b.reshape(1, d))


def _moe_ln(x, layer, w_router, b_router, w_gate, w_up, w_down, ws_gate, ws_up, ws_down, g, b):
    n, d = x.shape
    ne = w_router.shape[1]
    meta, counts = _router(x, w_router, b_router)
    counts = counts.astype(jnp.int32)
    padded = (counts + EXPERT_ROWS - 1) // EXPERT_ROWS * EXPERT_ROWS
    pend = jnp.cumsum(padded)
    dest = _dest_rows(meta, pend - padded).T.reshape(-1)
    n_blk = (n * TOP_K) // EXPERT_ROWS + ne
    blk_row0 = jnp.arange(n_blk, dtype=jnp.int32) * EXPERT_ROWS
    blk_e = jnp.minimum(jnp.sum((pend[None, :] <= blk_row0[:, None]).astype(jnp.int32), axis=1), ne - 1)
    n_used = (pend[-1] // EXPERT_ROWS).astype(jnp.int32).reshape(1)
    xs = _dispatch(dest, x, n_blk * EXPERT_ROWS)
    ys = _expert_ffn(blk_e, n_used, xs, layer, w_gate, w_up, w_down)
    return _combine_ln(dest, meta[TOP_K:2 * TOP_K].T, x, ys, ws_gate, ws_up, ws_down, g, b)


def _block_diag(w):
    nb, bs, _ = w.shape
    eye = jnp.eye(nb, dtype=w.dtype)
    return (eye[:, None, :, None] * w[:, :, None, :]).reshape(nb * bs, nb * bs)


def _even_mixer_ln(x, w_in, conv_w, conv_b, wa, ba, wx, bx, lam, lb, norm_g, w_out, g, b):
    bsz, t, d = x.shape
    xf = x.reshape(bsz * t, d)
    u = _proj(xf, w_in.astype(BF16)).reshape(bsz, t, -1)
    ya = _lru(u, conv_w, conv_b, _block_diag(wa).astype(BF16), ba, _block_diag(wx).astype(BF16), bx, lam)
    yb = _hgrn2(u, lb, norm_g)
    return _proj_ln(ya.reshape(bsz * t, -1), yb.reshape(bsz * t, -1), w_out, xf, g, b).reshape(bsz, t, d)


def _odd_mixer_ln(x, w_in, pe_k, pe_v, wk1, wk2, wv1, wv2, w_out, g, b):
    bsz, t, d = x.shape
    xf = x.reshape(bsz * t, d)
    u16, u32, kmean = _odd_proj(xf, w_in, t)
    u16 = u16.reshape(bsz, t, U16_W)
    kmean = kmean.reshape(bsz, t // MOBA_BLOCK, -1)

    def grp(off):
        z = u32[:, off:off + NSA_KV].reshape(bsz, t, NSA_GROUPS, HEAD_DIM)
        return z.transpose(0, 2, 1, 3)

    n_all = NSA_HEADS + MOBA_HEADS
    s_nsa = jnp.asarray((2.0 ** (-8.0 * np.arange(1, n_all + 1) / n_all))[0::2], F32)
    k_cmp, v_cmp = _nsa_compress(grp(0), grp(NSA_KV), pe_k, pe_v, wk1, wk2, wv1, wv2)
    o_c, mask_bias = _nsa_select(s_nsa, u16, k_cmp, v_cmp)
    o_s = _flash("slc", u16, mask_bias, tq=128, tiles_per_step=t // 128)
    o_w = _flash("win", u16, tq=128, tiles_per_step=t // 128)
    o_m = _flash("moba", u16, kmean, tq=MOBA_BLOCK, nkv=2, tiles_per_step=t // MOBA_BLOCK)
    flat = lambda z: z.reshape(bsz * t, -1)
    return _odd_out_ln(flat(o_c), flat(o_s), flat(o_w), flat(o_m), u32, w_out, xf, g, b).reshape(bsz, t, d)


def kernel(x, even_w_in, lru_conv_w, lru_conv_b, lru_wa, lru_ba, lru_wx, lru_bx, lru_lambda,
           hg_lower_bound, hg_norm_g, even_w_out, odd_w_in, nsa_pe_k, nsa_pe_v, nsa_wk1, nsa_wk2,
           nsa_wv1, nsa_wv2, odd_w_out, ln_g, ln_b, w_router, b_router, w_gate, w_up, w_down,
           ws_gate, ws_up, ws_down):
    bsz, t, d = x.shape
    lb_all = jnp.cumsum(jax.nn.softmax(hg_lower_bound.astype(F32), axis=0), axis=0)
    for layer in range(DEPTH):
        li = layer // 2
        if layer % 2 == 0:
            x = _even_mixer_ln(x, even_w_in[li], lru_conv_w[li], lru_conv_b[li], lru_wa[li], lru_ba[li],
                               lru_wx[li], lru_bx[li], lru_lambda[li], lb_all[layer], hg_norm_g[li],
                               even_w_out[li], ln_g[layer, 0], ln_b[layer, 0])
        else:
            x = _odd_mixer_ln(x, odd_w_in[li], nsa_pe_k[li], nsa_pe_v[li], nsa_wk1[li], nsa_wk2[li],
                              nsa_wv1[li], nsa_wv2[li], odd_w_out[li], ln_g[layer, 0], ln_b[layer, 0])
        x = _moe_ln(x.reshape(bsz * t, d), layer, w_router[layer], b_router[layer], w_gate, w_up,
                    w_down, ws_gate[layer], ws_up[layer], ws_down[layer],
                    ln_g[layer, 1], ln_b[layer, 1]).reshape(bsz, t, d)
    return x
```

```python
import functools

import numpy as np
import jax
import jax.numpy as jnp
from jax import lax
from jax.experimental import pallas as pl
from jax.experimental.pallas import tpu as pltpu

F32 = jnp.float32
BF16 = jnp.bfloat16

D_MODEL = 1024
DEPTH = 2
LRU_WIDTH = 512
LRU_BLOCKS = 8
LRU_BLOCK = LRU_WIDTH // LRU_BLOCKS
CONV_W = 4
LRU_C = 8.0
HG_HEADS = 4
HG_DK = 128
HG_FDIM = HG_HEADS * HG_DK
HG_VDIM = HG_HEADS * HG_DK
HEAD_DIM = 64
NSA_HEADS = 8
NSA_GROUPS = 2
NSA_HPG = NSA_HEADS // NSA_GROUPS
NSA_KV = NSA_GROUPS * HEAD_DIM
CMP_LEN = 32
CMP_STRIDE = 16
CMP_HIDDEN = 128
SLC_LEN = 64
SLC_TOPN = 16
WIN = 512
FORCE_BONUS = 1e4
MOBA_HEADS = 8
MOBA_BLOCK = 256
MOBA_TOPK = 3
N_EXPERTS = 256
TOP_K = 8
N_GROUPS = 8
TOPK_GROUPS = 4
EXPERT_HIDDEN = 256
ROUTED_SCALE = 2.5
DN_ALPHA = (2 * DEPTH) ** 0.25
LN_EPS = 1e-5
NEG = -1e30
NEG_INF = float("-inf")

LANES = 128
SUBLANES = 8
VMEM_LIMIT = 48 * 1024 * 1024

HG_SUB = 16
EXPERT_ROWS = 512


def _cparams(sem):
    return pltpu.CompilerParams(dimension_semantics=sem, vmem_limit_bytes=VMEM_LIMIT)


def _dot(a, b):
    return jnp.dot(a.astype(BF16), b.astype(BF16), preferred_element_type=F32)


def _dot_nt(a, b):
    return lax.dot_general(a.astype(BF16), b.astype(BF16), (((1,), (1,)), ((), ())),
                           preferred_element_type=F32)


def _dot_tn(a, b):
    return lax.dot_general(a.astype(BF16), b.astype(BF16), (((0,), (0,)), ((), ())),
                           preferred_element_type=F32)


def _split2(x):
    hi = x.astype(BF16)
    lo = (x - hi.astype(F32)).astype(BF16)
    return hi, lo


def _split3(x):
    hi = x.astype(BF16)
    r = x - hi.astype(F32)
    mid = r.astype(BF16)
    lo = (r - mid.astype(F32)).astype(BF16)
    return hi, mid, lo


def _layer_norm(v, g, b):
    mu = jnp.mean(v, axis=-1, keepdims=True)
    d = v - mu
    var = jnp.mean(d * d, axis=-1, keepdims=True)
    return d * lax.rsqrt(var + LN_EPS) * g + b


def _sigmoid(x):
    return 1.0 / (1.0 + jnp.exp(-x))


def _silu(x):
    return x * _sigmoid(x)


def _proj_kernel(x_ref, w_ref, o_ref):
    o_ref[...] = jnp.dot(x_ref[...].astype(BF16), w_ref[...], preferred_element_type=F32)


def _proj(x, w, tm=256):
    n, k = x.shape
    m = w.shape[1]
    return pl.pallas_call(
        _proj_kernel,
        grid=(n // tm,),
        in_specs=[pl.BlockSpec((tm, k), lambda i: (i, 0)),
                  pl.BlockSpec((k, m), lambda i: (0, 0))],
        out_specs=pl.BlockSpec((tm, m), lambda i: (i, 0)),
        out_shape=jax.ShapeDtypeStruct((n, m), F32),
        compiler_params=_cparams(("parallel",)),
        name="proj",
    )(x, w)


def _proj_ln_kernel(ya_ref, yb_ref, wa_ref, wb_ref, x_ref, g_ref, b_ref, o_ref):
    mix = (jnp.dot(ya_ref[...].astype(BF16), wa_ref[...], preferred_element_type=F32)
           + jnp.dot(yb_ref[...].astype(BF16), wb_ref[...], preferred_element_type=F32))
    o_ref[...] = _layer_norm(DN_ALPHA * x_ref[...] + mix, g_ref[...], b_ref[...])


def _proj_ln(ya, yb, w, xres, g, b, tm=256):
    n, ka = ya.shape
    kb = yb.shape[1]
    d = w.shape[1]
    wb16 = w.astype(BF16)
    full = lambda shp: pl.BlockSpec(shp, lambda i: (0, 0))
    return pl.pallas_call(
        _proj_ln_kernel,
        grid=(n // tm,),
        in_specs=[pl.BlockSpec((tm, ka), lambda i: (i, 0)),
                  pl.BlockSpec((tm, kb), lambda i: (i, 0)),
                  full((ka, d)), full((kb, d)),
                  pl.BlockSpec((tm, d), lambda i: (i, 0)),
                  full((1, d)), full((1, d))],
        out_specs=pl.BlockSpec((tm, d), lambda i: (i, 0)),
        out_shape=jax.ShapeDtypeStruct((n, d), F32),
        compiler_params=_cparams(("parallel",)),
        name="proj_ln",
    )(ya, yb, wb16[:ka], wb16[ka:], xres, g.reshape(1, d), b.reshape(1, d))


U16_QA = 0
U16_QM = U16_QA + NSA_HEADS * LANES
U16_KS = U16_QM + MOBA_HEADS * LANES
U16_KW = U16_KS + NSA_GROUPS * LANES
U16_KM = U16_KW + NSA_GROUPS * LANES
U16_VS = U16_KM + MOBA_HEADS * LANES
U16_VW = U16_VS + NSA_GROUPS * LANES
U16_VM = U16_VW + NSA_GROUPS * LANES
U16_W = U16_VM + MOBA_HEADS * LANES
U32_W = 3 * LANES


def _odd_proj_kernel(x_ref, w16_ref, w32_ref, qc_ref, kt_ref, o16_ref, o32_ref, km_ref):
    xb = x_ref[...].astype(BF16)
    acc = jnp.dot(xb, w16_ref[...], preferred_element_type=F32)
    km_ref[0] = jnp.mean(acc[:, U16_KM:U16_VS], axis=0, keepdims=True)
    o16_ref[:, 0:U16_KS] = (acc[:, 0:U16_KS] + qc_ref[...]).astype(BF16)
    o16_ref[:, U16_KS:U16_VS] = (acc[:, U16_KS:U16_VS] + kt_ref[...]).astype(BF16)
    o16_ref[:, U16_VS:U16_W] = acc[:, U16_VS:U16_W].astype(BF16)
    o32_ref[...] = jnp.dot(xb, w32_ref[...], preferred_element_type=F32)


def _odd_proj(x, w_in, t, tm=MOBA_BLOCK):
    n, d = x.shape
    assert tm == MOBA_BLOCK and t % tm == 0
    dh = HEAD_DIM
    nq = NSA_HEADS * dh
    nm = MOBA_HEADS * dh
    sizes = [nq] + [NSA_KV] * 6 + [3 * NSA_HEADS] + [nm] * 3
    offs = np.concatenate([[0], np.cumsum(sizes)])
    q, kc, vc, ks, vs, kw, vw, gates, mq, mk, mv = [w_in[:, offs[i]:offs[i + 1]] for i in range(11)]
    scale = dh ** -0.5

    def heads(w, nh, mult=1.0):
        w = (w * mult).reshape(d, nh, dh)
        return jnp.pad(w, ((0, 0), (0, 0), (0, LANES - dh))).reshape(d, nh * LANES)

    w16 = jnp.concatenate([heads(q, NSA_HEADS, scale), heads(mq, MOBA_HEADS, scale), heads(ks, NSA_GROUPS),
                           heads(kw, NSA_GROUPS), heads(mk, MOBA_HEADS), heads(vs, NSA_GROUPS),
                           heads(vw, NSA_GROUPS), heads(mv, MOBA_HEADS)], axis=1).astype(BF16)
    w32 = jnp.concatenate([kc, vc, jnp.pad(gates, ((0, 0), (0, LANES - 3 * NSA_HEADS)))], axis=1).astype(BF16)

    n_all = NSA_HEADS + MOBA_HEADS
    s_all = 2.0 ** (-8.0 * np.arange(1, n_all + 1) / n_all)
    slopes = jnp.asarray(np.concatenate([s_all[0::2], s_all[1::2]]), F32)
    s_hi = slopes.astype(BF16).astype(F32)
    s_lo = slopes - s_hi
    scols = jnp.stack([POS_SPLIT * s_hi, POS_SPLIT * s_lo, s_hi, s_lo], axis=-1)
    qconst = jnp.pad(scols, ((0, 0), (dh, LANES - dh - 4))).reshape(1, U16_KS)

    pos = np.arange(t)
    pos_cols = np.stack([pos // POS_SPLIT, pos // POS_SPLIT, pos % POS_SPLIT, pos % POS_SPLIT], axis=-1)

    def key_cols(blk):
        cols = np.zeros((t, LANES), np.float32)
        cols[:, dh:dh + 4] = pos_cols
        if blk:
            cols[pos, MASK_COL0 + pos // blk] = 1.0
        return cols

    ktab = np.concatenate([key_cols(SLC_LEN)] * NSA_GROUPS + [key_cols(0)] * NSA_GROUPS
                          + [key_cols(MOBA_BLOCK)] * MOBA_HEADS, axis=1)
    tiles_per_seq = t // tm
    return pl.pallas_call(
        _odd_proj_kernel,
        grid=(n // tm,),
        in_specs=[pl.BlockSpec((tm, d), lambda i: (i, 0)),
                  pl.BlockSpec((d, U16_W), lambda i: (0, 0)),
                  pl.BlockSpec((d, U32_W), lambda i: (0, 0)),
                  pl.BlockSpec((1, U16_KS), lambda i: (0, 0)),
                  pl.BlockSpec((tm, U16_VS - U16_KS), lambda i: (i % tiles_per_seq, 0))],
        out_specs=[pl.BlockSpec((tm, U16_W), lambda i: (i, 0)),
                   pl.BlockSpec((tm, U32_W), lambda i: (i, 0)),
                   pl.BlockSpec((1, 1, U16_VS - U16_KM), lambda i: (i, 0, 0))],
        out_shape=[jax.ShapeDtypeStruct((n, U16_W), BF16), jax.ShapeDtypeStruct((n, U32_W), F32),
                   jax.ShapeDtypeStruct((n // tm, 1, U16_VS - U16_KM), F32)],
        compiler_params=_cparams(("parallel",)),
        name="odd_proj",
    )(x, w16, w32, qconst, jnp.asarray(ktab, F32))


def _odd_out_kernel(oc_ref, os_ref, ow_ref, om_ref, gt_ref, wa_ref, wb_ref, x_ref, g_ref, b_ref, o_ref):
    sg = _sigmoid(gt_ref[...])
    parts = []
    for h in range(NSA_HEADS):
        cols = slice(h * LANES, (h + 1) * LANES)
        parts.append(sg[:, 3 * h:3 * h + 1] * oc_ref[:, cols] + sg[:, 3 * h + 1:3 * h + 2] * os_ref[:, cols]
                     + sg[:, 3 * h + 2:3 * h + 3] * ow_ref[:, cols])
    yc = jnp.concatenate(parts, axis=1)
    mix = _dot(yc, wa_ref[...]) + _dot(om_ref[...], wb_ref[...])
    o_ref[...] = _layer_norm(DN_ALPHA * x_ref[...] + mix, g_ref[...], b_ref[...])


def _odd_out_ln(o_c, o_s, o_w, o_m, u32, w_out, xres, g, b, tm=256):
    n, d = xres.shape
    dh = HEAD_DIM

    def pad_rows(w, nh):
        return jnp.pad(w.reshape(nh, dh, d), ((0, 0), (0, LANES - dh), (0, 0))).reshape(nh * LANES, d).astype(BF16)

    wa = pad_rows(w_out[:NSA_HEADS * dh], NSA_HEADS)
    wb = pad_rows(w_out[NSA_HEADS * dh:], MOBA_HEADS)
    row = lambda width: pl.BlockSpec((tm, width), lambda i: (i, 0))
    full = lambda shp: pl.BlockSpec(shp, lambda i: (0, 0))
    ka = NSA_HEADS * LANES
    kb = MOBA_HEADS * LANES
    return pl.pallas_call(
        _odd_out_kernel,
        grid=(n // tm,),
        in_specs=[row(ka), row(ka), row(ka), row(kb),
                  pl.BlockSpec((tm, LANES), lambda i: (i, U32_W // LANES - 1)),
                  full((ka, d)), full((kb, d)), row(d), full((1, d)), full((1, d))],
        out_specs=row(d),
        out_shape=jax.ShapeDtypeStruct((n, d), F32),
        compiler_params=_cparams(("parallel",)),
        name="odd_out_ln",
    )(o_c, o_s, o_w, o_m, u32, wa, wb, xres, g.reshape(1, d), b.reshape(1, d))


def _lru_kernel(xb_ref, gb_ref, cw_ref, cb_ref, wa_ref, ba_ref, wx_ref, bx_ref, lam_ref,
                o_ref, xs_ref, a_ref, b_ref, h_ref):
    ti = pl.program_id(1)
    tc = xb_ref.shape[1]
    c = xb_ref.shape[2]

    @pl.when(ti == 0)
    def _():
        xs_ref[0:SUBLANES, :] = jnp.zeros((SUBLANES, c), F32)
        h_ref[...] = jnp.zeros_like(h_ref)

    xs_ref[SUBLANES:SUBLANES + tc, :] = xb_ref[0]
    xc = cb_ref[...]
    for w in range(CONV_W):
        xc = xc + cw_ref[w:w + 1, :] * xs_ref[pl.ds(SUBLANES - (CONV_W - 1) + w, tc), :]
    xs_ref[0:SUBLANES, :] = xs_ref[tc:tc + SUBLANES, :]

    r = _sigmoid(_dot(xc, wa_ref[...]) + ba_ref[...])
    i = _sigmoid(_dot(xc, wx_ref[...]) + bx_ref[...])
    nl = -lam_ref[...]
    softplus = jnp.maximum(nl, 0.0) + jnp.log(1.0 + jnp.exp(-jnp.abs(nl)))
    log_a = (-LRU_C * r) * softplus
    a = jnp.exp(log_a)
    mult = jnp.sqrt(1.0 - jnp.exp(2.0 * log_a))
    trow = ti * tc + lax.broadcasted_iota(jnp.int32, (tc, c), 0)
    mult = jnp.where(trow == 0, 1.0, mult)
    a_ref[...] = a
    b_ref[...] = mult * (i * xc)

    row = lax.broadcasted_iota(jnp.int32, (SUBLANES, c), 0)

    def body(g, carry):
        r0 = pl.multiple_of(g * SUBLANES, SUBLANES)
        av = a_ref[pl.ds(r0, SUBLANES), :]
        bv = b_ref[pl.ds(r0, SUBLANES), :]
        for d in (1, 2, 4):
            a_sh = pltpu.roll(av, d, 0)
            b_sh = pltpu.roll(bv, d, 0)
            m = row >= d
            bv = jnp.where(m, av * b_sh + bv, bv)
            av = jnp.where(m, av * a_sh, av)
        h = av * carry + bv
        b_ref[pl.ds(r0, SUBLANES), :] = h
        return h[SUBLANES - 1:SUBLANES, :]

    h_ref[...] = lax.fori_loop(0, tc // SUBLANES, body, h_ref[...])
    o_ref[0] = b_ref[...] * jax.nn.gelu(gb_ref[0])


def _lru(u3, conv_w, conv_b, wa_bd, ba, wx_bd, bx, lam, tc=256):
    b, t, _ = u3.shape
    c = LRU_WIDTH
    row = lambda v: v.reshape(1, c)
    full = lambda shp: pl.BlockSpec(shp, lambda bi, ti: (0, 0))
    return pl.pallas_call(
        _lru_kernel,
        grid=(b, t // tc),
        in_specs=[pl.BlockSpec((1, tc, c), lambda bi, ti: (bi, ti, 0)),
                  pl.BlockSpec((1, tc, c), lambda bi, ti: (bi, ti, 1)),
                  full((CONV_W, c)), full((1, c)), full((c, c)), full((1, c)),
                  full((c, c)), full((1, c)), full((1, c))],
        out_specs=pl.BlockSpec((1, tc, c), lambda bi, ti: (bi, ti, 0)),
        out_shape=jax.ShapeDtypeStruct((b, t, c), F32),
        scratch_shapes=[pltpu.VMEM((tc + SUBLANES, c), F32), pltpu.VMEM((tc, c), F32),
                        pltpu.VMEM((tc, c), F32), pltpu.VMEM((1, c), F32)],
        compiler_params=_cparams(("parallel", "arbitrary")),
        name="rg_lru",
    )(u3, u3, conv_w, row(conv_b), wa_bd, row(ba), wx_bd, row(bx), row(lam))


def _hgrn_kernel(q_ref, f_ref, v_ref, g_ref, lb_ref, ng_ref, tri_ref, ones_ref, o_ref,
                 st_ref, q_s, k_s, c_s, qe_s, kd_s, dl_s, o_s):
    ti = pl.program_id(2)
    tc = q_ref.shape[1]
    dk = HG_DK
    nh = q_ref.shape[2] // dk

    @pl.when(ti == 0)
    def _():
        st_ref[...] = jnp.zeros_like(st_ref)

    lbv = lb_ref[...]
    q = _silu(q_ref[0])
    f = lbv + (1.0 - lbv) * _sigmoid(f_ref[0])
    kk = 1.0 - f
    logf = jnp.log(f)
    parts = _split3(logf)
    tri = tri_ref[...]
    ones = ones_ref[...]
    cum = sum(jnp.dot(tri, p, preferred_element_type=F32) for p in parts)
    last = sum(jnp.dot(ones, p, preferred_element_type=F32) for p in parts)
    q_s[...] = q
    k_s[...] = kk
    c_s[...] = cum
    qe_s[...] = q * jnp.exp(cum)
    kd_s[...] = kk * jnp.exp(last - cum)
    dl_s[...] = jnp.exp(last)

    s_idx = lax.broadcasted_iota(jnp.int32, (HG_SUB, dk), 0)

    def chunk(ci, _):
        r0 = pl.multiple_of(ci * HG_SUB, HG_SUB)
        rows = pl.ds(r0, HG_SUB)
        for h in range(nh):
            cols = slice(h * dk, (h + 1) * dk)
            st = st_ref[h]
            vc = v_ref[0, rows, cols]
            qc = q_s[rows, cols]
            kc = k_s[rows, cols]
            cc = c_s[rows, cols]
            o = _dot_nt(qe_s[rows, cols], st)
            for t in range(HG_SUB):
                n = (t // SUBLANES + 1) * SUBLANES
                d = cc[t:t + 1, :] - cc[0:n, :]
                dec = jnp.exp(jnp.where(s_idx[0:n, :] <= t, d, NEG))
                w = (qc[t:t + 1, :] * kc[0:n, :]) * dec
                a_t = jnp.sum(w, axis=1, keepdims=True)
                o_t = jnp.sum(a_t * vc[0:n, :], axis=0, keepdims=True)
                o = o + jnp.where(s_idx == t, o_t, 0.0)
            o_s[rows, cols] = o
            st_ref[h] = st * dl_s[pl.ds(r0, 1), cols] + _dot_tn(vc, kd_s[rows, cols])
        return 0

    lax.fori_loop(0, tc // HG_SUB, chunk, 0, unroll=2)
    gate = _sigmoid(g_ref[0])
    for h in range(nh):
        cols = slice(h * dk, (h + 1) * dk)
        o = o_s[:, cols]
        o = o * lax.rsqrt(jnp.mean(o * o, axis=-1, keepdims=True) + 1e-6) * ng_ref[:, cols]
        o_ref[0, :, cols] = o * gate[:, cols]


def _hgrn2(u3, lb, norm_g, tc=256, nh=4):
    b, t, _ = u3.shape
    w = nh * HG_DK
    base = (2 * LRU_WIDTH) // w
    per = HG_FDIM // w
    idx = np.arange(tc)
    same = (idx[:, None] // HG_SUB) == (idx[None, :] // HG_SUB)
    tri = jnp.asarray(same & (idx[None, :] <= idx[:, None]), BF16)
    ones = jnp.asarray(same, BF16)

    def col(off):
        return pl.BlockSpec((1, tc, w), lambda bi, hi, ti: (bi, ti, base + off * per + hi))

    vec = pl.BlockSpec((1, w), lambda bi, hi, ti: (0, hi))
    cst = pl.BlockSpec((tc, tc), lambda bi, hi, ti: (0, 0))
    return pl.pallas_call(
        _hgrn_kernel,
        grid=(b, per, t // tc),
        in_specs=[col(0), col(1), col(2), col(3), vec, vec, cst, cst],
        out_specs=pl.BlockSpec((1, tc, w), lambda bi, hi, ti: (bi, ti, hi)),
        out_shape=jax.ShapeDtypeStruct((b, t, HG_VDIM), F32),
        scratch_shapes=[pltpu.VMEM((nh, HG_DK, HG_DK), F32)] + [pltpu.VMEM((tc, w), F32)] * 7,
        compiler_params=_cparams(("parallel", "parallel", "arbitrary")),
        name="hgrn2",
    )(u3, u3, u3, u3, lb.reshape(1, HG_FDIM), norm_g.reshape(1, HG_VDIM), tri, ones)


def _nsa_cmp_kv_kernel(zk_ref, zv_ref, pek_ref, pev_ref, wk1_ref, wk2_ref, wv1_ref, wv2_ref,
                       ok_ref, ov_ref):
    half = zk_ref.shape[3]
    nrow = zk_ref.shape[2]

    def compress(z, pe_ref, w1_ref, w2_ref):
        lo = _dot(z + pe_ref[0:1, :], w1_ref[0:half, :])
        hi = _dot(z + pe_ref[1:2, :], w1_ref[half:2 * half, :])
        pre = lo + pltpu.roll(hi, nrow - 1, 0)
        return _dot(jax.nn.gelu(pre), w2_ref[...])

    ok_ref[0, 0] = compress(zk_ref[0, 0], pek_ref, wk1_ref, wk2_ref)
    ov_ref[0, 0] = compress(zv_ref[0, 0], pev_ref, wv1_ref, wv2_ref)


def _nsa_compress(kc, vc, pe_k, pe_v, wk1, wk2, wv1, wv2):
    b, g, t, dh = kc.shape
    nrow = t // CMP_STRIDE
    half = CMP_STRIDE * dh
    zk = kc.reshape(b, g, nrow, half)
    zv = vc.reshape(b, g, nrow, half)
    zspec = pl.BlockSpec((1, 1, nrow, half), lambda bi, gi: (bi, gi, 0, 0))
    full = lambda shp: pl.BlockSpec(shp, lambda bi, gi: (0, 0))
    ospec = pl.BlockSpec((1, 1, nrow, LANES), lambda bi, gi: (bi, gi, 0, 0))
    oshape = jax.ShapeDtypeStruct((b, g, nrow, LANES), F32)
    pad = lambda w: jnp.pad(w, ((0, 0), (0, LANES - dh))).astype(BF16)
    return pl.pallas_call(
        _nsa_cmp_kv_kernel,
        grid=(b, g),
        in_specs=[zspec, zspec, full((2, half)), full((2, half)),
                  full((2 * half, CMP_HIDDEN)), full((CMP_HIDDEN, LANES)),
                  full((2 * half, CMP_HIDDEN)), full((CMP_HIDDEN, LANES))],
        out_specs=[ospec, ospec],
        out_shape=[oshape, oshape],
        compiler_params=_cparams(("parallel", "parallel")),
        name="nsa_compress",
    )(zk, zv, pe_k.reshape(2, half), pe_v.reshape(2, half),
      wk1.astype(BF16), pad(wk2), wv1.astype(BF16), pad(wv2))


def _nsa_sel_kernel(slope_ref, q_ref, kc_ref, vc_ref, ov_ref, o_ref, mb_ref):
    gi = pl.program_id(0) % NSA_GROUPS
    qi = pl.program_id(1)
    tq = q_ref.shape[1]
    hpg = q_ref.shape[2] // LANES
    nc = kc_ref.shape[2]
    ns = ov_ref.shape[0]

    q = jnp.concatenate([q_ref[0, :, i * LANES:(i + 1) * LANES] for i in range(hpg)], axis=0)
    khi, klo = _split2(kc_ref[0, 0])
    s = _dot_nt(q, khi) + _dot_nt(q, klo)
    tpos = qi * tq + lax.broadcasted_iota(jnp.int32, (tq, nc), 0)
    cend = lax.broadcasted_iota(jnp.int32, (tq, nc), 1) * CMP_STRIDE + (CMP_LEN - 1)
    d_c = (tpos - cend).astype(F32)
    valid = d_c >= 0.0
    ps = []
    for i in range(hpg):
        si = s[i * tq:(i + 1) * tq, :] - slope_ref[gi * hpg + i] * d_c
        si = jnp.where(valid, si, NEG)
        m = jnp.max(si, axis=-1, keepdims=True)
        p = jnp.where(valid, jnp.exp(si - m), 0.0)
        ps.append(p / jnp.maximum(jnp.sum(p, axis=-1, keepdims=True), 1e-30))
    oc = _dot(jnp.concatenate(ps, axis=0), vc_ref[0, 0])
    o_ref[0] = jnp.concatenate([oc[i * tq:(i + 1) * tq, :] for i in range(hpg)], axis=1)

    psum = ps[0]
    for i in range(1, hpg):
        psum = psum + ps[i]
    phi, plo = _split2(psum)
    ovl = ov_ref[...]
    imp = _dot_nt(ovl, phi) + _dot_nt(ovl, plo)

    j = lax.broadcasted_iota(jnp.int32, (ns, tq), 0)
    own = (qi * tq + lax.broadcasted_iota(jnp.int32, (ns, tq), 1)) // SLC_LEN
    forced = (j == 0) | (j == own) | (j == own - 1)
    imp = jnp.where(j <= own, imp + FORCE_BONUS * forced.astype(F32), NEG)
    rank = jnp.zeros((ns, tq), F32)
    for jp in range(ns):
        row = imp[jp:jp + 1, :]
        beats = (row > imp) | ((row == imp) & (j > jp))
        rank = rank + beats.astype(F32)
    bias = jnp.where(rank < float(min(SLC_TOPN, ns)), 0.0, MASKED)
    place = (lax.broadcasted_iota(jnp.int32, (ns, LANES), 1)
             == lax.broadcasted_iota(jnp.int32, (ns, LANES), 0) + MASK_COL0)
    mb_ref[0] = _dot_tn(bias, place.astype(F32)).astype(BF16)


def _nsa_select(slopes, u16, k_cmp, v_cmp, tq=512):
    b, t, _ = u16.shape
    g = NSA_GROUPS
    hpg = NSA_HPG
    nc = k_cmp.shape[2]
    ns = t // SLC_LEN
    ci = np.arange(nc)[:, None]
    cj = np.arange(ns)[None, :]
    ovl = ((CMP_STRIDE * ci < SLC_LEN * (cj + 1)) & (CMP_STRIDE * ci + CMP_LEN > SLC_LEN * cj))
    ovl = ovl & (ci < (t - CMP_LEN) // CMP_STRIDE + 1)
    grid_spec = pltpu.PrefetchScalarGridSpec(
        num_scalar_prefetch=1,
        grid=(b * g, t // tq),
        in_specs=[pl.BlockSpec((1, tq, hpg * LANES), lambda bg, qi, s: (bg // g, qi, U16_QA // (hpg * LANES) + bg % g)),
                  pl.BlockSpec((1, 1, nc, LANES), lambda bg, qi, s: (bg // g, bg % g, 0, 0)),
                  pl.BlockSpec((1, 1, nc, LANES), lambda bg, qi, s: (bg // g, bg % g, 0, 0)),
                  pl.BlockSpec((ns, nc), lambda bg, qi, s: (0, 0))],
        out_specs=[pl.BlockSpec((1, tq, hpg * LANES), lambda bg, qi, s: (bg // g, qi, bg % g)),
                   pl.BlockSpec((1, tq, LANES), lambda bg, qi, s: (bg // g, qi, bg % g))],
    )
    return pl.pallas_call(
        _nsa_sel_kernel,
        grid_spec=grid_spec,
        out_shape=[jax.ShapeDtypeStruct((b, t, g * hpg * LANES), F32),
                   jax.ShapeDtypeStruct((b, t, g * LANES), BF16)],
        compiler_params=_cparams(("parallel", "parallel")),
        name="nsa_select",
    )(slopes, u16, k_cmp, v_cmp, jnp.asarray(ovl.T, BF16))


POS_SPLIT = 64
MASKED = -2e30


MASK_COL0 = HEAD_DIM + 4
KEY_STEP = 512


def _flash_kernel(q_ref, k_ref, v_ref, *rest, mode, nkv, tq):
    o_ref = rest[-1]
    hpg = q_ref.shape[2] // (LANES * nkv)
    t_all = k_ref.shape[1]
    nt = q_ref.shape[1] // tq
    whole = nt * tq == t_all
    step = MOBA_BLOCK if mode == "moba" else KEY_STEP

    def tile(ti, q0):
        rows = slice(ti * tq, (ti + 1) * tq)
        qs = []
        for j in range(nkv):
            kv_cols = slice(j * LANES, (j + 1) * LANES)
            heads = [q_ref[0, rows, (j * hpg + i) * LANES:(j * hpg + i + 1) * LANES] for i in range(hpg)]
            if mode == "slc":
                heads = [h + rest[0][0, rows, kv_cols] for h in heads]
            q = heads[0] if hpg == 1 else jnp.concatenate(heads, axis=0)
            if mode == "moba":
                kmean = rest[0][0, :, kv_cols]
                nb = kmean.shape[0]
                mhi, mlo = _split2(kmean)
                gate = _dot_nt(mhi, q) + _dot_nt(mlo, q)
                jb = lax.broadcasted_iota(jnp.int32, (nb, tq), 0)
                ob = q0 // MOBA_BLOCK
                gate = jnp.where(jb < ob, gate, NEG)
                rank = jnp.zeros((nb, tq), F32)
                for jp in range(nb):
                    row = gate[jp:jp + 1, :]
                    beats = (row > gate) | ((row == gate) & (jb > jp))
                    rank = rank + beats.astype(F32)
                n_sel = float(min(MOBA_TOPK, nb - 1))
                bsel = ((rank < n_sel) & (jb < ob)) | (jb == ob)
                bias = jnp.where(bsel, 0.0, MASKED)
                place = (lax.broadcasted_iota(jnp.int32, (nb, LANES), 1)
                         == lax.broadcasted_iota(jnp.int32, (nb, LANES), 0) + MASK_COL0)
                q = q + _dot_tn(bias, place.astype(F32)).astype(BF16)
            qs.append(q)

        def attend(kstart, klen, diag, off):
            rel = lax.broadcasted_iota(jnp.int32, (tq, diag), 0) - lax.broadcasted_iota(jnp.int32, (tq, diag), 1)
            valid = rel >= off
            if mode == "win":
                valid = valid & (rel < off + WIN)
            if hpg > 1:
                valid = jnp.concatenate([valid] * hpg, axis=0)
            for j in range(nkv):
                kv_cols = slice(j * LANES, (j + 1) * LANES)
                s = _dot_nt(qs[j], k_ref[0, pl.ds(kstart, klen), kv_cols])
                sd = jnp.where(valid, s[:, klen - diag:], MASKED)
                s = sd if diag == klen else jnp.concatenate([s[:, :klen - diag], sd], axis=1)
                m = jnp.maximum(jnp.max(s, axis=-1, keepdims=True), NEG)
                p = jnp.exp(s - m)
                l = jnp.sum(p, axis=-1, keepdims=True)
                o = _dot(p, v_ref[0, pl.ds(kstart, klen), kv_cols]) / jnp.maximum(l, 1e-30)
                for i in range(hpg):
                    o_ref[0, rows, (j * hpg + i) * LANES:(j * hpg + i + 1) * LANES] = o[i * tq:(i + 1) * tq, :]

        if mode == "win":
            kstart = max(q0 - WIN, 0) if whole else pl.multiple_of(jnp.maximum(q0 - WIN, 0), LANES)
            attend(kstart, WIN + tq, WIN + tq, kstart - q0)
        elif whole:
            v = (q0 + tq - 1) // step
            attend(0, step * (v + 1), step, step * v - q0)
        else:
            var = (q0 + tq - 1) // step
            for v in range(t_all // step):
                @pl.when(var == v)
                def _(v=v):
                    attend(0, step * (v + 1), step, step * v - q0)

    for ti in range(nt):
        tile(ti, ti * tq if whole else (pl.program_id(1) * nt + ti) * tq)


def _flash(mode, u16, extra=None, tq=128, nkv=1, tiles_per_step=1):
    b, t, _ = u16.shape
    q_off, k_off, v_off, n_kvh, hpg = {
        "slc": (U16_QA, U16_KS, U16_VS, NSA_GROUPS, NSA_HPG),
        "win": (U16_QA, U16_KW, U16_VW, NSA_GROUPS, NSA_HPG),
        "moba": (U16_QM, U16_KM, U16_VM, MOBA_HEADS, 1)}[mode]
    assert mode != "moba" or tq == MOBA_BLOCK
    steps = n_kvh // nkv
    qw = nkv * hpg * LANES
    kw = nkv * LANES
    tb = tq * tiles_per_step
    in_specs = [pl.BlockSpec((1, tb, qw), lambda bh, qi: (bh // steps, qi, q_off // qw + bh % steps)),
                pl.BlockSpec((1, t, kw), lambda bh, qi: (bh // steps, 0, k_off // kw + bh % steps)),
                pl.BlockSpec((1, t, kw), lambda bh, qi: (bh // steps, 0, v_off // kw + bh % steps))]
    args = [u16, u16, u16]
    if mode == "slc":
        in_specs.append(pl.BlockSpec((1, tb, kw), lambda bh, qi: (bh // steps, qi, bh % steps)))
        args.append(extra)
    elif mode == "moba":
        in_specs.append(pl.BlockSpec((1, t // MOBA_BLOCK, kw), lambda bh, qi: (bh // steps, 0, bh % steps)))
        args.append(extra)
    return pl.pallas_call(
        functools.partial(_flash_kernel, mode=mode, nkv=nkv, tq=tq),
        grid=(b * steps, t // tb),
        in_specs=in_specs,
        out_specs=pl.BlockSpec((1, tb, qw), lambda bh, qi: (bh // steps, qi, bh % steps)),
        out_shape=jax.ShapeDtypeStruct((b, t, n_kvh * hpg * LANES), F32),
        compiler_params=_cparams(("parallel", "parallel")),
        name="flash_" + mode,
    )(*args)


def _router_kernel(x_ref, whi_ref, wlo_ref, b_ref, triu_ref, meta_ref, cnt_ref, carry_ref):
    step = pl.program_id(0)
    tm = x_ref.shape[0]
    ne = whi_ref.shape[0]
    gsz = ne // N_GROUPS

    @pl.when(step == 0)
    def _():
        carry_ref[...] = jnp.zeros_like(carry_ref)

    xhi, xlo = _split2(x_ref[...])
    whi = whi_ref[...]
    logits = _dot_nt(whi, xhi) + _dot_nt(wlo_ref[...], xhi) + _dot_nt(whi, xlo)
    scores = _sigmoid(logits)
    biased = scores + b_ref[:, 0:1]

    v3 = biased.reshape(N_GROUPS, gsz, tm)
    i3 = lax.broadcasted_iota(jnp.int32, (N_GROUPS, gsz, tm), 1).astype(F32)
    m1 = jnp.max(v3, axis=1, keepdims=True)
    idx1 = jnp.min(jnp.where(v3 == m1, i3, float(gsz)), axis=1, keepdims=True)
    m2 = jnp.max(jnp.where(i3 == idx1, NEG_INF, v3), axis=1, keepdims=True)
    gs = (m1 + m2).reshape(N_GROUPS, tm)
    gi = lax.broadcasted_iota(jnp.int32, (N_GROUPS, tm), 0).astype(F32)
    gmask = jnp.zeros((N_GROUPS, tm), F32)
    for _ in range(TOPK_GROUPS):
        m = jnp.max(gs, axis=0, keepdims=True)
        pick = jnp.min(jnp.where(gs == m, gi, float(N_GROUPS)), axis=0, keepdims=True)
        hit = gi == pick
        gmask = jnp.where(hit, 1.0, gmask)
        gs = jnp.where(hit, NEG_INF, gs)
    emask = jnp.broadcast_to(gmask.reshape(N_GROUPS, 1, tm), (N_GROUPS, gsz, tm)).reshape(ne, tm)
    cur = jnp.where(emask > 0.5, biased, NEG)

    ei = lax.broadcasted_iota(jnp.int32, (ne, tm), 0).astype(F32)
    selm = jnp.zeros((ne, tm), F32)
    idxs, svals = [], []
    for _ in range(TOP_K):
        m = jnp.max(cur, axis=0, keepdims=True)
        idx = jnp.min(jnp.where(cur == m, ei, float(ne)), axis=0, keepdims=True)
        hit = ei == idx
        svals.append(jnp.sum(jnp.where(hit, scores, 0.0), axis=0, keepdims=True))
        idxs.append(idx)
        cur = jnp.where(hit, NEG_INF, cur)
        selm = jnp.where(hit, 1.0, selm)

    carry = carry_ref[:, 0:1]
    rank_full = jnp.dot(selm.astype(BF16), triu_ref[...], preferred_element_type=F32) + carry
    ssum = svals[0]
    for k in range(1, TOP_K):
        ssum = ssum + svals[k]
    for k in range(TOP_K):
        meta_ref[k:k + 1, :] = idxs[k]
        meta_ref[TOP_K + k:TOP_K + k + 1, :] = svals[k] / ssum * ROUTED_SCALE
        meta_ref[2 * TOP_K + k:2 * TOP_K + k + 1, :] = jnp.sum(
            jnp.where(ei == idxs[k], rank_full, 0.0), axis=0, keepdims=True)
    new_carry = carry + jnp.sum(selm, axis=1, keepdims=True)
    carry_ref[...] = jnp.broadcast_to(new_carry, carry_ref.shape)
    cnt_ref[...] = jnp.broadcast_to(new_carry, cnt_ref.shape)


def _router(x, w_router, b_router, tm=512):
    n, d = x.shape
    ne = w_router.shape[1]
    wt = w_router.T
    whi = wt.astype(BF16)
    wlo = (wt - whi.astype(F32)).astype(BF16)
    triu = jnp.asarray(np.arange(tm)[:, None] < np.arange(tm)[None, :], BF16)
    meta, cnt = pl.pallas_call(
        _router_kernel,
        grid=(n // tm,),
        in_specs=[pl.BlockSpec((tm, d), lambda i: (i, 0)),
                  pl.BlockSpec((ne, d), lambda i: (0, 0)),
                  pl.BlockSpec((ne, d), lambda i: (0, 0)),
                  pl.BlockSpec((ne, LANES), lambda i: (0, 0)),
                  pl.BlockSpec((tm, tm), lambda i: (0, 0))],
        out_specs=[pl.BlockSpec((3 * TOP_K, tm), lambda i: (0, i)),
                   pl.BlockSpec((ne, LANES), lambda i: (0, 0))],
        out_shape=[jax.ShapeDtypeStruct((3 * TOP_K, n), F32),
                   jax.ShapeDtypeStruct((ne, LANES), F32)],
        scratch_shapes=[pltpu.VMEM((ne, LANES), F32)],
        compiler_params=_cparams(("arbitrary",)),
        name="moe_router",
    )(x, whi, wlo, jnp.broadcast_to(b_router.reshape(ne, 1), (ne, LANES)), triu)
    return meta, cnt[:, 0]


def _expert_kernel(be_ref, first_ref, nxt_ref, slot_ref, nu_ref, xs_ref, wg_hbm, wu_hbm, wd_hbm, o_ref,
                   wg_buf, wu_buf, wd_buf, wgu16, wd16, sem, *, layer):
    bi = pl.program_id(0)

    def weight_copies(e, s):
        return [pltpu.make_async_copy(hbm.at[layer, e], buf.at[s], sem.at[s])
                for hbm, buf in ((wg_hbm, wg_buf), (wu_hbm, wu_buf), (wd_hbm, wd_buf))]

    @pl.when(bi < nu_ref[0])
    def _():
        s = slot_ref[bi]

        @pl.when(first_ref[bi] == 1)
        def _():
            @pl.when(bi == 0)
            def _():
                for c in weight_copies(be_ref[0], s):
                    c.start()

            @pl.when(nxt_ref[bi] >= 0)
            def _():
                for c in weight_copies(nxt_ref[bi], 1 - s):
                    c.start()

            for c in weight_copies(be_ref[bi], s):
                c.wait()
            hd = wd16.shape[0]
            wgu16[:, 0:hd] = wg_buf[s].astype(BF16)
            wgu16[:, hd:2 * hd] = wu_buf[s].astype(BF16)
            wd16[...] = wd_buf[s].astype(BF16)

        xa, xb = _unpack_bf16_pair(xs_ref[...])
        half = xa.shape[1]

        def xw(w_ref):
            return (jnp.dot(xa, w_ref[0:half, :], preferred_element_type=F32)
                    + jnp.dot(xb, w_ref[half:2 * half, :], preferred_element_type=F32))

        gu = xw(wgu16)
        hd = wd16.shape[0]
        h = _silu(gu[:, 0:hd]) * gu[:, hd:2 * hd]
        o_ref[...] = _pack_bf16_pair(jnp.dot(h.astype(BF16), wd16[...], preferred_element_type=F32))

    @pl.when(bi >= nu_ref[0])
    def _():
        o_ref[...] = jnp.zeros_like(o_ref)


def _expert_ffn(blk_e, n_used, xs, layer, w_gate, w_up, w_down):
    n_rows = xs.shape[0]
    d = w_gate.shape[2]
    hdim = w_gate.shape[3]
    n_blk = n_rows // EXPERT_ROWS
    idx = jnp.arange(n_blk, dtype=jnp.int32)
    first = jnp.concatenate([jnp.ones((1,), bool), blk_e[1:] != blk_e[:-1]]) & (idx < n_used[0])
    slot = ((jnp.cumsum(first.astype(jnp.int32)) - 1) % 2).astype(jnp.int32)
    nxt_first = lax.cummin(jnp.where(first, idx, n_blk), reverse=True)
    nxt_first = jnp.concatenate([nxt_first[1:], jnp.full((1,), n_blk, jnp.int32)])
    nxt_e = jnp.where(nxt_first < n_blk, blk_e[jnp.minimum(nxt_first, n_blk - 1)], -1).astype(jnp.int32)
    grid_spec = pltpu.PrefetchScalarGridSpec(
        num_scalar_prefetch=5,
        grid=(n_blk,),
        in_specs=[pl.BlockSpec((EXPERT_ROWS, d // 2),
                               lambda i, be, fi, nx, sl, nu: (jnp.minimum(i, jnp.maximum(nu[0] - 1, 0)), 0)),
                  pl.BlockSpec(memory_space=pl.ANY),
                  pl.BlockSpec(memory_space=pl.ANY),
                  pl.BlockSpec(memory_space=pl.ANY)],
        out_specs=pl.BlockSpec((EXPERT_ROWS, d // 2), lambda i, be, fi, nx, sl, nu: (i, 0)),
        scratch_shapes=[pltpu.VMEM((2, d, hdim), F32), pltpu.VMEM((2, d, hdim), F32), pltpu.VMEM((2, hdim, d), F32),
                        pltpu.VMEM((d, 2 * hdim), BF16), pltpu.VMEM((hdim, d), BF16),
                        pltpu.SemaphoreType.DMA((2,))],
    )
    return pl.pallas_call(
        functools.partial(_expert_kernel, layer=layer),
        grid_spec=grid_spec,
        out_shape=jax.ShapeDtypeStruct((n_rows, d // 2), jnp.uint32),
        compiler_params=_cparams(("arbitrary",)),
        name="moe_experts",
    )(blk_e, first.astype(jnp.int32), nxt_e, slot, n_used, xs, w_gate, w_up, w_down)


def _dest_kernel(meta_ref, pstart_ref, dest_ref):
    ne = pstart_ref.shape[0]
    tm = meta_ref.shape[1]
    ei = lax.broadcasted_iota(jnp.int32, (ne, tm), 0).astype(F32)
    ps = pstart_ref[:, 0:1]
    for k in range(TOP_K):
        base = jnp.sum(jnp.where(ei == meta_ref[k:k + 1, :], ps, 0.0), axis=0, keepdims=True)
        dest_ref[k:k + 1, :] = (base + meta_ref[2 * TOP_K + k:2 * TOP_K + k + 1, :]).astype(jnp.int32)


def _dest_rows(meta, pstart, tm=512):
    n = meta.shape[1]
    ne = pstart.shape[0]
    return pl.pallas_call(
        _dest_kernel,
        grid=(n // tm,),
        in_specs=[pl.BlockSpec((3 * TOP_K, tm), lambda i: (0, i)),
                  pl.BlockSpec((ne, LANES), lambda i: (0, 0))],
        out_specs=pl.BlockSpec((TOP_K, tm), lambda i: (0, i)),
        out_shape=jax.ShapeDtypeStruct((TOP_K, n), jnp.int32),
        compiler_params=_cparams(("parallel",)),
        name="moe_dest",
    )(meta, jnp.broadcast_to(pstart.astype(F32).reshape(ne, 1), (ne, LANES)))


def _pack_bf16_pair(x):
    half = x.shape[1] // 2
    hi = pltpu.bitcast(x[:, 0:half].astype(BF16).astype(F32), jnp.uint32)
    lo = pltpu.bitcast(x[:, half:2 * half].astype(BF16).astype(F32), jnp.uint32)
    return hi | (lo >> 16)


def _unpack_bf16_pair(p):
    hi = pltpu.bitcast(p & jnp.uint32(0xFFFF0000), F32).astype(BF16)
    lo = pltpu.bitcast(p << 16, F32).astype(BF16)
    return hi, lo


def _dispatch_kernel(dest_ref, x_ref, xs_in_ref, xs_ref, pk_ref, sem):
    del xs_in_ref
    tm = x_ref.shape[0]
    pk_ref[...] = _pack_bf16_pair(x_ref[...])

    def row_copy(t, k):
        return pltpu.make_async_copy(pk_ref.at[pl.ds(t, 1), :], xs_ref.at[pl.ds(dest_ref[t * TOP_K + k], 1), :], sem)

    def body(t, _):
        for k in range(TOP_K):
            row_copy(t, k).start(priority=k % 2)
        return 0

    lax.fori_loop(0, tm, body, 0, unroll=4)
    done = xs_ref.at[pl.ds(0, TOP_K * tm), :]
    pltpu.make_async_copy(done, done, sem).wait()


def _dispatch(dest, x, n_rows, tm=256):
    n, d = x.shape
    return pl.pallas_call(
        _dispatch_kernel,
        grid=(n // tm,),
        in_specs=[pl.BlockSpec((TOP_K * tm,), lambda i: (i,), memory_space=pltpu.SMEM),
                  pl.BlockSpec((tm, d), lambda i: (i, 0)),
                  pl.BlockSpec(memory_space=pl.ANY)],
        out_specs=pl.BlockSpec(memory_space=pl.ANY),
        out_shape=jax.ShapeDtypeStruct((n_rows, d // 2), jnp.uint32),
        scratch_shapes=[pltpu.VMEM((tm, d // 2), jnp.uint32), pltpu.SemaphoreType.DMA(())],
        input_output_aliases={2: 0},
        compiler_params=_cparams(("arbitrary",)),
        name="moe_dispatch",
    )(dest, x, jnp.zeros((n_rows, d // 2), jnp.uint32))


def _combine_ln_kernel(dest_ref, dest_next_ref, w_ref, x_ref, ys_ref, wg_ref, wu_ref, wd_ref, g_ref, b_ref,
                       o_ref, buf, sem):
    step = pl.program_id(0)
    tm = x_ref.shape[0]
    slot = step % 2

    def gather_rows(idx_ref, to_slot):
        def body(t, _):
            for k in range(TOP_K):
                pltpu.make_async_copy(ys_ref.at[pl.ds(idx_ref[t * TOP_K + k], 1), :],
                                      buf.at[to_slot, k, pl.ds(t, 1), :], sem.at[to_slot]).start(priority=k % 2)
            return 0

        lax.fori_loop(0, tm, body, 0, unroll=4)

    @pl.when(step == 0)
    def _():
        gather_rows(dest_ref, 0)

    pltpu.make_async_copy(buf.at[slot], buf.at[slot], sem.at[slot]).wait()

    for t in range(tm):
        for k in range(TOP_K):
            pltpu.make_async_copy(ys_ref.at[pl.ds(dest_next_ref[t * TOP_K + k], 1), :],
                                  buf.at[1 - slot, k, pl.ds(t, 1), :], sem.at[1 - slot]).start(priority=k % 2)

    x = x_ref[...]
    xb = x.astype(BF16)
    h = _silu(jnp.dot(xb, wg_ref[...], preferred_element_type=F32)) * jnp.dot(
        xb, wu_ref[...], preferred_element_type=F32)
    ff = jnp.dot(h.astype(BF16), wd_ref[...], preferred_element_type=F32)
    routed_a = jnp.zeros((tm, buf.shape[3]), F32)
    routed_b = jnp.zeros((tm, buf.shape[3]), F32)
    for k in range(TOP_K):
        ya, yb = _unpack_bf16_pair(buf[slot, k])
        routed_a = routed_a + ya.astype(F32) * w_ref[:, k:k + 1]
        routed_b = routed_b + yb.astype(F32) * w_ref[:, k:k + 1]
    ff = ff + jnp.concatenate([routed_a, routed_b], axis=1)
    o_ref[...] = _layer_norm(DN_ALPHA * x + ff, g_ref[...], b_ref[...])

    @pl.when(step + 1 == pl.num_programs(0))
    def _():
        pltpu.make_async_copy(buf.at[1 - slot], buf.at[1 - slot], sem.at[1 - slot]).wait()


def _combine_ln(dest, wts, x, ys, ws_gate, ws_up, ws_down, g, b, tm=256):
    n, d = x.shape
    hdim = ws_gate.shape[1]
    row = pl.BlockSpec((tm, d), lambda i: (i, 0))
    last = n // tm - 1
    return pl.pallas_call(
        _combine_ln_kernel,
        grid=(n // tm,),
        in_specs=[pl.BlockSpec((TOP_K * tm,), lambda i: (i,), memory_space=pltpu.SMEM),
                  pl.BlockSpec((TOP_K * tm,), lambda i: (jnp.minimum(i + 1, last),), memory_space=pltpu.SMEM),
                  pl.BlockSpec((tm, TOP_K), lambda i: (i, 0)),
                  row,
                  pl.BlockSpec(memory_space=pl.ANY),
                  pl.BlockSpec((d, hdim), lambda i: (0, 0)),
                  pl.BlockSpec((d, hdim), lambda i: (0, 0)),
                  pl.BlockSpec((hdim, d), lambda i: (0, 0)),
                  pl.BlockSpec((1, d), lambda i: (0, 0)),
                  pl.BlockSpec((1, d), lambda i: (0, 0))],
        out_specs=row,
        out_shape=jax.ShapeDtypeStruct((n, d), F32),
        scratch_shapes=[pltpu.VMEM((2, TOP_K, tm, d // 2), jnp.uint32), pltpu.SemaphoreType.DMA((2,))],
        compiler_params=_cparams(("arbitrary",)),
        name="moe_combine_ln",
    )(dest, dest, wts, x, ys, ws_gate.astype(BF16), ws_up.astype(BF16), ws_down.astype(BF16),
      g.reshape(1, d), b.reshape(1, d))


def _moe_ln(x, layer, w_router, b_router, w_gate, w_up, w_down, ws_gate, ws_up, ws_down, g, b):
    n, d = x.shape
    ne = w_router.shape[1]
    meta, counts = _router(x, w_router, b_router)
    counts = counts.astype(jnp.int32)
    padded = (counts + EXPERT_ROWS - 1) // EXPERT_ROWS * EXPERT_ROWS
    pend = jnp.cumsum(padded)
    dest = _dest_rows(meta, pend - padded).T.reshape(-1)
    n_blk = (n * TOP_K) // EXPERT_ROWS + ne
    blk_row0 = jnp.arange(n_blk, dtype=jnp.int32) * EXPERT_ROWS
    blk_e = jnp.minimum(jnp.sum((pend[None, :] <= blk_row0[:, None]).astype(jnp.int32), axis=1), ne - 1)
    n_used = (pend[-1] // EXPERT_ROWS).astype(jnp.int32).reshape(1)
    xs = _dispatch(dest, x, n_blk * EXPERT_ROWS)
    ys = _expert_ffn(blk_e, n_used, xs, layer, w_gate, w_up, w_down)
    return _combine_ln(dest, meta[TOP_K:2 * TOP_K].T, x, ys, ws_gate, ws_up, ws_down, g, b)


def _block_diag(w):
    nb, bs, _ = w.shape
    eye = jnp.eye(nb, dtype=w.dtype)
    return (eye[:, None, :, None] * w[:, :, None, :]).reshape(nb * bs, nb * bs)


def _even_mixer_ln(x, w_in, conv_w, conv_b, wa, ba, wx, bx, lam, lb, norm_g, w_out, g, b):
    bsz, t, d = x.shape
    xf = x.reshape(bsz * t, d)
    u = _proj(xf, w_in.astype(BF16)).reshape(bsz, t, -1)
    ya = _lru(u, conv_w, conv_b, _block_diag(wa).astype(BF16), ba, _block_diag(wx).astype(BF16), bx, lam)
    yb = _hgrn2(u, lb, norm_g)
    return _proj_ln(ya.reshape(bsz * t, -1), yb.reshape(bsz * t, -1), w_out, xf, g, b).reshape(bsz, t, d)


def _odd_mixer_ln(x, w_in, pe_k, pe_v, wk1, wk2, wv1, wv2, w_out, g, b):
    bsz, t, d = x.shape
    xf = x.reshape(bsz * t, d)
    u16, u32, kmean = _odd_proj(xf, w_in, t)
    u16 = u16.reshape(bsz, t, U16_W)
    kmean = kmean.reshape(bsz, t // MOBA_BLOCK, -1)

    def grp(off):
        z = u32[:, off:off + NSA_KV].reshape(bsz, t, NSA_GROUPS, HEAD_DIM)
        return z.transpose(0, 2, 1, 3)

    n_all = NSA_HEADS + MOBA_HEADS
    s_nsa = jnp.asarray((2.0 ** (-8.0 * np.arange(1, n_all + 1) / n_all))[0::2], F32)
    k_cmp, v_cmp = _nsa_compress(grp(0), grp(NSA_KV), pe_k, pe_v, wk1, wk2, wv1, wv2)
    o_c, mask_bias = _nsa_select(s_nsa, u16, k_cmp, v_cmp)
    o_s = _flash("slc", u16, mask_bias, tq=128, tiles_per_step=t // 128)
    o_w = _flash("win", u16, tq=128, tiles_per_step=t // 128)
    o_m = _flash("moba", u16, kmean, tq=MOBA_BLOCK, nkv=2, tiles_per_step=t // MOBA_BLOCK)
    flat = lambda z: z.reshape(bsz * t, -1)
    return _odd_out_ln(flat(o_c), flat(o_s), flat(o_w), flat(o_m), u32, w_out, xf, g, b).reshape(bsz, t, d)


def kernel(x, even_w_in, lru_conv_w, lru_conv_b, lru_wa, lru_ba, lru_wx, lru_bx, lru_lambda,
           hg_lower_bound, hg_norm_g, even_w_out, odd_w_in, nsa_pe_k, nsa_pe_v, nsa_wk1, nsa_wk2,
           nsa_wv1, nsa_wv2, odd_w_out, ln_g, ln_b, w_router, b_router, w_gate, w_up, w_down,
           ws_gate, ws_up, ws_down):
    bsz, t, d = x.shape
    lb_all = jnp.cumsum(jax.nn.softmax(hg_lower_bound.astype(F32), axis=0), axis=0)
    for layer in range(DEPTH):
        li = layer // 2
        if layer % 2 == 0:
            x = _even_mixer_ln(x, even_w_in[li], lru_conv_w[li], lru_conv_b[li], lru_wa[li], lru_ba[li],
                               lru_wx[li], lru_bx[li], lru_lambda[li], lb_all[layer], hg_norm_g[li],
                               even_w_out[li], ln_g[layer, 0], ln_b[layer, 0])
        else:
            x = _odd_mixer_ln(x, odd_w_in[li], nsa_pe_k[li], nsa_pe_v[li], nsa_wk1[li], nsa_wk2[li],
                              nsa_wv1[li], nsa_wv2[li], odd_w_out[li], ln_g[layer, 0], ln_b[layer, 0])
        x = _moe_ln(x.reshape(bsz * t, d), layer, w_router[layer], b_router[layer], w_gate, w_up,
                    w_down, ws_gate[layer], ws_up[layer], ws_down[layer],
                    ln_g[layer, 1], ln_b[layer, 1]).reshape(bsz, t, d)
    return x
```
